```python
import math
import jax, jax.numpy as jnp
from jax import lax
import numpy as np

D_MODEL = 1024
BATCH = 1
SEQ = 16384
DEPTH = 1

HEAD_DIM = 128
HEADS_PER_GROUP = 4
ATTN_GROUPS = ((128, 1), (512, 4), (2048, 16))
N_ATTN_GROUPS = len(ATTN_GROUPS)
ATTN_HEADS = HEADS_PER_GROUP * N_ATTN_GROUPS
ATTN_QK_WIDTH = ATTN_HEADS * HEAD_DIM
ATTN_OUT_WIDTH = HEADS_PER_GROUP * HEAD_DIM
BLOCK = 128
ROPE_THETA = 500000.0
ROPE_DIM = HEAD_DIM // 4
SSM_WIDTH = 512
SSM_GROUP = 16
SSM_GROUPS = SSM_WIDTH // SSM_GROUP
SSM_STATE = 64
DT_MIN = 0.001
DT_MAX = 0.1
D_FF = -(-8 * D_MODEL // (3 * 256)) * 256
PLE_DIM = 256
EPS = 1e-6
IN_WIDTH = 3 * ATTN_QK_WIDTH + SSM_WIDTH + 2 * D_MODEL

kernel_name = "hybrid_dilated_attn_s5_gated_block"


def rmsnorm(x, g):
    xf = x.astype(jnp.float32)
    y = xf * lax.rsqrt(jnp.mean(xf * xf, axis=-1, keepdims=True) + EPS)
    return (y * g.astype(jnp.float32)).astype(x.dtype)


def partial_rotary(x, positions):
    half = ROPE_DIM // 2
    inv_freq = ROPE_THETA ** (-jnp.arange(half, dtype=jnp.float32) * 2.0 / ROPE_DIM)
    ang = positions.astype(jnp.float32)[..., None] * inv_freq
    cos = jnp.cos(ang)[:, :, None, :]
    sin = jnp.sin(ang)[:, :, None, :]
    xr = x[..., :ROPE_DIM].astype(jnp.float32)
    x1, x2 = xr[..., :half], xr[..., half:]
    rot = jnp.concatenate([x1 * cos - x2 * sin, x2 * cos + x1 * sin], axis=-1).astype(x.dtype)
    return jnp.concatenate([rot, x[..., ROPE_DIM:]], axis=-1)


def dilated_band_attention(q, k, v, dilation, band):
    B, S, H, Dh = q.shape
    L = S // dilation
    Lp = -(-L // BLOCK) * BLOCK
    nb = Lp // BLOCK

    def to_sub(t):
        t = jnp.moveaxis(t.reshape(B, L, dilation, H, Dh), 2, 1)
        t = jnp.pad(t, ((0, 0), (0, 0), (0, Lp - L), (0, 0), (0, 0)))
        return t.reshape(B, dilation, nb, BLOCK, H, Dh)

    def with_prev(t):
        prev = jnp.pad(t, ((0, 0), (0, 0), (1, 0), (0, 0), (0, 0), (0, 0)))[:, :, :-1]
        return jnp.concatenate([prev, t], axis=3)

    def from_sub(t):
        rest = t.shape[4:]
        t = t.reshape((B, dilation, Lp) + rest)[:, :, :L]
        return jnp.moveaxis(t, 1, 2).reshape((B, S) + rest)

    qb = to_sub(q)
    kk = with_prev(to_sub(k))
    vv = with_prev(to_sub(v))
    scale = 1.0 / math.sqrt(Dh)
    s = jnp.einsum('brnqhd,brnkhd->brnhqk', qb, kk,
                   preferred_element_type=jnp.float32) * scale
    qi = jnp.arange(BLOCK)[:, None]
    kj = jnp.arange(2 * BLOCK)[None, :]
    rel = BLOCK + qi - kj
    in_band = (rel >= 0) & (rel <= band)
    not_first = (jnp.arange(nb) > 0)[:, None, None]
    valid = in_band[None] & (not_first | (kj >= BLOCK)[None])
    s = jnp.where(valid[None, None, :, None], s, -jnp.inf)
    m = jnp.max(s, axis=-1, keepdims=True)
    pexp = jnp.exp(s - m)
    den = jnp.sum(pexp, axis=-1, keepdims=True)
    o = jnp.einsum('brnhqk,brnkhd->brnqhd', pexp, vv.astype(jnp.float32))
    o = o / jnp.swapaxes(den[..., 0], -1, -2)[..., None]
    lse = jnp.swapaxes(m[..., 0] + jnp.log(den[..., 0]), -1, -2)
    return from_sub(o), from_sub(lse)


def _complex_affine_combine(e1, e2):
    a1r, a1i, b1r, b1i = e1
    a2r, a2i, b2r, b2i = e2
    ar = a1r * a2r - a1i * a2i
    ai = a1r * a2i + a1i * a2r
    br = a2r * b1r - a2i * b1i + b2r
    bi = a2r * b1i + a2i * b1r + b2i
    return (ar, ai, br, bi)


def s5_ssm(u, a_re, a_im, log_dt, b_re, b_im, c_re, c_im, d_skip):
    B, S, _ = u.shape
    uf = u.astype(jnp.float32).reshape(B, S, SSM_GROUPS, SSM_GROUP)
    dt = jnp.exp(log_dt.astype(jnp.float32))[:, None]
    lr = a_re.astype(jnp.float32)
    li = a_im.astype(jnp.float32)
    mag = jnp.exp(lr * dt)
    bar_re = mag * jnp.cos(li * dt)
    bar_im = mag * jnp.sin(li * dt)
    nr = bar_re - 1.0
    ni = bar_im
    den = lr * lr + li * li
    z_re = (nr * lr + ni * li) / den
    z_im = (ni * lr - nr * li) / den
    br_ = b_re.astype(jnp.float32)
    bi_ = b_im.astype(jnp.float32)
    bb_re = z_re[..., None] * br_ - z_im[..., None] * bi_
    bb_im = z_re[..., None] * bi_ + z_im[..., None] * br_
    bu_re = jnp.einsum('bsgc,gpc->bsgp', uf, bb_re)
    bu_im = jnp.einsum('bsgc,gpc->bsgp', uf, bb_im)
    ar = jnp.broadcast_to(bar_re, bu_re.shape)
    ai = jnp.broadcast_to(bar_im, bu_im.shape)
    _, _, h_re, h_im = lax.associative_scan(_complex_affine_combine, (ar, ai, bu_re, bu_im), axis=1)
    y = (jnp.einsum('bsgp,gcp->bsgc', h_re, c_re.astype(jnp.float32))
         - jnp.einsum('bsgp,gcp->bsgc', h_im, c_im.astype(jnp.float32))
         + d_skip.astype(jnp.float32) * uf)
    return y.reshape(B, S, SSM_WIDTH).astype(u.dtype)


def hybrid_layer(h, p_l, positions, g_mix, w_in, a_re, a_im, log_dt, b_re, b_im, c_re, c_im,
                 d_skip, w_attn_proj, w_glu_a, w_glu_b, w_out, g_ffn, w_ffn_gate, w_ffn_up,
                 w_ffn_down, w_ple_gate, w_ple_proj):
    B, S, _ = h.shape
    n = rmsnorm(h, g_mix)
    z = n @ w_in
    o0 = ATTN_QK_WIDTH
    q = z[..., 0:o0].reshape(B, S, ATTN_HEADS, HEAD_DIM)
    k = z[..., o0:2 * o0].reshape(B, S, ATTN_HEADS, HEAD_DIM)
    v = z[..., 2 * o0:3 * o0].reshape(B, S, ATTN_HEADS, HEAD_DIM)
    o1 = 3 * o0
    u = z[..., o1:o1 + SSM_WIDTH]
    o2 = o1 + SSM_WIDTH
    gate_attn = jax.nn.sigmoid(z[..., o2:o2 + D_MODEL])
    gate_ssm = jax.nn.sigmoid(z[..., o2 + D_MODEL:o2 + 2 * D_MODEL])

    q = partial_rotary(q, positions)
    k = partial_rotary(k, positions)
    outs = []
    lses = []
    for gi, (window, dilation) in enumerate(ATTN_GROUPS):
        hs = slice(gi * HEADS_PER_GROUP, (gi + 1) * HEADS_PER_GROUP)
        o_g, l_g = dilated_band_attention(q[:, :, hs], k[:, :, hs], v[:, :, hs],
                                          dilation, window // dilation)
        outs.append(o_g)
        lses.append(l_g)
    wts = jax.nn.softmax(jnp.stack(lses, axis=0), axis=0)
    attn = jnp.sum(wts[..., None] * jnp.stack(outs, axis=0), axis=0)
    attn_d = attn.reshape(B, S, ATTN_OUT_WIDTH).astype(h.dtype) @ w_attn_proj

    y = jax.nn.gelu(s5_ssm(u, a_re, a_im, log_dt, b_re, b_im, c_re, c_im, d_skip))
    ssm_d = (y @ w_glu_a) * jax.nn.sigmoid(y @ w_glu_b)

    h = h + (gate_attn * attn_d + gate_ssm * ssm_d) @ w_out

    n2 = rmsnorm(h, g_ffn)
    h = h + (jax.nn.silu(n2 @ w_ffn_gate) * (n2 @ w_ffn_up)) @ w_ffn_down

    h = h + jax.nn.sigmoid(h @ w_ple_gate) * (p_l.astype(h.dtype) @ w_ple_proj)
    return h


def setup_inputs(seed: int = 0) -> dict:
    key = jax.random.key(seed)
    ks = jax.random.split(key, 26)
    f32 = jnp.float32

    def nrm(k, shape, fan_in):
        return jax.random.normal(k, shape, f32) * (fan_in ** -0.5)

    x = jax.random.normal(ks[0], (BATCH, SEQ, D_MODEL), f32)
    p = jax.random.normal(ks[1], (DEPTH, BATCH, SEQ, PLE_DIM), f32)
    positions = jnp.broadcast_to(jnp.arange(SEQ, dtype=jnp.int32)[None, :], (BATCH, SEQ))
    g_mix = 1.0 + 0.05 * jax.random.normal(ks[2], (DEPTH, D_MODEL), f32)
    w_in = nrm(ks[3], (DEPTH, D_MODEL, IN_WIDTH), D_MODEL)
    a_re = -0.5 + 0.01 * jax.random.normal(ks[4], (DEPTH, SSM_GROUPS, SSM_STATE), f32)
    a_im = (jnp.pi * jnp.arange(SSM_STATE, dtype=f32)[None, None, :]
            + 0.01 * jax.random.normal(ks[5], (DEPTH, SSM_GROUPS, SSM_STATE), f32))
    log_dt = jax.random.uniform(ks[6], (DEPTH, SSM_GROUPS), f32,
                                minval=math.log(DT_MIN), maxval=math.log(DT_MAX))
    b_re = nrm(ks[7], (DEPTH, SSM_GROUPS, SSM_STATE, SSM_GROUP), 2 * SSM_GROUP)
    b_im = nrm(ks[8], (DEPTH, SSM_GROUPS, SSM_STATE, SSM_GROUP), 2 * SSM_GROUP)
    c_re = nrm(ks[9], (DEPTH, SSM_GROUPS, SSM_GROUP, SSM_STATE), SSM_STATE)
    c_im = nrm(ks[10], (DEPTH, SSM_GROUPS, SSM_GROUP, SSM_STATE), SSM_STATE)
    d_skip = jax.random.normal(ks[11], (DEPTH, SSM_GROUPS, SSM_GROUP), f32)
    w_attn_proj = nrm(ks[12], (DEPTH, ATTN_OUT_WIDTH, D_MODEL), ATTN_OUT_WIDTH)
    w_glu_a = nrm(ks[13], (DEPTH, SSM_WIDTH, D_MODEL), SSM_WIDTH)
    w_glu_b = nrm(ks[14], (DEPTH, SSM_WIDTH, D_MODEL), SSM_WIDTH)
    w_out = nrm(ks[15], (DEPTH, D_MODEL, D_MODEL), D_MODEL)
    g_ffn = 1.0 + 0.05 * jax.random.normal(ks[16], (DEPTH, D_MODEL), f32)
    w_ffn_gate = nrm(ks[17], (DEPTH, D_MODEL, D_FF), D_MODEL)
    w_ffn_up = nrm(ks[18], (DEPTH, D_MODEL, D_FF), D_MODEL)
    w_ffn_down = nrm(ks[19], (DEPTH, D_FF, D_MODEL), D_FF)
    w_ple_gate = nrm(ks[20], (DEPTH, D_MODEL, D_MODEL), D_MODEL)
    w_ple_proj = nrm(ks[21], (DEPTH, PLE_DIM, D_MODEL), PLE_DIM)
    g_final = 1.0 + 0.05 * jax.random.normal(ks[22], (D_MODEL,), f32)
    return {"x": x, "p": p, "positions": positions, "g_mix": g_mix, "w_in": w_in,
            "a_re": a_re, "a_im": a_im, "log_dt": log_dt, "b_re": b_re, "b_im": b_im,
            "c_re": c_re, "c_im": c_im, "d_skip": d_skip, "w_attn_proj": w_attn_proj,
            "w_glu_a": w_glu_a, "w_glu_b": w_glu_b, "w_out": w_out, "g_ffn": g_ffn,
            "w_ffn_gate": w_ffn_gate, "w_ffn_up": w_ffn_up, "w_ffn_down": w_ffn_down,
            "w_ple_gate": w_ple_gate, "w_ple_proj": w_ple_proj, "g_final": g_final}


def reference(x, p, positions, g_mix, w_in, a_re, a_im, log_dt, b_re, b_im, c_re, c_im, d_skip,
              w_attn_proj, w_glu_a, w_glu_b, w_out, g_ffn, w_ffn_gate, w_ffn_up, w_ffn_down,
              w_ple_gate, w_ple_proj, g_final):
    h = x
    for i in range(DEPTH):
        h = hybrid_layer(h, p[i], positions, g_mix[i], w_in[i], a_re[i], a_im[i], log_dt[i],
                         b_re[i], b_im[i], c_re[i], c_im[i], d_skip[i], w_attn_proj[i],
                         w_glu_a[i], w_glu_b[i], w_out[i], g_ffn[i], w_ffn_gate[i], w_ffn_up[i],
                         w_ffn_down[i], w_ple_gate[i], w_ple_proj[i])
    return rmsnorm(h, g_final)
```

```python
import functools
import math

import jax
import jax.numpy as jnp
from jax import lax
from jax.experimental import pallas as pl
from jax.experimental.pallas import tpu as pltpu

F32 = jnp.float32
BF16 = jnp.bfloat16

D_MODEL = 1024
HEAD_DIM = 128
HEADS_PER_GROUP = 4
GROUP_WIDTH = HEADS_PER_GROUP * HEAD_DIM
ATTN_DILATIONS = (1, 4, 16)
N_GROUPS = len(ATTN_DILATIONS)
QK_WIDTH = N_GROUPS * GROUP_WIDTH
BLOCK = 128
ROPE_THETA = 500000.0
ROPE_DIM = HEAD_DIM // 4
ROPE_HALF = ROPE_DIM // 2
SSM_WIDTH = 512
SSM_GROUP = 16
SSM_GROUPS = SSM_WIDTH // SSM_GROUP
SSM_STATE = 64
N_STATES = SSM_GROUPS * SSM_STATE
D_FF = 2816
PLE_DIM = 256
EPS = 1e-6
MASK_VALUE = -1e30

V7X_VMEM_LIMIT_BYTES = 56 * 1024 * 1024

PROJ_ROWS = 512
ATTN_ROWS = 1024
SSM_CHUNK = 128
SSM_ROWS = 512
MIX_ROWS = 512
FFN_ROWS = 512
FFN_CHUNKS = ((0, 1024), (1024, 1024), (2048, 768))


def _resident(shape):
    return pl.BlockSpec(shape, lambda *_: (0,) * len(shape), pipeline_mode=pl.Buffered(1))


def _params(*semantics):
    return pltpu.CompilerParams(dimension_semantics=semantics,
                                vmem_limit_bytes=V7X_VMEM_LIMIT_BYTES)


def _rmsnorm(x, g):
    return (x * lax.rsqrt(jnp.mean(x * x, axis=-1, keepdims=True) + EPS)) * g


def _bdot(a, b):
    return jnp.dot(a, b, preferred_element_type=F32)


def _proj_kernel(x_ref, pos_ref, g_ref, invf_ref, w_ref,
                 q0, q1, q2, k0, k1, k2, v0, v1, v2, u_ref, ga_ref, gs_ref):
    n = _rmsnorm(x_ref[...], g_ref[...]).astype(BF16)
    rows = n.shape[0]

    ang = pos_ref[...].astype(F32) * invf_ref[...]
    cos = jnp.cos(ang)
    sin = jnp.sin(ang)
    lane = lax.broadcasted_iota(jnp.int32, (rows, HEAD_DIM), 1)
    first_half = lane < ROPE_HALF
    sin_signed = jnp.where(first_half, -sin, sin)

    def rotary(t):
        partner = jnp.where(first_half,
                            pltpu.roll(t, HEAD_DIM - ROPE_HALF, 1),
                            pltpu.roll(t, ROPE_HALF, 1))
        return t * cos + partner * sin_signed

    scale = 1.0 / math.sqrt(HEAD_DIM)
    for g, (q_ref, k_ref, v_ref) in enumerate(((q0, k0, v0), (q1, k1, v1), (q2, k2, v2))):
        c0 = g * GROUP_WIDTH
        zq = _bdot(n, w_ref[:, c0:c0 + GROUP_WIDTH])
        zk = _bdot(n, w_ref[:, QK_WIDTH + c0:QK_WIDTH + c0 + GROUP_WIDTH])
        for h in range(HEADS_PER_GROUP):
            hs = slice(h * HEAD_DIM, (h + 1) * HEAD_DIM)
            q_ref[:, hs] = (rotary(zq[:, hs]) * scale).astype(BF16)
            k_ref[:, hs] = rotary(zk[:, hs]).astype(BF16)
        v_ref[...] = _bdot(n, w_ref[:, 2 * QK_WIDTH + c0:2 * QK_WIDTH + c0 + GROUP_WIDTH]).astype(BF16)
    o1 = 3 * QK_WIDTH
    u_ref[...] = _bdot(n, w_ref[:, o1:o1 + SSM_WIDTH])
    o2 = o1 + SSM_WIDTH
    ga_ref[...] = jax.nn.sigmoid(_bdot(n, w_ref[:, o2:o2 + D_MODEL])).astype(BF16)
    gs_ref[...] = jax.nn.sigmoid(_bdot(n, w_ref[:, o2 + D_MODEL:o2 + 2 * D_MODEL])).astype(BF16)


def _proj(x, pos_col, g_mix, inv_freq_lanes, w_in):
    s = x.shape[0]
    tm = min(PROJ_ROWS, s)
    row = lambda w: pl.BlockSpec((tm, w), lambda i: (i, 0))
    grp = jax.ShapeDtypeStruct((s, GROUP_WIDTH), BF16)
    return pl.pallas_call(
        _proj_kernel,
        grid=(s // tm,),
        in_specs=[row(D_MODEL), row(1), _resident((1, D_MODEL)), _resident((1, HEAD_DIM)),
                  _resident(w_in.shape)],
        out_specs=[row(GROUP_WIDTH)] * 9 + [row(SSM_WIDTH), row(D_MODEL), row(D_MODEL)],
        out_shape=[grp] * 9 + [jax.ShapeDtypeStruct((s, SSM_WIDTH), F32),
                               jax.ShapeDtypeStruct((s, D_MODEL), BF16),
                               jax.ShapeDtypeStruct((s, D_MODEL), BF16)],
        compiler_params=_params("parallel"),
        name="proj",
    )(x, pos_col, g_mix, inv_freq_lanes, w_in)


def _attn_kernel(*refs, nsub, has_prev, is_last):
    q_ref, k_ref, kp_ref, v_ref, vp_ref = refs[:5]
    refs = refs[5:]
    if has_prev:
        o_in, l_in = refs[:2]
        refs = refs[2:]
    if is_last:
        (attn_out,) = refs
    else:
        o_out, l_out = refs

    not_first_block = pl.program_id(1) > 0
    qi = lax.broadcasted_iota(jnp.int32, (BLOCK, 2 * BLOCK), 0)
    kj = lax.broadcasted_iota(jnp.int32, (BLOCK, 2 * BLOCK), 1)
    rel = BLOCK + qi - kj
    band = (rel >= 0) & (rel <= BLOCK)
    band_first = band & ((kj >= BLOCK) | not_first_block)

    for b in range(nsub):
        rows = slice(b * BLOCK, (b + 1) * BLOCK)
        mask = band_first if b == 0 else band
        for h in range(HEADS_PER_GROUP):
            hs = slice(h * HEAD_DIM, (h + 1) * HEAD_DIM)
            q = q_ref[rows, hs]
            if b == 0:
                kw = jnp.concatenate([kp_ref[:, hs], k_ref[rows, hs]], axis=0)
                vw = jnp.concatenate([vp_ref[:, hs], v_ref[rows, hs]], axis=0)
            else:
                win = slice((b - 1) * BLOCK, (b + 1) * BLOCK)
                kw = k_ref[win, hs]
                vw = v_ref[win, hs]
            s = lax.dot_general(q, kw, (((1,), (1,)), ((), ())), preferred_element_type=F32)
            s = jnp.where(mask, s, MASK_VALUE)
            m = jnp.max(s, axis=-1, keepdims=True)
            p = jnp.exp(s - m)
            den = jnp.sum(p, axis=-1, keepdims=True)
            o = _bdot(p.astype(BF16), vw) / den
            lse = m + jnp.log(den)
            if has_prev:
                l_prev = l_in[rows, hs][:, 0:1]
                l_max = jnp.maximum(l_prev, lse)
                l_new = l_max + jnp.log(jnp.exp(l_prev - l_max) + jnp.exp(lse - l_max))
                o = o_in[rows, hs] * jnp.exp(l_prev - l_new) + o * jnp.exp(lse - l_new)
                lse = l_new
            if is_last:
                attn_out[rows, hs] = o.astype(BF16)
            else:
                o_out[rows, hs] = o
                l_out[rows, hs] = jnp.broadcast_to(lse, (BLOCK, HEAD_DIM))


def _attn_group(q, k, v, dilation, running, is_last):
    s = q.shape[0]
    sub_len = s // dilation
    qb = min(ATTN_ROWS, sub_len)
    nsub = qb // BLOCK
    view = lambda t: t.reshape(sub_len, dilation * GROUP_WIDTH)
    cur = pl.BlockSpec((qb, GROUP_WIDTH), lambda r, i: (i, r))
    prev = pl.BlockSpec((BLOCK, GROUP_WIDTH), lambda r, i: (jnp.maximum(i * nsub - 1, 0), r))
    operands = [view(q), view(k), view(k), view(v), view(v)]
    in_specs = [cur, cur, prev, cur, prev]
    aliases = {}
    if running is not None:
        operands += [view(running[0]), view(running[1])]
        in_specs += [cur, cur]
    if is_last:
        out_shape = jax.ShapeDtypeStruct((sub_len, dilation * GROUP_WIDTH), BF16)
        out_specs = cur
    else:
        out_shape = [jax.ShapeDtypeStruct((sub_len, dilation * GROUP_WIDTH), F32)] * 2
        out_specs = [cur, cur]
        if running is not None:
            aliases = {5: 0, 6: 1}
    out = pl.pallas_call(
        functools.partial(_attn_kernel, nsub=nsub, has_prev=running is not None, is_last=is_last),
        grid=(dilation, sub_len // qb),
        in_specs=in_specs,
        out_specs=out_specs,
        out_shape=out_shape,
        input_output_aliases=aliases,
        compiler_params=_params("parallel", "arbitrary"),
        name=f"attn_d{dilation}",
    )(*operands)
    if is_last:
        return out.reshape(s, GROUP_WIDTH)
    return out[0].reshape(s, GROUP_WIDTH), out[1].reshape(s, GROUP_WIDTH)


def _ssm_prep_kernel(lr_ref, li_ref, logdt_ref, bre_ref, bim_ref,
                     bmat_ref, e_re_ref, e_im_ref, einv_re_ref, einv_im_ref, lam_ref):
    lr = lr_ref[...]
    li = li_ref[...]
    dt = jnp.exp(logdt_ref[...])
    mag = jnp.exp(lr * dt)
    bar_re = mag * jnp.cos(li * dt)
    bar_im = mag * jnp.sin(li * dt)
    nr = bar_re - 1.0
    ni = bar_im
    den = lr * lr + li * li
    z_re = (nr * lr + ni * li) / den
    z_im = (ni * lr - nr * li) / den
    bre = bre_ref[...]
    bim = bim_ref[...]
    bmat_ref[:, :N_STATES] = (z_re * bre - z_im * bim).astype(BF16)
    bmat_ref[:, N_STATES:] = (z_re * bim + z_im * bre).astype(BF16)
    lam_ref[0:1, :] = bar_re
    lam_ref[1:2, :] = bar_im
    t = lax.broadcasted_iota(jnp.int32, (SSM_CHUNK, N_STATES), 0).astype(F32)
    grow = jnp.exp(t * (lr * dt))
    theta = t * (li * dt)
    c = jnp.cos(theta)
    s = jnp.sin(theta)
    e_re_ref[...] = grow * c
    e_im_ref[...] = grow * s
    shrink = jnp.exp(-t * (lr * dt))
    einv_re_ref[...] = shrink * c
    einv_im_ref[...] = -(shrink * s)


def _ssm_prep(lr, li, logdt, bre_bd, bim_bd):
    tab = jax.ShapeDtypeStruct((SSM_CHUNK, N_STATES), F32)
    return pl.pallas_call(
        _ssm_prep_kernel,
        out_shape=[jax.ShapeDtypeStruct((SSM_WIDTH, 2 * N_STATES), BF16), tab, tab, tab, tab,
                   jax.ShapeDtypeStruct((2, N_STATES), F32)],
        compiler_params=pltpu.CompilerParams(vmem_limit_bytes=V7X_VMEM_LIMIT_BYTES),
        name="ssm_prep",
    )(lr, li, logdt, bre_bd, bim_bd)


def _gelu_tanh(x):
    return 0.5 * x * (1.0 + jnp.tanh(math.sqrt(2.0 / math.pi) * (x + 0.044715 * (x * x * x))))


def _ssm_kernel(u_ref, bmat_ref, tri_ref, e_re_ref, e_im_ref, einv_re_ref, einv_im_ref, lam_ref,
                cre_ref, cim_ref, d_ref, y_ref, h_re_scr, h_im_scr, carry_scr):
    @pl.when(pl.program_id(0) == 0)
    def _():
        carry_scr[...] = jnp.zeros_like(carry_scr)

    u = u_ref[...]
    bu = _bdot(u.astype(BF16), bmat_ref[...])
    lam_re = lam_ref[0:1, :]
    lam_im = lam_ref[1:2, :]
    carry_re = carry_scr[0:1, :]
    carry_im = carry_scr[1:2, :]
    for c in range(u.shape[0] // SSM_CHUNK):
        rows = slice(c * SSM_CHUNK, (c + 1) * SSM_CHUNK)
        bu_re = bu[rows, :N_STATES]
        bu_im = bu[rows, N_STATES:]
        einv_re = einv_re_ref[...]
        einv_im = einv_im_ref[...]
        x_re = (bu_re * einv_re - bu_im * einv_im).astype(BF16)
        x_im = (bu_re * einv_im + bu_im * einv_re).astype(BF16)
        a_re = _bdot(tri_ref[...], x_re) + carry_re
        a_im = _bdot(tri_ref[...], x_im) + carry_im
        e_re = e_re_ref[...]
        e_im = e_im_ref[...]
        h_re = e_re * a_re - e_im * a_im
        h_im = e_re * a_im + e_im * a_re
        h_re_scr[rows, :] = h_re.astype(BF16)
        h_im_scr[rows, :] = h_im.astype(BF16)
        last_re = h_re[SSM_CHUNK - 1:SSM_CHUNK, :]
        last_im = h_im[SSM_CHUNK - 1:SSM_CHUNK, :]
        carry_re = lam_re * last_re - lam_im * last_im
        carry_im = lam_re * last_im + lam_im * last_re
    carry_scr[0:1, :] = carry_re
    carry_scr[1:2, :] = carry_im
    y = _bdot(h_re_scr[...], cre_ref[...]) - _bdot(h_im_scr[...], cim_ref[...]) + d_ref[...] * u
    y_ref[...] = _gelu_tanh(y).astype(BF16)


def _ssm(u, bmat, tri, e_re, e_im, einv_re, einv_im, lam, cre, cim, d_row):
    s = u.shape[0]
    tm = min(SSM_ROWS, s)
    row = pl.BlockSpec((tm, SSM_WIDTH), lambda i: (i, 0))
    consts = (bmat, tri, e_re, e_im, einv_re, einv_im, lam, cre, cim, d_row)
    return pl.pallas_call(
        _ssm_kernel,
        grid=(s // tm,),
        in_specs=[row] + [_resident(c.shape) for c in consts],
        out_specs=row,
        out_shape=jax.ShapeDtypeStruct((s, SSM_WIDTH), BF16),
        scratch_shapes=[pltpu.VMEM((tm, N_STATES), BF16), pltpu.VMEM((tm, N_STATES), BF16),
                        pltpu.VMEM((2, N_STATES), F32)],
        compiler_params=_params("arbitrary"),
        name="ssm",
    )(u, *consts)


def _mix_kernel(x_ref, attn_ref, y_ref, ga_ref, gs_ref, wap_ref, wa_ref, wb_ref, wout_ref, h_ref):
    attn_d = _bdot(attn_ref[...], wap_ref[...])
    y = y_ref[...]
    ssm_d = _bdot(y, wa_ref[...]) * jax.nn.sigmoid(_bdot(y, wb_ref[...]))
    mix = ga_ref[...].astype(F32) * attn_d + gs_ref[...].astype(F32) * ssm_d
    h_ref[...] = x_ref[...] + _bdot(mix.astype(BF16), wout_ref[...])


def _mix(x, attn, y, ga, gs, wap, wa, wb, wout):
    s = x.shape[0]
    tm = min(MIX_ROWS, s)
    row = lambda w: pl.BlockSpec((tm, w), lambda i: (i, 0))
    weights = (wap, wa, wb, wout)
    return pl.pallas_call(
        _mix_kernel,
        grid=(s // tm,),
        in_specs=[row(D_MODEL), row(GROUP_WIDTH), row(SSM_WIDTH), row(D_MODEL), row(D_MODEL)]
        + [_resident(w.shape) for w in weights],
        out_specs=row(D_MODEL),
        out_shape=jax.ShapeDtypeStruct((s, D_MODEL), F32),
        compiler_params=_params("parallel"),
        name="mix",
    )(x, attn, y, ga, gs, *weights)


def _ffn_kernel(h_ref, p_ref, gffn_ref, wg_ref, wu_ref, wd_ref, wpg_ref, wpp_ref, gfin_ref,
                out_ref, acc_scr):
    h = h_ref[...]
    n2 = _rmsnorm(h, gffn_ref[...]).astype(BF16)
    for idx, (c0, width) in enumerate(FFN_CHUNKS):
        gate = _bdot(n2, wg_ref[:, c0:c0 + width])
        up = _bdot(n2, wu_ref[:, c0:c0 + width])
        act = (gate * jax.nn.sigmoid(gate) * up).astype(BF16)
        part = _bdot(act, wd_ref[c0:c0 + width, :])
        if idx == 0:
            acc_scr[...] = h + part
        else:
            acc_scr[...] += part
    h2 = acc_scr[...]
    ple = jax.nn.sigmoid(_bdot(h2.astype(BF16), wpg_ref[...])) * _bdot(p_ref[...].astype(BF16), wpp_ref[...])
    out_ref[...] = _rmsnorm(h2 + ple, gfin_ref[...])


def _ffn(h, p, g_ffn, wg, wu, wd, wpg, wpp, g_final):
    s = h.shape[0]
    tm = min(FFN_ROWS, s)
    row = lambda w: pl.BlockSpec((tm, w), lambda i: (i, 0))
    consts = (g_ffn, wg, wu, wd, wpg, wpp, g_final)
    return pl.pallas_call(
        _ffn_kernel,
        grid=(s // tm,),
        in_specs=[row(D_MODEL), row(PLE_DIM)] + [_resident(c.shape) for c in consts],
        out_specs=row(D_MODEL),
        out_shape=jax.ShapeDtypeStruct((s, D_MODEL), F32),
        scratch_shapes=[pltpu.VMEM((tm, D_MODEL), F32)],
        compiler_params=_params("parallel"),
        name="ffn",
    )(h, p, *consts)


def _block_diag(t):
    g, a, b = t.shape
    eye = jnp.eye(g, dtype=t.dtype)
    return (t[:, :, None, :] * eye[:, None, :, None]).reshape(g * a, g * b)


def _layer(x, p, positions, g_mix, w_in, a_re, a_im, log_dt, b_re, b_im, c_re, c_im, d_skip,
           w_attn_proj, w_glu_a, w_glu_b, w_out, g_ffn, w_ffn_gate, w_ffn_up, w_ffn_down,
           w_ple_gate, w_ple_proj, g_final):
    s = x.shape[0]
    bf = lambda w: w.astype(BF16)
    row = lambda v: v.reshape(1, -1).astype(F32)

    inv_freq = ROPE_THETA ** (-jnp.arange(ROPE_HALF, dtype=F32) * 2.0 / ROPE_DIM)
    inv_freq_lanes = jnp.concatenate(
        [inv_freq, inv_freq, jnp.zeros((HEAD_DIM - ROPE_DIM,), F32)]).reshape(1, HEAD_DIM)
    outs = _proj(x, positions.reshape(s, 1), row(g_mix), inv_freq_lanes, bf(w_in))
    qs, ks, vs = outs[0:3], outs[3:6], outs[6:9]
    u, gate_attn, gate_ssm = outs[9:]

    running = None
    for g, dilation in enumerate(ATTN_DILATIONS):
        running = _attn_group(qs[g], ks[g], vs[g], dilation, running, is_last=g == N_GROUPS - 1)
    attn = running

    bmat, e_re, e_im, einv_re, einv_im, lam = _ssm_prep(
        row(a_re), row(a_im), row(jnp.repeat(log_dt, SSM_STATE)),
        _block_diag(jnp.swapaxes(b_re, 1, 2)), _block_diag(jnp.swapaxes(b_im, 1, 2)))
    tri = jnp.tril(jnp.ones((SSM_CHUNK, SSM_CHUNK), F32)).astype(BF16)
    y = _ssm(u, bmat, tri, e_re, e_im, einv_re, einv_im, lam,
             bf(_block_diag(jnp.swapaxes(c_re, 1, 2))), bf(_block_diag(jnp.swapaxes(c_im, 1, 2))),
             row(d_skip))

    h1 = _mix(x, attn, y, gate_attn, gate_ssm, bf(w_attn_proj), bf(w_glu_a), bf(w_glu_b), bf(w_out))
    return _ffn(h1, p, row(g_ffn), bf(w_ffn_gate), bf(w_ffn_up), bf(w_ffn_down),
                bf(w_ple_gate), bf(w_ple_proj), row(g_final))


def kernel(x, p, positions, g_mix, w_in, a_re, a_im, log_dt, b_re, b_im, c_re, c_im, d_skip,
           w_attn_proj, w_glu_a, w_glu_b, w_out, g_ffn, w_ffn_gate, w_ffn_up, w_ffn_down,
           w_ple_gate, w_ple_proj, g_final):
    batch, depth = x.shape[0], p.shape[0]
    assert batch == 1 and depth == 1, "kernel supports the stated BATCH=1, DEPTH=1 problem"
    out = _layer(x[0], p[0, 0], positions[0], g_mix[0], w_in[0], a_re[0], a_im[0], log_dt[0],
                 b_re[0], b_im[0], c_re[0], c_im[0], d_skip[0], w_attn_proj[0], w_glu_a[0],
                 w_glu_b[0], w_out[0], g_ffn[0], w_ffn_gate[0], w_ffn_up[0], w_ffn_down[0],
                 w_ple_gate[0], w_ple_proj[0], g_final)
    return out[None]
```

```python
import functools
import math

import jax
import jax.numpy as jnp
from jax import lax
from jax.experimental import pallas as pl
from jax.experimental.pallas import tpu as pltpu

F32 = jnp.float32
BF16 = jnp.bfloat16

D_MODEL = 1024
HEAD_DIM = 128
HEADS_PER_GROUP = 4
GROUP_WIDTH = HEADS_PER_GROUP * HEAD_DIM
ATTN_DILATIONS = (1, 4, 16)
N_GROUPS = len(ATTN_DILATIONS)
QK_WIDTH = N_GROUPS * GROUP_WIDTH
BLOCK = 128
ROPE_THETA = 500000.0
ROPE_DIM = HEAD_DIM // 4
ROPE_HALF = ROPE_DIM // 2
SSM_WIDTH = 512
SSM_GROUP = 16
SSM_GROUPS = SSM_WIDTH // SSM_GROUP
SSM_STATE = 64
N_STATES = SSM_GROUPS * SSM_STATE
D_FF = 2816
PLE_DIM = 256
EPS = 1e-6
MASK_VALUE = -1e30

V7X_VMEM_LIMIT_BYTES = 56 * 1024 * 1024

PROJ_ROWS = 512
ATTN_TOKENS = {1: 1024, 4: 2048, 16: 2048}
LSE_LANES = HEAD_DIM // HEADS_PER_GROUP
SSM_CHUNK = 128
SSM_ROWS = 512
MIX_ROWS = 512
FFN_ROWS = 512
FFN_CHUNKS = ((0, 1024), (1024, 1024), (2048, 768))


def _resident(shape):
    return pl.BlockSpec(shape, lambda *_: (0,) * len(shape), pipeline_mode=pl.Buffered(1))


def _params(*semantics):
    return pltpu.CompilerParams(dimension_semantics=semantics,
                                vmem_limit_bytes=V7X_VMEM_LIMIT_BYTES)


def _rmsnorm(x, g):
    return (x * lax.rsqrt(jnp.mean(x * x, axis=-1, keepdims=True) + EPS)) * g


def _bdot(a, b):
    return jnp.dot(a, b, preferred_element_type=F32)


def _proj_kernel(x_ref, pos1_ref, pos4_ref, pos16_ref, g_ref, invf_ref, perm4_ref, perm16_ref, w_ref,
                 q0, q1, q2, k0, k1, k2, v0, v1, v2, u_ref, ga_ref, gs_ref):
    n = _rmsnorm(x_ref[...], g_ref[...]).astype(BF16)
    rows = n.shape[0]
    lane = lax.broadcasted_iota(jnp.int32, (rows, HEAD_DIM), 1)
    first_half = lane < ROPE_HALF
    scale = 1.0 / math.sqrt(HEAD_DIM)

    groups = ((1, pos1_ref, None, q0, k0, v0), (4, pos4_ref, perm4_ref, q1, k1, v1),
              (16, pos16_ref, perm16_ref, q2, k2, v2))
    for g, (d, pos_ref, perm_ref, q_ref, k_ref, v_ref) in enumerate(groups):
        ng = n if perm_ref is None else _bdot(perm_ref[...], n).astype(BF16)

        ang = pos_ref[...].astype(F32) * invf_ref[...]
        cos = jnp.cos(ang)
        sin = jnp.sin(ang)
        sin_signed = jnp.where(first_half, -sin, sin)

        def rotary(t):
            partner = jnp.where(first_half,
                                pltpu.roll(t, HEAD_DIM - ROPE_HALF, 1),
                                pltpu.roll(t, ROPE_HALF, 1))
            return t * cos + partner * sin_signed

        c0 = g * GROUP_WIDTH
        zq = _bdot(ng, w_ref[:, c0:c0 + GROUP_WIDTH])
        zk = _bdot(ng, w_ref[:, QK_WIDTH + c0:QK_WIDTH + c0 + GROUP_WIDTH])
        zv = _bdot(ng, w_ref[:, 2 * QK_WIDTH + c0:2 * QK_WIDTH + c0 + GROUP_WIDTH])
        for h in range(HEADS_PER_GROUP):
            hs = slice(h * HEAD_DIM, (h + 1) * HEAD_DIM)
            q_ref[:, :, hs] = (rotary(zq[:, hs]) * scale).astype(BF16).reshape(d, rows // d, HEAD_DIM)
            k_ref[:, :, hs] = rotary(zk[:, hs]).astype(BF16).reshape(d, rows // d, HEAD_DIM)
        v_ref[...] = zv.astype(BF16).reshape(d, rows // d, GROUP_WIDTH)
    o1 = 3 * QK_WIDTH
    u_ref[...] = _bdot(n, w_ref[:, o1:o1 + SSM_WIDTH])
    o2 = o1 + SSM_WIDTH
    ga_ref[...] = jax.nn.sigmoid(_bdot(n, w_ref[:, o2:o2 + D_MODEL])).astype(BF16)
    gs_ref[...] = jax.nn.sigmoid(_bdot(n, w_ref[:, o2 + D_MODEL:o2 + 2 * D_MODEL])).astype(BF16)


def _residue_major_perm(rows, d):
    j = jnp.arange(rows)
    src = (j % (rows // d)) * d + j // (rows // d)
    return (src[:, None] == jnp.arange(rows)[None, :]).astype(BF16)


def _proj(x, positions, g_mix, inv_freq_lanes, w_in):
    s = x.shape[0]
    tm = min(PROJ_ROWS, s)
    row = lambda w: pl.BlockSpec((tm, w), lambda i: (i, 0))
    pos_cols = [positions.reshape(s // tm, tm // d, d).swapaxes(1, 2).reshape(s, 1) for d in ATTN_DILATIONS]
    perms = [_residue_major_perm(tm, d) for d in ATTN_DILATIONS[1:]]
    grp_specs, grp_shapes = [], []
    for d in ATTN_DILATIONS:
        grp_specs.append(pl.BlockSpec((d, tm // d, GROUP_WIDTH), lambda i: (0, i, 0)))
        grp_shapes.append(jax.ShapeDtypeStruct((d, s // d, GROUP_WIDTH), BF16))
    return pl.pallas_call(
        _proj_kernel,
        grid=(s // tm,),
        in_specs=[row(D_MODEL), row(1), row(1), row(1), _resident((1, D_MODEL)), _resident((1, HEAD_DIM)),
                  _resident((tm, tm)), _resident((tm, tm)), _resident(w_in.shape)],
        out_specs=grp_specs * 3 + [row(SSM_WIDTH), row(D_MODEL), row(D_MODEL)],
        out_shape=grp_shapes * 3 + [jax.ShapeDtypeStruct((s, SSM_WIDTH), F32),
                                    jax.ShapeDtypeStruct((s, D_MODEL), BF16),
                                    jax.ShapeDtypeStruct((s, D_MODEL), BF16)],
        compiler_params=_params("parallel"),
        name="proj",
    )(x, *pos_cols, g_mix, inv_freq_lanes, *perms, w_in)


def _attn_kernel(q_ref, k_ref, kp_ref, v_ref, vp_ref, o_ref, l_ref, o_scr, l_scr, *, d, nsub):
    not_first_block = pl.program_id(0) > 0
    qi = lax.broadcasted_iota(jnp.int32, (BLOCK, 2 * BLOCK), 0)
    kj = lax.broadcasted_iota(jnp.int32, (BLOCK, 2 * BLOCK), 1)
    rel = BLOCK + qi - kj
    band = (rel >= 0) & (rel <= BLOCK)
    band_first = band & ((kj >= BLOCK) | not_first_block)
    lane = lax.broadcasted_iota(jnp.int32, (BLOCK, HEAD_DIM), 1)

    def residue(r):
        for b in range(nsub):
            rows = slice(b * BLOCK, (b + 1) * BLOCK)
            mask = band_first if b == 0 else band
            token_rows = pl.ds(b * BLOCK * d + r, BLOCK, stride=d) if d > 1 else rows
            lse_tile = None
            for h in range(HEADS_PER_GROUP):
                hs = slice(h * HEAD_DIM, (h + 1) * HEAD_DIM)
                q = q_ref[r, rows, hs]
                if b == 0:
                    kw = jnp.concatenate([kp_ref[r, :, hs], k_ref[r, rows, hs]], axis=0)
                    vw = jnp.concatenate([vp_ref[r, :, hs], v_ref[r, rows, hs]], axis=0)
                else:
                    win = slice((b - 1) * BLOCK, (b + 1) * BLOCK)
                    kw = k_ref[r, win, hs]
                    vw = v_ref[r, win, hs]
                s = lax.dot_general(q, kw, (((1,), (1,)), ((), ())), preferred_element_type=F32)
                s = jnp.where(mask, s, MASK_VALUE)
                m = jnp.max(s, axis=-1, keepdims=True)
                p = jnp.exp(s - m)
                den = jnp.sum(p, axis=-1, keepdims=True)
                o_scr[h, token_rows, :] = _bdot(p.astype(BF16), vw) / den
                lse = jnp.broadcast_to(m + jnp.log(den), (BLOCK, HEAD_DIM))
                lse_tile = lse if h == 0 else jnp.where(lane >= h * LSE_LANES, lse, lse_tile)
            l_scr[token_rows, :] = lse_tile

    if d == 1:
        residue(0)
    else:
        lax.fori_loop(0, d, lambda r, c: (residue(r), c)[1], 0)

    for h in range(HEADS_PER_GROUP):
        o_ref[:, h * HEAD_DIM:(h + 1) * HEAD_DIM] = o_scr[h].astype(BF16)
    l_ref[...] = l_scr[...]


def _attn_group(q, k, v, dilation):
    sub_len = q.shape[1]
    s = sub_len * dilation
    step_tokens = min(ATTN_TOKENS[dilation], s)
    qb = step_tokens // dilation
    nsub = qb // BLOCK
    cur = pl.BlockSpec((dilation, qb, GROUP_WIDTH), lambda i: (0, i, 0))
    prev = pl.BlockSpec((dilation, BLOCK, GROUP_WIDTH), lambda i: (0, jnp.maximum(i * nsub - 1, 0), 0))
    return pl.pallas_call(
        functools.partial(_attn_kernel, d=dilation, nsub=nsub),
        grid=(s // step_tokens,),
        in_specs=[cur, cur, prev, cur, prev],
        out_specs=[pl.BlockSpec((step_tokens, GROUP_WIDTH), lambda i: (i, 0)),
                   pl.BlockSpec((step_tokens, HEAD_DIM), lambda i: (i, 0))],
        out_shape=[jax.ShapeDtypeStruct((s, GROUP_WIDTH), BF16), jax.ShapeDtypeStruct((s, HEAD_DIM), F32)],
        scratch_shapes=[pltpu.VMEM((HEADS_PER_GROUP, step_tokens, HEAD_DIM), F32),
                        pltpu.VMEM((step_tokens, HEAD_DIM), F32)],
        compiler_params=_params("arbitrary"),
        name=f"attn_d{dilation}",
    )(q, k, k, v, v)


def _ssm_prep_kernel(lr_ref, li_ref, logdt_ref, bre_ref, bim_ref,
                     bmat_ref, e_re_ref, e_im_ref, einv_re_ref, einv_im_ref, lam_ref):
    lr = lr_ref[...]
    li = li_ref[...]
    dt = jnp.exp(logdt_ref[...])
    mag = jnp.exp(lr * dt)
    bar_re = mag * jnp.cos(li * dt)
    bar_im = mag * jnp.sin(li * dt)
    nr = bar_re - 1.0
    ni = bar_im
    den = lr * lr + li * li
    z_re = (nr * lr + ni * li) / den
    z_im = (ni * lr - nr * li) / den
    bre = bre_ref[...]
    bim = bim_ref[...]
    bmat_ref[:, :N_STATES] = (z_re * bre - z_im * bim).astype(BF16)
    bmat_ref[:, N_STATES:] = (z_re * bim + z_im * bre).astype(BF16)
    lam_ref[0:1, :] = bar_re
    lam_ref[1:2, :] = bar_im
    t = lax.broadcasted_iota(jnp.int32, (SSM_CHUNK, N_STATES), 0).astype(F32)
    grow = jnp.exp(t * (lr * dt))
    theta = t * (li * dt)
    c = jnp.cos(theta)
    s = jnp.sin(theta)
    e_re_ref[...] = grow * c
    e_im_ref[...] = grow * s
    shrink = jnp.exp(-t * (lr * dt))
    einv_re_ref[...] = shrink * c
    einv_im_ref[...] = -(shrink * s)


def _ssm_prep(lr, li, logdt, bre_bd, bim_bd):
    tab = jax.ShapeDtypeStruct((SSM_CHUNK, N_STATES), F32)
    return pl.pallas_call(
        _ssm_prep_kernel,
        out_shape=[jax.ShapeDtypeStruct((SSM_WIDTH, 2 * N_STATES), BF16), tab, tab, tab, tab,
                   jax.ShapeDtypeStruct((2, N_STATES), F32)],
        compiler_params=pltpu.CompilerParams(vmem_limit_bytes=V7X_VMEM_LIMIT_BYTES),
        name="ssm_prep",
    )(lr, li, logdt, bre_bd, bim_bd)


def _gelu_tanh(x):
    return 0.5 * x * (1.0 + jnp.tanh(math.sqrt(2.0 / math.pi) * (x + 0.044715 * (x * x * x))))


def _ssm_kernel(u_ref, bmat_ref, tri_ref, e_re_ref, e_im_ref, einv_re_ref, einv_im_ref, lam_ref,
                cre_ref, cim_ref, d_ref, y_ref, h_re_scr, h_im_scr, carry_scr):
    @pl.when(pl.program_id(0) == 0)
    def _():
        carry_scr[...] = jnp.zeros_like(carry_scr)

    u = u_ref[...]
    bu = _bdot(u.astype(BF16), bmat_ref[...])
    lam_re = lam_ref[0:1, :]
    lam_im = lam_ref[1:2, :]
    carry_re = carry_scr[0:1, :]
    carry_im = carry_scr[1:2, :]
    for c in range(u.shape[0] // SSM_CHUNK):
        rows = slice(c * SSM_CHUNK, (c + 1) * SSM_CHUNK)
        bu_re = bu[rows, :N_STATES]
        bu_im = bu[rows, N_STATES:]
        einv_re = einv_re_ref[...]
        einv_im = einv_im_ref[...]
        x_re = (bu_re * einv_re - bu_im * einv_im).astype(BF16)
        x_im = (bu_re * einv_im + bu_im * einv_re).astype(BF16)
        a_re = _bdot(tri_ref[...], x_re) + carry_re
        a_im = _bdot(tri_ref[...], x_im) + carry_im
        e_re = e_re_ref[...]
        e_im = e_im_ref[...]
        h_re = e_re * a_re - e_im * a_im
        h_im = e_re * a_im + e_im * a_re
        h_re_scr[rows, :] = h_re.astype(BF16)
        h_im_scr[rows, :] = h_im.astype(BF16)
        last_re = h_re[SSM_CHUNK - 1:SSM_CHUNK, :]
        last_im = h_im[SSM_CHUNK - 1:SSM_CHUNK, :]
        carry_re = lam_re * last_re - lam_im * last_im
        carry_im = lam_re * last_im + lam_im * last_re
    carry_scr[0:1, :] = carry_re
    carry_scr[1:2, :] = carry_im
    y = _bdot(h_re_scr[...], cre_ref[...]) - _bdot(h_im_scr[...], cim_ref[...]) + d_ref[...] * u
    y_ref[...] = _gelu_tanh(y).astype(BF16)


def _ssm(u, bmat, tri, e_re, e_im, einv_re, einv_im, lam, cre, cim, d_row):
    s = u.shape[0]
    tm = min(SSM_ROWS, s)
    row = pl.BlockSpec((tm, SSM_WIDTH), lambda i: (i, 0))
    consts = (bmat, tri, e_re, e_im, einv_re, einv_im, lam, cre, cim, d_row)
    return pl.pallas_call(
        _ssm_kernel,
        grid=(s // tm,),
        in_specs=[row] + [_resident(c.shape) for c in consts],
        out_specs=row,
        out_shape=jax.ShapeDtypeStruct((s, SSM_WIDTH), BF16),
        scratch_shapes=[pltpu.VMEM((tm, N_STATES), BF16), pltpu.VMEM((tm, N_STATES), BF16),
                        pltpu.VMEM((2, N_STATES), F32)],
        compiler_params=_params("arbitrary"),
        name="ssm",
    )(u, *consts)


def _mix_kernel(x_ref, o0_ref, o1_ref, o2_ref, l0_ref, l1_ref, l2_ref, y_ref, ga_ref, gs_ref,
                wap_ref, wa_ref, wb_ref, wout_ref, h_ref):
    ls = (l0_ref[...], l1_ref[...], l2_ref[...])
    l_max = jnp.maximum(jnp.maximum(ls[0], ls[1]), ls[2])
    es = [jnp.exp(l - l_max) for l in ls]
    inv = 1.0 / (es[0] + es[1] + es[2])
    heads = []
    for h in range(HEADS_PER_GROUP):
        hs = slice(h * HEAD_DIM, (h + 1) * HEAD_DIM)
        col = slice(h * LSE_LANES, h * LSE_LANES + 1)
        heads.append(sum((e * inv)[:, col] * o_ref[:, hs].astype(F32)
                         for e, o_ref in zip(es, (o0_ref, o1_ref, o2_ref))))
    attn = jnp.concatenate(heads, axis=1).astype(BF16)
    attn_d = _bdot(attn, wap_ref[...])
    y = y_ref[...]
    ssm_d = _bdot(y, wa_ref[...]) * jax.nn.sigmoid(_bdot(y, wb_ref[...]))
    mix = ga_ref[...].astype(F32) * attn_d + gs_ref[...].astype(F32) * ssm_d
    h_ref[...] = x_ref[...] + _bdot(mix.astype(BF16), wout_ref[...])


def _mix(x, os, ls, y, ga, gs, wap, wa, wb, wout):
    s = x.shape[0]
    tm = min(MIX_ROWS, s)
    row = lambda w: pl.BlockSpec((tm, w), lambda i: (i, 0))
    weights = (wap, wa, wb, wout)
    return pl.pallas_call(
        _mix_kernel,
        grid=(s // tm,),
        in_specs=[row(D_MODEL)] + [row(GROUP_WIDTH)] * 3 + [row(HEAD_DIM)] * 3
        + [row(SSM_WIDTH), row(D_MODEL), row(D_MODEL)] + [_resident(w.shape) for w in weights],
        out_specs=row(D_MODEL),
        out_shape=jax.ShapeDtypeStruct((s, D_MODEL), F32),
        compiler_params=_params("parallel"),
        name="mix",
    )(x, *os, *ls, y, ga, gs, *weights)


def _ffn_kernel(h_ref, p_ref, gffn_ref, wg_ref, wu_ref, wd_ref, wpg_ref, wpp_ref, gfin_ref,
                out_ref, acc_scr):
    h = h_ref[...]
    n2 = _rmsnorm(h, gffn_ref[...]).astype(BF16)
    for idx, (c0, width) in enumerate(FFN_CHUNKS):
        gate = _bdot(n2, wg_ref[:, c0:c0 + width])
        up = _bdot(n2, wu_ref[:, c0:c0 + width])
        act = (gate * jax.nn.sigmoid(gate) * up).astype(BF16)
        part = _bdot(act, wd_ref[c0:c0 + width, :])
        if idx == 0:
            acc_scr[...] = h + part
        else:
            acc_scr[...] += part
    h2 = acc_scr[...]
    ple = jax.nn.sigmoid(_bdot(h2.astype(BF16), wpg_ref[...])) * _bdot(p_ref[...].astype(BF16), wpp_ref[...])
    out_ref[...] = _rmsnorm(h2 + ple, gfin_ref[...])


def _ffn(h, p, g_ffn, wg, wu, wd, wpg, wpp, g_final):
    s = h.shape[0]
    tm = min(FFN_ROWS, s)
    row = lambda w: pl.BlockSpec((tm, w), lambda i: (i, 0))
    consts = (g_ffn, wg, wu, wd, wpg, wpp, g_final)
    return pl.pallas_call(
        _ffn_kernel,
        grid=(s // tm,),
        in_specs=[row(D_MODEL), row(PLE_DIM)] + [_resident(c.shape) for c in consts],
        out_specs=row(D_MODEL),
        out_shape=jax.ShapeDtypeStruct((s, D_MODEL), F32),
        scratch_shapes=[pltpu.VMEM((tm, D_MODEL), F32)],
        compiler_params=_params("parallel"),
        name="ffn",
    )(h, p, *consts)


def _block_diag(t):
    g, a, b = t.shape
    eye = jnp.eye(g, dtype=t.dtype)
    return (t[:, :, None, :] * eye[:, None, :, None]).reshape(g * a, g * b)


def _layer(x, p, positions, g_mix, w_in, a_re, a_im, log_dt, b_re, b_im, c_re, c_im, d_skip,
           w_attn_proj, w_glu_a, w_glu_b, w_out, g_ffn, w_ffn_gate, w_ffn_up, w_ffn_down,
           w_ple_gate, w_ple_proj, g_final):
    bf = lambda w: w.astype(BF16)
    row = lambda v: v.reshape(1, -1).astype(F32)

    inv_freq = ROPE_THETA ** (-jnp.arange(ROPE_HALF, dtype=F32) * 2.0 / ROPE_DIM)
    inv_freq_lanes = jnp.concatenate(
        [inv_freq, inv_freq, jnp.zeros((HEAD_DIM - ROPE_DIM,), F32)]).reshape(1, HEAD_DIM)
    outs = _proj(x, positions, row(g_mix), inv_freq_lanes, bf(w_in))
    qs, ks, vs = outs[0:3], outs[3:6], outs[6:9]
    u, gate_attn, gate_ssm = outs[9:]

    attn_os, attn_ls = zip(*[_attn_group(qs[g], ks[g], vs[g], d) for g, d in enumerate(ATTN_DILATIONS)])

    bmat, e_re, e_im, einv_re, einv_im, lam = _ssm_prep(
        row(a_re), row(a_im), row(jnp.repeat(log_dt, SSM_STATE)),
        _block_diag(jnp.swapaxes(b_re, 1, 2)), _block_diag(jnp.swapaxes(b_im, 1, 2)))
    tri = jnp.tril(jnp.ones((SSM_CHUNK, SSM_CHUNK), F32)).astype(BF16)
    y = _ssm(u, bmat, tri, e_re, e_im, einv_re, einv_im, lam,
             bf(_block_diag(jnp.swapaxes(c_re, 1, 2))), bf(_block_diag(jnp.swapaxes(c_im, 1, 2))),
             row(d_skip))

    h1 = _mix(x, attn_os, attn_ls, y, gate_attn, gate_ssm, bf(w_attn_proj), bf(w_glu_a), bf(w_glu_b), bf(w_out))
    return _ffn(h1, p, row(g_ffn), bf(w_ffn_gate), bf(w_ffn_up), bf(w_ffn_down),
                bf(w_ple_gate), bf(w_ple_proj), row(g_final))


def kernel(x, p, positions, g_mix, w_in, a_re, a_im, log_dt, b_re, b_im, c_re, c_im, d_skip,
           w_attn_proj, w_glu_a, w_glu_b, w_out, g_ffn, w_ffn_gate, w_ffn_up, w_ffn_down,
           w_ple_gate, w_ple_proj, g_final):
    batch, depth = x.shape[0], p.shape[0]
    assert batch == 1 and depth == 1, "kernel supports the stated BATCH=1, DEPTH=1 problem"
    out = _layer(x[0], p[0, 0], positions[0], g_mix[0], w_in[0], a_re[0], a_im[0], log_dt[0],
                 b_re[0], b_im[0], c_re[0], c_im[0], d_skip[0], w_attn_proj[0], w_glu_a[0],
                 w_glu_b[0], w_out[0], g_ffn[0], w_ffn_gate[0], w_ffn_up[0], w_ffn_down[0],
                 w_ple_gate[0], w_ple_proj[0], g_final)
    return out[None]
```

```python
import functools
import math

import jax
import jax.numpy as jnp
from jax import lax
from jax.experimental import pallas as pl
from jax.experimental.pallas import tpu as pltpu

F32 = jnp.float32
BF16 = jnp.bfloat16

D_MODEL = 1024
HEAD_DIM = 128
HEADS_PER_GROUP = 4
GROUP_WIDTH = HEADS_PER_GROUP * HEAD_DIM
ATTN_DILATIONS = (1, 4, 16)
N_GROUPS = len(ATTN_DILATIONS)
QK_WIDTH = N_GROUPS * GROUP_WIDTH
BLOCK = 128
ROPE_THETA = 500000.0
ROPE_DIM = HEAD_DIM // 4
ROPE_HALF = ROPE_DIM // 2
SSM_WIDTH = 512
SSM_GROUP = 16
SSM_GROUPS = SSM_WIDTH // SSM_GROUP
SSM_STATE = 64
N_STATES = SSM_GROUPS * SSM_STATE
SSM_HALVES = 2
HALF_GROUPS = SSM_GROUPS // SSM_HALVES
HALF_WIDTH = SSM_WIDTH // SSM_HALVES
HALF_STATES = N_STATES // SSM_HALVES
D_FF = 2816
PLE_DIM = 256
EPS = 1e-6
MASK_VALUE = -1e30

V7X_VMEM_LIMIT_BYTES = 56 * 1024 * 1024

PROJ_ROWS = 512
ATTN_TOKENS = {1: 1024, 4: 2048, 16: 2048}
ATTN_TILES_PER_BODY = 32
LSE_LANES = HEAD_DIM // HEADS_PER_GROUP
SSM_CHUNK = 128
SSM_ROWS = 512
MIX_ROWS = 512
FFN_ROWS = 512
FFN_CHUNKS = ((0, 1024), (1024, 1024), (2048, 768))


def _resident(shape):
    return pl.BlockSpec(shape, lambda *_: (0,) * len(shape), pipeline_mode=pl.Buffered(1))


def _params(*semantics):
    return pltpu.CompilerParams(dimension_semantics=semantics,
                                vmem_limit_bytes=V7X_VMEM_LIMIT_BYTES)


def _rmsnorm(x, g):
    return (x * lax.rsqrt(jnp.mean(x * x, axis=-1, keepdims=True) + EPS)) * g


def _bdot(a, b):
    return jnp.dot(a, b, preferred_element_type=F32)


def _proj_kernel(x_ref, pos_ref, g_ref, invf_ref, w_ref,
                 q0, q1, q2, k0, k1, k2, v0, v1, v2, u_ref, ga_ref, gs_ref,
                 n_scr, nperm_scr, cos_scr, sin_scr):
    xn = _rmsnorm(x_ref[...], g_ref[...])
    n = xn.astype(BF16)
    rows = xn.shape[0]
    for c in range(D_MODEL // HEAD_DIM):
        n_scr[c] = xn[:, c * HEAD_DIM:(c + 1) * HEAD_DIM]

    ang = invf_ref[...] * pos_ref[...].astype(F32)
    cos_t = jnp.cos(ang)
    sin_t = jnp.sin(ang)
    rest = (HEAD_DIM - ROPE_DIM, rows)
    cos_scr[...] = jnp.concatenate([cos_t, cos_t, jnp.ones(rest, F32)], axis=0).T
    sin_scr[...] = jnp.concatenate([-sin_t, sin_t, jnp.zeros(rest, F32)], axis=0).T

    lane = lax.broadcasted_iota(jnp.int32, (rows, HEAD_DIM), 1)
    first_half = lane < ROPE_HALF
    scale = 1.0 / math.sqrt(HEAD_DIM)

    for g, (d, q_ref, k_ref, v_ref) in enumerate(((1, q0, k0, v0), (4, q1, k1, v1), (16, q2, k2, v2))):
        def residue_major(ref_2d):
            if d == 1:
                return ref_2d[...]
            return jnp.concatenate([ref_2d[pl.ds(r, rows // d, stride=d), :] for r in range(d)], axis=0)

        cos = residue_major(cos_scr)
        sin_signed = residue_major(sin_scr)
        if d == 1:
            ng = n
        else:
            for c in range(D_MODEL // HEAD_DIM):
                nperm_scr[:, c * HEAD_DIM:(c + 1) * HEAD_DIM] = residue_major(n_scr.at[c]).astype(BF16)
            ng = nperm_scr[...]

        def rotary(t):
            partner = jnp.where(first_half,
                                pltpu.roll(t, HEAD_DIM - ROPE_HALF, 1),
                                pltpu.roll(t, ROPE_HALF, 1))
            return t * cos + partner * sin_signed

        c0 = g * GROUP_WIDTH
        zq = _bdot(ng, w_ref[:, c0:c0 + GROUP_WIDTH])
        zk = _bdot(ng, w_ref[:, QK_WIDTH + c0:QK_WIDTH + c0 + GROUP_WIDTH])
        zv = _bdot(ng, w_ref[:, 2 * QK_WIDTH + c0:2 * QK_WIDTH + c0 + GROUP_WIDTH])
        for h in range(HEADS_PER_GROUP):
            hs = slice(h * HEAD_DIM, (h + 1) * HEAD_DIM)
            q_ref[:, :, hs] = (rotary(zq[:, hs]) * scale).astype(BF16).reshape(d, rows // d, HEAD_DIM)
            k_ref[:, :, hs] = rotary(zk[:, hs]).astype(BF16).reshape(d, rows // d, HEAD_DIM)
        v_ref[...] = zv.astype(BF16).reshape(d, rows // d, GROUP_WIDTH)
    o1 = 3 * QK_WIDTH
    u_ref[...] = _bdot(n, w_ref[:, o1:o1 + SSM_WIDTH])
    o2 = o1 + SSM_WIDTH
    ga_ref[...] = jax.nn.sigmoid(_bdot(n, w_ref[:, o2:o2 + D_MODEL])).astype(BF16)
    gs_ref[...] = jax.nn.sigmoid(_bdot(n, w_ref[:, o2 + D_MODEL:o2 + 2 * D_MODEL])).astype(BF16)


def _proj(x, pos_row, g_mix, inv_freq_col, w_in):
    s = x.shape[0]
    tm = min(PROJ_ROWS, s)
    row = lambda w: pl.BlockSpec((tm, w), lambda i: (i, 0))
    grp_specs, grp_shapes = [], []
    for d in ATTN_DILATIONS:
        grp_specs.append(pl.BlockSpec((d, tm // d, GROUP_WIDTH), lambda i: (0, i, 0)))
        grp_shapes.append(jax.ShapeDtypeStruct((d, s // d, GROUP_WIDTH), BF16))
    return pl.pallas_call(
        _proj_kernel,
        grid=(s // tm,),
        in_specs=[row(D_MODEL), pl.BlockSpec((1, tm), lambda i: (0, i)), _resident((1, D_MODEL)),
                  _resident((ROPE_HALF, 1)), _resident(w_in.shape)],
        out_specs=grp_specs * 3 + [row(SSM_WIDTH), row(D_MODEL), row(D_MODEL)],
        out_shape=grp_shapes * 3 + [jax.ShapeDtypeStruct((s, SSM_WIDTH), F32),
                                    jax.ShapeDtypeStruct((s, D_MODEL), BF16),
                                    jax.ShapeDtypeStruct((s, D_MODEL), BF16)],
        scratch_shapes=[pltpu.VMEM((D_MODEL // HEAD_DIM, tm, HEAD_DIM), F32), pltpu.VMEM((tm, D_MODEL), BF16),
                        pltpu.VMEM((tm, HEAD_DIM), F32), pltpu.VMEM((tm, HEAD_DIM), F32)],
        compiler_params=_params("parallel"),
        name="proj",
    )(x, pos_row, g_mix, inv_freq_col, w_in)


def _attn_kernel(q_ref, k_ref, kp_ref, v_ref, vp_ref, o_ref, l_ref, o_scr, l_scr, *, d, nsub):
    not_first_block = pl.program_id(0) > 0
    qi = lax.broadcasted_iota(jnp.int32, (BLOCK, 2 * BLOCK), 0)
    kj = lax.broadcasted_iota(jnp.int32, (BLOCK, 2 * BLOCK), 1)
    rel = BLOCK + qi - kj
    band = (rel >= 0) & (rel <= BLOCK)
    band_first = band & ((kj >= BLOCK) | not_first_block)
    lane = lax.broadcasted_iota(jnp.int32, (BLOCK, HEAD_DIM), 1)

    def residue(r):
        for b in range(nsub):
            rows = slice(b * BLOCK, (b + 1) * BLOCK)
            mask = band_first if b == 0 else band
            token_rows = pl.ds(b * BLOCK * d + r, BLOCK, stride=d) if d > 1 else rows
            lse_tile = None
            for h in range(HEADS_PER_GROUP):
                hs = slice(h * HEAD_DIM, (h + 1) * HEAD_DIM)
                q = q_ref[r, rows, hs]
                if b == 0:
                    kw = jnp.concatenate([kp_ref[r, :, hs], k_ref[r, rows, hs]], axis=0)
                    vw = jnp.concatenate([vp_ref[r, :, hs], v_ref[r, rows, hs]], axis=0)
                else:
                    win = slice((b - 1) * BLOCK, (b + 1) * BLOCK)
                    kw = k_ref[r, win, hs]
                    vw = v_ref[r, win, hs]
                s = lax.dot_general(q, kw, (((1,), (1,)), ((), ())), preferred_element_type=F32)
                s = jnp.where(mask, s, MASK_VALUE)
                m = jnp.max(s, axis=-1, keepdims=True)
                p = jnp.exp(s - m)
                den = jnp.sum(p, axis=-1, keepdims=True)
                o_scr[h, token_rows, :] = _bdot(p.astype(BF16), vw) / den
                lse = jnp.broadcast_to(m + jnp.log(den), (BLOCK, HEAD_DIM))
                lse_tile = lse if h == 0 else jnp.where(lane >= h * LSE_LANES, lse, lse_tile)
            l_scr[token_rows, :] = lse_tile

    per_iter = max(1, min(d, ATTN_TILES_PER_BODY // (nsub * HEADS_PER_GROUP)))
    if per_iter == d:
        for r in range(d):
            residue(r)
    else:
        def body(i, carry):
            for j in range(per_iter):
                residue(i * per_iter + j)
            return carry
        lax.fori_loop(0, d // per_iter, body, 0)

    for h in range(HEADS_PER_GROUP):
        o_ref[:, h * HEAD_DIM:(h + 1) * HEAD_DIM] = o_scr[h].astype(BF16)
    l_ref[...] = l_scr[...]


def _attn_group(q, k, v, dilation):
    sub_len = q.shape[1]
    s = sub_len * dilation
    step_tokens = min(ATTN_TOKENS[dilation], s)
    qb = step_tokens // dilation
    nsub = qb // BLOCK
    cur = pl.BlockSpec((dilation, qb, GROUP_WIDTH), lambda i: (0, i, 0))
    prev = pl.BlockSpec((dilation, BLOCK, GROUP_WIDTH), lambda i: (0, jnp.maximum(i * nsub - 1, 0), 0))
    return pl.pallas_call(
        functools.partial(_attn_kernel, d=dilation, nsub=nsub),
        grid=(s // step_tokens,),
        in_specs=[cur, cur, prev, cur, prev],
        out_specs=[pl.BlockSpec((step_tokens, GROUP_WIDTH), lambda i: (i, 0)),
                   pl.BlockSpec((step_tokens, HEAD_DIM), lambda i: (i, 0))],
        out_shape=[jax.ShapeDtypeStruct((s, GROUP_WIDTH), BF16), jax.ShapeDtypeStruct((s, HEAD_DIM), F32)],
        scratch_shapes=[pltpu.VMEM((HEADS_PER_GROUP, step_tokens, HEAD_DIM), F32),
                        pltpu.VMEM((step_tokens, HEAD_DIM), F32)],
        compiler_params=_params("arbitrary"),
        name=f"attn_d{dilation}",
    )(q, k, k, v, v)


def _discretize(lr, li, log_dt):
    dt = jnp.exp(log_dt)
    mag = jnp.exp(lr * dt)
    bar_re = mag * jnp.cos(li * dt)
    bar_im = mag * jnp.sin(li * dt)
    nr = bar_re - 1.0
    ni = bar_im
    den = lr * lr + li * li
    return bar_re, bar_im, (nr * lr + ni * li) / den, (ni * lr - nr * li) / den


def _expand_block_diag(compact, n_blocks):
    rows, b = compact.shape
    a = rows // n_blocks
    wide = n_blocks * b
    src_lane = lax.broadcasted_iota(jnp.int32, (b, wide), 0)
    dst_lane = lax.broadcasted_iota(jnp.int32, (b, wide), 1)
    tiled = _bdot(compact, (dst_lane % b == src_lane).astype(BF16))
    row_block = lax.broadcasted_iota(jnp.int32, (rows, wide), 0) // a
    col_block = lax.broadcasted_iota(jnp.int32, (rows, wide), 1) // b
    return jnp.where(row_block == col_block, tiled, 0.0).astype(BF16)


def _ssm_prep_kernel(lr_ref, li_ref, logdt_ref, lr_rep_ref, li_rep_ref, logdt_rep_ref, b_re_ref, b_im_ref,
                     c_re_ref, c_im_ref,
                     bmat_ref, cre_ref, cim_ref, e_re_ref, e_im_ref, einv_re_ref, einv_im_ref, lam_ref):
    _, _, z_re, z_im = _discretize(lr_rep_ref[...], li_rep_ref[...], logdt_rep_ref[...])
    b_re = b_re_ref[...]
    b_im = b_im_ref[...]
    bb_re = (z_re * b_re - z_im * b_im).astype(BF16)
    bb_im = (z_re * b_im + z_im * b_re).astype(BF16)
    for hf in range(SSM_HALVES):
        rows = slice(hf * HALF_WIDTH, (hf + 1) * HALF_WIDTH)
        bmat_ref[rows, :HALF_STATES] = _expand_block_diag(bb_re[rows], HALF_GROUPS)
        bmat_ref[rows, HALF_STATES:] = _expand_block_diag(bb_im[rows], HALF_GROUPS)
        srows = slice(hf * HALF_STATES, (hf + 1) * HALF_STATES)
        cre_ref[srows, :] = _expand_block_diag(c_re_ref[srows, :].astype(BF16), HALF_GROUPS)
        cim_ref[srows, :] = _expand_block_diag(c_im_ref[srows, :].astype(BF16), HALF_GROUPS)

    lr = lr_ref[...]
    li = li_ref[...]
    dt = jnp.exp(logdt_ref[...])
    bar_re, bar_im, _, _ = _discretize(lr, li, logdt_ref[...])
    lam_ref[0:1, :] = bar_re
    lam_ref[1:2, :] = bar_im
    t = lax.broadcasted_iota(jnp.int32, (SSM_CHUNK, N_STATES), 0).astype(F32)
    grow = jnp.exp(t * (lr * dt))
    theta = t * (li * dt)
    c = jnp.cos(theta)
    s = jnp.sin(theta)
    e_re_ref[...] = grow * c
    e_im_ref[...] = grow * s
    shrink = jnp.exp(-t * (lr * dt))
    einv_re_ref[...] = shrink * c
    einv_im_ref[...] = -(shrink * s)


def _ssm_prep(a_re, a_im, log_dt, b_re, b_im, c_re, c_im):
    row = lambda v: v.reshape(1, N_STATES)
    rep = lambda v: jnp.repeat(v, SSM_GROUP, axis=0)
    logdt_gp = jnp.broadcast_to(log_dt[:, None], (SSM_GROUPS, SSM_STATE))
    chan_state = lambda b: jnp.swapaxes(b, 1, 2).reshape(SSM_WIDTH, SSM_STATE)
    state_chan = lambda c: jnp.swapaxes(c, 1, 2).reshape(N_STATES, SSM_GROUP)
    tab = jax.ShapeDtypeStruct((SSM_CHUNK, N_STATES), F32)
    cmat = jax.ShapeDtypeStruct((N_STATES, HALF_WIDTH), BF16)
    return pl.pallas_call(
        _ssm_prep_kernel,
        out_shape=[jax.ShapeDtypeStruct((SSM_WIDTH, 2 * HALF_STATES), BF16), cmat, cmat, tab, tab, tab, tab,
                   jax.ShapeDtypeStruct((2, N_STATES), F32)],
        compiler_params=pltpu.CompilerParams(vmem_limit_bytes=V7X_VMEM_LIMIT_BYTES),
        name="ssm_prep",
    )(row(a_re), row(a_im), row(logdt_gp), rep(a_re), rep(a_im), rep(logdt_gp),
      chan_state(b_re), chan_state(b_im), state_chan(c_re), state_chan(c_im))


def _gelu_tanh(x):
    return 0.5 * x * (1.0 + jnp.tanh(math.sqrt(2.0 / math.pi) * (x + 0.044715 * (x * x * x))))


def _ssm_kernel(u_ref, bmat_ref, cre_ref, cim_ref, tri_ref, e_re_ref, e_im_ref, einv_re_ref, einv_im_ref,
                lam_ref, d_ref, y_ref, h_re_scr, h_im_scr, carry_scr):
    @pl.when(pl.program_id(0) == 0)
    def _():
        carry_scr[...] = jnp.zeros_like(carry_scr)

    u = u_ref[...]
    ub = u.astype(BF16)
    for hf in range(SSM_HALVES):
        chans = slice(hf * HALF_WIDTH, (hf + 1) * HALF_WIDTH)
        states = slice(hf * HALF_STATES, (hf + 1) * HALF_STATES)
        bu = _bdot(ub[:, chans], bmat_ref[chans, :])
        lam_re = lam_ref[0:1, states]
        lam_im = lam_ref[1:2, states]
        carry_re = carry_scr[0:1, states]
        carry_im = carry_scr[1:2, states]
        for c in range(u.shape[0] // SSM_CHUNK):
            rows = slice(c * SSM_CHUNK, (c + 1) * SSM_CHUNK)
            bu_re = bu[rows, :HALF_STATES]
            bu_im = bu[rows, HALF_STATES:]
            einv_re = einv_re_ref[:, states]
            einv_im = einv_im_ref[:, states]
            x = jnp.concatenate([(bu_re * einv_re - bu_im * einv_im).astype(BF16),
                                 (bu_re * einv_im + bu_im * einv_re).astype(BF16)], axis=1)
            a = _bdot(tri_ref[...], x)
            a_re = a[:, :HALF_STATES] + carry_re
            a_im = a[:, HALF_STATES:] + carry_im
            e_re = e_re_ref[:, states]
            e_im = e_im_ref[:, states]
            h_re = e_re * a_re - e_im * a_im
            h_im = e_re * a_im + e_im * a_re
            h_re_scr[rows, :] = h_re.astype(BF16)
            h_im_scr[rows, :] = h_im.astype(BF16)
            last_re = h_re[SSM_CHUNK - 1:SSM_CHUNK, :]
            last_im = h_im[SSM_CHUNK - 1:SSM_CHUNK, :]
            carry_re = lam_re * last_re - lam_im * last_im
            carry_im = lam_re * last_im + lam_im * last_re
        carry_scr[0:1, states] = carry_re
        carry_scr[1:2, states] = carry_im
        y = (_bdot(h_re_scr[...], cre_ref[states, :]) - _bdot(h_im_scr[...], cim_ref[states, :])
             + d_ref[:, chans] * u[:, chans])
        y_ref[:, chans] = _gelu_tanh(y).astype(BF16)


def _ssm(u, bmat, cre, cim, tri, e_re, e_im, einv_re, einv_im, lam, d_row):
    s = u.shape[0]
    tm = min(SSM_ROWS, s)
    row = pl.BlockSpec((tm, SSM_WIDTH), lambda i: (i, 0))
    consts = (bmat, cre, cim, tri, e_re, e_im, einv_re, einv_im, lam, d_row)
    return pl.pallas_call(
        _ssm_kernel,
        grid=(s // tm,),
        in_specs=[row] + [_resident(c.shape) for c in consts],
        out_specs=row,
        out_shape=jax.ShapeDtypeStruct((s, SSM_WIDTH), BF16),
        scratch_shapes=[pltpu.VMEM((tm, HALF_STATES), BF16), pltpu.VMEM((tm, HALF_STATES), BF16),
                        pltpu.VMEM((2, N_STATES), F32)],
        compiler_params=_params("arbitrary"),
        name="ssm",
    )(u, *consts)


def _mix_kernel(x_ref, o0_ref, o1_ref, o2_ref, l0_ref, l1_ref, l2_ref, y_ref, ga_ref, gs_ref,
                wap_ref, wa_ref, wb_ref, wout_ref, h_ref):
    ls = (l0_ref[...], l1_ref[...], l2_ref[...])
    l_max = jnp.maximum(jnp.maximum(ls[0], ls[1]), ls[2])
    es = [jnp.exp(l - l_max) for l in ls]
    inv = 1.0 / (es[0] + es[1] + es[2])
    heads = []
    for h in range(HEADS_PER_GROUP):
        hs = slice(h * HEAD_DIM, (h + 1) * HEAD_DIM)
        col = slice(h * LSE_LANES, h * LSE_LANES + 1)
        heads.append(sum((e * inv)[:, col] * o_ref[:, hs].astype(F32)
                         for e, o_ref in zip(es, (o0_ref, o1_ref, o2_ref))))
    attn = jnp.concatenate(heads, axis=1).astype(BF16)
    attn_d = _bdot(attn, wap_ref[...])
    y = y_ref[...]
    ssm_d = _bdot(y, wa_ref[...]) * jax.nn.sigmoid(_bdot(y, wb_ref[...]))
    mix = ga_ref[...].astype(F32) * attn_d + gs_ref[...].astype(F32) * ssm_d
    h_ref[...] = x_ref[...] + _bdot(mix.astype(BF16), wout_ref[...])


def _mix(x, os, ls, y, ga, gs, wap, wa, wb, wout):
    s = x.shape[0]
    tm = min(MIX_ROWS, s)
    row = lambda w: pl.BlockSpec((tm, w), lambda i: (i, 0))
    weights = (wap, wa, wb, wout)
    return pl.pallas_call(
        _mix_kernel,
        grid=(s // tm,),
        in_specs=[row(D_MODEL)] + [row(GROUP_WIDTH)] * 3 + [row(HEAD_DIM)] * 3
        + [row(SSM_WIDTH), row(D_MODEL), row(D_MODEL)] + [_resident(w.shape) for w in weights],
        out_specs=row(D_MODEL),
        out_shape=jax.ShapeDtypeStruct((s, D_MODEL), F32),
        compiler_params=_params("parallel"),
        name="mix",
    )(x, *os, *ls, y, ga, gs, *weights)


def _ffn_kernel(h_ref, p_ref, gffn_ref, wg_ref, wu_ref, wd_ref, wpg_ref, wpp_ref, gfin_ref,
                out_ref, acc_scr):
    h = h_ref[...]
    n2 = _rmsnorm(h, gffn_ref[...]).astype(BF16)
    for idx, (c0, width) in enumerate(FFN_CHUNKS):
        gate = _bdot(n2, wg_ref[:, c0:c0 + width])
        up = _bdot(n2, wu_ref[:, c0:c0 + width])
        act = (gate * jax.nn.sigmoid(gate) * up).astype(BF16)
        part = _bdot(act, wd_ref[c0:c0 + width, :])
        if idx == 0:
            acc_scr[...] = h + part
        else:
            acc_scr[...] += part
    h2 = acc_scr[...]
    ple = jax.nn.sigmoid(_bdot(h2.astype(BF16), wpg_ref[...])) * _bdot(p_ref[...].astype(BF16), wpp_ref[...])
    out_ref[...] = _rmsnorm(h2 + ple, gfin_ref[...])


def _ffn(h, p, g_ffn, wg, wu, wd, wpg, wpp, g_final):
    s = h.shape[0]
    tm = min(FFN_ROWS, s)
    row = lambda w: pl.BlockSpec((tm, w), lambda i: (i, 0))
    consts = (g_ffn, wg, wu, wd, wpg, wpp, g_final)
    return pl.pallas_call(
        _ffn_kernel,
        grid=(s // tm,),
        in_specs=[row(D_MODEL), row(PLE_DIM)] + [_resident(c.shape) for c in consts],
        out_specs=row(D_MODEL),
        out_shape=jax.ShapeDtypeStruct((s, D_MODEL), F32),
        scratch_shapes=[pltpu.VMEM((tm, D_MODEL), F32)],
        compiler_params=_params("parallel"),
        name="ffn",
    )(h, p, *consts)


def _layer(x, p, positions, g_mix, w_in, a_re, a_im, log_dt, b_re, b_im, c_re, c_im, d_skip,
           w_attn_proj, w_glu_a, w_glu_b, w_out, g_ffn, w_ffn_gate, w_ffn_up, w_ffn_down,
           w_ple_gate, w_ple_proj, g_final):
    bf = lambda w: w.astype(BF16)
    row = lambda v: v.reshape(1, -1).astype(F32)

    inv_freq = ROPE_THETA ** (-jnp.arange(ROPE_HALF, dtype=F32) * 2.0 / ROPE_DIM)
    outs = _proj(x, positions.reshape(1, -1), row(g_mix), inv_freq.reshape(ROPE_HALF, 1), bf(w_in))
    qs, ks, vs = outs[0:3], outs[3:6], outs[6:9]
    u, gate_attn, gate_ssm = outs[9:]

    attn_os, attn_ls = zip(*[_attn_group(qs[g], ks[g], vs[g], d) for g, d in enumerate(ATTN_DILATIONS)])

    bmat, cre, cim, e_re, e_im, einv_re, einv_im, lam = _ssm_prep(a_re, a_im, log_dt, b_re, b_im, c_re, c_im)
    tri = jnp.tril(jnp.ones((SSM_CHUNK, SSM_CHUNK), F32)).astype(BF16)
    y = _ssm(u, bmat, cre, cim, tri, e_re, e_im, einv_re, einv_im, lam, row(d_skip))

    h1 = _mix(x, attn_os, attn_ls, y, gate_attn, gate_ssm, bf(w_attn_proj), bf(w_glu_a), bf(w_glu_b), bf(w_out))
    return _ffn(h1, p, row(g_ffn), bf(w_ffn_gate), bf(w_ffn_up), bf(w_ffn_down),
                bf(w_ple_gate), bf(w_ple_proj), row(g_final))


def kernel(x, p, positions, g_mix, w_in, a_re, a_im, log_dt, b_re, b_im, c_re, c_im, d_skip,
           w_attn_proj, w_glu_a, w_glu_b, w_out, g_ffn, w_ffn_gate, w_ffn_up, w_ffn_down,
           w_ple_gate, w_ple_proj, g_final):
    batch, depth = x.shape[0], p.shape[0]
    assert batch == 1 and depth == 1, "kernel supports the stated BATCH=1, DEPTH=1 problem"
    out = _layer(x[0], p[0, 0], positions[0], g_mix[0], w_in[0], a_re[0], a_im[0], log_dt[0],
                 b_re[0], b_im[0], c_re[0], c_im[0], d_skip[0], w_attn_proj[0], w_glu_a[0],
                 w_glu_b[0], w_out[0], g_ffn[0], w_ffn_gate[0], w_ffn_up[0], w_ffn_down[0],
                 w_ple_gate[0], w_ple_proj[0], g_final)
    return out[None]
```

```python
import functools
import math

import jax
import jax.numpy as jnp
from jax import lax
from jax.experimental import pallas as pl
from jax.experimental.pallas import tpu as pltpu

F32 = jnp.float32
BF16 = jnp.bfloat16

D_MODEL = 1024
HEAD_DIM = 128
HEADS_PER_GROUP = 4
GROUP_WIDTH = HEADS_PER_GROUP * HEAD_DIM
ATTN_DILATIONS = (1, 4, 16)
N_GROUPS = len(ATTN_DILATIONS)
QK_WIDTH = N_GROUPS * GROUP_WIDTH
BLOCK = 128
ROPE_THETA = 500000.0
ROPE_DIM = HEAD_DIM // 4
ROPE_HALF = ROPE_DIM // 2
SSM_WIDTH = 512
SSM_GROUP = 16
SSM_GROUPS = SSM_WIDTH // SSM_GROUP
SSM_STATE = 64
N_STATES = SSM_GROUPS * SSM_STATE
SSM_HALVES = 2
HALF_GROUPS = SSM_GROUPS // SSM_HALVES
HALF_WIDTH = SSM_WIDTH // SSM_HALVES
HALF_STATES = N_STATES // SSM_HALVES
STATE_BLOCK = 256
D_FF = 2816
PLE_DIM = 256
EPS = 1e-6
MASK_VALUE = -1e30

V7X_VMEM_LIMIT_BYTES = 56 * 1024 * 1024

PROJ_ROWS = 512
ATTN_TOKENS = {1: 1024, 4: 2048, 16: 2048}
ATTN_TILES_PER_BODY = 32
LSE_LANES = HEAD_DIM // HEADS_PER_GROUP
SSM_CHUNK = 128
SSM_ROWS = 512
MIX_ROWS = 1024
FFN_ROWS = 512
FFN_CHUNKS = ((0, 1024), (1024, 1024), (2048, 768))


def _resident(shape):
    return pl.BlockSpec(shape, lambda *_: (0,) * len(shape), pipeline_mode=pl.Buffered(1))


def _params(*semantics):
    return pltpu.CompilerParams(dimension_semantics=semantics,
                                vmem_limit_bytes=V7X_VMEM_LIMIT_BYTES)


def _rmsnorm(x, g):
    return (x * lax.rsqrt(jnp.mean(x * x, axis=-1, keepdims=True) + EPS)) * g


def _bdot(a, b):
    return jnp.dot(a, b, preferred_element_type=F32)


def _proj_kernel(x_ref, pos_ref, g_ref, invf_ref, w_ref,
                 q0, q1, q2, k0, k1, k2, v0, v1, v2, u_ref, ga_ref, gs_ref,
                 n_scr, nperm_scr, cos_scr, sin_scr):
    xn = _rmsnorm(x_ref[...], g_ref[...])
    n = xn.astype(BF16)
    rows = xn.shape[0]
    for c in range(D_MODEL // HEAD_DIM):
        n_scr[c] = xn[:, c * HEAD_DIM:(c + 1) * HEAD_DIM]

    ang = invf_ref[...] * pos_ref[...].astype(F32)
    cos_t = jnp.cos(ang)
    sin_t = jnp.sin(ang)
    rest = (HEAD_DIM - ROPE_DIM, rows)
    cos_scr[...] = jnp.concatenate([cos_t, cos_t, jnp.ones(rest, F32)], axis=0).T
    sin_scr[...] = jnp.concatenate([-sin_t, sin_t, jnp.zeros(rest, F32)], axis=0).T

    lane = lax.broadcasted_iota(jnp.int32, (rows, HEAD_DIM), 1)
    first_half = lane < ROPE_HALF
    scale = 1.0 / math.sqrt(HEAD_DIM)

    for g, (d, q_ref, k_ref, v_ref) in enumerate(((1, q0, k0, v0), (4, q1, k1, v1), (16, q2, k2, v2))):
        def residue_major(ref_2d):
            if d == 1:
                return ref_2d[...]
            return jnp.concatenate([ref_2d[pl.ds(r, rows // d, stride=d), :] for r in range(d)], axis=0)

        cos = residue_major(cos_scr)
        sin_signed = residue_major(sin_scr)
        if d == 1:
            ng = n
        else:
            for c in range(D_MODEL // HEAD_DIM):
                nperm_scr[:, c * HEAD_DIM:(c + 1) * HEAD_DIM] = residue_major(n_scr.at[c]).astype(BF16)
            ng = nperm_scr[...]

        def rotary(t):
            partner = jnp.where(first_half,
                                pltpu.roll(t, HEAD_DIM - ROPE_HALF, 1),
                                pltpu.roll(t, ROPE_HALF, 1))
            return t * cos + partner * sin_signed

        c0 = g * GROUP_WIDTH
        zq = _bdot(ng, w_ref[:, c0:c0 + GROUP_WIDTH])
        zk = _bdot(ng, w_ref[:, QK_WIDTH + c0:QK_WIDTH + c0 + GROUP_WIDTH])
        zv = _bdot(ng, w_ref[:, 2 * QK_WIDTH + c0:2 * QK_WIDTH + c0 + GROUP_WIDTH])
        for h in range(HEADS_PER_GROUP):
            hs = slice(h * HEAD_DIM, (h + 1) * HEAD_DIM)
            q_ref[:, :, hs] = (rotary(zq[:, hs]) * scale).astype(BF16).reshape(d, rows // d, HEAD_DIM)
            k_ref[:, :, hs] = rotary(zk[:, hs]).astype(BF16).reshape(d, rows // d, HEAD_DIM)
        v_ref[...] = zv.astype(BF16).reshape(d, rows // d, GROUP_WIDTH)
    o1 = 3 * QK_WIDTH
    u_ref[...] = _bdot(n, w_ref[:, o1:o1 + SSM_WIDTH])
    o2 = o1 + SSM_WIDTH
    ga_ref[...] = jax.nn.sigmoid(_bdot(n, w_ref[:, o2:o2 + D_MODEL])).astype(BF16)
    gs_ref[...] = jax.nn.sigmoid(_bdot(n, w_ref[:, o2 + D_MODEL:o2 + 2 * D_MODEL])).astype(BF16)


def _proj(x, pos_row, g_mix, inv_freq_col, w_in):
    s = x.shape[0]
    tm = min(PROJ_ROWS, s)
    row = lambda w: pl.BlockSpec((tm, w), lambda i: (i, 0))
    grp_specs, grp_shapes = [], []
    for d in ATTN_DILATIONS:
        grp_specs.append(pl.BlockSpec((d, tm // d, GROUP_WIDTH), lambda i: (0, i, 0)))
        grp_shapes.append(jax.ShapeDtypeStruct((d, s // d, GROUP_WIDTH), BF16))
    return pl.pallas_call(
        _proj_kernel,
        grid=(s // tm,),
        in_specs=[row(D_MODEL), pl.BlockSpec((1, tm), lambda i: (0, i)), _resident((1, D_MODEL)),
                  _resident((ROPE_HALF, 1)), _resident(w_in.shape)],
        out_specs=grp_specs * 3 + [row(SSM_WIDTH), row(D_MODEL), row(D_MODEL)],
        out_shape=grp_shapes * 3 + [jax.ShapeDtypeStruct((s, SSM_WIDTH), F32),
                                    jax.ShapeDtypeStruct((s, D_MODEL), BF16),
                                    jax.ShapeDtypeStruct((s, D_MODEL), BF16)],
        scratch_shapes=[pltpu.VMEM((D_MODEL // HEAD_DIM, tm, HEAD_DIM), F32), pltpu.VMEM((tm, D_MODEL), BF16),
                        pltpu.VMEM((tm, HEAD_DIM), F32), pltpu.VMEM((tm, HEAD_DIM), F32)],
        compiler_params=_params("parallel"),
        name="proj",
    )(x, pos_row, g_mix, inv_freq_col, w_in)


def _attn_kernel(q_ref, k_ref, kp_ref, v_ref, vp_ref, o_ref, l_ref, o_scr, l_scr, *, d, nsub):
    not_first_block = pl.program_id(0) > 0
    qi = lax.broadcasted_iota(jnp.int32, (BLOCK, 2 * BLOCK), 0)
    kj = lax.broadcasted_iota(jnp.int32, (BLOCK, 2 * BLOCK), 1)
    rel = BLOCK + qi - kj
    band = (rel >= 0) & (rel <= BLOCK)
    band_first = band & ((kj >= BLOCK) | not_first_block)
    lane = lax.broadcasted_iota(jnp.int32, (BLOCK, HEAD_DIM), 1)

    def residue(r):
        for b in range(nsub):
            rows = slice(b * BLOCK, (b + 1) * BLOCK)
            mask = band_first if b == 0 else band
            token_rows = pl.ds(b * BLOCK * d + r, BLOCK, stride=d) if d > 1 else rows
            lse_tile = None
            for h in range(HEADS_PER_GROUP):
                hs = slice(h * HEAD_DIM, (h + 1) * HEAD_DIM)
                q = q_ref[r, rows, hs]
                if b == 0:
                    kw = jnp.concatenate([kp_ref[r, :, hs], k_ref[r, rows, hs]], axis=0)
                    vw = jnp.concatenate([vp_ref[r, :, hs], v_ref[r, rows, hs]], axis=0)
                else:
                    win = slice((b - 1) * BLOCK, (b + 1) * BLOCK)
                    kw = k_ref[r, win, hs]
                    vw = v_ref[r, win, hs]
                s = lax.dot_general(q, kw, (((1,), (1,)), ((), ())), preferred_element_type=F32)
                s = jnp.where(mask, s, MASK_VALUE)
                m = jnp.max(s, axis=-1, keepdims=True)
                p = jnp.exp(s - m)
                den = jnp.sum(p, axis=-1, keepdims=True)
                o_scr[h, token_rows, :] = _bdot(p.astype(BF16), vw) / den
                lse = jnp.broadcast_to(m + jnp.log(den), (BLOCK, HEAD_DIM))
                lse_tile = lse if h == 0 else jnp.where(lane >= h * LSE_LANES, lse, lse_tile)
            l_scr[token_rows, :] = lse_tile

    per_iter = max(1, min(d, ATTN_TILES_PER_BODY // (nsub * HEADS_PER_GROUP)))
    if per_iter == d:
        for r in range(d):
            residue(r)
    else:
        def body(i, carry):
            for j in range(per_iter):
                residue(i * per_iter + j)
            return carry
        lax.fori_loop(0, d // per_iter, body, 0)

    for h in range(HEADS_PER_GROUP):
        o_ref[:, h * HEAD_DIM:(h + 1) * HEAD_DIM] = o_scr[h].astype(BF16)
    l_ref[...] = l_scr[...]


def _attn_group(q, k, v, dilation):
    sub_len = q.shape[1]
    s = sub_len * dilation
    step_tokens = min(ATTN_TOKENS[dilation], s)
    qb = step_tokens // dilation
    nsub = qb // BLOCK
    cur = pl.BlockSpec((dilation, qb, GROUP_WIDTH), lambda i: (0, i, 0))
    prev = pl.BlockSpec((dilation, BLOCK, GROUP_WIDTH), lambda i: (0, jnp.maximum(i * nsub - 1, 0), 0))
    return pl.pallas_call(
        functools.partial(_attn_kernel, d=dilation, nsub=nsub),
        grid=(s // step_tokens,),
        in_specs=[cur, cur, prev, cur, prev],
        out_specs=[pl.BlockSpec((step_tokens, GROUP_WIDTH), lambda i: (i, 0)),
                   pl.BlockSpec((step_tokens, HEAD_DIM), lambda i: (i, 0))],
        out_shape=[jax.ShapeDtypeStruct((s, GROUP_WIDTH), BF16), jax.ShapeDtypeStruct((s, HEAD_DIM), F32)],
        scratch_shapes=[pltpu.VMEM((HEADS_PER_GROUP, step_tokens, HEAD_DIM), F32),
                        pltpu.VMEM((step_tokens, HEAD_DIM), F32)],
        compiler_params=_params("arbitrary"),
        name=f"attn_d{dilation}",
    )(q, k, k, v, v)


def _discretize(lr, li, log_dt):
    dt = jnp.exp(log_dt)
    mag = jnp.exp(lr * dt)
    bar_re = mag * jnp.cos(li * dt)
    bar_im = mag * jnp.sin(li * dt)
    nr = bar_re - 1.0
    ni = bar_im
    den = lr * lr + li * li
    return bar_re, bar_im, (nr * lr + ni * li) / den, (ni * lr - nr * li) / den


def _expand_block_diag(compact, n_blocks):
    rows, b = compact.shape
    a = rows // n_blocks
    wide = n_blocks * b
    src_lane = lax.broadcasted_iota(jnp.int32, (b, wide), 0)
    dst_lane = lax.broadcasted_iota(jnp.int32, (b, wide), 1)
    tiled = _bdot(compact, (dst_lane % b == src_lane).astype(BF16))
    row_block = lax.broadcasted_iota(jnp.int32, (rows, wide), 0) // a
    col_block = lax.broadcasted_iota(jnp.int32, (rows, wide), 1) // b
    return jnp.where(row_block == col_block, tiled, 0.0).astype(BF16)


def _ssm_prep_kernel(lr_ref, li_ref, logdt_ref, lr_rep_ref, li_rep_ref, logdt_rep_ref, b_re_ref, b_im_ref,
                     c_re_ref, c_im_ref,
                     bmat_ref, cre_ref, cim_ref, e_re_ref, e_im_ref, einv_re_ref, einv_im_ref, lam_ref):
    _, _, z_re, z_im = _discretize(lr_rep_ref[...], li_rep_ref[...], logdt_rep_ref[...])
    b_re = b_re_ref[...]
    b_im = b_im_ref[...]
    bb_re = (z_re * b_re - z_im * b_im).astype(BF16)
    bb_im = (z_re * b_im + z_im * b_re).astype(BF16)
    for hf in range(SSM_HALVES):
        rows = slice(hf * HALF_WIDTH, (hf + 1) * HALF_WIDTH)
        bmat_ref[rows, :HALF_STATES] = _expand_block_diag(bb_re[rows], HALF_GROUPS)
        bmat_ref[rows, HALF_STATES:] = _expand_block_diag(bb_im[rows], HALF_GROUPS)
        srows = slice(hf * HALF_STATES, (hf + 1) * HALF_STATES)
        cre_ref[srows, :] = _expand_block_diag(c_re_ref[srows, :].astype(BF16), HALF_GROUPS)
        cim_ref[srows, :] = _expand_block_diag(c_im_ref[srows, :].astype(BF16), HALF_GROUPS)

    lr = lr_ref[...]
    li = li_ref[...]
    dt = jnp.exp(logdt_ref[...])
    bar_re, bar_im, _, _ = _discretize(lr, li, logdt_ref[...])
    lam_ref[0:1, :] = bar_re
    lam_ref[1:2, :] = bar_im
    t = lax.broadcasted_iota(jnp.int32, (SSM_CHUNK, N_STATES), 0).astype(F32)
    grow = jnp.exp(t * (lr * dt))
    theta = t * (li * dt)
    c = jnp.cos(theta)
    s = jnp.sin(theta)
    e_re_ref[...] = grow * c
    e_im_ref[...] = grow * s
    shrink = jnp.exp(-t * (lr * dt))
    einv_re_ref[...] = shrink * c
    einv_im_ref[...] = -(shrink * s)


def _ssm_prep(a_re, a_im, log_dt, b_re, b_im, c_re, c_im):
    row = lambda v: v.reshape(1, N_STATES)
    rep = lambda v: jnp.repeat(v, SSM_GROUP, axis=0)
    logdt_gp = jnp.broadcast_to(log_dt[:, None], (SSM_GROUPS, SSM_STATE))
    chan_state = lambda b: jnp.swapaxes(b, 1, 2).reshape(SSM_WIDTH, SSM_STATE)
    state_chan = lambda c: jnp.swapaxes(c, 1, 2).reshape(N_STATES, SSM_GROUP)
    tab = jax.ShapeDtypeStruct((SSM_CHUNK, N_STATES), F32)
    cmat = jax.ShapeDtypeStruct((N_STATES, HALF_WIDTH), BF16)
    return pl.pallas_call(
        _ssm_prep_kernel,
        out_shape=[jax.ShapeDtypeStruct((SSM_WIDTH, 2 * HALF_STATES), BF16), cmat, cmat, tab, tab, tab, tab,
                   jax.ShapeDtypeStruct((2, N_STATES), F32)],
        compiler_params=pltpu.CompilerParams(vmem_limit_bytes=V7X_VMEM_LIMIT_BYTES),
        name="ssm_prep",
    )(row(a_re), row(a_im), row(logdt_gp), rep(a_re), rep(a_im), rep(logdt_gp),
      chan_state(b_re), chan_state(b_im), state_chan(c_re), state_chan(c_im))


def _gelu_tanh(x):
    return 0.5 * x * (1.0 + jnp.tanh(math.sqrt(2.0 / math.pi) * (x + 0.044715 * (x * x * x))))


def _ssm_kernel(u_ref, bmat_ref, cre_ref, cim_ref, tri_ref, e_re_ref, e_im_ref, einv_re_ref, einv_im_ref,
                lam_ref, d_ref, y_ref, h_re_scr, h_im_scr, carry_scr):
    @pl.when(pl.program_id(0) == 0)
    def _():
        carry_scr[...] = jnp.zeros_like(carry_scr)

    u = u_ref[...]
    ub = u.astype(BF16)
    for hf in range(SSM_HALVES):
        chans = slice(hf * HALF_WIDTH, (hf + 1) * HALF_WIDTH)
        y = d_ref[:, chans] * u[:, chans]
        for blk in range(HALF_STATES // STATE_BLOCK):
            local = slice(blk * STATE_BLOCK, (blk + 1) * STATE_BLOCK)
            local_im = slice(HALF_STATES + blk * STATE_BLOCK, HALF_STATES + (blk + 1) * STATE_BLOCK)
            states = slice(hf * HALF_STATES + blk * STATE_BLOCK, hf * HALF_STATES + (blk + 1) * STATE_BLOCK)
            bu_re_all = _bdot(ub[:, chans], bmat_ref[chans, local])
            bu_im_all = _bdot(ub[:, chans], bmat_ref[chans, local_im])
            lam_re = lam_ref[0:1, states]
            lam_im = lam_ref[1:2, states]
            carry_re = carry_scr[0:1, states]
            carry_im = carry_scr[1:2, states]
            for c in range(u.shape[0] // SSM_CHUNK):
                rows = slice(c * SSM_CHUNK, (c + 1) * SSM_CHUNK)
                bu_re = bu_re_all[rows]
                bu_im = bu_im_all[rows]
                einv_re = einv_re_ref[:, states]
                einv_im = einv_im_ref[:, states]
                x = jnp.concatenate([(bu_re * einv_re - bu_im * einv_im).astype(BF16),
                                     (bu_re * einv_im + bu_im * einv_re).astype(BF16)], axis=1)
                a = _bdot(tri_ref[...], x)
                a_re = a[:, :STATE_BLOCK] + carry_re
                a_im = a[:, STATE_BLOCK:] + carry_im
                e_re = e_re_ref[:, states]
                e_im = e_im_ref[:, states]
                h_re = e_re * a_re - e_im * a_im
                h_im = e_re * a_im + e_im * a_re
                h_re_scr[rows, :] = h_re.astype(BF16)
                h_im_scr[rows, :] = h_im.astype(BF16)
                last_re = h_re[SSM_CHUNK - 1:SSM_CHUNK, :]
                last_im = h_im[SSM_CHUNK - 1:SSM_CHUNK, :]
                carry_re = lam_re * last_re - lam_im * last_im
                carry_im = lam_re * last_im + lam_im * last_re
            carry_scr[0:1, states] = carry_re
            carry_scr[1:2, states] = carry_im
            y = y + _bdot(h_re_scr[...], cre_ref[states, :]) - _bdot(h_im_scr[...], cim_ref[states, :])
        y_ref[:, chans] = _gelu_tanh(y).astype(BF16)


def _ssm(u, bmat, cre, cim, tri, e_re, e_im, einv_re, einv_im, lam, d_row):
    s = u.shape[0]
    tm = min(SSM_ROWS, s)
    row = pl.BlockSpec((tm, SSM_WIDTH), lambda i: (i, 0))
    consts = (bmat, cre, cim, tri, e_re, e_im, einv_re, einv_im, lam, d_row)
    return pl.pallas_call(
        _ssm_kernel,
        grid=(s // tm,),
        in_specs=[row] + [_resident(c.shape) for c in consts],
        out_specs=row,
        out_shape=jax.ShapeDtypeStruct((s, SSM_WIDTH), BF16),
        scratch_shapes=[pltpu.VMEM((tm, STATE_BLOCK), BF16), pltpu.VMEM((tm, STATE_BLOCK), BF16),
                        pltpu.VMEM((2, N_STATES), F32)],
        compiler_params=_params("arbitrary"),
        name="ssm",
    )(u, *consts)


def _mix_kernel(x_ref, o0_ref, o1_ref, o2_ref, l0_ref, l1_ref, l2_ref, y_ref, ga_ref, gs_ref,
                wap_ref, wa_ref, wb_ref, wout_ref, h_ref):
    ls = (l0_ref[...], l1_ref[...], l2_ref[...])
    l_max = jnp.maximum(jnp.maximum(ls[0], ls[1]), ls[2])
    es = [jnp.exp(l - l_max) for l in ls]
    inv = 1.0 / (es[0] + es[1] + es[2])
    heads = []
    for h in range(HEADS_PER_GROUP):
        hs = slice(h * HEAD_DIM, (h + 1) * HEAD_DIM)
        col = slice(h * LSE_LANES, h * LSE_LANES + 1)
        heads.append(sum((e * inv)[:, col] * o_ref[:, hs].astype(F32)
                         for e, o_ref in zip(es, (o0_ref, o1_ref, o2_ref))))
    attn = jnp.concatenate(heads, axis=1).astype(BF16)
    attn_d = _bdot(attn, wap_ref[...])
    y = y_ref[...]
    ssm_d = _bdot(y, wa_ref[...]) * jax.nn.sigmoid(_bdot(y, wb_ref[...]))
    mix = ga_ref[...].astype(F32) * attn_d + gs_ref[...].astype(F32) * ssm_d
    h_ref[...] = x_ref[...] + _bdot(mix.astype(BF16), wout_ref[...])


def _mix(x, os, ls, y, ga, gs, wap, wa, wb, wout):
    s = x.shape[0]
    tm = min(MIX_ROWS, s)
    row = lambda w: pl.BlockSpec((tm, w), lambda i: (i, 0))
    weights = (wap, wa, wb, wout)
    return pl.pallas_call(
        _mix_kernel,
        grid=(s // tm,),
        in_specs=[row(D_MODEL)] + [row(GROUP_WIDTH)] * 3 + [row(HEAD_DIM)] * 3
        + [row(SSM_WIDTH), row(D_MODEL), row(D_MODEL)] + [_resident(w.shape) for w in weights],
        out_specs=row(D_MODEL),
        out_shape=jax.ShapeDtypeStruct((s, D_MODEL), F32),
        compiler_params=_params("parallel"),
        name="mix",
    )(x, *os, *ls, y, ga, gs, *weights)


def _ffn_kernel(h_ref, p_ref, gffn_ref, wg_ref, wu_ref, wd_ref, wpg_ref, wpp_ref, gfin_ref,
                out_ref, acc_scr):
    h = h_ref[...]
    n2 = _rmsnorm(h, gffn_ref[...]).astype(BF16)
    for idx, (c0, width) in enumerate(FFN_CHUNKS):
        gate = _bdot(n2, wg_ref[:, c0:c0 + width])
        up = _bdot(n2, wu_ref[:, c0:c0 + width])
        act = (gate * jax.nn.sigmoid(gate) * up).astype(BF16)
        part = _bdot(act, wd_ref[c0:c0 + width, :])
        if idx == 0:
            acc_scr[...] = h + part
        else:
            acc_scr[...] += part
    h2 = acc_scr[...]
    ple = jax.nn.sigmoid(_bdot(h2.astype(BF16), wpg_ref[...])) * _bdot(p_ref[...].astype(BF16), wpp_ref[...])
    out_ref[...] = _rmsnorm(h2 + ple, gfin_ref[...])


def _ffn(h, p, g_ffn, wg, wu, wd, wpg, wpp, g_final):
    s = h.shape[0]
    tm = min(FFN_ROWS, s)
    row = lambda w: pl.BlockSpec((tm, w), lambda i: (i, 0))
    consts = (g_ffn, wg, wu, wd, wpg, wpp, g_final)
    return pl.pallas_call(
        _ffn_kernel,
        grid=(s // tm,),
        in_specs=[row(D_MODEL), row(PLE_DIM)] + [_resident(c.shape) for c in consts],
        out_specs=row(D_MODEL),
        out_shape=jax.ShapeDtypeStruct((s, D_MODEL), F32),
        scratch_shapes=[pltpu.VMEM((tm, D_MODEL), F32)],
        compiler_params=_params("parallel"),
        name="ffn",
    )(h, p, *consts)


def _layer(x, p, positions, g_mix, w_in, a_re, a_im, log_dt, b_re, b_im, c_re, c_im, d_skip,
           w_attn_proj, w_glu_a, w_glu_b, w_out, g_ffn, w_ffn_gate, w_ffn_up, w_ffn_down,
           w_ple_gate, w_ple_proj, g_final):
    bf = lambda w: w.astype(BF16)
    row = lambda v: v.reshape(1, -1).astype(F32)

    inv_freq = ROPE_THETA ** (-jnp.arange(ROPE_HALF, dtype=F32) * 2.0 / ROPE_DIM)
    outs = _proj(x, positions.reshape(1, -1), row(g_mix), inv_freq.reshape(ROPE_HALF, 1), bf(w_in))
    qs, ks, vs = outs[0:3], outs[3:6], outs[6:9]
    u, gate_attn, gate_ssm = outs[9:]

    attn_os, attn_ls = zip(*[_attn_group(qs[g], ks[g], vs[g], d) for g, d in enumerate(ATTN_DILATIONS)])

    bmat, cre, cim, e_re, e_im, einv_re, einv_im, lam = _ssm_prep(a_re, a_im, log_dt, b_re, b_im, c_re, c_im)
    tri = jnp.tril(jnp.ones((SSM_CHUNK, SSM_CHUNK), F32)).astype(BF16)
    y = _ssm(u, bmat, cre, cim, tri, e_re, e_im, einv_re, einv_im, lam, row(d_skip))

    h1 = _mix(x, attn_os, attn_ls, y, gate_attn, gate_ssm, bf(w_attn_proj), bf(w_glu_a), bf(w_glu_b), bf(w_out))
    return _ffn(h1, p, row(g_ffn), bf(w_ffn_gate), bf(w_ffn_up), bf(w_ffn_down),
                bf(w_ple_gate), bf(w_ple_proj), row(g_final))


def kernel(x, p, positions, g_mix, w_in, a_re, a_im, log_dt, b_re, b_im, c_re, c_im, d_skip,
           w_attn_proj, w_glu_a, w_glu_b, w_out, g_ffn, w_ffn_gate, w_ffn_up, w_ffn_down,
           w_ple_gate, w_ple_proj, g_final):
    batch, depth = x.shape[0], p.shape[0]
    assert batch == 1 and depth == 1, "kernel supports the stated BATCH=1, DEPTH=1 problem"
    out = _layer(x[0], p[0, 0], positions[0], g_mix[0], w_in[0], a_re[0], a_im[0], log_dt[0],
                 b_re[0], b_im[0], c_re[0], c_im[0], d_skip[0], w_attn_proj[0], w_glu_a[0],
                 w_glu_b[0], w_out[0], g_ffn[0], w_ffn_gate[0], w_ffn_up[0], w_ffn_down[0],
                 w_ple_gate[0], w_ple_proj[0], g_final)
    return out[None]
```

```python
import functools
import math

import jax
import jax.numpy as jnp
from jax import lax
from jax.experimental import pallas as pl
from jax.experimental.pallas import tpu as pltpu

F32 = jnp.float32
BF16 = jnp.bfloat16

D_MODEL = 1024
HEAD_DIM = 128
HEADS_PER_GROUP = 4
GROUP_WIDTH = HEADS_PER_GROUP * HEAD_DIM
ATTN_DILATIONS = (1, 4, 16)
N_GROUPS = len(ATTN_DILATIONS)
QK_WIDTH = N_GROUPS * GROUP_WIDTH
BLOCK = 128
ROPE_THETA = 500000.0
ROPE_DIM = HEAD_DIM // 4
ROPE_HALF = ROPE_DIM // 2
SSM_WIDTH = 512
SSM_GROUP = 16
SSM_GROUPS = SSM_WIDTH // SSM_GROUP
SSM_STATE = 64
N_STATES = SSM_GROUPS * SSM_STATE
SSM_HALVES = 2
HALF_GROUPS = SSM_GROUPS // SSM_HALVES
HALF_WIDTH = SSM_WIDTH // SSM_HALVES
HALF_STATES = N_STATES // SSM_HALVES
STATE_BLOCK = 256
D_FF = 2816
PLE_DIM = 256
EPS = 1e-6
MASK_VALUE = -1e30

V7X_VMEM_LIMIT_BYTES = 56 * 1024 * 1024

PROJ_ROWS = 512
ATTN_TOKENS = {1: 1024, 4: 2048, 16: 2048}
ATTN_TILES_PER_BODY = 32
LSE_LANES = HEAD_DIM // HEADS_PER_GROUP
SSM_CHUNK = 128
SSM_ROWS = 512
MIX_ROWS = 1024
MIX_SUB_ROWS = 256
FFN_ROWS = 512
FFN_CHUNKS = ((0, 1024), (1024, 1024), (2048, 768))


def _resident(shape):
    return pl.BlockSpec(shape, lambda *_: (0,) * len(shape), pipeline_mode=pl.Buffered(1))


def _params(*semantics):
    return pltpu.CompilerParams(dimension_semantics=semantics,
                                vmem_limit_bytes=V7X_VMEM_LIMIT_BYTES)


def _rmsnorm(x, g):
    return (x * lax.rsqrt(jnp.mean(x * x, axis=-1, keepdims=True) + EPS)) * g


def _bdot(a, b):
    return jnp.dot(a, b, preferred_element_type=F32)


def _proj_kernel(x_ref, pos_ref, g_ref, invf_ref, w_ref,
                 qkv0, qkv1, qkv2, u_ref, gates_ref,
                 n_scr, nperm_scr, cos_scr, sin_scr):
    xn = _rmsnorm(x_ref[...], g_ref[...])
    n = xn.astype(BF16)
    rows = xn.shape[0]
    for c in range(D_MODEL // HEAD_DIM):
        n_scr[c] = xn[:, c * HEAD_DIM:(c + 1) * HEAD_DIM]

    ang = invf_ref[...] * pos_ref[...].astype(F32)
    cos_t = jnp.cos(ang)
    sin_t = jnp.sin(ang)
    rest = (HEAD_DIM - ROPE_DIM, rows)
    cos_scr[...] = jnp.concatenate([cos_t, cos_t, jnp.ones(rest, F32)], axis=0).T
    sin_scr[...] = jnp.concatenate([-sin_t, sin_t, jnp.zeros(rest, F32)], axis=0).T

    lane = lax.broadcasted_iota(jnp.int32, (rows, HEAD_DIM), 1)
    first_half = lane < ROPE_HALF
    scale = math.log2(math.e) / math.sqrt(HEAD_DIM)

    for g, (d, qkv_ref) in enumerate(zip(ATTN_DILATIONS, (qkv0, qkv1, qkv2))):
        def residue_major(ref_2d):
            if d == 1:
                return ref_2d[...]
            return jnp.concatenate([ref_2d[pl.ds(r, rows // d, stride=d), :] for r in range(d)], axis=0)

        cos = residue_major(cos_scr)
        sin_signed = residue_major(sin_scr)
        if d == 1:
            ng = n
        else:
            for c in range(D_MODEL // HEAD_DIM):
                nperm_scr[:, c * HEAD_DIM:(c + 1) * HEAD_DIM] = residue_major(n_scr.at[c]).astype(BF16)
            ng = nperm_scr[...]

        def rotary(t):
            partner = jnp.where(first_half,
                                pltpu.roll(t, HEAD_DIM - ROPE_HALF, 1),
                                pltpu.roll(t, ROPE_HALF, 1))
            return t * cos + partner * sin_signed

        c0 = g * GROUP_WIDTH
        zq = _bdot(ng, w_ref[:, c0:c0 + GROUP_WIDTH])
        zk = _bdot(ng, w_ref[:, QK_WIDTH + c0:QK_WIDTH + c0 + GROUP_WIDTH])
        zv = _bdot(ng, w_ref[:, 2 * QK_WIDTH + c0:2 * QK_WIDTH + c0 + GROUP_WIDTH])
        for h in range(HEADS_PER_GROUP):
            hs = slice(h * HEAD_DIM, (h + 1) * HEAD_DIM)
            ks = slice(GROUP_WIDTH + h * HEAD_DIM, GROUP_WIDTH + (h + 1) * HEAD_DIM)
            qkv_ref[:, :, hs] = (rotary(zq[:, hs]) * scale).astype(BF16).reshape(d, rows // d, HEAD_DIM)
            qkv_ref[:, :, ks] = rotary(zk[:, hs]).astype(BF16).reshape(d, rows // d, HEAD_DIM)
        qkv_ref[:, :, 2 * GROUP_WIDTH:] = zv.astype(BF16).reshape(d, rows // d, GROUP_WIDTH)
    o1 = 3 * QK_WIDTH
    u_ref[...] = _bdot(n, w_ref[:, o1:o1 + SSM_WIDTH])
    o2 = o1 + SSM_WIDTH
    gates_ref[...] = jax.nn.sigmoid(_bdot(n, w_ref[:, o2:o2 + 2 * D_MODEL])).astype(BF16)


def _proj(x, pos_row, g_mix, inv_freq_col, w_in):
    s = x.shape[0]
    tm = min(PROJ_ROWS, s)
    row = lambda w: pl.BlockSpec((tm, w), lambda i: (i, 0))
    grp_specs, grp_shapes = [], []
    for d in ATTN_DILATIONS:
        grp_specs.append(pl.BlockSpec((d, tm // d, 3 * GROUP_WIDTH), lambda i: (0, i, 0)))
        grp_shapes.append(jax.ShapeDtypeStruct((d, s // d, 3 * GROUP_WIDTH), BF16))
    return pl.pallas_call(
        _proj_kernel,
        grid=(s // tm,),
        in_specs=[row(D_MODEL), pl.BlockSpec((1, tm), lambda i: (0, i)), _resident((1, D_MODEL)),
                  _resident((ROPE_HALF, 1)), _resident(w_in.shape)],
        out_specs=grp_specs + [row(SSM_WIDTH), row(2 * D_MODEL)],
        out_shape=grp_shapes + [jax.ShapeDtypeStruct((s, SSM_WIDTH), F32),
                                jax.ShapeDtypeStruct((s, 2 * D_MODEL), BF16)],
        scratch_shapes=[pltpu.VMEM((D_MODEL // HEAD_DIM, tm, HEAD_DIM), F32), pltpu.VMEM((tm, D_MODEL), BF16),
                        pltpu.VMEM((tm, HEAD_DIM), F32), pltpu.VMEM((tm, HEAD_DIM), F32)],
        compiler_params=_params("parallel"),
        name="proj",
    )(x, pos_row, g_mix, inv_freq_col, w_in)


def _attn_kernel(qkv_ref, prev_ref, o_ref, l_ref, o_scr, l_scr, *, d, nsub):
    not_first_block = pl.program_id(0) > 0
    qi = lax.broadcasted_iota(jnp.int32, (BLOCK, 2 * BLOCK), 0)
    kj = lax.broadcasted_iota(jnp.int32, (BLOCK, 2 * BLOCK), 1)
    rel = BLOCK + qi - kj
    band = (rel >= 0) & (rel <= BLOCK)
    band_first = band & ((kj >= BLOCK) | not_first_block)
    lane = lax.broadcasted_iota(jnp.int32, (BLOCK, HEAD_DIM), 1)

    def residue(r):
        for b in range(nsub):
            rows = slice(b * BLOCK, (b + 1) * BLOCK)
            mask = band_first if b == 0 else band
            token_rows = pl.ds(b * BLOCK * d + r, BLOCK, stride=d) if d > 1 else rows
            lse_tile = None
            for h in range(HEADS_PER_GROUP):
                hs = slice(h * HEAD_DIM, (h + 1) * HEAD_DIM)
                ks = slice(GROUP_WIDTH + h * HEAD_DIM, GROUP_WIDTH + (h + 1) * HEAD_DIM)
                vs = slice(2 * GROUP_WIDTH + h * HEAD_DIM, 2 * GROUP_WIDTH + (h + 1) * HEAD_DIM)
                q = qkv_ref[r, rows, hs]
                if b == 0:
                    kw = jnp.concatenate([prev_ref[r, :, ks], qkv_ref[r, rows, ks]], axis=0)
                    vw = jnp.concatenate([prev_ref[r, :, vs], qkv_ref[r, rows, vs]], axis=0)
                else:
                    win = slice((b - 1) * BLOCK, (b + 1) * BLOCK)
                    kw = qkv_ref[r, win, ks]
                    vw = qkv_ref[r, win, vs]
                s = lax.dot_general(q, kw, (((1,), (1,)), ((), ())), preferred_element_type=F32)
                s = jnp.where(mask, s, MASK_VALUE)
                m = jnp.max(s, axis=-1, keepdims=True)
                p = jnp.exp2(s - m)
                den = jnp.sum(p, axis=-1, keepdims=True)
                o_scr[h, token_rows, :] = _bdot(p.astype(BF16), vw) / den
                lse = jnp.broadcast_to((m + jnp.log2(den)) * math.log(2.0), (BLOCK, HEAD_DIM))
                lse_tile = lse if h == 0 else jnp.where(lane >= h * LSE_LANES, lse, lse_tile)
            l_scr[token_rows, :] = lse_tile

    per_iter = max(1, min(d, ATTN_TILES_PER_BODY // (nsub * HEADS_PER_GROUP)))
    if per_iter == d:
        for r in range(d):
            residue(r)
    else:
        def body(i, carry):
            for j in range(per_iter):
                residue(i * per_iter + j)
            return carry
        lax.fori_loop(0, d // per_iter, body, 0)

    for h in range(HEADS_PER_GROUP):
        o_ref[:, h * HEAD_DIM:(h + 1) * HEAD_DIM] = o_scr[h].astype(BF16)
    l_ref[...] = l_scr[...]


def _attn_group(qkv, dilation):
    sub_len = qkv.shape[1]
    s = sub_len * dilation
    step_tokens = min(ATTN_TOKENS[dilation], s)
    qb = step_tokens // dilation
    nsub = qb // BLOCK
    cur = pl.BlockSpec((dilation, qb, 3 * GROUP_WIDTH), lambda i: (0, i, 0))
    prev = pl.BlockSpec((dilation, BLOCK, 3 * GROUP_WIDTH), lambda i: (0, jnp.maximum(i * nsub - 1, 0), 0))
    return pl.pallas_call(
        functools.partial(_attn_kernel, d=dilation, nsub=nsub),
        grid=(s // step_tokens,),
        in_specs=[cur, prev],
        out_specs=[pl.BlockSpec((step_tokens, GROUP_WIDTH), lambda i: (i, 0)),
                   pl.BlockSpec((step_tokens, HEAD_DIM), lambda i: (i, 0))],
        out_shape=[jax.ShapeDtypeStruct((s, GROUP_WIDTH), BF16), jax.ShapeDtypeStruct((s, HEAD_DIM), F32)],
        scratch_shapes=[pltpu.VMEM((HEADS_PER_GROUP, step_tokens, HEAD_DIM), F32),
                        pltpu.VMEM((step_tokens, HEAD_DIM), F32)],
        compiler_params=_params("arbitrary"),
        name=f"attn_d{dilation}",
    )(qkv, qkv)


def _discretize(lr, li, log_dt):
    dt = jnp.exp(log_dt)
    mag = jnp.exp(lr * dt)
    bar_re = mag * jnp.cos(li * dt)
    bar_im = mag * jnp.sin(li * dt)
    nr = bar_re - 1.0
    ni = bar_im
    den = lr * lr + li * li
    return bar_re, bar_im, (nr * lr + ni * li) / den, (ni * lr - nr * li) / den


def _expand_block_diag(compact, n_blocks):
    rows, b = compact.shape
    a = rows // n_blocks
    wide = n_blocks * b
    src_lane = lax.broadcasted_iota(jnp.int32, (b, wide), 0)
    dst_lane = lax.broadcasted_iota(jnp.int32, (b, wide), 1)
    tiled = _bdot(compact, (dst_lane % b == src_lane).astype(BF16))
    row_block = lax.broadcasted_iota(jnp.int32, (rows, wide), 0) // a
    col_block = lax.broadcasted_iota(jnp.int32, (rows, wide), 1) // b
    return jnp.where(row_block == col_block, tiled, 0.0).astype(BF16)


def _ssm_prep_kernel(lr_ref, li_ref, logdt_ref, lr_rep_ref, li_rep_ref, logdt_rep_ref, b_re_ref, b_im_ref,
                     c_re_ref, c_im_ref,
                     bmat_ref, cre_ref, cim_ref, e_re_ref, e_im_ref, einv_re_ref, einv_im_ref, lam_ref):
    _, _, z_re, z_im = _discretize(lr_rep_ref[...], li_rep_ref[...], logdt_rep_ref[...])
    b_re = b_re_ref[...]
    b_im = b_im_ref[...]
    bb_re = (z_re * b_re - z_im * b_im).astype(BF16)
    bb_im = (z_re * b_im + z_im * b_re).astype(BF16)
    for hf in range(SSM_HALVES):
        rows = slice(hf * HALF_WIDTH, (hf + 1) * HALF_WIDTH)
        bmat_ref[rows, :HALF_STATES] = _expand_block_diag(bb_re[rows], HALF_GROUPS)
        bmat_ref[rows, HALF_STATES:] = _expand_block_diag(bb_im[rows], HALF_GROUPS)
        srows = slice(hf * HALF_STATES, (hf + 1) * HALF_STATES)
        cre_ref[srows, :] = _expand_block_diag(c_re_ref[srows, :].astype(BF16), HALF_GROUPS)
        cim_ref[srows, :] = _expand_block_diag(c_im_ref[srows, :].astype(BF16), HALF_GROUPS)

    lr = lr_ref[...]
    li = li_ref[...]
    dt = jnp.exp(logdt_ref[...])
    bar_re, bar_im, _, _ = _discretize(lr, li, logdt_ref[...])
    lam_ref[0:1, :] = bar_re
    lam_ref[1:2, :] = bar_im
    t = lax.broadcasted_iota(jnp.int32, (SSM_CHUNK, N_STATES), 0).astype(F32)
    grow = jnp.exp(t * (lr * dt))
    theta = t * (li * dt)
    c = jnp.cos(theta)
    s = jnp.sin(theta)
    e_re_ref[...] = grow * c
    e_im_ref[...] = grow * s
    shrink = jnp.exp(-t * (lr * dt))
    einv_re_ref[...] = shrink * c
    einv_im_ref[...] = -(shrink * s)


def _ssm_prep(a_re, a_im, log_dt, b_re, b_im, c_re, c_im):
    row = lambda v: v.reshape(1, N_STATES)
    rep = lambda v: jnp.repeat(v, SSM_GROUP, axis=0)
    logdt_gp = jnp.broadcast_to(log_dt[:, None], (SSM_GROUPS, SSM_STATE))
    chan_state = lambda b: jnp.swapaxes(b, 1, 2).reshape(SSM_WIDTH, SSM_STATE)
    state_chan = lambda c: jnp.swapaxes(c, 1, 2).reshape(N_STATES, SSM_GROUP)
    tab = jax.ShapeDtypeStruct((SSM_CHUNK, N_STATES), F32)
    cmat = jax.ShapeDtypeStruct((N_STATES, HALF_WIDTH), BF16)
    return pl.pallas_call(
        _ssm_prep_kernel,
        out_shape=[jax.ShapeDtypeStruct((SSM_WIDTH, 2 * HALF_STATES), BF16), cmat, cmat, tab, tab, tab, tab,
                   jax.ShapeDtypeStruct((2, N_STATES), F32)],
        compiler_params=pltpu.CompilerParams(vmem_limit_bytes=V7X_VMEM_LIMIT_BYTES),
        name="ssm_prep",
    )(row(a_re), row(a_im), row(logdt_gp), rep(a_re), rep(a_im), rep(logdt_gp),
      chan_state(b_re), chan_state(b_im), state_chan(c_re), state_chan(c_im))


def _gelu_tanh(x):
    return 0.5 * x * (1.0 + jnp.tanh(math.sqrt(2.0 / math.pi) * (x + 0.044715 * (x * x * x))))


def _ssm_kernel(u_ref, bmat_ref, cre_ref, cim_ref, tri_ref, e_re_ref, e_im_ref, einv_re_ref, einv_im_ref,
                lam_ref, d_ref, y_ref, h_re_scr, h_im_scr, carry_scr):
    @pl.when(pl.program_id(0) == 0)
    def _():
        carry_scr[...] = jnp.zeros_like(carry_scr)

    u = u_ref[...]
    ub = u.astype(BF16)
    for hf in range(SSM_HALVES):
        chans = slice(hf * HALF_WIDTH, (hf + 1) * HALF_WIDTH)
        y = d_ref[:, chans] * u[:, chans]
        for blk in range(HALF_STATES // STATE_BLOCK):
            local = slice(blk * STATE_BLOCK, (blk + 1) * STATE_BLOCK)
            local_im = slice(HALF_STATES + blk * STATE_BLOCK, HALF_STATES + (blk + 1) * STATE_BLOCK)
            states = slice(hf * HALF_STATES + blk * STATE_BLOCK, hf * HALF_STATES + (blk + 1) * STATE_BLOCK)
            bu_re_all = _bdot(ub[:, chans], bmat_ref[chans, local])
            bu_im_all = _bdot(ub[:, chans], bmat_ref[chans, local_im])
            lam_re = lam_ref[0:1, states]
            lam_im = lam_ref[1:2, states]
            carry_re = carry_scr[0:1, states]
            carry_im = carry_scr[1:2, states]
            for c in range(u.shape[0] // SSM_CHUNK):
                rows = slice(c * SSM_CHUNK, (c + 1) * SSM_CHUNK)
                bu_re = bu_re_all[rows]
                bu_im = bu_im_all[rows]
                einv_re = einv_re_ref[:, states]
                einv_im = einv_im_ref[:, states]
                x = jnp.concatenate([(bu_re * einv_re - bu_im * einv_im).astype(BF16),
                                     (bu_re * einv_im + bu_im * einv_re).astype(BF16)], axis=1)
                a = _bdot(tri_ref[...], x)
                a_re = a[:, :STATE_BLOCK] + carry_re
                a_im = a[:, STATE_BLOCK:] + carry_im
                e_re = e_re_ref[:, states]
                e_im = e_im_ref[:, states]
                h_re = e_re * a_re - e_im * a_im
                h_im = e_re * a_im + e_im * a_re
                h_re_scr[rows, :] = h_re.astype(BF16)
                h_im_scr[rows, :] = h_im.astype(BF16)
                last_re = h_re[SSM_CHUNK - 1:SSM_CHUNK, :]
                last_im = h_im[SSM_CHUNK - 1:SSM_CHUNK, :]
                carry_re = lam_re * last_re - lam_im * last_im
                carry_im = lam_re * last_im + lam_im * last_re
            carry_scr[0:1, states] = carry_re
            carry_scr[1:2, states] = carry_im
            y = y + _bdot(h_re_scr[...], cre_ref[states, :]) - _bdot(h_im_scr[...], cim_ref[states, :])
        y_ref[:, chans] = _gelu_tanh(y).astype(BF16)


def _ssm(u, bmat, cre, cim, tri, e_re, e_im, einv_re, einv_im, lam, d_row):
    s = u.shape[0]
    tm = min(SSM_ROWS, s)
    row = pl.BlockSpec((tm, SSM_WIDTH), lambda i: (i, 0))
    consts = (bmat, cre, cim, tri, e_re, e_im, einv_re, einv_im, lam, d_row)
    return pl.pallas_call(
        _ssm_kernel,
        grid=(s // tm,),
        in_specs=[row] + [_resident(c.shape) for c in consts],
        out_specs=row,
        out_shape=jax.ShapeDtypeStruct((s, SSM_WIDTH), BF16),
        scratch_shapes=[pltpu.VMEM((tm, STATE_BLOCK), BF16), pltpu.VMEM((tm, STATE_BLOCK), BF16),
                        pltpu.VMEM((2, N_STATES), F32)],
        compiler_params=_params("arbitrary"),
        name="ssm",
    )(u, *consts)


def _mix_kernel(x_ref, o0_ref, o1_ref, o2_ref, l0_ref, l1_ref, l2_ref, y_ref, gates_ref,
                wap_ref, wa_ref, wb_ref, wout_ref, h_ref):
    for rb in range(x_ref.shape[0] // MIX_SUB_ROWS):
        rows = slice(rb * MIX_SUB_ROWS, (rb + 1) * MIX_SUB_ROWS)
        ls = (l0_ref[rows, :], l1_ref[rows, :], l2_ref[rows, :])
        l_max = jnp.maximum(jnp.maximum(ls[0], ls[1]), ls[2])
        es = [jnp.exp(l - l_max) for l in ls]
        inv = 1.0 / (es[0] + es[1] + es[2])
        heads = []
        for h in range(HEADS_PER_GROUP):
            hs = slice(h * HEAD_DIM, (h + 1) * HEAD_DIM)
            col = slice(h * LSE_LANES, h * LSE_LANES + 1)
            heads.append(sum((e * inv)[:, col] * o_ref[rows, hs].astype(F32)
                             for e, o_ref in zip(es, (o0_ref, o1_ref, o2_ref))))
        attn = jnp.concatenate(heads, axis=1).astype(BF16)
        attn_d = _bdot(attn, wap_ref[...])
        y = y_ref[rows, :]
        ssm_d = _bdot(y, wa_ref[...]) * jax.nn.sigmoid(_bdot(y, wb_ref[...]))
        mix = (gates_ref[rows, :D_MODEL].astype(F32) * attn_d
               + gates_ref[rows, D_MODEL:].astype(F32) * ssm_d)
        h_ref[rows, :] = x_ref[rows, :] + _bdot(mix.astype(BF16), wout_ref[...])


def _mix(x, os, ls, y, gates, wap, wa, wb, wout):
    s = x.shape[0]
    tm = min(MIX_ROWS, s)
    row = lambda w: pl.BlockSpec((tm, w), lambda i: (i, 0))
    weights = (wap, wa, wb, wout)
    return pl.pallas_call(
        _mix_kernel,
        grid=(s // tm,),
        in_specs=[row(D_MODEL)] + [row(GROUP_WIDTH)] * 3 + [row(HEAD_DIM)] * 3
        + [row(SSM_WIDTH), row(2 * D_MODEL)] + [_resident(w.shape) for w in weights],
        out_specs=row(D_MODEL),
        out_shape=jax.ShapeDtypeStruct((s, D_MODEL), F32),
        compiler_params=_params("parallel"),
        name="mix",
    )(x, *os, *ls, y, gates, *weights)


def _ffn_kernel(h_ref, p_ref, gffn_ref, wg_ref, wu_ref, wd_ref, wpg_ref, wpp_ref, gfin_ref,
                out_ref, acc_scr):
    h = h_ref[...]
    n2 = _rmsnorm(h, gffn_ref[...]).astype(BF16)
    for idx, (c0, width) in enumerate(FFN_CHUNKS):
        gate = _bdot(n2, wg_ref[:, c0:c0 + width])
        up = _bdot(n2, wu_ref[:, c0:c0 + width])
        act = (gate * jax.nn.sigmoid(gate) * up).astype(BF16)
        part = _bdot(act, wd_ref[c0:c0 + width, :])
        if idx == 0:
            acc_scr[...] = h + part
        else:
            acc_scr[...] += part
    h2 = acc_scr[...]
    ple = jax.nn.sigmoid(_bdot(h2.astype(BF16), wpg_ref[...])) * _bdot(p_ref[...].astype(BF16), wpp_ref[...])
    out_ref[...] = _rmsnorm(h2 + ple, gfin_ref[...])


def _ffn(h, p, g_ffn, wg, wu, wd, wpg, wpp, g_final):
    s = h.shape[0]
    tm = min(FFN_ROWS, s)
    row = lambda w: pl.BlockSpec((tm, w), lambda i: (i, 0))
    consts = (g_ffn, wg, wu, wd, wpg, wpp, g_final)
    return pl.pallas_call(
        _ffn_kernel,
        grid=(s // tm,),
        in_specs=[row(D_MODEL), row(PLE_DIM)] + [_resident(c.shape) for c in consts],
        out_specs=row(D_MODEL),
        out_shape=jax.ShapeDtypeStruct((s, D_MODEL), F32),
        scratch_shapes=[pltpu.VMEM((tm, D_MODEL), F32)],
        compiler_params=_params("parallel"),
        name="ffn",
    )(h, p, *consts)


def _layer(x, p, positions, g_mix, w_in, a_re, a_im, log_dt, b_re, b_im, c_re, c_im, d_skip,
           w_attn_proj, w_glu_a, w_glu_b, w_out, g_ffn, w_ffn_gate, w_ffn_up, w_ffn_down,
           w_ple_gate, w_ple_proj, g_final):
    bf = lambda w: w.astype(BF16)
    row = lambda v: v.reshape(1, -1).astype(F32)

    inv_freq = ROPE_THETA ** (-jnp.arange(ROPE_HALF, dtype=F32) * 2.0 / ROPE_DIM)
    outs = _proj(x, positions.reshape(1, -1), row(g_mix), inv_freq.reshape(ROPE_HALF, 1), bf(w_in))
    qkvs, (u, gates) = outs[:N_GROUPS], outs[N_GROUPS:]

    attn_os, attn_ls = zip(*[_attn_group(qkvs[g], d) for g, d in enumerate(ATTN_DILATIONS)])

    bmat, cre, cim, e_re, e_im, einv_re, einv_im, lam = _ssm_prep(a_re, a_im, log_dt, b_re, b_im, c_re, c_im)
    tri = jnp.tril(jnp.ones((SSM_CHUNK, SSM_CHUNK), F32)).astype(BF16)
    y = _ssm(u, bmat, cre, cim, tri, e_re, e_im, einv_re, einv_im, lam, row(d_skip))

    h1 = _mix(x, attn_os, attn_ls, y, gates, bf(w_attn_proj), bf(w_glu_a), bf(w_glu_b), bf(w_out))
    return _ffn(h1, p, row(g_ffn), bf(w_ffn_gate), bf(w_ffn_up), bf(w_ffn_down),
                bf(w_ple_gate), bf(w_ple_proj), row(g_final))


def kernel(x, p, positions, g_mix, w_in, a_re, a_im, log_dt, b_re, b_im, c_re, c_im, d_skip,
           w_attn_proj, w_glu_a, w_glu_b, w_out, g_ffn, w_ffn_gate, w_ffn_up, w_ffn_down,
           w_ple_gate, w_ple_proj, g_final):
    batch, depth = x.shape[0], p.shape[0]
    assert batch == 1 and depth == 1, "kernel supports the stated BATCH=1, DEPTH=1 problem"
    out = _layer(x[0], p[0, 0], positions[0], g_mix[0], w_in[0], a_re[0], a_im[0], log_dt[0],
                 b_re[0], b_im[0], c_re[0], c_im[0], d_skip[0], w_attn_proj[0], w_glu_a[0],
                 w_glu_b[0], w_out[0], g_ffn[0], w_ffn_gate[0], w_ffn_up[0], w_ffn_down[0],
                 w_ple_gate[0], w_ple_proj[0], g_final)
    return out[None]
```

```python
import functools
import math

import jax
import jax.numpy as jnp
from jax import lax
from jax.experimental import pallas as pl
from jax.experimental.pallas import tpu as pltpu

F32 = jnp.float32
BF16 = jnp.bfloat16

D_MODEL = 1024
HEAD_DIM = 128
HEADS_PER_GROUP = 4
GROUP_WIDTH = HEADS_PER_GROUP * HEAD_DIM
ATTN_DILATIONS = (1, 4, 16)
N_GROUPS = len(ATTN_DILATIONS)
QK_WIDTH = N_GROUPS * GROUP_WIDTH
BLOCK = 128
ROPE_THETA = 500000.0
ROPE_DIM = HEAD_DIM // 4
ROPE_HALF = ROPE_DIM // 2
SSM_WIDTH = 512
SSM_GROUP = 16
SSM_GROUPS = SSM_WIDTH // SSM_GROUP
SSM_STATE = 64
N_STATES = SSM_GROUPS * SSM_STATE
SSM_HALVES = 2
HALF_GROUPS = SSM_GROUPS // SSM_HALVES
HALF_WIDTH = SSM_WIDTH // SSM_HALVES
HALF_STATES = N_STATES // SSM_HALVES
STATE_BLOCK = 256
D_FF = 2816
PLE_DIM = 256
EPS = 1e-6
MASK_VALUE = -1e30

V7X_VMEM_LIMIT_BYTES = 56 * 1024 * 1024

PROJ_ROWS = 512
ATTN_TOKENS = {1: 1024, 4: 2048, 16: 2048}
ATTN_TILES_PER_BODY = 32
LSE_LANES = HEAD_DIM // HEADS_PER_GROUP
SSM_CHUNK = 128
SSM_ROWS = 512
MIX_ROWS = 1024
MIX_SUB_ROWS = 256
FFN_ROWS = 512
FFN_SUB_ROWS = 256
FFN_CHUNKS = ((0, 1024), (1024, 1024), (2048, 768))


def _resident(shape):
    return pl.BlockSpec(shape, lambda *_: (0,) * len(shape), pipeline_mode=pl.Buffered(1))


def _params(*semantics):
    return pltpu.CompilerParams(dimension_semantics=semantics,
                                vmem_limit_bytes=V7X_VMEM_LIMIT_BYTES)


def _rmsnorm(x, g):
    return (x * lax.rsqrt(jnp.mean(x * x, axis=-1, keepdims=True) + EPS)) * g


def _bdot(a, b):
    return jnp.dot(a, b, preferred_element_type=F32)


def _cast_plan(weights, steps):
    specs = [pl.BlockSpec((w.shape[0] // steps, w.shape[1]), lambda i: (i, 0)) for w in weights]
    shapes = [jax.ShapeDtypeStruct(w.shape, BF16) for w in weights]
    return specs, shapes


def _cast_blocks(in_refs, out_refs):
    for src, dst in zip(in_refs, out_refs):
        dst[...] = src[...].astype(BF16)


def _proj_kernel(*refs, n_cast):
    x_ref, pos_ref, g_ref, invf_ref, w_ref = refs[:5]
    cast_in, refs = refs[5:5 + n_cast], refs[5 + n_cast:]
    qkv0, qkv1, qkv2, u_ref, gates_ref = refs[:5]
    cast_out, (n_scr, nperm_scr, cos_scr, sin_scr) = refs[5:5 + n_cast], refs[5 + n_cast:]
    _cast_blocks(cast_in, cast_out)
    xn = _rmsnorm(x_ref[...], g_ref[...])
    n = xn.astype(BF16)
    rows = xn.shape[0]
    for c in range(D_MODEL // HEAD_DIM):
        n_scr[c] = xn[:, c * HEAD_DIM:(c + 1) * HEAD_DIM]

    ang = invf_ref[...] * pos_ref[...].astype(F32)
    cos_t = jnp.cos(ang)
    sin_t = jnp.sin(ang)
    rest = (HEAD_DIM - ROPE_DIM, rows)
    cos_scr[...] = jnp.concatenate([cos_t, cos_t, jnp.ones(rest, F32)], axis=0).T
    sin_scr[...] = jnp.concatenate([-sin_t, sin_t, jnp.zeros(rest, F32)], axis=0).T

    lane = lax.broadcasted_iota(jnp.int32, (rows, HEAD_DIM), 1)
    first_half = lane < ROPE_HALF
    scale = math.log2(math.e) / math.sqrt(HEAD_DIM)

    for g, (d, qkv_ref) in enumerate(zip(ATTN_DILATIONS, (qkv0, qkv1, qkv2))):
        def residue_major(ref_2d):
            if d == 1:
                return ref_2d[...]
            return jnp.concatenate([ref_2d[pl.ds(r, rows // d, stride=d), :] for r in range(d)], axis=0)

        cos = residue_major(cos_scr)
        sin_signed = residue_major(sin_scr)
        if d == 1:
            ng = n
        else:
            for c in range(D_MODEL // HEAD_DIM):
                nperm_scr[:, c * HEAD_DIM:(c + 1) * HEAD_DIM] = residue_major(n_scr.at[c]).astype(BF16)
            ng = nperm_scr[...]

        def rotary(t):
            partner = jnp.where(first_half,
                                pltpu.roll(t, HEAD_DIM - ROPE_HALF, 1),
                                pltpu.roll(t, ROPE_HALF, 1))
            return t * cos + partner * sin_signed

        c0 = g * GROUP_WIDTH
        zq = _bdot(ng, w_ref[:, c0:c0 + GROUP_WIDTH])
        zk = _bdot(ng, w_ref[:, QK_WIDTH + c0:QK_WIDTH + c0 + GROUP_WIDTH])
        zv = _bdot(ng, w_ref[:, 2 * QK_WIDTH + c0:2 * QK_WIDTH + c0 + GROUP_WIDTH])
        for h in range(HEADS_PER_GROUP):
            hs = slice(h * HEAD_DIM, (h + 1) * HEAD_DIM)
            ks = slice(GROUP_WIDTH + h * HEAD_DIM, GROUP_WIDTH + (h + 1) * HEAD_DIM)
            qkv_ref[:, :, hs] = (rotary(zq[:, hs]) * scale).astype(BF16).reshape(d, rows // d, HEAD_DIM)
            qkv_ref[:, :, ks] = rotary(zk[:, hs]).astype(BF16).reshape(d, rows // d, HEAD_DIM)
        qkv_ref[:, :, 2 * GROUP_WIDTH:] = zv.astype(BF16).reshape(d, rows // d, GROUP_WIDTH)
    o1 = 3 * QK_WIDTH
    u_ref[...] = _bdot(n, w_ref[:, o1:o1 + SSM_WIDTH])
    o2 = o1 + SSM_WIDTH
    gates_ref[...] = jax.nn.sigmoid(_bdot(n, w_ref[:, o2:o2 + 2 * D_MODEL])).astype(BF16)


def _proj(x, pos_row, g_mix, inv_freq_col, w_in, later_weights):
    s = x.shape[0]
    tm = min(PROJ_ROWS, s)
    row = lambda w: pl.BlockSpec((tm, w), lambda i: (i, 0))
    cast_specs, cast_shapes = _cast_plan(later_weights, s // tm)
    grp_specs, grp_shapes = [], []
    for d in ATTN_DILATIONS:
        grp_specs.append(pl.BlockSpec((d, tm // d, 3 * GROUP_WIDTH), lambda i: (0, i, 0)))
        grp_shapes.append(jax.ShapeDtypeStruct((d, s // d, 3 * GROUP_WIDTH), BF16))
    return pl.pallas_call(
        functools.partial(_proj_kernel, n_cast=len(later_weights)),
        grid=(s // tm,),
        in_specs=[row(D_MODEL), pl.BlockSpec((1, tm), lambda i: (0, i)), _resident((1, D_MODEL)),
                  _resident((ROPE_HALF, 1)), _resident(w_in.shape)] + cast_specs,
        out_specs=grp_specs + [row(SSM_WIDTH), row(2 * D_MODEL)] + cast_specs,
        out_shape=grp_shapes + [jax.ShapeDtypeStruct((s, SSM_WIDTH), F32),
                                jax.ShapeDtypeStruct((s, 2 * D_MODEL), BF16)] + cast_shapes,
        scratch_shapes=[pltpu.VMEM((D_MODEL // HEAD_DIM, tm, HEAD_DIM), F32), pltpu.VMEM((tm, D_MODEL), BF16),
                        pltpu.VMEM((tm, HEAD_DIM), F32), pltpu.VMEM((tm, HEAD_DIM), F32)],
        compiler_params=_params("parallel"),
        name="proj",
    )(x, pos_row, g_mix, inv_freq_col, w_in, *later_weights)


def _attn_kernel(qkv_ref, prev_ref, o_ref, l_ref, o_scr, l_scr, *, d, nsub):
    not_first_block = pl.program_id(0) > 0
    qi = lax.broadcasted_iota(jnp.int32, (BLOCK, 2 * BLOCK), 0)
    kj = lax.broadcasted_iota(jnp.int32, (BLOCK, 2 * BLOCK), 1)
    rel = BLOCK + qi - kj
    band = (rel >= 0) & (rel <= BLOCK)
    band_first = band & ((kj >= BLOCK) | not_first_block)
    lane = lax.broadcasted_iota(jnp.int32, (BLOCK, HEAD_DIM), 1)

    def residue(r):
        for b in range(nsub):
            rows = slice(b * BLOCK, (b + 1) * BLOCK)
            mask = band_first if b == 0 else band
            token_rows = pl.ds(b * BLOCK * d + r, BLOCK, stride=d) if d > 1 else rows
            lse_tile = None
            for h in range(HEADS_PER_GROUP):
                hs = slice(h * HEAD_DIM, (h + 1) * HEAD_DIM)
                ks = slice(GROUP_WIDTH + h * HEAD_DIM, GROUP_WIDTH + (h + 1) * HEAD_DIM)
                vs = slice(2 * GROUP_WIDTH + h * HEAD_DIM, 2 * GROUP_WIDTH + (h + 1) * HEAD_DIM)
                q = qkv_ref[r, rows, hs]
                if b == 0:
                    kw = jnp.concatenate([prev_ref[r, :, ks], qkv_ref[r, rows, ks]], axis=0)
                    vw = jnp.concatenate([prev_ref[r, :, vs], qkv_ref[r, rows, vs]], axis=0)
                else:
                    win = slice((b - 1) * BLOCK, (b + 1) * BLOCK)
                    kw = qkv_ref[r, win, ks]
                    vw = qkv_ref[r, win, vs]
                s = lax.dot_general(q, kw, (((1,), (1,)), ((), ())), preferred_element_type=F32)
                s = jnp.where(mask, s, MASK_VALUE)
                m = jnp.max(s, axis=-1, keepdims=True)
                p = jnp.exp2(s - m)
                den = jnp.sum(p, axis=-1, keepdims=True)
                o_scr[h, token_rows, :] = _bdot(p.astype(BF16), vw) / den
                lse = jnp.broadcast_to((m + jnp.log2(den)) * math.log(2.0), (BLOCK, HEAD_DIM))
                lse_tile = lse if h == 0 else jnp.where(lane >= h * LSE_LANES, lse, lse_tile)
            l_scr[token_rows, :] = lse_tile

    per_iter = max(1, min(d, ATTN_TILES_PER_BODY // (nsub * HEADS_PER_GROUP)))
    if per_iter == d:
        for r in range(d):
            residue(r)
    else:
        def body(i, carry):
            for j in range(per_iter):
                residue(i * per_iter + j)
            return carry
        lax.fori_loop(0, d // per_iter, body, 0)

    for h in range(HEADS_PER_GROUP):
        o_ref[:, h * HEAD_DIM:(h + 1) * HEAD_DIM] = o_scr[h].astype(BF16)
    l_ref[...] = l_scr[...]


def _attn_group(qkv, dilation):
    sub_len = qkv.shape[1]
    s = sub_len * dilation
    step_tokens = min(ATTN_TOKENS[dilation], s)
    qb = step_tokens // dilation
    nsub = qb // BLOCK
    cur = pl.BlockSpec((dilation, qb, 3 * GROUP_WIDTH), lambda i: (0, i, 0))
    prev = pl.BlockSpec((dilation, BLOCK, 3 * GROUP_WIDTH), lambda i: (0, jnp.maximum(i * nsub - 1, 0), 0))
    return pl.pallas_call(
        functools.partial(_attn_kernel, d=dilation, nsub=nsub),
        grid=(s // step_tokens,),
        in_specs=[cur, prev],
        out_specs=[pl.BlockSpec((step_tokens, GROUP_WIDTH), lambda i: (i, 0)),
                   pl.BlockSpec((step_tokens, HEAD_DIM), lambda i: (i, 0))],
        out_shape=[jax.ShapeDtypeStruct((s, GROUP_WIDTH), BF16), jax.ShapeDtypeStruct((s, HEAD_DIM), F32)],
        scratch_shapes=[pltpu.VMEM((HEADS_PER_GROUP, step_tokens, HEAD_DIM), F32),
                        pltpu.VMEM((step_tokens, HEAD_DIM), F32)],
        compiler_params=_params("arbitrary"),
        name=f"attn_d{dilation}",
    )(qkv, qkv)


def _discretize(lr, li, log_dt):
    dt = jnp.exp(log_dt)
    mag = jnp.exp(lr * dt)
    bar_re = mag * jnp.cos(li * dt)
    bar_im = mag * jnp.sin(li * dt)
    nr = bar_re - 1.0
    ni = bar_im
    den = lr * lr + li * li
    return bar_re, bar_im, (nr * lr + ni * li) / den, (ni * lr - nr * li) / den


def _expand_block_diag(compact, n_blocks):
    rows, b = compact.shape
    a = rows // n_blocks
    wide = n_blocks * b
    src_lane = lax.broadcasted_iota(jnp.int32, (b, wide), 0)
    dst_lane = lax.broadcasted_iota(jnp.int32, (b, wide), 1)
    tiled = _bdot(compact, (dst_lane % b == src_lane).astype(BF16))
    row_block = lax.broadcasted_iota(jnp.int32, (rows, wide), 0) // a
    col_block = lax.broadcasted_iota(jnp.int32, (rows, wide), 1) // b
    return jnp.where(row_block == col_block, tiled, 0.0).astype(BF16)


def _ssm_prep_kernel(lr_ref, li_ref, logdt_ref, lr_rep_ref, li_rep_ref, logdt_rep_ref, b_re_ref, b_im_ref,
                     c_re_ref, c_im_ref,
                     bmat_ref, cre_ref, cim_ref, e_re_ref, e_im_ref, einv_re_ref, einv_im_ref, lam_ref):
    _, _, z_re, z_im = _discretize(lr_rep_ref[...], li_rep_ref[...], logdt_rep_ref[...])
    b_re = b_re_ref[...]
    b_im = b_im_ref[...]
    bb_re = (z_re * b_re - z_im * b_im).astype(BF16)
    bb_im = (z_re * b_im + z_im * b_re).astype(BF16)
    for hf in range(SSM_HALVES):
        rows = slice(hf * HALF_WIDTH, (hf + 1) * HALF_WIDTH)
        bmat_ref[rows, :HALF_STATES] = _expand_block_diag(bb_re[rows], HALF_GROUPS)
        bmat_ref[rows, HALF_STATES:] = _expand_block_diag(bb_im[rows], HALF_GROUPS)
        srows = slice(hf * HALF_STATES, (hf + 1) * HALF_STATES)
        cre_ref[srows, :] = _expand_block_diag(c_re_ref[srows, :].astype(BF16), HALF_GROUPS)
        cim_ref[srows, :] = _expand_block_diag(c_im_ref[srows, :].astype(BF16), HALF_GROUPS)

    lr = lr_ref[...]
    li = li_ref[...]
    dt = jnp.exp(logdt_ref[...])
    bar_re, bar_im, _, _ = _discretize(lr, li, logdt_ref[...])
    lam_ref[0:1, :] = bar_re
    lam_ref[1:2, :] = bar_im
    t = lax.broadcasted_iota(jnp.int32, (SSM_CHUNK, N_STATES), 0).astype(F32)
    grow = jnp.exp(t * (lr * dt))
    theta = t * (li * dt)
    c = jnp.cos(theta)
    s = jnp.sin(theta)
    e_re_ref[...] = grow * c
    e_im_ref[...] = grow * s
    shrink = jnp.exp(-t * (lr * dt))
    einv_re_ref[...] = shrink * c
    einv_im_ref[...] = -(shrink * s)


def _ssm_prep(a_re, a_im, log_dt, b_re, b_im, c_re, c_im):
    row = lambda v: v.reshape(1, N_STATES)
    rep = lambda v: jnp.repeat(v, SSM_GROUP, axis=0)
    logdt_gp = jnp.broadcast_to(log_dt[:, None], (SSM_GROUPS, SSM_STATE))
    chan_state = lambda b: jnp.swapaxes(b, 1, 2).reshape(SSM_WIDTH, SSM_STATE)
    state_chan = lambda c: jnp.swapaxes(c, 1, 2).reshape(N_STATES, SSM_GROUP)
    tab = jax.ShapeDtypeStruct((SSM_CHUNK, N_STATES), F32)
    cmat = jax.ShapeDtypeStruct((N_STATES, HALF_WIDTH), BF16)
    return pl.pallas_call(
        _ssm_prep_kernel,
        out_shape=[jax.ShapeDtypeStruct((SSM_WIDTH, 2 * HALF_STATES), BF16), cmat, cmat, tab, tab, tab, tab,
                   jax.ShapeDtypeStruct((2, N_STATES), F32)],
        compiler_params=pltpu.CompilerParams(vmem_limit_bytes=V7X_VMEM_LIMIT_BYTES),
        name="ssm_prep",
    )(row(a_re), row(a_im), row(logdt_gp), rep(a_re), rep(a_im), rep(logdt_gp),
      chan_state(b_re), chan_state(b_im), state_chan(c_re), state_chan(c_im))


def _gelu_tanh(x):
    return 0.5 * x * (1.0 + jnp.tanh(math.sqrt(2.0 / math.pi) * (x + 0.044715 * (x * x * x))))


def _ssm_kernel(u_ref, bmat_ref, cre_ref, cim_ref, tri_ref, e_re_ref, e_im_ref, einv_re_ref, einv_im_ref,
                lam_ref, d_ref, y_ref, h_re_scr, h_im_scr, carry_scr):
    @pl.when(pl.program_id(0) == 0)
    def _():
        carry_scr[...] = jnp.zeros_like(carry_scr)

    u = u_ref[...]
    ub = u.astype(BF16)
    for hf in range(SSM_HALVES):
        chans = slice(hf * HALF_WIDTH, (hf + 1) * HALF_WIDTH)
        y = d_ref[:, chans] * u[:, chans]
        for blk in range(HALF_STATES // STATE_BLOCK):
            local = slice(blk * STATE_BLOCK, (blk + 1) * STATE_BLOCK)
            local_im = slice(HALF_STATES + blk * STATE_BLOCK, HALF_STATES + (blk + 1) * STATE_BLOCK)
            states = slice(hf * HALF_STATES + blk * STATE_BLOCK, hf * HALF_STATES + (blk + 1) * STATE_BLOCK)
            bu_re_all = _bdot(ub[:, chans], bmat_ref[chans, local])
            bu_im_all = _bdot(ub[:, chans], bmat_ref[chans, local_im])
            lam_re = lam_ref[0:1, states]
            lam_im = lam_ref[1:2, states]
            carry_re = carry_scr[0:1, states]
            carry_im = carry_scr[1:2, states]
            for c in range(u.shape[0] // SSM_CHUNK):
                rows = slice(c * SSM_CHUNK, (c + 1) * SSM_CHUNK)
                bu_re = bu_re_all[rows]
                bu_im = bu_im_all[rows]
                einv_re = einv_re_ref[:, states]
                einv_im = einv_im_ref[:, states]
                x = jnp.concatenate([(bu_re * einv_re - bu_im * einv_im).astype(BF16),
                                     (bu_re * einv_im + bu_im * einv_re).astype(BF16)], axis=1)
                a = _bdot(tri_ref[...], x)
                a_re = a[:, :STATE_BLOCK] + carry_re
                a_im = a[:, STATE_BLOCK:] + carry_im
                e_re = e_re_ref[:, states]
                e_im = e_im_ref[:, states]
                h_re = e_re * a_re - e_im * a_im
                h_im = e_re * a_im + e_im * a_re
                h_re_scr[rows, :] = h_re.astype(BF16)
                h_im_scr[rows, :] = h_im.astype(BF16)
                last_re = h_re[SSM_CHUNK - 1:SSM_CHUNK, :]
                last_im = h_im[SSM_CHUNK - 1:SSM_CHUNK, :]
                carry_re = lam_re * last_re - lam_im * last_im
                carry_im = lam_re * last_im + lam_im * last_re
            carry_scr[0:1, states] = carry_re
            carry_scr[1:2, states] = carry_im
            y = y + _bdot(h_re_scr[...], cre_ref[states, :]) - _bdot(h_im_scr[...], cim_ref[states, :])
        y_ref[:, chans] = _gelu_tanh(y).astype(BF16)


def _ssm(u, bmat, cre, cim, tri, e_re, e_im, einv_re, einv_im, lam, d_row):
    s = u.shape[0]
    tm = min(SSM_ROWS, s)
    row = pl.BlockSpec((tm, SSM_WIDTH), lambda i: (i, 0))
    consts = (bmat, cre, cim, tri, e_re, e_im, einv_re, einv_im, lam, d_row)
    return pl.pallas_call(
        _ssm_kernel,
        grid=(s // tm,),
        in_specs=[row] + [_resident(c.shape) for c in consts],
        out_specs=row,
        out_shape=jax.ShapeDtypeStruct((s, SSM_WIDTH), BF16),
        scratch_shapes=[pltpu.VMEM((tm, STATE_BLOCK), BF16), pltpu.VMEM((tm, STATE_BLOCK), BF16),
                        pltpu.VMEM((2, N_STATES), F32)],
        compiler_params=_params("arbitrary"),
        name="ssm",
    )(u, *consts)


def _mix_kernel(*refs, n_cast):
    x_ref, o0_ref, o1_ref, o2_ref, l0_ref, l1_ref, l2_ref, y_ref, gates_ref = refs[:9]
    wap_ref, wa_ref, wb_ref, wout_ref = refs[9:13]
    cast_in, h_ref, cast_out = refs[13:13 + n_cast], refs[13 + n_cast], refs[14 + n_cast:]
    _cast_blocks(cast_in, cast_out)
    for rb in range(x_ref.shape[0] // MIX_SUB_ROWS):
        rows = slice(rb * MIX_SUB_ROWS, (rb + 1) * MIX_SUB_ROWS)
        ls = (l0_ref[rows, :], l1_ref[rows, :], l2_ref[rows, :])
        l_max = jnp.maximum(jnp.maximum(ls[0], ls[1]), ls[2])
        es = [jnp.exp(l - l_max) for l in ls]
        inv = 1.0 / (es[0] + es[1] + es[2])
        heads = []
        for h in range(HEADS_PER_GROUP):
            hs = slice(h * HEAD_DIM, (h + 1) * HEAD_DIM)
            col = slice(h * LSE_LANES, h * LSE_LANES + 1)
            heads.append(sum((e * inv)[:, col] * o_ref[rows, hs].astype(F32)
                             for e, o_ref in zip(es, (o0_ref, o1_ref, o2_ref))))
        attn = jnp.concatenate(heads, axis=1).astype(BF16)
        attn_d = _bdot(attn, wap_ref[...])
        y = y_ref[rows, :]
        ssm_d = _bdot(y, wa_ref[...]) * jax.nn.sigmoid(_bdot(y, wb_ref[...]))
        mix = (gates_ref[rows, :D_MODEL].astype(F32) * attn_d
               + gates_ref[rows, D_MODEL:].astype(F32) * ssm_d)
        h_ref[rows, :] = x_ref[rows, :] + _bdot(mix.astype(BF16), wout_ref[...])


def _mix(x, os, ls, y, gates, wap, wa, wb, wout, later_weights):
    s = x.shape[0]
    tm = min(MIX_ROWS, s)
    row = lambda w: pl.BlockSpec((tm, w), lambda i: (i, 0))
    weights = (wap, wa, wb, wout)
    cast_specs, cast_shapes = _cast_plan(later_weights, s // tm)
    return pl.pallas_call(
        functools.partial(_mix_kernel, n_cast=len(later_weights)),
        grid=(s // tm,),
        in_specs=[row(D_MODEL)] + [row(GROUP_WIDTH)] * 3 + [row(HEAD_DIM)] * 3
        + [row(SSM_WIDTH), row(2 * D_MODEL)] + [_resident(w.shape) for w in weights] + cast_specs,
        out_specs=[row(D_MODEL)] + cast_specs,
        out_shape=[jax.ShapeDtypeStruct((s, D_MODEL), F32)] + cast_shapes,
        compiler_params=_params("parallel"),
        name="mix",
    )(x, *os, *ls, y, gates, *weights, *later_weights)


def _ffn_kernel(h_ref, p_ref, gffn_ref, wg_ref, wu_ref, wd_ref, wpg_ref, wpp_ref, gfin_ref,
                out_ref, acc_scr):
    for rb in range(h_ref.shape[0] // FFN_SUB_ROWS):
        rows = slice(rb * FFN_SUB_ROWS, (rb + 1) * FFN_SUB_ROWS)
        h = h_ref[rows, :]
        n2 = _rmsnorm(h, gffn_ref[...]).astype(BF16)
        for idx, (c0, width) in enumerate(FFN_CHUNKS):
            gate = _bdot(n2, wg_ref[:, c0:c0 + width])
            up = _bdot(n2, wu_ref[:, c0:c0 + width])
            act = (gate * jax.nn.sigmoid(gate) * up).astype(BF16)
            part = _bdot(act, wd_ref[c0:c0 + width, :])
            if idx == 0:
                acc_scr[rows, :] = h + part
            else:
                acc_scr[rows, :] += part
        h2 = acc_scr[rows, :]
        ple = (jax.nn.sigmoid(_bdot(h2.astype(BF16), wpg_ref[...]))
               * _bdot(p_ref[rows, :].astype(BF16), wpp_ref[...]))
        out_ref[rows, :] = _rmsnorm(h2 + ple, gfin_ref[...])


def _ffn(h, p, g_ffn, wg, wu, wd, wpg, wpp, g_final):
    s = h.shape[0]
    tm = min(FFN_ROWS, s)
    row = lambda w: pl.BlockSpec((tm, w), lambda i: (i, 0))
    consts = (g_ffn, wg, wu, wd, wpg, wpp, g_final)
    return pl.pallas_call(
        _ffn_kernel,
        grid=(s // tm,),
        in_specs=[row(D_MODEL), row(PLE_DIM)] + [_resident(c.shape) for c in consts],
        out_specs=row(D_MODEL),
        out_shape=jax.ShapeDtypeStruct((s, D_MODEL), F32),
        scratch_shapes=[pltpu.VMEM((tm, D_MODEL), F32)],
        compiler_params=_params("parallel"),
        name="ffn",
    )(h, p, *consts)


def _layer(x, p, positions, g_mix, w_in, a_re, a_im, log_dt, b_re, b_im, c_re, c_im, d_skip,
           w_attn_proj, w_glu_a, w_glu_b, w_out, g_ffn, w_ffn_gate, w_ffn_up, w_ffn_down,
           w_ple_gate, w_ple_proj, g_final):
    bf = lambda w: w.astype(BF16)
    row = lambda v: v.reshape(1, -1).astype(F32)

    inv_freq = ROPE_THETA ** (-jnp.arange(ROPE_HALF, dtype=F32) * 2.0 / ROPE_DIM)
    outs = _proj(x, positions.reshape(1, -1), row(g_mix), inv_freq.reshape(ROPE_HALF, 1), bf(w_in),
                 (w_attn_proj, w_glu_a, w_glu_b, w_out, w_ffn_gate, w_ffn_up, w_ple_gate))
    qkvs, (u, gates) = outs[:N_GROUPS], outs[N_GROUPS:N_GROUPS + 2]
    wap, wga, wgb, wout, wfg, wfu, wpg = outs[N_GROUPS + 2:]

    attn_os, attn_ls = zip(*[_attn_group(qkvs[g], d) for g, d in enumerate(ATTN_DILATIONS)])

    bmat, cre, cim, e_re, e_im, einv_re, einv_im, lam = _ssm_prep(a_re, a_im, log_dt, b_re, b_im, c_re, c_im)
    tri = jnp.tril(jnp.ones((SSM_CHUNK, SSM_CHUNK), F32)).astype(BF16)
    y = _ssm(u, bmat, cre, cim, tri, e_re, e_im, einv_re, einv_im, lam, row(d_skip))

    h1, wfd, wpp = _mix(x, attn_os, attn_ls, y, gates, wap, wga, wgb, wout, (w_ffn_down, w_ple_proj))
    return _ffn(h1, p, row(g_ffn), wfg, wfu, wfd, wpg, wpp, row(g_final))


def kernel(x, p, positions, g_mix, w_in, a_re, a_im, log_dt, b_re, b_im, c_re, c_im, d_skip,
           w_attn_proj, w_glu_a, w_glu_b, w_out, g_ffn, w_ffn_gate, w_ffn_up, w_ffn_down,
           w_ple_gate, w_ple_proj, g_final):
    batch, depth = x.shape[0], p.shape[0]
    assert batch == 1 and depth == 1, "kernel supports the stated BATCH=1, DEPTH=1 problem"
    out = _layer(x[0], p[0, 0], positions[0], g_mix[0], w_in[0], a_re[0], a_im[0], log_dt[0],
                 b_re[0], b_im[0], c_re[0], c_im[0], d_skip[0], w_attn_proj[0], w_glu_a[0],
                 w_glu_b[0], w_out[0], g_ffn[0], w_ffn_gate[0], w_ffn_up[0], w_ffn_down[0],
                 w_ple_gate[0], w_ple_proj[0], g_final)
    return out[None]
```

```python
import functools
import math

import jax
import jax.numpy as jnp
from jax import lax
from jax.experimental import pallas as pl
from jax.experimental.pallas import tpu as pltpu

F32 = jnp.float32
BF16 = jnp.bfloat16

D_MODEL = 1024
HEAD_DIM = 128
HEADS_PER_GROUP = 4
GROUP_WIDTH = HEADS_PER_GROUP * HEAD_DIM
ATTN_DILATIONS = (1, 4, 16)
N_GROUPS = len(ATTN_DILATIONS)
QK_WIDTH = N_GROUPS * GROUP_WIDTH
BLOCK = 128
ROPE_THETA = 500000.0
ROPE_DIM = HEAD_DIM // 4
ROPE_HALF = ROPE_DIM // 2
SSM_WIDTH = 512
SSM_GROUP = 16
SSM_GROUPS = SSM_WIDTH // SSM_GROUP
SSM_STATE = 64
N_STATES = SSM_GROUPS * SSM_STATE
SSM_HALVES = 2
HALF_GROUPS = SSM_GROUPS // SSM_HALVES
HALF_WIDTH = SSM_WIDTH // SSM_HALVES
HALF_STATES = N_STATES // SSM_HALVES
STATE_BLOCK = 256
D_FF = 2816
PLE_DIM = 256
EPS = 1e-6
MASK_VALUE = -1e30

V7X_VMEM_LIMIT_BYTES = 56 * 1024 * 1024

PROJ_ROWS = 512
PROJ_SUB_ROWS = 256
ATTN_TOKENS = {1: 1024, 4: 2048, 16: 2048}
ATTN_TILES_PER_BODY = 32
LSE_LANES = HEAD_DIM // HEADS_PER_GROUP
SSM_CHUNK = 128
SSM_ROWS = 512
MIX_ROWS = 1024
MIX_SUB_ROWS = 256
FFN_ROWS = 512
FFN_SUB_ROWS = 256
FFN_CHUNKS = ((0, 1024), (1024, 1024), (2048, 768))


def _resident(shape):
    return pl.BlockSpec(shape, lambda *_: (0,) * len(shape), pipeline_mode=pl.Buffered(1))


def _params(*semantics):
    return pltpu.CompilerParams(dimension_semantics=semantics,
                                vmem_limit_bytes=V7X_VMEM_LIMIT_BYTES)


def _rmsnorm(x, g):
    return (x * lax.rsqrt(jnp.mean(x * x, axis=-1, keepdims=True) + EPS)) * g


def _bdot(a, b):
    return jnp.dot(a, b, preferred_element_type=F32)


def _cast_plan(weights, steps):
    specs = [pl.BlockSpec((w.shape[0] // steps, w.shape[1]), lambda i: (i, 0)) for w in weights]
    shapes = [jax.ShapeDtypeStruct(w.shape, BF16) for w in weights]
    return specs, shapes


def _cast_blocks(in_refs, out_refs):
    for src, dst in zip(in_refs, out_refs):
        dst[...] = src[...].astype(BF16)


def _proj_kernel(*refs, n_cast):
    x_ref, pos_ref, g_ref, invf_ref, w_ref = refs[:5]
    cast_in, refs = refs[5:5 + n_cast], refs[5 + n_cast:]
    qkv0, qkv1, qkv2, u_ref, gates_ref = refs[:5]
    cast_out, (n_scr, nperm_scr, cos_scr, sin_scr) = refs[5:5 + n_cast], refs[5 + n_cast:]
    _cast_blocks(cast_in, cast_out)
    rows = PROJ_SUB_ROWS
    lane = lax.broadcasted_iota(jnp.int32, (rows, HEAD_DIM), 1)
    first_half = lane < ROPE_HALF
    scale = math.log2(math.e) / math.sqrt(HEAD_DIM)

    for sb in range(x_ref.shape[0] // rows):
        base = sb * rows
        tile_rows = slice(base, base + rows)
        xn = _rmsnorm(x_ref[tile_rows, :], g_ref[...])
        n = xn.astype(BF16)
        for c in range(D_MODEL // HEAD_DIM):
            n_scr[c, tile_rows, :] = xn[:, c * HEAD_DIM:(c + 1) * HEAD_DIM]

        ang = invf_ref[...] * pos_ref[:, tile_rows].astype(F32)
        cos_t = jnp.cos(ang)
        sin_t = jnp.sin(ang)
        rest = (HEAD_DIM - ROPE_DIM, rows)
        cos_scr[tile_rows, :] = jnp.concatenate([cos_t, cos_t, jnp.ones(rest, F32)], axis=0).T
        sin_scr[tile_rows, :] = jnp.concatenate([-sin_t, sin_t, jnp.zeros(rest, F32)], axis=0).T

        for g, (d, qkv_ref) in enumerate(zip(ATTN_DILATIONS, (qkv0, qkv1, qkv2))):
            def residue_major(ref_2d):
                if d == 1:
                    return ref_2d[tile_rows, :]
                return jnp.concatenate(
                    [ref_2d[pl.ds(base + r, rows // d, stride=d), :] for r in range(d)], axis=0)

            cos = residue_major(cos_scr)
            sin_signed = residue_major(sin_scr)
            if d == 1:
                ng = n
            else:
                for c in range(D_MODEL // HEAD_DIM):
                    nperm_scr[tile_rows, c * HEAD_DIM:(c + 1) * HEAD_DIM] = (
                        residue_major(n_scr.at[c]).astype(BF16))
                ng = nperm_scr[tile_rows, :]

            def rotary(t):
                partner = jnp.where(first_half,
                                    pltpu.roll(t, HEAD_DIM - ROPE_HALF, 1),
                                    pltpu.roll(t, ROPE_HALF, 1))
                return t * cos + partner * sin_signed

            c0 = g * GROUP_WIDTH
            zq = _bdot(ng, w_ref[:, c0:c0 + GROUP_WIDTH])
            zk = _bdot(ng, w_ref[:, QK_WIDTH + c0:QK_WIDTH + c0 + GROUP_WIDTH])
            zv = _bdot(ng, w_ref[:, 2 * QK_WIDTH + c0:2 * QK_WIDTH + c0 + GROUP_WIDTH])
            sub_rows = slice(base // d, (base + rows) // d)
            for h in range(HEADS_PER_GROUP):
                hs = slice(h * HEAD_DIM, (h + 1) * HEAD_DIM)
                ks = slice(GROUP_WIDTH + h * HEAD_DIM, GROUP_WIDTH + (h + 1) * HEAD_DIM)
                qkv_ref[:, sub_rows, hs] = (
                    (rotary(zq[:, hs]) * scale).astype(BF16).reshape(d, rows // d, HEAD_DIM))
                qkv_ref[:, sub_rows, ks] = rotary(zk[:, hs]).astype(BF16).reshape(d, rows // d, HEAD_DIM)
            qkv_ref[:, sub_rows, 2 * GROUP_WIDTH:] = zv.astype(BF16).reshape(d, rows // d, GROUP_WIDTH)
        o1 = 3 * QK_WIDTH
        u_ref[tile_rows, :] = _bdot(n, w_ref[:, o1:o1 + SSM_WIDTH])
        o2 = o1 + SSM_WIDTH
        gates_ref[tile_rows, :] = jax.nn.sigmoid(_bdot(n, w_ref[:, o2:o2 + 2 * D_MODEL])).astype(BF16)


def _proj(x, pos_row, g_mix, inv_freq_col, w_in, later_weights):
    s = x.shape[0]
    tm = min(PROJ_ROWS, s)
    row = lambda w: pl.BlockSpec((tm, w), lambda i: (i, 0))
    cast_specs, cast_shapes = _cast_plan(later_weights, s // tm)
    grp_specs, grp_shapes = [], []
    for d in ATTN_DILATIONS:
        grp_specs.append(pl.BlockSpec((d, tm // d, 3 * GROUP_WIDTH), lambda i: (0, i, 0)))
        grp_shapes.append(jax.ShapeDtypeStruct((d, s // d, 3 * GROUP_WIDTH), BF16))
    return pl.pallas_call(
        functools.partial(_proj_kernel, n_cast=len(later_weights)),
        grid=(s // tm,),
        in_specs=[row(D_MODEL), pl.BlockSpec((1, tm), lambda i: (0, i)), _resident((1, D_MODEL)),
                  _resident((ROPE_HALF, 1)), _resident(w_in.shape)] + cast_specs,
        out_specs=grp_specs + [row(SSM_WIDTH), row(2 * D_MODEL)] + cast_specs,
        out_shape=grp_shapes + [jax.ShapeDtypeStruct((s, SSM_WIDTH), F32),
                                jax.ShapeDtypeStruct((s, 2 * D_MODEL), BF16)] + cast_shapes,
        scratch_shapes=[pltpu.VMEM((D_MODEL // HEAD_DIM, tm, HEAD_DIM), F32), pltpu.VMEM((tm, D_MODEL), BF16),
                        pltpu.VMEM((tm, HEAD_DIM), F32), pltpu.VMEM((tm, HEAD_DIM), F32)],
        compiler_params=_params("parallel"),
        name="proj",
    )(x, pos_row, g_mix, inv_freq_col, w_in, *later_weights)


def _attn_kernel(qkv_ref, prev_ref, o_ref, l_ref, o_scr, l_scr, *, d, nsub):
    not_first_block = pl.program_id(0) > 0
    qi = lax.broadcasted_iota(jnp.int32, (BLOCK, 2 * BLOCK), 0)
    kj = lax.broadcasted_iota(jnp.int32, (BLOCK, 2 * BLOCK), 1)
    rel = BLOCK + qi - kj
    band = (rel >= 0) & (rel <= BLOCK)
    band_first = band & ((kj >= BLOCK) | not_first_block)
    lane = lax.broadcasted_iota(jnp.int32, (BLOCK, HEAD_DIM), 1)

    def residue(r):
        for b in range(nsub):
            rows = slice(b * BLOCK, (b + 1) * BLOCK)
            mask = band_first if b == 0 else band
            token_rows = pl.ds(b * BLOCK * d + r, BLOCK, stride=d) if d > 1 else rows
            lse_tile = None
            for h in range(HEADS_PER_GROUP):
                hs = slice(h * HEAD_DIM, (h + 1) * HEAD_DIM)
                ks = slice(GROUP_WIDTH + h * HEAD_DIM, GROUP_WIDTH + (h + 1) * HEAD_DIM)
                vs = slice(2 * GROUP_WIDTH + h * HEAD_DIM, 2 * GROUP_WIDTH + (h + 1) * HEAD_DIM)
                q = qkv_ref[r, rows, hs]
                if b == 0:
                    kw = jnp.concatenate([prev_ref[r, :, ks], qkv_ref[r, rows, ks]], axis=0)
                    vw = jnp.concatenate([prev_ref[r, :, vs], qkv_ref[r, rows, vs]], axis=0)
                else:
                    win = slice((b - 1) * BLOCK, (b + 1) * BLOCK)
                    kw = qkv_ref[r, win, ks]
                    vw = qkv_ref[r, win, vs]
                s = lax.dot_general(q, kw, (((1,), (1,)), ((), ())), preferred_element_type=F32)
                s = jnp.where(mask, s, MASK_VALUE)
                m = jnp.max(s, axis=-1, keepdims=True)
                p = jnp.exp2(s - m)
                den = jnp.sum(p, axis=-1, keepdims=True)
                o_scr[h, token_rows, :] = _bdot(p.astype(BF16), vw) / den
                lse = jnp.broadcast_to((m + jnp.log2(den)) * math.log(2.0), (BLOCK, HEAD_DIM))
                lse_tile = lse if h == 0 else jnp.where(lane >= h * LSE_LANES, lse, lse_tile)
            l_scr[token_rows, :] = lse_tile

    per_iter = max(1, min(d, ATTN_TILES_PER_BODY // (nsub * HEADS_PER_GROUP)))
    if per_iter == d:
        for r in range(d):
            residue(r)
    else:
        def body(i, carry):
            for j in range(per_iter):
                residue(i * per_iter + j)
            return carry
        lax.fori_loop(0, d // per_iter, body, 0)

    for h in range(HEADS_PER_GROUP):
        o_ref[:, h * HEAD_DIM:(h + 1) * HEAD_DIM] = o_scr[h].astype(BF16)
    l_ref[...] = l_scr[...]


def _attn_group(qkv, dilation):
    sub_len = qkv.shape[1]
    s = sub_len * dilation
    step_tokens = min(ATTN_TOKENS[dilation], s)
    qb = step_tokens // dilation
    nsub = qb // BLOCK
    cur = pl.BlockSpec((dilation, qb, 3 * GROUP_WIDTH), lambda i: (0, i, 0))
    prev = pl.BlockSpec((dilation, BLOCK, 3 * GROUP_WIDTH), lambda i: (0, jnp.maximum(i * nsub - 1, 0), 0))
    return pl.pallas_call(
        functools.partial(_attn_kernel, d=dilation, nsub=nsub),
        grid=(s // step_tokens,),
        in_specs=[cur, prev],
        out_specs=[pl.BlockSpec((step_tokens, GROUP_WIDTH), lambda i: (i, 0)),
                   pl.BlockSpec((step_tokens, HEAD_DIM), lambda i: (i, 0))],
        out_shape=[jax.ShapeDtypeStruct((s, GROUP_WIDTH), BF16), jax.ShapeDtypeStruct((s, HEAD_DIM), F32)],
        scratch_shapes=[pltpu.VMEM((HEADS_PER_GROUP, step_tokens, HEAD_DIM), F32),
                        pltpu.VMEM((step_tokens, HEAD_DIM), F32)],
        compiler_params=_params("arbitrary"),
        name=f"attn_d{dilation}",
    )(qkv, qkv)


def _discretize(lr, li, log_dt):
    dt = jnp.exp(log_dt)
    mag = jnp.exp(lr * dt)
    bar_re = mag * jnp.cos(li * dt)
    bar_im = mag * jnp.sin(li * dt)
    nr = bar_re - 1.0
    ni = bar_im
    den = lr * lr + li * li
    return bar_re, bar_im, (nr * lr + ni * li) / den, (ni * lr - nr * li) / den


def _expand_block_diag(compact, n_blocks):
    rows, b = compact.shape
    a = rows // n_blocks
    wide = n_blocks * b
    src_lane = lax.broadcasted_iota(jnp.int32, (b, wide), 0)
    dst_lane = lax.broadcasted_iota(jnp.int32, (b, wide), 1)
    tiled = _bdot(compact, (dst_lane % b == src_lane).astype(BF16))
    row_block = lax.broadcasted_iota(jnp.int32, (rows, wide), 0) // a
    col_block = lax.broadcasted_iota(jnp.int32, (rows, wide), 1) // b
    return jnp.where(row_block == col_block, tiled, 0.0).astype(BF16)


def _ssm_prep_kernel(lr_ref, li_ref, logdt_ref, lr_rep_ref, li_rep_ref, logdt_rep_ref, b_re_ref, b_im_ref,
                     c_re_ref, c_im_ref,
                     bmat_ref, cre_ref, cim_ref, e_re_ref, e_im_ref, einv_re_ref, einv_im_ref, lam_ref):
    _, _, z_re, z_im = _discretize(lr_rep_ref[...], li_rep_ref[...], logdt_rep_ref[...])
    b_re = b_re_ref[...]
    b_im = b_im_ref[...]
    bb_re = (z_re * b_re - z_im * b_im).astype(BF16)
    bb_im = (z_re * b_im + z_im * b_re).astype(BF16)
    for hf in range(SSM_HALVES):
        rows = slice(hf * HALF_WIDTH, (hf + 1) * HALF_WIDTH)
        bmat_ref[rows, :HALF_STATES] = _expand_block_diag(bb_re[rows], HALF_GROUPS)
        bmat_ref[rows, HALF_STATES:] = _expand_block_diag(bb_im[rows], HALF_GROUPS)
        srows = slice(hf * HALF_STATES, (hf + 1) * HALF_STATES)
        cre_ref[srows, :] = _expand_block_diag(c_re_ref[srows, :].astype(BF16), HALF_GROUPS)
        cim_ref[srows, :] = _expand_block_diag(c_im_ref[srows, :].astype(BF16), HALF_GROUPS)

    lr = lr_ref[...]
    li = li_ref[...]
    dt = jnp.exp(logdt_ref[...])
    whole = float(SSM_CHUNK)
    chunk_mag = jnp.exp(whole * (lr * dt))
    lam_ref[0:1, :] = chunk_mag * jnp.cos(whole * (li * dt))
    lam_ref[1:2, :] = chunk_mag * jnp.sin(whole * (li * dt))
    t = lax.broadcasted_iota(jnp.int32, (SSM_CHUNK, N_STATES), 0).astype(F32)
    grow = jnp.exp(t * (lr * dt))
    theta = t * (li * dt)
    c = jnp.cos(theta)
    s = jnp.sin(theta)
    e_re_ref[...] = (grow * c).astype(BF16)
    e_im_ref[...] = (grow * s).astype(BF16)
    shrink = jnp.exp(-t * (lr * dt))
    einv_re_ref[...] = (shrink * c).astype(BF16)
    einv_im_ref[...] = (-(shrink * s)).astype(BF16)


def _ssm_prep(a_re, a_im, log_dt, b_re, b_im, c_re, c_im):
    row = lambda v: v.reshape(1, N_STATES)
    rep = lambda v: jnp.repeat(v, SSM_GROUP, axis=0)
    logdt_gp = jnp.broadcast_to(log_dt[:, None], (SSM_GROUPS, SSM_STATE))
    chan_state = lambda b: jnp.swapaxes(b, 1, 2).reshape(SSM_WIDTH, SSM_STATE)
    state_chan = lambda c: jnp.swapaxes(c, 1, 2).reshape(N_STATES, SSM_GROUP)
    tab = jax.ShapeDtypeStruct((SSM_CHUNK, N_STATES), BF16)
    cmat = jax.ShapeDtypeStruct((N_STATES, HALF_WIDTH), BF16)
    return pl.pallas_call(
        _ssm_prep_kernel,
        out_shape=[jax.ShapeDtypeStruct((SSM_WIDTH, 2 * HALF_STATES), BF16), cmat, cmat, tab, tab, tab, tab,
                   jax.ShapeDtypeStruct((2, N_STATES), F32)],
        compiler_params=pltpu.CompilerParams(vmem_limit_bytes=V7X_VMEM_LIMIT_BYTES),
        name="ssm_prep",
    )(row(a_re), row(a_im), row(logdt_gp), rep(a_re), rep(a_im), rep(logdt_gp),
      chan_state(b_re), chan_state(b_im), state_chan(c_re), state_chan(c_im))


def _gelu_tanh(x):
    return 0.5 * x * (1.0 + jnp.tanh(math.sqrt(2.0 / math.pi) * (x + 0.044715 * (x * x * x))))


def _ssm_kernel(u_ref, bmat_ref, cre_ref, cim_ref, tri_ref, e_re_ref, e_im_ref, einv_re_ref, einv_im_ref,
                lam_ref, d_ref, y_ref, h_re_scr, h_im_scr, carry_scr):
    @pl.when(pl.program_id(0) == 0)
    def _():
        carry_scr[...] = jnp.zeros_like(carry_scr)

    u = u_ref[...]
    ub = u.astype(BF16)
    for hf in range(SSM_HALVES):
        chans = slice(hf * HALF_WIDTH, (hf + 1) * HALF_WIDTH)
        y = d_ref[:, chans] * u[:, chans]
        for blk in range(HALF_STATES // STATE_BLOCK):
            local = slice(blk * STATE_BLOCK, (blk + 1) * STATE_BLOCK)
            local_im = slice(HALF_STATES + blk * STATE_BLOCK, HALF_STATES + (blk + 1) * STATE_BLOCK)
            states = slice(hf * HALF_STATES + blk * STATE_BLOCK, hf * HALF_STATES + (blk + 1) * STATE_BLOCK)
            bu_re_all = _bdot(ub[:, chans], bmat_ref[chans, local])
            bu_im_all = _bdot(ub[:, chans], bmat_ref[chans, local_im])
            lam_re = lam_ref[0:1, states]
            lam_im = lam_ref[1:2, states]
            carry_re = carry_scr[0:1, states]
            carry_im = carry_scr[1:2, states]
            for c in range(u.shape[0] // SSM_CHUNK):
                rows = slice(c * SSM_CHUNK, (c + 1) * SSM_CHUNK)
                bu_re = bu_re_all[rows].astype(BF16)
                bu_im = bu_im_all[rows].astype(BF16)
                einv_re = einv_re_ref[:, states]
                einv_im = einv_im_ref[:, states]
                x = jnp.concatenate([bu_re * einv_re - bu_im * einv_im,
                                     bu_re * einv_im + bu_im * einv_re], axis=1)
                a = _bdot(tri_ref[...], x)
                a_re = a[:, :STATE_BLOCK] + carry_re
                a_im = a[:, STATE_BLOCK:] + carry_im
                e_re = e_re_ref[:, states]
                e_im = e_im_ref[:, states]
                a_re_b = a_re.astype(BF16)
                a_im_b = a_im.astype(BF16)
                h_re_scr[rows, :] = e_re * a_re_b - e_im * a_im_b
                h_im_scr[rows, :] = e_re * a_im_b + e_im * a_re_b
                last_re = a_re[SSM_CHUNK - 1:SSM_CHUNK, :]
                last_im = a_im[SSM_CHUNK - 1:SSM_CHUNK, :]
                carry_re = lam_re * last_re - lam_im * last_im
                carry_im = lam_re * last_im + lam_im * last_re
            carry_scr[0:1, states] = carry_re
            carry_scr[1:2, states] = carry_im
            y = y + _bdot(h_re_scr[...], cre_ref[states, :]) - _bdot(h_im_scr[...], cim_ref[states, :])
        y_ref[:, chans] = _gelu_tanh(y).astype(BF16)


def _ssm(u, bmat, cre, cim, tri, e_re, e_im, einv_re, einv_im, lam, d_row):
    s = u.shape[0]
    tm = min(SSM_ROWS, s)
    row = pl.BlockSpec((tm, SSM_WIDTH), lambda i: (i, 0))
    consts = (bmat, cre, cim, tri, e_re, e_im, einv_re, einv_im, lam, d_row)
    return pl.pallas_call(
        _ssm_kernel,
        grid=(s // tm,),
        in_specs=[row] + [_resident(c.shape) for c in consts],
        out_specs=row,
        out_shape=jax.ShapeDtypeStruct((s, SSM_WIDTH), BF16),
        scratch_shapes=[pltpu.VMEM((tm, STATE_BLOCK), BF16), pltpu.VMEM((tm, STATE_BLOCK), BF16),
                        pltpu.VMEM((2, N_STATES), F32)],
        compiler_params=_params("arbitrary"),
        name="ssm",
    )(u, *consts)


def _mix_kernel(*refs, n_cast):
    x_ref, o0_ref, o1_ref, o2_ref, l0_ref, l1_ref, l2_ref, y_ref, gates_ref = refs[:9]
    wap_ref, wa_ref, wb_ref, wout_ref = refs[9:13]
    cast_in, h_ref, cast_out = refs[13:13 + n_cast], refs[13 + n_cast], refs[14 + n_cast:]
    _cast_blocks(cast_in, cast_out)
    for rb in range(x_ref.shape[0] // MIX_SUB_ROWS):
        rows = slice(rb * MIX_SUB_ROWS, (rb + 1) * MIX_SUB_ROWS)
        ls = (l0_ref[rows, :], l1_ref[rows, :], l2_ref[rows, :])
        l_max = jnp.maximum(jnp.maximum(ls[0], ls[1]), ls[2])
        es = [jnp.exp(l - l_max) for l in ls]
        inv = 1.0 / (es[0] + es[1] + es[2])
        heads = []
        for h in range(HEADS_PER_GROUP):
            hs = slice(h * HEAD_DIM, (h + 1) * HEAD_DIM)
            col = slice(h * LSE_LANES, h * LSE_LANES + 1)
            heads.append(sum((e * inv)[:, col] * o_ref[rows, hs].astype(F32)
                             for e, o_ref in zip(es, (o0_ref, o1_ref, o2_ref))))
        attn = jnp.concatenate(heads, axis=1).astype(BF16)
        attn_d = _bdot(attn, wap_ref[...])
        y = y_ref[rows, :]
        ssm_d = _bdot(y, wa_ref[...]) * jax.nn.sigmoid(_bdot(y, wb_ref[...]))
        mix = (gates_ref[rows, :D_MODEL].astype(F32) * attn_d
               + gates_ref[rows, D_MODEL:].astype(F32) * ssm_d)
        h_ref[rows, :] = x_ref[rows, :] + _bdot(mix.astype(BF16), wout_ref[...])


def _mix(x, os, ls, y, gates, wap, wa, wb, wout, later_weights):
    s = x.shape[0]
    tm = min(MIX_ROWS, s)
    row = lambda w: pl.BlockSpec((tm, w), lambda i: (i, 0))
    weights = (wap, wa, wb, wout)
    cast_specs, cast_shapes = _cast_plan(later_weights, s // tm)
    return pl.pallas_call(
        functools.partial(_mix_kernel, n_cast=len(later_weights)),
        grid=(s // tm,),
        in_specs=[row(D_MODEL)] + [row(GROUP_WIDTH)] * 3 + [row(HEAD_DIM)] * 3
        + [row(SSM_WIDTH), row(2 * D_MODEL)] + [_resident(w.shape) for w in weights] + cast_specs,
        out_specs=[row(D_MODEL)] + cast_specs,
        out_shape=[jax.ShapeDtypeStruct((s, D_MODEL), F32)] + cast_shapes,
        compiler_params=_params("parallel"),
        name="mix",
    )(x, *os, *ls, y, gates, *weights, *later_weights)


def _ffn_kernel(h_ref, p_ref, gffn_ref, wg_ref, wu_ref, wd_ref, wpg_ref, wpp_ref, gfin_ref,
                out_ref, acc_scr):
    for rb in range(h_ref.shape[0] // FFN_SUB_ROWS):
        rows = slice(rb * FFN_SUB_ROWS, (rb + 1) * FFN_SUB_ROWS)
        h = h_ref[rows, :]
        n2 = _rmsnorm(h, gffn_ref[...]).astype(BF16)
        for idx, (c0, width) in enumerate(FFN_CHUNKS):
            gate = _bdot(n2, wg_ref[:, c0:c0 + width])
            up = _bdot(n2, wu_ref[:, c0:c0 + width])
            act = (gate * jax.nn.sigmoid(gate) * up).astype(BF16)
            part = _bdot(act, wd_ref[c0:c0 + width, :])
            if idx == 0:
                acc_scr[rows, :] = h + part
            else:
                acc_scr[rows, :] += part
        h2 = acc_scr[rows, :]
        ple = (jax.nn.sigmoid(_bdot(h2.astype(BF16), wpg_ref[...]))
               * _bdot(p_ref[rows, :].astype(BF16), wpp_ref[...]))
        out_ref[rows, :] = _rmsnorm(h2 + ple, gfin_ref[...])


def _ffn(h, p, g_ffn, wg, wu, wd, wpg, wpp, g_final):
    s = h.shape[0]
    tm = min(FFN_ROWS, s)
    row = lambda w: pl.BlockSpec((tm, w), lambda i: (i, 0))
    consts = (g_ffn, wg, wu, wd, wpg, wpp, g_final)
    return pl.pallas_call(
        _ffn_kernel,
        grid=(s // tm,),
        in_specs=[row(D_MODEL), row(PLE_DIM)] + [_resident(c.shape) for c in consts],
        out_specs=row(D_MODEL),
        out_shape=jax.ShapeDtypeStruct((s, D_MODEL), F32),
        scratch_shapes=[pltpu.VMEM((tm, D_MODEL), F32)],
        compiler_params=_params("parallel"),
        name="ffn",
    )(h, p, *consts)


def _layer(x, p, positions, g_mix, w_in, a_re, a_im, log_dt, b_re, b_im, c_re, c_im, d_skip,
           w_attn_proj, w_glu_a, w_glu_b, w_out, g_ffn, w_ffn_gate, w_ffn_up, w_ffn_down,
           w_ple_gate, w_ple_proj, g_final):
    bf = lambda w: w.astype(BF16)
    row = lambda v: v.reshape(1, -1).astype(F32)

    inv_freq = ROPE_THETA ** (-jnp.arange(ROPE_HALF, dtype=F32) * 2.0 / ROPE_DIM)
    outs = _proj(x, positions.reshape(1, -1), row(g_mix), inv_freq.reshape(ROPE_HALF, 1), bf(w_in),
                 (w_attn_proj, w_glu_a, w_glu_b, w_out, w_ffn_gate, w_ffn_up, w_ple_gate))
    qkvs, (u, gates) = outs[:N_GROUPS], outs[N_GROUPS:N_GROUPS + 2]
    wap, wga, wgb, wout, wfg, wfu, wpg = outs[N_GROUPS + 2:]

    attn_os, attn_ls = zip(*[_attn_group(qkvs[g], d) for g, d in enumerate(ATTN_DILATIONS)])

    bmat, cre, cim, e_re, e_im, einv_re, einv_im, lam = _ssm_prep(a_re, a_im, log_dt, b_re, b_im, c_re, c_im)
    tri = jnp.tril(jnp.ones((SSM_CHUNK, SSM_CHUNK), F32)).astype(BF16)
    y = _ssm(u, bmat, cre, cim, tri, e_re, e_im, einv_re, einv_im, lam, row(d_skip))

    h1, wfd, wpp = _mix(x, attn_os, attn_ls, y, gates, wap, wga, wgb, wout, (w_ffn_down, w_ple_proj))
    return _ffn(h1, p, row(g_ffn), wfg, wfu, wfd, wpg, wpp, row(g_final))


def kernel(x, p, positions, g_mix, w_in, a_re, a_im, log_dt, b_re, b_im, c_re, c_im, d_skip,
           w_attn_proj, w_glu_a, w_glu_b, w_out, g_ffn, w_ffn_gate, w_ffn_up, w_ffn_down,
           w_ple_gate, w_ple_proj, g_final):
    batch, depth = x.shape[0], p.shape[0]
    assert batch == 1 and depth == 1, "kernel supports the stated BATCH=1, DEPTH=1 problem"
    out = _layer(x[0], p[0, 0], positions[0], g_mix[0], w_in[0], a_re[0], a_im[0], log_dt[0],
                 b_re[0], b_im[0], c_re[0], c_im[0], d_skip[0], w_attn_proj[0], w_glu_a[0],
                 w_glu_b[0], w_out[0], g_ffn[0], w_ffn_gate[0], w_ffn_up[0], w_ffn_down[0],
                 w_ple_gate[0], w_ple_proj[0], g_final)
    return out[None]
```

```python
import functools
import math

import jax
import jax.numpy as jnp
from jax import lax
from jax.experimental import pallas as pl
from jax.experimental.pallas import tpu as pltpu

F32 = jnp.float32
BF16 = jnp.bfloat16

D_MODEL = 1024
HEAD_DIM = 128
HEADS_PER_GROUP = 4
GROUP_WIDTH = HEADS_PER_GROUP * HEAD_DIM
ATTN_DILATIONS = (1, 4, 16)
N_GROUPS = len(ATTN_DILATIONS)
QK_WIDTH = N_GROUPS * GROUP_WIDTH
BLOCK = 128
ROPE_THETA = 500000.0
ROPE_DIM = HEAD_DIM // 4
ROPE_HALF = ROPE_DIM // 2
SSM_WIDTH = 512
SSM_GROUP = 16
SSM_GROUPS = SSM_WIDTH // SSM_GROUP
SSM_STATE = 64
N_STATES = SSM_GROUPS * SSM_STATE
SSM_HALVES = 2
HALF_GROUPS = SSM_GROUPS // SSM_HALVES
HALF_WIDTH = SSM_WIDTH // SSM_HALVES
HALF_STATES = N_STATES // SSM_HALVES
STATE_BLOCK = 256
D_FF = 2816
PLE_DIM = 256
EPS = 1e-6
MASK_VALUE = -1e30

V7X_VMEM_LIMIT_BYTES = 56 * 1024 * 1024

PROJ_ROWS = 512
PROJ_SUB_ROWS = 256
ATTN_TOKENS = {1: 1024, 4: 2048, 16: 2048}
ATTN_TILES_PER_BODY = 32
STAT_LANES = HEAD_DIM // HEADS_PER_GROUP
SSM_CHUNK = 128
SSM_ROWS = 512
MIX_ROWS = 1024
MIX_SUB_ROWS = 256
FFN_ROWS = 512
FFN_SUB_ROWS = 256
FFN_CHUNKS = ((0, 1024), (1024, 1024), (2048, 768))


def _resident(shape):
    return pl.BlockSpec(shape, lambda *_: (0,) * len(shape), pipeline_mode=pl.Buffered(1))


def _params(*semantics):
    return pltpu.CompilerParams(dimension_semantics=semantics,
                                vmem_limit_bytes=V7X_VMEM_LIMIT_BYTES)


def _rmsnorm(x, g):
    return (x * lax.rsqrt(jnp.mean(x * x, axis=-1, keepdims=True) + EPS)) * g


def _bdot(a, b):
    return jnp.dot(a, b, preferred_element_type=F32)


def _cast_plan(weights, steps):
    specs = [pl.BlockSpec((w.shape[0] // steps, w.shape[1]), lambda i: (i, 0)) for w in weights]
    shapes = [jax.ShapeDtypeStruct(w.shape, BF16) for w in weights]
    return specs, shapes


def _cast_blocks(in_refs, out_refs):
    for src, dst in zip(in_refs, out_refs):
        dst[...] = src[...].astype(BF16)


def _proj_kernel(*refs, n_cast):
    x_ref, pos_ref, g_ref, invf_ref, w_ref = refs[:5]
    cast_in, refs = refs[5:5 + n_cast], refs[5 + n_cast:]
    qkv0, qkv1, qkv2, u_ref, gates_ref = refs[:5]
    cast_out, (n_scr, nperm_scr, cos_scr, sin_scr) = refs[5:5 + n_cast], refs[5 + n_cast:]
    _cast_blocks(cast_in, cast_out)
    rows = PROJ_SUB_ROWS
    lane = lax.broadcasted_iota(jnp.int32, (rows, HEAD_DIM), 1)
    first_half = lane < ROPE_HALF
    scale = math.log2(math.e) / math.sqrt(HEAD_DIM)

    for sb in range(x_ref.shape[0] // rows):
        base = sb * rows
        tile_rows = slice(base, base + rows)
        xn = _rmsnorm(x_ref[tile_rows, :], g_ref[...])
        n = xn.astype(BF16)
        for c in range(D_MODEL // HEAD_DIM):
            n_scr[c, tile_rows, :] = xn[:, c * HEAD_DIM:(c + 1) * HEAD_DIM]

        ang = invf_ref[...] * pos_ref[:, tile_rows].astype(F32)
        cos_t = jnp.cos(ang)
        sin_t = jnp.sin(ang)
        rest = (HEAD_DIM - ROPE_DIM, rows)
        cos_scr[tile_rows, :] = jnp.concatenate([cos_t, cos_t, jnp.ones(rest, F32)], axis=0).T
        sin_scr[tile_rows, :] = jnp.concatenate([-sin_t, sin_t, jnp.zeros(rest, F32)], axis=0).T

        for g, (d, qkv_ref) in enumerate(zip(ATTN_DILATIONS, (qkv0, qkv1, qkv2))):
            def residue_major(ref_2d):
                if d == 1:
                    return ref_2d[tile_rows, :]
                return jnp.concatenate(
                    [ref_2d[pl.ds(base + r, rows // d, stride=d), :] for r in range(d)], axis=0)

            cos = residue_major(cos_scr)
            sin_signed = residue_major(sin_scr)
            if d == 1:
                ng = n
            else:
                for c in range(D_MODEL // HEAD_DIM):
                    nperm_scr[tile_rows, c * HEAD_DIM:(c + 1) * HEAD_DIM] = (
                        residue_major(n_scr.at[c]).astype(BF16))
                ng = nperm_scr[tile_rows, :]

            def rotary(t):
                partner = jnp.where(first_half,
                                    pltpu.roll(t, HEAD_DIM - ROPE_HALF, 1),
                                    pltpu.roll(t, ROPE_HALF, 1))
                return t * cos + partner * sin_signed

            c0 = g * GROUP_WIDTH
            zq = _bdot(ng, w_ref[:, c0:c0 + GROUP_WIDTH])
            zk = _bdot(ng, w_ref[:, QK_WIDTH + c0:QK_WIDTH + c0 + GROUP_WIDTH])
            zv = _bdot(ng, w_ref[:, 2 * QK_WIDTH + c0:2 * QK_WIDTH + c0 + GROUP_WIDTH])
            sub_rows = slice(base // d, (base + rows) // d)
            for h in range(HEADS_PER_GROUP):
                hs = slice(h * HEAD_DIM, (h + 1) * HEAD_DIM)
                ks = slice(GROUP_WIDTH + h * HEAD_DIM, GROUP_WIDTH + (h + 1) * HEAD_DIM)
                qkv_ref[:, sub_rows, hs] = (
                    (rotary(zq[:, hs]) * scale).astype(BF16).reshape(d, rows // d, HEAD_DIM))
                qkv_ref[:, sub_rows, ks] = rotary(zk[:, hs]).astype(BF16).reshape(d, rows // d, HEAD_DIM)
            qkv_ref[:, sub_rows, 2 * GROUP_WIDTH:] = zv.astype(BF16).reshape(d, rows // d, GROUP_WIDTH)
        o1 = 3 * QK_WIDTH
        u_ref[tile_rows, :] = _bdot(n, w_ref[:, o1:o1 + SSM_WIDTH])
        o2 = o1 + SSM_WIDTH
        gates_ref[tile_rows, :] = jax.nn.sigmoid(_bdot(n, w_ref[:, o2:o2 + 2 * D_MODEL])).astype(BF16)


def _proj(x, pos_row, g_mix, inv_freq_col, w_in, later_weights):
    s = x.shape[0]
    tm = min(PROJ_ROWS, s)
    row = lambda w: pl.BlockSpec((tm, w), lambda i: (i, 0))
    cast_specs, cast_shapes = _cast_plan(later_weights, s // tm)
    grp_specs, grp_shapes = [], []
    for d in ATTN_DILATIONS:
        grp_specs.append(pl.BlockSpec((d, tm // d, 3 * GROUP_WIDTH), lambda i: (0, i, 0)))
        grp_shapes.append(jax.ShapeDtypeStruct((d, s // d, 3 * GROUP_WIDTH), BF16))
    return pl.pallas_call(
        functools.partial(_proj_kernel, n_cast=len(later_weights)),
        grid=(s // tm,),
        in_specs=[row(D_MODEL), pl.BlockSpec((1, tm), lambda i: (0, i)), _resident((1, D_MODEL)),
                  _resident((ROPE_HALF, 1)), _resident(w_in.shape)] + cast_specs,
        out_specs=grp_specs + [row(SSM_WIDTH), row(2 * D_MODEL)] + cast_specs,
        out_shape=grp_shapes + [jax.ShapeDtypeStruct((s, SSM_WIDTH), F32),
                                jax.ShapeDtypeStruct((s, 2 * D_MODEL), BF16)] + cast_shapes,
        scratch_shapes=[pltpu.VMEM((D_MODEL // HEAD_DIM, tm, HEAD_DIM), F32), pltpu.VMEM((tm, D_MODEL), BF16),
                        pltpu.VMEM((tm, HEAD_DIM), F32), pltpu.VMEM((tm, HEAD_DIM), F32)],
        compiler_params=_params("parallel"),
        name="proj",
    )(x, pos_row, g_mix, inv_freq_col, w_in, *later_weights)


def _attn_kernel(qkv_ref, prev_ref, o_ref, l_ref, o_scr, l_scr, *, d, nsub):
    not_first_block = pl.program_id(0) > 0
    qi = lax.broadcasted_iota(jnp.int32, (BLOCK, 2 * BLOCK), 0)
    kj = lax.broadcasted_iota(jnp.int32, (BLOCK, 2 * BLOCK), 1)
    rel = BLOCK + qi - kj
    band = (rel >= 0) & (rel <= BLOCK)
    band_first = band & ((kj >= BLOCK) | not_first_block)
    lane = lax.broadcasted_iota(jnp.int32, (BLOCK, HEAD_DIM), 1)

    def residue(r):
        for b in range(nsub):
            rows = slice(b * BLOCK, (b + 1) * BLOCK)
            mask = band_first if b == 0 else band
            token_rows = pl.ds(b * BLOCK * d + r, BLOCK, stride=d) if d > 1 else rows
            for h in range(HEADS_PER_GROUP):
                hs = slice(h * HEAD_DIM, (h + 1) * HEAD_DIM)
                ks = slice(GROUP_WIDTH + h * HEAD_DIM, GROUP_WIDTH + (h + 1) * HEAD_DIM)
                vs = slice(2 * GROUP_WIDTH + h * HEAD_DIM, 2 * GROUP_WIDTH + (h + 1) * HEAD_DIM)
                q = qkv_ref[r, rows, hs]
                if b == 0:
                    kw = jnp.concatenate([prev_ref[r, :, ks], qkv_ref[r, rows, ks]], axis=0)
                    vw = jnp.concatenate([prev_ref[r, :, vs], qkv_ref[r, rows, vs]], axis=0)
                else:
                    win = slice((b - 1) * BLOCK, (b + 1) * BLOCK)
                    kw = qkv_ref[r, win, ks]
                    vw = qkv_ref[r, win, vs]
                s = lax.dot_general(q, kw, (((1,), (1,)), ((), ())), preferred_element_type=F32)
                s = jnp.where(mask, s, MASK_VALUE)
                m = jnp.max(s, axis=-1, keepdims=True)
                p = jnp.exp2(s - m)
                den = jnp.sum(p, axis=-1, keepdims=True)
                o_scr[h, token_rows, :] = _bdot(p.astype(BF16), vw)
                m_b = jnp.broadcast_to(m, (BLOCK, HEAD_DIM))
                den_b = jnp.broadcast_to(den, (BLOCK, HEAD_DIM))
                if h > 0:
                    m_b = jnp.where(lane >= h * STAT_LANES, m_b, stats)
                stats = jnp.where(lane >= h * STAT_LANES + STAT_LANES // 2, den_b, m_b)
            l_scr[token_rows, :] = stats

    per_iter = max(1, min(d, ATTN_TILES_PER_BODY // (nsub * HEADS_PER_GROUP)))
    if per_iter == d:
        for r in range(d):
            residue(r)
    else:
        def body(i, carry):
            for j in range(per_iter):
                residue(i * per_iter + j)
            return carry
        lax.fori_loop(0, d // per_iter, body, 0)

    for h in range(HEADS_PER_GROUP):
        o_ref[:, h * HEAD_DIM:(h + 1) * HEAD_DIM] = o_scr[h].astype(BF16)
    l_ref[...] = l_scr[...]


def _attn_group(qkv, dilation):
    sub_len = qkv.shape[1]
    s = sub_len * dilation
    step_tokens = min(ATTN_TOKENS[dilation], s)
    qb = step_tokens // dilation
    nsub = qb // BLOCK
    cur = pl.BlockSpec((dilation, qb, 3 * GROUP_WIDTH), lambda i: (0, i, 0))
    prev = pl.BlockSpec((dilation, BLOCK, 3 * GROUP_WIDTH), lambda i: (0, jnp.maximum(i * nsub - 1, 0), 0))
    return pl.pallas_call(
        functools.partial(_attn_kernel, d=dilation, nsub=nsub),
        grid=(s // step_tokens,),
        in_specs=[cur, prev],
        out_specs=[pl.BlockSpec((step_tokens, GROUP_WIDTH), lambda i: (i, 0)),
                   pl.BlockSpec((step_tokens, HEAD_DIM), lambda i: (i, 0))],
        out_shape=[jax.ShapeDtypeStruct((s, GROUP_WIDTH), BF16), jax.ShapeDtypeStruct((s, HEAD_DIM), F32)],
        scratch_shapes=[pltpu.VMEM((HEADS_PER_GROUP, step_tokens, HEAD_DIM), F32),
                        pltpu.VMEM((step_tokens, HEAD_DIM), F32)],
        compiler_params=_params("arbitrary"),
        name=f"attn_d{dilation}",
    )(qkv, qkv)


def _discretize(lr, li, log_dt):
    dt = jnp.exp(log_dt)
    mag = jnp.exp(lr * dt)
    bar_re = mag * jnp.cos(li * dt)
    bar_im = mag * jnp.sin(li * dt)
    nr = bar_re - 1.0
    ni = bar_im
    den = lr * lr + li * li
    return bar_re, bar_im, (nr * lr + ni * li) / den, (ni * lr - nr * li) / den


def _expand_block_diag(compact, n_blocks):
    rows, b = compact.shape
    a = rows // n_blocks
    wide = n_blocks * b
    src_lane = lax.broadcasted_iota(jnp.int32, (b, wide), 0)
    dst_lane = lax.broadcasted_iota(jnp.int32, (b, wide), 1)
    tiled = _bdot(compact, (dst_lane % b == src_lane).astype(BF16))
    row_block = lax.broadcasted_iota(jnp.int32, (rows, wide), 0) // a
    col_block = lax.broadcasted_iota(jnp.int32, (rows, wide), 1) // b
    return jnp.where(row_block == col_block, tiled, 0.0).astype(BF16)


def _ssm_prep_kernel(lr_ref, li_ref, logdt_ref, lr_rep_ref, li_rep_ref, logdt_rep_ref, b_re_ref, b_im_ref,
                     c_re_ref, c_im_ref,
                     bmat_ref, cre_ref, cim_ref, e_re_ref, e_im_ref, einv_re_ref, einv_im_ref, lam_ref):
    _, _, z_re, z_im = _discretize(lr_rep_ref[...], li_rep_ref[...], logdt_rep_ref[...])
    b_re = b_re_ref[...]
    b_im = b_im_ref[...]
    bb_re = (z_re * b_re - z_im * b_im).astype(BF16)
    bb_im = (z_re * b_im + z_im * b_re).astype(BF16)
    for hf in range(SSM_HALVES):
        rows = slice(hf * HALF_WIDTH, (hf + 1) * HALF_WIDTH)
        bmat_ref[rows, :HALF_STATES] = _expand_block_diag(bb_re[rows], HALF_GROUPS)
        bmat_ref[rows, HALF_STATES:] = _expand_block_diag(bb_im[rows], HALF_GROUPS)
        srows = slice(hf * HALF_STATES, (hf + 1) * HALF_STATES)
        cre_ref[srows, :] = _expand_block_diag(c_re_ref[srows, :].astype(BF16), HALF_GROUPS)
        cim_ref[srows, :] = _expand_block_diag(c_im_ref[srows, :].astype(BF16), HALF_GROUPS)

    lr = lr_ref[...]
    li = li_ref[...]
    dt = jnp.exp(logdt_ref[...])
    whole = float(SSM_CHUNK)
    chunk_mag = jnp.exp(whole * (lr * dt))
    lam_ref[0:1, :] = chunk_mag * jnp.cos(whole * (li * dt))
    lam_ref[1:2, :] = chunk_mag * jnp.sin(whole * (li * dt))
    t = lax.broadcasted_iota(jnp.int32, (SSM_CHUNK, N_STATES), 0).astype(F32)
    grow = jnp.exp(t * (lr * dt))
    theta = t * (li * dt)
    c = jnp.cos(theta)
    s = jnp.sin(theta)
    e_re_ref[...] = (grow * c).astype(BF16)
    e_im_ref[...] = (grow * s).astype(BF16)
    shrink = jnp.exp(-t * (lr * dt))
    einv_re_ref[...] = (shrink * c).astype(BF16)
    einv_im_ref[...] = (-(shrink * s)).astype(BF16)


def _ssm_prep(a_re, a_im, log_dt, b_re, b_im, c_re, c_im):
    row = lambda v: v.reshape(1, N_STATES)
    rep = lambda v: jnp.repeat(v, SSM_GROUP, axis=0)
    logdt_gp = jnp.broadcast_to(log_dt[:, None], (SSM_GROUPS, SSM_STATE))
    chan_state = lambda b: jnp.swapaxes(b, 1, 2).reshape(SSM_WIDTH, SSM_STATE)
    state_chan = lambda c: jnp.swapaxes(c, 1, 2).reshape(N_STATES, SSM_GROUP)
    tab = jax.ShapeDtypeStruct((SSM_CHUNK, N_STATES), BF16)
    cmat = jax.ShapeDtypeStruct((N_STATES, HALF_WIDTH), BF16)
    return pl.pallas_call(
        _ssm_prep_kernel,
        out_shape=[jax.ShapeDtypeStruct((SSM_WIDTH, 2 * HALF_STATES), BF16), cmat, cmat, tab, tab, tab, tab,
                   jax.ShapeDtypeStruct((2, N_STATES), F32)],
        compiler_params=pltpu.CompilerParams(vmem_limit_bytes=V7X_VMEM_LIMIT_BYTES),
        name="ssm_prep",
    )(row(a_re), row(a_im), row(logdt_gp), rep(a_re), rep(a_im), rep(logdt_gp),
      chan_state(b_re), chan_state(b_im), state_chan(c_re), state_chan(c_im))


def _gelu_tanh(x):
    return 0.5 * x * (1.0 + jnp.tanh(math.sqrt(2.0 / math.pi) * (x + 0.044715 * (x * x * x))))


def _ssm_kernel(u_ref, bmat_ref, cre_ref, cim_ref, tri_ref, e_re_ref, e_im_ref, einv_re_ref, einv_im_ref,
                lam_ref, d_ref, y_ref, h_re_scr, h_im_scr, carry_scr):
    @pl.when(pl.program_id(0) == 0)
    def _():
        carry_scr[...] = jnp.zeros_like(carry_scr)

    u = u_ref[...]
    ub = u.astype(BF16)
    for hf in range(SSM_HALVES):
        chans = slice(hf * HALF_WIDTH, (hf + 1) * HALF_WIDTH)
        y = d_ref[:, chans] * u[:, chans]
        for blk in range(HALF_STATES // STATE_BLOCK):
            local = slice(blk * STATE_BLOCK, (blk + 1) * STATE_BLOCK)
            local_im = slice(HALF_STATES + blk * STATE_BLOCK, HALF_STATES + (blk + 1) * STATE_BLOCK)
            states = slice(hf * HALF_STATES + blk * STATE_BLOCK, hf * HALF_STATES + (blk + 1) * STATE_BLOCK)
            bu_re_all = _bdot(ub[:, chans], bmat_ref[chans, local])
            bu_im_all = _bdot(ub[:, chans], bmat_ref[chans, local_im])
            lam_re = lam_ref[0:1, states]
            lam_im = lam_ref[1:2, states]
            carry_re = carry_scr[0:1, states]
            carry_im = carry_scr[1:2, states]
            for c in range(u.shape[0] // SSM_CHUNK):
                rows = slice(c * SSM_CHUNK, (c + 1) * SSM_CHUNK)
                bu_re = bu_re_all[rows].astype(BF16)
                bu_im = bu_im_all[rows].astype(BF16)
                einv_re = einv_re_ref[:, states]
                einv_im = einv_im_ref[:, states]
                x = jnp.concatenate([bu_re * einv_re - bu_im * einv_im,
                                     bu_re * einv_im + bu_im * einv_re], axis=1)
                a = _bdot(tri_ref[...], x)
                a_re = a[:, :STATE_BLOCK] + carry_re
                a_im = a[:, STATE_BLOCK:] + carry_im
                e_re = e_re_ref[:, states]
                e_im = e_im_ref[:, states]
                a_re_b = a_re.astype(BF16)
                a_im_b = a_im.astype(BF16)
                h_re_scr[rows, :] = e_re * a_re_b - e_im * a_im_b
                h_im_scr[rows, :] = e_re * a_im_b + e_im * a_re_b
                last_re = a_re[SSM_CHUNK - 1:SSM_CHUNK, :]
                last_im = a_im[SSM_CHUNK - 1:SSM_CHUNK, :]
                carry_re = lam_re * last_re - lam_im * last_im
                carry_im = lam_re * last_im + lam_im * last_re
            carry_scr[0:1, states] = carry_re
            carry_scr[1:2, states] = carry_im
            y = y + _bdot(h_re_scr[...], cre_ref[states, :]) - _bdot(h_im_scr[...], cim_ref[states, :])
        y_ref[:, chans] = _gelu_tanh(y).astype(BF16)


def _ssm(u, bmat, cre, cim, tri, e_re, e_im, einv_re, einv_im, lam, d_row):
    s = u.shape[0]
    tm = min(SSM_ROWS, s)
    row = pl.BlockSpec((tm, SSM_WIDTH), lambda i: (i, 0))
    consts = (bmat, cre, cim, tri, e_re, e_im, einv_re, einv_im, lam, d_row)
    return pl.pallas_call(
        _ssm_kernel,
        grid=(s // tm,),
        in_specs=[row] + [_resident(c.shape) for c in consts],
        out_specs=row,
        out_shape=jax.ShapeDtypeStruct((s, SSM_WIDTH), BF16),
        scratch_shapes=[pltpu.VMEM((tm, STATE_BLOCK), BF16), pltpu.VMEM((tm, STATE_BLOCK), BF16),
                        pltpu.VMEM((2, N_STATES), F32)],
        compiler_params=_params("arbitrary"),
        name="ssm",
    )(u, *consts)


def _mix_kernel(*refs, n_cast):
    x_ref, o0_ref, o1_ref, o2_ref, l0_ref, l1_ref, l2_ref, y_ref, gates_ref = refs[:9]
    wap_ref, wa_ref, wb_ref, wout_ref = refs[9:13]
    cast_in, h_ref, cast_out = refs[13:13 + n_cast], refs[13 + n_cast], refs[14 + n_cast:]
    _cast_blocks(cast_in, cast_out)
    for rb in range(x_ref.shape[0] // MIX_SUB_ROWS):
        rows = slice(rb * MIX_SUB_ROWS, (rb + 1) * MIX_SUB_ROWS)
        ls = (l0_ref[rows, :], l1_ref[rows, :], l2_ref[rows, :])
        m_max = jnp.maximum(jnp.maximum(ls[0], ls[1]), ls[2])
        es = [jnp.exp2(l - m_max) for l in ls]
        dens = [pltpu.roll(l, HEAD_DIM - STAT_LANES // 2, 1) for l in ls]
        inv = 1.0 / (es[0] * dens[0] + es[1] * dens[1] + es[2] * dens[2])
        heads = []
        for h in range(HEADS_PER_GROUP):
            hs = slice(h * HEAD_DIM, (h + 1) * HEAD_DIM)
            col = slice(h * STAT_LANES, h * STAT_LANES + 1)
            heads.append(sum((e * inv)[:, col] * o_ref[rows, hs].astype(F32)
                             for e, o_ref in zip(es, (o0_ref, o1_ref, o2_ref))))
        attn = jnp.concatenate(heads, axis=1).astype(BF16)
        attn_d = _bdot(attn, wap_ref[...])
        y = y_ref[rows, :]
        ssm_d = _bdot(y, wa_ref[...]) * jax.nn.sigmoid(_bdot(y, wb_ref[...]))
        mix = (gates_ref[rows, :D_MODEL].astype(F32) * attn_d
               + gates_ref[rows, D_MODEL:].astype(F32) * ssm_d)
        h_ref[rows, :] = x_ref[rows, :] + _bdot(mix.astype(BF16), wout_ref[...])


def _mix(x, os, ls, y, gates, wap, wa, wb, wout, later_weights):
    s = x.shape[0]
    tm = min(MIX_ROWS, s)
    row = lambda w: pl.BlockSpec((tm, w), lambda i: (i, 0))
    weights = (wap, wa, wb, wout)
    cast_specs, cast_shapes = _cast_plan(later_weights, s // tm)
    return pl.pallas_call(
        functools.partial(_mix_kernel, n_cast=len(later_weights)),
        grid=(s // tm,),
        in_specs=[row(D_MODEL)] + [row(GROUP_WIDTH)] * 3 + [row(HEAD_DIM)] * 3
        + [row(SSM_WIDTH), row(2 * D_MODEL)] + [_resident(w.shape) for w in weights] + cast_specs,
        out_specs=[row(D_MODEL)] + cast_specs,
        out_shape=[jax.ShapeDtypeStruct((s, D_MODEL), F32)] + cast_shapes,
        compiler_params=_params("parallel"),
        name="mix",
    )(x, *os, *ls, y, gates, *weights, *later_weights)


def _ffn_kernel(h_ref, p_ref, gffn_ref, wg_ref, wu_ref, wd_ref, wpg_ref, wpp_ref, gfin_ref,
                out_ref, acc_scr):
    for rb in range(h_ref.shape[0] // FFN_SUB_ROWS):
        rows = slice(rb * FFN_SUB_ROWS, (rb + 1) * FFN_SUB_ROWS)
        h = h_ref[rows, :]
        n2 = _rmsnorm(h, gffn_ref[...]).astype(BF16)
        for idx, (c0, width) in enumerate(FFN_CHUNKS):
            gate = _bdot(n2, wg_ref[:, c0:c0 + width])
            up = _bdot(n2, wu_ref[:, c0:c0 + width])
            act = (gate * jax.nn.sigmoid(gate) * up).astype(BF16)
            part = _bdot(act, wd_ref[c0:c0 + width, :])
            if idx == 0:
                acc_scr[rows, :] = h + part
            else:
                acc_scr[rows, :] += part
        h2 = acc_scr[rows, :]
        ple = (jax.nn.sigmoid(_bdot(h2.astype(BF16), wpg_ref[...]))
               * _bdot(p_ref[rows, :].astype(BF16), wpp_ref[...]))
        out_ref[rows, :] = _rmsnorm(h2 + ple, gfin_ref[...])


def _ffn(h, p, g_ffn, wg, wu, wd, wpg, wpp, g_final):
    s = h.shape[0]
    tm = min(FFN_ROWS, s)
    row = lambda w: pl.BlockSpec((tm, w), lambda i: (i, 0))
    consts = (g_ffn, wg, wu, wd, wpg, wpp, g_final)
    return pl.pallas_call(
        _ffn_kernel,
        grid=(s // tm,),
        in_specs=[row(D_MODEL), row(PLE_DIM)] + [_resident(c.shape) for c in consts],
        out_specs=row(D_MODEL),
        out_shape=jax.ShapeDtypeStruct((s, D_MODEL), F32),
        scratch_shapes=[pltpu.VMEM((tm, D_MODEL), F32)],
        compiler_params=_params("parallel"),
        name="ffn",
    )(h, p, *consts)


def _layer(x, p, positions, g_mix, w_in, a_re, a_im, log_dt, b_re, b_im, c_re, c_im, d_skip,
           w_attn_proj, w_glu_a, w_glu_b, w_out, g_ffn, w_ffn_gate, w_ffn_up, w_ffn_down,
           w_ple_gate, w_ple_proj, g_final):
    bf = lambda w: w.astype(BF16)
    row = lambda v: v.reshape(1, -1).astype(F32)

    inv_freq = ROPE_THETA ** (-jnp.arange(ROPE_HALF, dtype=F32) * 2.0 / ROPE_DIM)
    outs = _proj(x, positions.reshape(1, -1), row(g_mix), inv_freq.reshape(ROPE_HALF, 1), bf(w_in),
                 (w_attn_proj, w_glu_a, w_glu_b, w_out, w_ffn_gate, w_ffn_up, w_ple_gate))
    qkvs, (u, gates) = outs[:N_GROUPS], outs[N_GROUPS:N_GROUPS + 2]
    wap, wga, wgb, wout, wfg, wfu, wpg = outs[N_GROUPS + 2:]

    attn_os, attn_ls = zip(*[_attn_group(qkvs[g], d) for g, d in enumerate(ATTN_DILATIONS)])

    bmat, cre, cim, e_re, e_im, einv_re, einv_im, lam = _ssm_prep(a_re, a_im, log_dt, b_re, b_im, c_re, c_im)
    tri = jnp.tril(jnp.ones((SSM_CHUNK, SSM_CHUNK), F32)).astype(BF16)
    y = _ssm(u, bmat, cre, cim, tri, e_re, e_im, einv_re, einv_im, lam, row(d_skip))

    h1, wfd, wpp = _mix(x, attn_os, attn_ls, y, gates, wap, wga, wgb, wout, (w_ffn_down, w_ple_proj))
    return _ffn(h1, p, row(g_ffn), wfg, wfu, wfd, wpg, wpp, row(g_final))


def kernel(x, p, positions, g_mix, w_in, a_re, a_im, log_dt, b_re, b_im, c_re, c_im, d_skip,
           w_attn_proj, w_glu_a, w_glu_b, w_out, g_ffn, w_ffn_gate, w_ffn_up, w_ffn_down,
           w_ple_gate, w_ple_proj, g_final):
    batch, depth = x.shape[0], p.shape[0]
    assert batch == 1 and depth == 1, "kernel supports the stated BATCH=1, DEPTH=1 problem"
    out = _layer(x[0], p[0, 0], positions[0], g_mix[0], w_in[0], a_re[0], a_im[0], log_dt[0],
                 b_re[0], b_im[0], c_re[0], c_im[0], d_skip[0], w_attn_proj[0], w_glu_a[0],
                 w_glu_b[0], w_out[0], g_ffn[0], w_ffn_gate[0], w_ffn_up[0], w_ffn_down[0],
                 w_ple_gate[0], w_ple_proj[0], g_final)
    return out[None]
```

```python
import functools
import math

import jax
import jax.numpy as jnp
from jax import lax
from jax.experimental import pallas as pl
from jax.experimental.pallas import tpu as pltpu

F32 = jnp.float32
BF16 = jnp.bfloat16

D_MODEL = 1024
HEAD_DIM = 128
HEADS_PER_GROUP = 4
GROUP_WIDTH = HEADS_PER_GROUP * HEAD_DIM
ATTN_DILATIONS = (1, 4, 16)
N_GROUPS = len(ATTN_DILATIONS)
QK_WIDTH = N_GROUPS * GROUP_WIDTH
BLOCK = 128
ROPE_THETA = 500000.0
ROPE_DIM = HEAD_DIM // 4
ROPE_HALF = ROPE_DIM // 2
SSM_WIDTH = 512
SSM_GROUP = 16
SSM_GROUPS = SSM_WIDTH // SSM_GROUP
SSM_STATE = 64
N_STATES = SSM_GROUPS * SSM_STATE
SSM_HALVES = 2
HALF_GROUPS = SSM_GROUPS // SSM_HALVES
HALF_WIDTH = SSM_WIDTH // SSM_HALVES
HALF_STATES = N_STATES // SSM_HALVES
STATE_BLOCK = 256
D_FF = 2816
PLE_DIM = 256
EPS = 1e-6
MASK_VALUE = -1e30

V7X_VMEM_LIMIT_BYTES = 56 * 1024 * 1024

PROJ_ROWS = 512
PROJ_SUB_ROWS = 256
ATTN_TOKENS = {1: 1024, 4: 2048, 16: 2048}
ATTN_TILES_PER_BODY = 32
STAT_LANES = HEAD_DIM // HEADS_PER_GROUP
SSM_CHUNK = 128
SSM_ROWS = 512
MIX_ROWS = 1024
MIX_SUB_ROWS = 256
FFN_ROWS = 512
FFN_SUB_ROWS = 256
FFN_CHUNKS = ((0, 1024), (1024, 1024), (2048, 768))


def _resident(shape):
    return pl.BlockSpec(shape, lambda *_: (0,) * len(shape), pipeline_mode=pl.Buffered(1))


def _params(*semantics):
    return pltpu.CompilerParams(dimension_semantics=semantics,
                                vmem_limit_bytes=V7X_VMEM_LIMIT_BYTES)


def _rmsnorm(x, g):
    return (x * lax.rsqrt(jnp.mean(x * x, axis=-1, keepdims=True) + EPS)) * g


def _bdot(a, b):
    return jnp.dot(a, b, preferred_element_type=F32)


def _cast_plan(weights, steps):
    specs = [pl.BlockSpec((w.shape[0] // steps, w.shape[1]), lambda i: (i, 0)) for w in weights]
    shapes = [jax.ShapeDtypeStruct(w.shape, BF16) for w in weights]
    return specs, shapes


def _cast_blocks(in_refs, out_refs):
    for src, dst in zip(in_refs, out_refs):
        dst[...] = src[...].astype(BF16)


def _proj_kernel(*refs, n_cast):
    x_ref, pos_ref, g_ref, invf_ref, w_ref = refs[:5]
    cast_in, refs = refs[5:5 + n_cast], refs[5 + n_cast:]
    qkv0, qkv1, qkv2, u_ref, gates_ref = refs[:5]
    cast_out, (n_scr, nperm_scr, cos_scr, sin_scr) = refs[5:5 + n_cast], refs[5 + n_cast:]
    _cast_blocks(cast_in, cast_out)
    rows = PROJ_SUB_ROWS
    lane = lax.broadcasted_iota(jnp.int32, (rows, HEAD_DIM), 1)
    first_half = lane < ROPE_HALF
    scale = math.log2(math.e) / math.sqrt(HEAD_DIM)

    for sb in range(x_ref.shape[0] // rows):
        base = sb * rows
        tile_rows = slice(base, base + rows)
        xn = _rmsnorm(x_ref[tile_rows, :], g_ref[...])
        n = xn.astype(BF16)
        for c in range(D_MODEL // HEAD_DIM):
            n_scr[c, tile_rows, :] = xn[:, c * HEAD_DIM:(c + 1) * HEAD_DIM]

        ang = invf_ref[...] * pos_ref[:, tile_rows].astype(F32)
        cos_t = jnp.cos(ang)
        sin_t = jnp.sin(ang)
        rest = (HEAD_DIM - ROPE_DIM, rows)
        cos_scr[tile_rows, :] = jnp.concatenate([cos_t, cos_t, jnp.ones(rest, F32)], axis=0).T
        sin_scr[tile_rows, :] = jnp.concatenate([-sin_t, sin_t, jnp.zeros(rest, F32)], axis=0).T

        for g, (d, qkv_ref) in enumerate(zip(ATTN_DILATIONS, (qkv0, qkv1, qkv2))):
            def residue_major(ref_2d):
                if d == 1:
                    return ref_2d[tile_rows, :]
                return jnp.concatenate(
                    [ref_2d[pl.ds(base + r, rows // d, stride=d), :] for r in range(d)], axis=0)

            cos = residue_major(cos_scr)
            sin_signed = residue_major(sin_scr)
            if d == 1:
                ng = n
            else:
                for c in range(D_MODEL // HEAD_DIM):
                    nperm_scr[tile_rows, c * HEAD_DIM:(c + 1) * HEAD_DIM] = (
                        residue_major(n_scr.at[c]).astype(BF16))
                ng = nperm_scr[tile_rows, :]

            def rotary(t):
                partner = jnp.where(first_half,
                                    pltpu.roll(t, HEAD_DIM - ROPE_HALF, 1),
                                    pltpu.roll(t, ROPE_HALF, 1))
                return t * cos + partner * sin_signed

            c0 = g * GROUP_WIDTH
            zq = _bdot(ng, w_ref[:, c0:c0 + GROUP_WIDTH])
            zk = _bdot(ng, w_ref[:, QK_WIDTH + c0:QK_WIDTH + c0 + GROUP_WIDTH])
            zv = _bdot(ng, w_ref[:, 2 * QK_WIDTH + c0:2 * QK_WIDTH + c0 + GROUP_WIDTH])
            sub_rows = slice(base // d, (base + rows) // d)
            for h in range(HEADS_PER_GROUP):
                hs = slice(h * HEAD_DIM, (h + 1) * HEAD_DIM)
                ks = slice(GROUP_WIDTH + h * HEAD_DIM, GROUP_WIDTH + (h + 1) * HEAD_DIM)
                qkv_ref[:, sub_rows, hs] = (
                    (rotary(zq[:, hs]) * scale).astype(BF16).reshape(d, rows // d, HEAD_DIM))
                qkv_ref[:, sub_rows, ks] = rotary(zk[:, hs]).astype(BF16).reshape(d, rows // d, HEAD_DIM)
            qkv_ref[:, sub_rows, 2 * GROUP_WIDTH:] = zv.astype(BF16).reshape(d, rows // d, GROUP_WIDTH)
        o1 = 3 * QK_WIDTH
        u_ref[tile_rows, :] = _bdot(n, w_ref[:, o1:o1 + SSM_WIDTH])
        o2 = o1 + SSM_WIDTH
        gates_ref[tile_rows, :] = jax.nn.sigmoid(_bdot(n, w_ref[:, o2:o2 + 2 * D_MODEL])).astype(BF16)


def _proj(x, pos_row, g_mix, inv_freq_col, w_in, later_weights):
    s = x.shape[0]
    tm = min(PROJ_ROWS, s)
    row = lambda w: pl.BlockSpec((tm, w), lambda i: (i, 0))
    cast_specs, cast_shapes = _cast_plan(later_weights, s // tm)
    grp_specs, grp_shapes = [], []
    for d in ATTN_DILATIONS:
        grp_specs.append(pl.BlockSpec((d, tm // d, 3 * GROUP_WIDTH), lambda i: (0, i, 0)))
        grp_shapes.append(jax.ShapeDtypeStruct((d, s // d, 3 * GROUP_WIDTH), BF16))
    return pl.pallas_call(
        functools.partial(_proj_kernel, n_cast=len(later_weights)),
        grid=(s // tm,),
        in_specs=[row(D_MODEL), pl.BlockSpec((1, tm), lambda i: (0, i)), _resident((1, D_MODEL)),
                  _resident((ROPE_HALF, 1)), _resident(w_in.shape)] + cast_specs,
        out_specs=grp_specs + [row(SSM_WIDTH), row(2 * D_MODEL)] + cast_specs,
        out_shape=grp_shapes + [jax.ShapeDtypeStruct((s, SSM_WIDTH), F32),
                                jax.ShapeDtypeStruct((s, 2 * D_MODEL), BF16)] + cast_shapes,
        scratch_shapes=[pltpu.VMEM((D_MODEL // HEAD_DIM, tm, HEAD_DIM), F32), pltpu.VMEM((tm, D_MODEL), BF16),
                        pltpu.VMEM((tm, HEAD_DIM), F32), pltpu.VMEM((tm, HEAD_DIM), F32)],
        compiler_params=_params("parallel"),
        name="proj",
    )(x, pos_row, g_mix, inv_freq_col, w_in, *later_weights)


def _attn_kernel(qkv_ref, o_ref, l_ref, o_scr, l_scr, prev_scr, *, d, nsub):
    not_first_block = pl.program_id(0) > 0

    @pl.when(pl.program_id(0) == 0)
    def _():
        prev_scr[...] = jnp.zeros_like(prev_scr)

    qi = lax.broadcasted_iota(jnp.int32, (BLOCK, 2 * BLOCK), 0)
    kj = lax.broadcasted_iota(jnp.int32, (BLOCK, 2 * BLOCK), 1)
    rel = BLOCK + qi - kj
    band = (rel >= 0) & (rel <= BLOCK)
    band_first = band & ((kj >= BLOCK) | not_first_block)
    lane = lax.broadcasted_iota(jnp.int32, (BLOCK, HEAD_DIM), 1)

    def residue(r):
        for b in range(nsub):
            rows = slice(b * BLOCK, (b + 1) * BLOCK)
            mask = band_first if b == 0 else band
            token_rows = pl.ds(b * BLOCK * d + r, BLOCK, stride=d) if d > 1 else rows
            for h in range(HEADS_PER_GROUP):
                hs = slice(h * HEAD_DIM, (h + 1) * HEAD_DIM)
                ks = slice(GROUP_WIDTH + h * HEAD_DIM, GROUP_WIDTH + (h + 1) * HEAD_DIM)
                vs = slice(2 * GROUP_WIDTH + h * HEAD_DIM, 2 * GROUP_WIDTH + (h + 1) * HEAD_DIM)
                q = qkv_ref[r, rows, hs]
                if b == 0:
                    pk = slice(h * HEAD_DIM, (h + 1) * HEAD_DIM)
                    pv = slice(GROUP_WIDTH + h * HEAD_DIM, GROUP_WIDTH + (h + 1) * HEAD_DIM)
                    kw = jnp.concatenate([prev_scr[r, :, pk], qkv_ref[r, rows, ks]], axis=0)
                    vw = jnp.concatenate([prev_scr[r, :, pv], qkv_ref[r, rows, vs]], axis=0)
                else:
                    win = slice((b - 1) * BLOCK, (b + 1) * BLOCK)
                    kw = qkv_ref[r, win, ks]
                    vw = qkv_ref[r, win, vs]
                s = lax.dot_general(q, kw, (((1,), (1,)), ((), ())), preferred_element_type=F32)
                s = jnp.where(mask, s, MASK_VALUE)
                m = jnp.max(s, axis=-1, keepdims=True)
                p = jnp.exp2(s - m)
                den = jnp.sum(p, axis=-1, keepdims=True)
                o_scr[h, token_rows, :] = _bdot(p.astype(BF16), vw)
                m_b = jnp.broadcast_to(m, (BLOCK, HEAD_DIM))
                den_b = jnp.broadcast_to(den, (BLOCK, HEAD_DIM))
                if h > 0:
                    m_b = jnp.where(lane >= h * STAT_LANES, m_b, stats)
                stats = jnp.where(lane >= h * STAT_LANES + STAT_LANES // 2, den_b, m_b)
            l_scr[token_rows, :] = stats

    per_iter = max(1, min(d, ATTN_TILES_PER_BODY // (nsub * HEADS_PER_GROUP)))
    if per_iter == d:
        for r in range(d):
            residue(r)
    else:
        def body(i, carry):
            for j in range(per_iter):
                residue(i * per_iter + j)
            return carry
        lax.fori_loop(0, d // per_iter, body, 0)

    for h in range(HEADS_PER_GROUP):
        o_ref[:, h * HEAD_DIM:(h + 1) * HEAD_DIM] = o_scr[h].astype(BF16)
    l_ref[...] = l_scr[...]
    prev_scr[...] = qkv_ref[:, (nsub - 1) * BLOCK:nsub * BLOCK, GROUP_WIDTH:]


def _attn_group(qkv, dilation):
    sub_len = qkv.shape[1]
    s = sub_len * dilation
    step_tokens = min(ATTN_TOKENS[dilation], s)
    qb = step_tokens // dilation
    nsub = qb // BLOCK
    cur = pl.BlockSpec((dilation, qb, 3 * GROUP_WIDTH), lambda i: (0, i, 0))
    return pl.pallas_call(
        functools.partial(_attn_kernel, d=dilation, nsub=nsub),
        grid=(s // step_tokens,),
        in_specs=[cur],
        out_specs=[pl.BlockSpec((step_tokens, GROUP_WIDTH), lambda i: (i, 0)),
                   pl.BlockSpec((step_tokens, HEAD_DIM), lambda i: (i, 0))],
        out_shape=[jax.ShapeDtypeStruct((s, GROUP_WIDTH), BF16), jax.ShapeDtypeStruct((s, HEAD_DIM), F32)],
        scratch_shapes=[pltpu.VMEM((HEADS_PER_GROUP, step_tokens, HEAD_DIM), F32),
                        pltpu.VMEM((step_tokens, HEAD_DIM), F32),
                        pltpu.VMEM((dilation, BLOCK, 2 * GROUP_WIDTH), BF16)],
        compiler_params=_params("arbitrary"),
        name=f"attn_d{dilation}",
    )(qkv)


def _discretize(lr, li, log_dt):
    dt = jnp.exp(log_dt)
    mag = jnp.exp(lr * dt)
    bar_re = mag * jnp.cos(li * dt)
    bar_im = mag * jnp.sin(li * dt)
    nr = bar_re - 1.0
    ni = bar_im
    den = lr * lr + li * li
    return bar_re, bar_im, (nr * lr + ni * li) / den, (ni * lr - nr * li) / den


def _expand_block_diag(compact, n_blocks):
    rows, b = compact.shape
    a = rows // n_blocks
    wide = n_blocks * b
    src_lane = lax.broadcasted_iota(jnp.int32, (b, wide), 0)
    dst_lane = lax.broadcasted_iota(jnp.int32, (b, wide), 1)
    tiled = _bdot(compact, (dst_lane % b == src_lane).astype(BF16))
    row_block = lax.broadcasted_iota(jnp.int32, (rows, wide), 0) // a
    col_block = lax.broadcasted_iota(jnp.int32, (rows, wide), 1) // b
    return jnp.where(row_block == col_block, tiled, 0.0).astype(BF16)


def _ssm_prep_kernel(lr_ref, li_ref, logdt_ref, lr_rep_ref, li_rep_ref, logdt_rep_ref, b_re_ref, b_im_ref,
                     c_re_ref, c_im_ref,
                     bmat_ref, cre_ref, cim_ref, e_re_ref, e_im_ref, einv_re_ref, einv_im_ref, lam_ref):
    _, _, z_re, z_im = _discretize(lr_rep_ref[...], li_rep_ref[...], logdt_rep_ref[...])
    b_re = b_re_ref[...]
    b_im = b_im_ref[...]
    bb_re = (z_re * b_re - z_im * b_im).astype(BF16)
    bb_im = (z_re * b_im + z_im * b_re).astype(BF16)
    for hf in range(SSM_HALVES):
        rows = slice(hf * HALF_WIDTH, (hf + 1) * HALF_WIDTH)
        bmat_ref[rows, :HALF_STATES] = _expand_block_diag(bb_re[rows], HALF_GROUPS)
        bmat_ref[rows, HALF_STATES:] = _expand_block_diag(bb_im[rows], HALF_GROUPS)
        srows = slice(hf * HALF_STATES, (hf + 1) * HALF_STATES)
        cre_ref[srows, :] = _expand_block_diag(c_re_ref[srows, :].astype(BF16), HALF_GROUPS)
        cim_ref[srows, :] = _expand_block_diag(c_im_ref[srows, :].astype(BF16), HALF_GROUPS)

    lr = lr_ref[...]
    li = li_ref[...]
    dt = jnp.exp(logdt_ref[...])
    whole = float(SSM_CHUNK)
    chunk_mag = jnp.exp(whole * (lr * dt))
    lam_ref[0:1, :] = chunk_mag * jnp.cos(whole * (li * dt))
    lam_ref[1:2, :] = chunk_mag * jnp.sin(whole * (li * dt))
    t = lax.broadcasted_iota(jnp.int32, (SSM_CHUNK, N_STATES), 0).astype(F32)
    grow = jnp.exp(t * (lr * dt))
    theta = t * (li * dt)
    c = jnp.cos(theta)
    s = jnp.sin(theta)
    e_re_ref[...] = (grow * c).astype(BF16)
    e_im_ref[...] = (grow * s).astype(BF16)
    shrink = jnp.exp(-t * (lr * dt))
    einv_re_ref[...] = (shrink * c).astype(BF16)
    einv_im_ref[...] = (-(shrink * s)).astype(BF16)


def _ssm_prep(a_re, a_im, log_dt, b_re, b_im, c_re, c_im):
    row = lambda v: v.reshape(1, N_STATES)
    rep = lambda v: jnp.repeat(v, SSM_GROUP, axis=0)
    logdt_gp = jnp.broadcast_to(log_dt[:, None], (SSM_GROUPS, SSM_STATE))
    chan_state = lambda b: jnp.swapaxes(b, 1, 2).reshape(SSM_WIDTH, SSM_STATE)
    state_chan = lambda c: jnp.swapaxes(c, 1, 2).reshape(N_STATES, SSM_GROUP)
    tab = jax.ShapeDtypeStruct((SSM_CHUNK, N_STATES), BF16)
    cmat = jax.ShapeDtypeStruct((N_STATES, HALF_WIDTH), BF16)
    return pl.pallas_call(
        _ssm_prep_kernel,
        out_shape=[jax.ShapeDtypeStruct((SSM_WIDTH, 2 * HALF_STATES), BF16), cmat, cmat, tab, tab, tab, tab,
                   jax.ShapeDtypeStruct((2, N_STATES), F32)],
        compiler_params=pltpu.CompilerParams(vmem_limit_bytes=V7X_VMEM_LIMIT_BYTES),
        name="ssm_prep",
    )(row(a_re), row(a_im), row(logdt_gp), rep(a_re), rep(a_im), rep(logdt_gp),
      chan_state(b_re), chan_state(b_im), state_chan(c_re), state_chan(c_im))


def _gelu_tanh(x):
    return 0.5 * x * (1.0 + jnp.tanh(math.sqrt(2.0 / math.pi) * (x + 0.044715 * (x * x * x))))


def _ssm_kernel(u_ref, bmat_ref, cre_ref, cim_ref, tri_ref, e_re_ref, e_im_ref, einv_re_ref, einv_im_ref,
                lam_ref, d_ref, y_ref, h_re_scr, h_im_scr, carry_scr):
    @pl.when(pl.program_id(0) == 0)
    def _():
        carry_scr[...] = jnp.zeros_like(carry_scr)

    u = u_ref[...]
    ub = u.astype(BF16)
    for hf in range(SSM_HALVES):
        chans = slice(hf * HALF_WIDTH, (hf + 1) * HALF_WIDTH)
        y = d_ref[:, chans] * u[:, chans]
        for blk in range(HALF_STATES // STATE_BLOCK):
            local = slice(blk * STATE_BLOCK, (blk + 1) * STATE_BLOCK)
            local_im = slice(HALF_STATES + blk * STATE_BLOCK, HALF_STATES + (blk + 1) * STATE_BLOCK)
            states = slice(hf * HALF_STATES + blk * STATE_BLOCK, hf * HALF_STATES + (blk + 1) * STATE_BLOCK)
            bu_re_all = _bdot(ub[:, chans], bmat_ref[chans, local])
            bu_im_all = _bdot(ub[:, chans], bmat_ref[chans, local_im])
            lam_re = lam_ref[0:1, states]
            lam_im = lam_ref[1:2, states]
            carry_re = carry_scr[0:1, states]
            carry_im = carry_scr[1:2, states]
            for c in range(u.shape[0] // SSM_CHUNK):
                rows = slice(c * SSM_CHUNK, (c + 1) * SSM_CHUNK)
                bu_re = bu_re_all[rows].astype(BF16)
                bu_im = bu_im_all[rows].astype(BF16)
                einv_re = einv_re_ref[:, states]
                einv_im = einv_im_ref[:, states]
                x = jnp.concatenate([bu_re * einv_re - bu_im * einv_im,
                                     bu_re * einv_im + bu_im * einv_re], axis=1)
                a = _bdot(tri_ref[...], x)
                a_re = a[:, :STATE_BLOCK] + carry_re
                a_im = a[:, STATE_BLOCK:] + carry_im
                e_re = e_re_ref[:, states]
                e_im = e_im_ref[:, states]
                a_re_b = a_re.astype(BF16)
                a_im_b = a_im.astype(BF16)
                h_re_scr[rows, :] = e_re * a_re_b - e_im * a_im_b
                h_im_scr[rows, :] = e_re * a_im_b + e_im * a_re_b
                last_re = a_re[SSM_CHUNK - 1:SSM_CHUNK, :]
                last_im = a_im[SSM_CHUNK - 1:SSM_CHUNK, :]
                carry_re = lam_re * last_re - lam_im * last_im
                carry_im = lam_re * last_im + lam_im * last_re
            carry_scr[0:1, states] = carry_re
            carry_scr[1:2, states] = carry_im
            y = y + _bdot(h_re_scr[...], cre_ref[states, :]) - _bdot(h_im_scr[...], cim_ref[states, :])
        y_ref[:, chans] = _gelu_tanh(y).astype(BF16)


def _ssm(u, bmat, cre, cim, tri, e_re, e_im, einv_re, einv_im, lam, d_row):
    s = u.shape[0]
    tm = min(SSM_ROWS, s)
    row = pl.BlockSpec((tm, SSM_WIDTH), lambda i: (i, 0))
    consts = (bmat, cre, cim, tri, e_re, e_im, einv_re, einv_im, lam, d_row)
    return pl.pallas_call(
        _ssm_kernel,
        grid=(s // tm,),
        in_specs=[row] + [_resident(c.shape) for c in consts],
        out_specs=row,
        out_shape=jax.ShapeDtypeStruct((s, SSM_WIDTH), BF16),
        scratch_shapes=[pltpu.VMEM((tm, STATE_BLOCK), BF16), pltpu.VMEM((tm, STATE_BLOCK), BF16),
                        pltpu.VMEM((2, N_STATES), F32)],
        compiler_params=_params("arbitrary"),
        name="ssm",
    )(u, *consts)


def _mix_kernel(*refs, n_cast):
    x_ref, o0_ref, o1_ref, o2_ref, l0_ref, l1_ref, l2_ref, y_ref, gates_ref = refs[:9]
    wap_ref, wa_ref, wb_ref, wout_ref = refs[9:13]
    cast_in, h_ref, cast_out = refs[13:13 + n_cast], refs[13 + n_cast], refs[14 + n_cast:]
    _cast_blocks(cast_in, cast_out)
    for rb in range(x_ref.shape[0] // MIX_SUB_ROWS):
        rows = slice(rb * MIX_SUB_ROWS, (rb + 1) * MIX_SUB_ROWS)
        ls = (l0_ref[rows, :], l1_ref[rows, :], l2_ref[rows, :])
        m_max = jnp.maximum(jnp.maximum(ls[0], ls[1]), ls[2])
        es = [jnp.exp2(l - m_max) for l in ls]
        dens = [pltpu.roll(l, HEAD_DIM - STAT_LANES // 2, 1) for l in ls]
        inv = 1.0 / (es[0] * dens[0] + es[1] * dens[1] + es[2] * dens[2])
        heads = []
        for h in range(HEADS_PER_GROUP):
            hs = slice(h * HEAD_DIM, (h + 1) * HEAD_DIM)
            col = slice(h * STAT_LANES, h * STAT_LANES + 1)
            heads.append(sum((e * inv)[:, col] * o_ref[rows, hs].astype(F32)
                             for e, o_ref in zip(es, (o0_ref, o1_ref, o2_ref))))
        attn = jnp.concatenate(heads, axis=1).astype(BF16)
        attn_d = _bdot(attn, wap_ref[...])
        y = y_ref[rows, :]
        ssm_d = _bdot(y, wa_ref[...]) * jax.nn.sigmoid(_bdot(y, wb_ref[...]))
        mix = (gates_ref[rows, :D_MODEL].astype(F32) * attn_d
               + gates_ref[rows, D_MODEL:].astype(F32) * ssm_d)
        h_ref[rows, :] = x_ref[rows, :] + _bdot(mix.astype(BF16), wout_ref[...])


def _mix(x, os, ls, y, gates, wap, wa, wb, wout, later_weights):
    s = x.shape[0]
    tm = min(MIX_ROWS, s)
    row = lambda w: pl.BlockSpec((tm, w), lambda i: (i, 0))
    weights = (wap, wa, wb, wout)
    cast_specs, cast_shapes = _cast_plan(later_weights, s // tm)
    return pl.pallas_call(
        functools.partial(_mix_kernel, n_cast=len(later_weights)),
        grid=(s // tm,),
        in_specs=[row(D_MODEL)] + [row(GROUP_WIDTH)] * 3 + [row(HEAD_DIM)] * 3
        + [row(SSM_WIDTH), row(2 * D_MODEL)] + [_resident(w.shape) for w in weights] + cast_specs,
        out_specs=[row(D_MODEL)] + cast_specs,
        out_shape=[jax.ShapeDtypeStruct((s, D_MODEL), F32)] + cast_shapes,
        compiler_params=_params("parallel"),
        name="mix",
    )(x, *os, *ls, y, gates, *weights, *later_weights)


def _ffn_kernel(h_ref, p_ref, gffn_ref, wg_ref, wu_ref, wd_ref, wpg_ref, wpp_ref, gfin_ref,
                out_ref, acc_scr):
    for rb in range(h_ref.shape[0] // FFN_SUB_ROWS):
        rows = slice(rb * FFN_SUB_ROWS, (rb + 1) * FFN_SUB_ROWS)
        h = h_ref[rows, :]
        n2 = _rmsnorm(h, gffn_ref[...]).astype(BF16)
        for idx, (c0, width) in enumerate(FFN_CHUNKS):
            gate = _bdot(n2, wg_ref[:, c0:c0 + width])
            up = _bdot(n2, wu_ref[:, c0:c0 + width])
            act = (gate * jax.nn.sigmoid(gate) * up).astype(BF16)
            part = _bdot(act, wd_ref[c0:c0 + width, :])
            if idx == 0:
                acc_scr[rows, :] = h + part
            else:
                acc_scr[rows, :] += part
        h2 = acc_scr[rows, :]
        ple = (jax.nn.sigmoid(_bdot(h2.astype(BF16), wpg_ref[...]))
               * _bdot(p_ref[rows, :].astype(BF16), wpp_ref[...]))
        out_ref[rows, :] = _rmsnorm(h2 + ple, gfin_ref[...])


def _ffn(h, p, g_ffn, wg, wu, wd, wpg, wpp, g_final):
    s = h.shape[0]
    tm = min(FFN_ROWS, s)
    row = lambda w: pl.BlockSpec((tm, w), lambda i: (i, 0))
    consts = (g_ffn, wg, wu, wd, wpg, wpp, g_final)
    return pl.pallas_call(
        _ffn_kernel,
        grid=(s // tm,),
        in_specs=[row(D_MODEL), row(PLE_DIM)] + [_resident(c.shape) for c in consts],
        out_specs=row(D_MODEL),
        out_shape=jax.ShapeDtypeStruct((s, D_MODEL), F32),
        scratch_shapes=[pltpu.VMEM((tm, D_MODEL), F32)],
        compiler_params=_params("parallel"),
        name="ffn",
    )(h, p, *consts)


def _layer(x, p, positions, g_mix, w_in, a_re, a_im, log_dt, b_re, b_im, c_re, c_im, d_skip,
           w_attn_proj, w_glu_a, w_glu_b, w_out, g_ffn, w_ffn_gate, w_ffn_up, w_ffn_down,
           w_ple_gate, w_ple_proj, g_final):
    bf = lambda w: w.astype(BF16)
    row = lambda v: v.reshape(1, -1).astype(F32)

    inv_freq = ROPE_THETA ** (-jnp.arange(ROPE_HALF, dtype=F32) * 2.0 / ROPE_DIM)
    outs = _proj(x, positions.reshape(1, -1), row(g_mix), inv_freq.reshape(ROPE_HALF, 1), bf(w_in),
                 (w_attn_proj, w_glu_a, w_glu_b, w_out, w_ffn_gate, w_ffn_up, w_ple_gate))
    qkvs, (u, gates) = outs[:N_GROUPS], outs[N_GROUPS:N_GROUPS + 2]
    wap, wga, wgb, wout, wfg, wfu, wpg = outs[N_GROUPS + 2:]

    attn_os, attn_ls = zip(*[_attn_group(qkvs[g], d) for g, d in enumerate(ATTN_DILATIONS)])

    bmat, cre, cim, e_re, e_im, einv_re, einv_im, lam = _ssm_prep(a_re, a_im, log_dt, b_re, b_im, c_re, c_im)
    tri = jnp.tril(jnp.ones((SSM_CHUNK, SSM_CHUNK), F32)).astype(BF16)
    y = _ssm(u, bmat, cre, cim, tri, e_re, e_im, einv_re, einv_im, lam, row(d_skip))

    h1, wfd, wpp = _mix(x, attn_os, attn_ls, y, gates, wap, wga, wgb, wout, (w_ffn_down, w_ple_proj))
    return _ffn(h1, p, row(g_ffn), wfg, wfu, wfd, wpg, wpp, row(g_final))


def kernel(x, p, positions, g_mix, w_in, a_re, a_im, log_dt, b_re, b_im, c_re, c_im, d_skip,
           w_attn_proj, w_glu_a, w_glu_b, w_out, g_ffn, w_ffn_gate, w_ffn_up, w_ffn_down,
           w_ple_gate, w_ple_proj, g_final):
    batch, depth = x.shape[0], p.shape[0]
    assert batch == 1 and depth == 1, "kernel supports the stated BATCH=1, DEPTH=1 problem"
    out = _layer(x[0], p[0, 0], positions[0], g_mix[0], w_in[0], a_re[0], a_im[0], log_dt[0],
                 b_re[0], b_im[0], c_re[0], c_im[0], d_skip[0], w_attn_proj[0], w_glu_a[0],
                 w_glu_b[0], w_out[0], g_ffn[0], w_ffn_gate[0], w_ffn_up[0], w_ffn_down[0],
                 w_ple_gate[0], w_ple_proj[0], g_final)
    return out[None]
```

```python
import functools
import math

import jax
import jax.numpy as jnp
from jax import lax
from jax.experimental import pallas as pl
from jax.experimental.pallas import tpu as pltpu

F32 = jnp.float32
BF16 = jnp.bfloat16

D_MODEL = 1024
HEAD_DIM = 128
HEADS_PER_GROUP = 4
GROUP_WIDTH = HEADS_PER_GROUP * HEAD_DIM
ATTN_DILATIONS = (1, 4, 16)
N_GROUPS = len(ATTN_DILATIONS)
QK_WIDTH = N_GROUPS * GROUP_WIDTH
BLOCK = 128
ROPE_THETA = 500000.0
ROPE_DIM = HEAD_DIM // 4
ROPE_HALF = ROPE_DIM // 2
SSM_WIDTH = 512
SSM_GROUP = 16
SSM_GROUPS = SSM_WIDTH // SSM_GROUP
SSM_STATE = 64
N_STATES = SSM_GROUPS * SSM_STATE
SSM_HALVES = 2
HALF_GROUPS = SSM_GROUPS // SSM_HALVES
HALF_WIDTH = SSM_WIDTH // SSM_HALVES
HALF_STATES = N_STATES // SSM_HALVES
STATE_BLOCK = 256
D_FF = 2816
PLE_DIM = 256
EPS = 1e-6
MASK_VALUE = -1e30

V7X_VMEM_LIMIT_BYTES = 56 * 1024 * 1024

PROJ_ROWS = 512
PROJ_SUB_ROWS = 256
ATTN_TOKENS = {1: 1024, 4: 2048, 16: 2048}
ATTN_TILES_PER_BODY = 32
STAT_LANES = HEAD_DIM // HEADS_PER_GROUP
SSM_CHUNK = 128
SSM_ROWS = 512
PREP_STEPS = 8
MIX_ROWS = 1024
MIX_SUB_ROWS = 256
FFN_ROWS = 512
FFN_SUB_ROWS = 256
FFN_CHUNKS = ((0, 1024), (1024, 1024), (2048, 768))


def _resident(shape):
    return pl.BlockSpec(shape, lambda *_: (0,) * len(shape), pipeline_mode=pl.Buffered(1))


def _params(*semantics):
    return pltpu.CompilerParams(dimension_semantics=semantics,
                                vmem_limit_bytes=V7X_VMEM_LIMIT_BYTES)


def _rmsnorm(x, g):
    return (x * lax.rsqrt(jnp.mean(x * x, axis=-1, keepdims=True) + EPS)) * g


def _bdot(a, b):
    return jnp.dot(a, b, preferred_element_type=F32)


def _cast_plan(weights, steps):
    specs = [pl.BlockSpec((w.shape[0] // steps, w.shape[1]), lambda i: (i, 0)) for w in weights]
    shapes = [jax.ShapeDtypeStruct(w.shape, BF16) for w in weights]
    return specs, shapes


def _cast_blocks(in_refs, out_refs):
    for src, dst in zip(in_refs, out_refs):
        dst[...] = src[...].astype(BF16)


def _proj_kernel(*refs, n_cast):
    x_ref, pos_ref, g_ref, invf_ref, w_ref = refs[:5]
    cast_in, refs = refs[5:5 + n_cast], refs[5 + n_cast:]
    qkv0, qkv1, qkv2, u_ref, gates_ref = refs[:5]
    cast_out, (n_scr, nperm_scr, cos_scr, sin_scr) = refs[5:5 + n_cast], refs[5 + n_cast:]
    _cast_blocks(cast_in, cast_out)
    rows = PROJ_SUB_ROWS
    lane = lax.broadcasted_iota(jnp.int32, (rows, HEAD_DIM), 1)
    first_half = lane < ROPE_HALF
    scale = math.log2(math.e) / math.sqrt(HEAD_DIM)

    for sb in range(x_ref.shape[0] // rows):
        base = sb * rows
        tile_rows = slice(base, base + rows)
        xn = _rmsnorm(x_ref[tile_rows, :], g_ref[...])
        n = xn.astype(BF16)
        for c in range(D_MODEL // HEAD_DIM):
            n_scr[c, tile_rows, :] = xn[:, c * HEAD_DIM:(c + 1) * HEAD_DIM]

        ang = invf_ref[...] * pos_ref[:, tile_rows].astype(F32)
        cos_t = jnp.cos(ang)
        sin_t = jnp.sin(ang)
        rest = (HEAD_DIM - ROPE_DIM, rows)
        cos_scr[tile_rows, :] = jnp.concatenate([cos_t, cos_t, jnp.ones(rest, F32)], axis=0).T
        sin_scr[tile_rows, :] = jnp.concatenate([-sin_t, sin_t, jnp.zeros(rest, F32)], axis=0).T

        for g, (d, qkv_ref) in enumerate(zip(ATTN_DILATIONS, (qkv0, qkv1, qkv2))):
            def residue_major(ref_2d):
                if d == 1:
                    return ref_2d[tile_rows, :]
                return jnp.concatenate(
                    [ref_2d[pl.ds(base + r, rows // d, stride=d), :] for r in range(d)], axis=0)

            cos = residue_major(cos_scr)
            sin_signed = residue_major(sin_scr)
            if d == 1:
                ng = n
            else:
                for c in range(D_MODEL // HEAD_DIM):
                    nperm_scr[tile_rows, c * HEAD_DIM:(c + 1) * HEAD_DIM] = (
                        residue_major(n_scr.at[c]).astype(BF16))
                ng = nperm_scr[tile_rows, :]

            def rotary(t):
                partner = jnp.where(first_half,
                                    pltpu.roll(t, HEAD_DIM - ROPE_HALF, 1),
                                    pltpu.roll(t, ROPE_HALF, 1))
                return t * cos + partner * sin_signed

            c0 = g * GROUP_WIDTH
            zq = _bdot(ng, w_ref[:, c0:c0 + GROUP_WIDTH])
            zk = _bdot(ng, w_ref[:, QK_WIDTH + c0:QK_WIDTH + c0 + GROUP_WIDTH])
            zv = _bdot(ng, w_ref[:, 2 * QK_WIDTH + c0:2 * QK_WIDTH + c0 + GROUP_WIDTH])
            sub_rows = slice(base // d, (base + rows) // d)
            for h in range(HEADS_PER_GROUP):
                hs = slice(h * HEAD_DIM, (h + 1) * HEAD_DIM)
                ks = slice(GROUP_WIDTH + h * HEAD_DIM, GROUP_WIDTH + (h + 1) * HEAD_DIM)
                qkv_ref[:, sub_rows, hs] = (
                    (rotary(zq[:, hs]) * scale).astype(BF16).reshape(d, rows // d, HEAD_DIM))
                qkv_ref[:, sub_rows, ks] = rotary(zk[:, hs]).astype(BF16).reshape(d, rows // d, HEAD_DIM)
            qkv_ref[:, sub_rows, 2 * GROUP_WIDTH:] = zv.astype(BF16).reshape(d, rows // d, GROUP_WIDTH)
        o1 = 3 * QK_WIDTH
        u_ref[tile_rows, :] = _bdot(n, w_ref[:, o1:o1 + SSM_WIDTH])
        o2 = o1 + SSM_WIDTH
        gates_ref[tile_rows, :] = jax.nn.sigmoid(_bdot(n, w_ref[:, o2:o2 + 2 * D_MODEL])).astype(BF16)


def _proj(x, pos_row, g_mix, inv_freq_col, w_in, later_weights):
    s = x.shape[0]
    tm = min(PROJ_ROWS, s)
    row = lambda w: pl.BlockSpec((tm, w), lambda i: (i, 0))
    cast_specs, cast_shapes = _cast_plan(later_weights, s // tm)
    grp_specs, grp_shapes = [], []
    for d in ATTN_DILATIONS:
        grp_specs.append(pl.BlockSpec((d, tm // d, 3 * GROUP_WIDTH), lambda i: (0, i, 0)))
        grp_shapes.append(jax.ShapeDtypeStruct((d, s // d, 3 * GROUP_WIDTH), BF16))
    return pl.pallas_call(
        functools.partial(_proj_kernel, n_cast=len(later_weights)),
        grid=(s // tm,),
        in_specs=[row(D_MODEL), pl.BlockSpec((1, tm), lambda i: (0, i)), _resident((1, D_MODEL)),
                  _resident((ROPE_HALF, 1)), _resident(w_in.shape)] + cast_specs,
        out_specs=grp_specs + [row(SSM_WIDTH), row(2 * D_MODEL)] + cast_specs,
        out_shape=grp_shapes + [jax.ShapeDtypeStruct((s, SSM_WIDTH), F32),
                                jax.ShapeDtypeStruct((s, 2 * D_MODEL), BF16)] + cast_shapes,
        scratch_shapes=[pltpu.VMEM((D_MODEL // HEAD_DIM, tm, HEAD_DIM), F32), pltpu.VMEM((tm, D_MODEL), BF16),
                        pltpu.VMEM((tm, HEAD_DIM), F32), pltpu.VMEM((tm, HEAD_DIM), F32)],
        compiler_params=_params("parallel"),
        name="proj",
    )(x, pos_row, g_mix, inv_freq_col, w_in, *later_weights)


def _attn_kernel(qkv_ref, o_ref, l_ref, o_scr, l_scr, prev_scr, *, d, nsub):
    not_first_block = pl.program_id(0) > 0

    @pl.when(pl.program_id(0) == 0)
    def _():
        prev_scr[...] = jnp.zeros_like(prev_scr)

    qi = lax.broadcasted_iota(jnp.int32, (BLOCK, 2 * BLOCK), 0)
    kj = lax.broadcasted_iota(jnp.int32, (BLOCK, 2 * BLOCK), 1)
    rel = BLOCK + qi - kj
    band = (rel >= 0) & (rel <= BLOCK)
    band_first = band & ((kj >= BLOCK) | not_first_block)
    lane = lax.broadcasted_iota(jnp.int32, (BLOCK, HEAD_DIM), 1)

    def residue(r):
        for b in range(nsub):
            rows = slice(b * BLOCK, (b + 1) * BLOCK)
            mask = band_first if b == 0 else band
            token_rows = pl.ds(b * BLOCK * d + r, BLOCK, stride=d) if d > 1 else rows
            for h in range(HEADS_PER_GROUP):
                hs = slice(h * HEAD_DIM, (h + 1) * HEAD_DIM)
                ks = slice(GROUP_WIDTH + h * HEAD_DIM, GROUP_WIDTH + (h + 1) * HEAD_DIM)
                vs = slice(2 * GROUP_WIDTH + h * HEAD_DIM, 2 * GROUP_WIDTH + (h + 1) * HEAD_DIM)
                q = qkv_ref[r, rows, hs]
                if b == 0:
                    pk = slice(h * HEAD_DIM, (h + 1) * HEAD_DIM)
                    pv = slice(GROUP_WIDTH + h * HEAD_DIM, GROUP_WIDTH + (h + 1) * HEAD_DIM)
                    kw = jnp.concatenate([prev_scr[r, :, pk], qkv_ref[r, rows, ks]], axis=0)
                    vw = jnp.concatenate([prev_scr[r, :, pv], qkv_ref[r, rows, vs]], axis=0)
                else:
                    win = slice((b - 1) * BLOCK, (b + 1) * BLOCK)
                    kw = qkv_ref[r, win, ks]
                    vw = qkv_ref[r, win, vs]
                s = lax.dot_general(q, kw, (((1,), (1,)), ((), ())), preferred_element_type=F32)
                s = jnp.where(mask, s, MASK_VALUE)
                m = jnp.max(s, axis=-1, keepdims=True)
                p = jnp.exp2(s - m)
                den = jnp.sum(p, axis=-1, keepdims=True)
                o_scr[h, token_rows, :] = _bdot(p.astype(BF16), vw)
                m_b = jnp.broadcast_to(m, (BLOCK, HEAD_DIM))
                den_b = jnp.broadcast_to(den, (BLOCK, HEAD_DIM))
                if h > 0:
                    m_b = jnp.where(lane >= h * STAT_LANES, m_b, stats)
                stats = jnp.where(lane >= h * STAT_LANES + STAT_LANES // 2, den_b, m_b)
            l_scr[token_rows, :] = stats

    per_iter = max(1, min(d, ATTN_TILES_PER_BODY // (nsub * HEADS_PER_GROUP)))
    if per_iter == d:
        for r in range(d):
            residue(r)
    else:
        def body(i, carry):
            for j in range(per_iter):
                residue(i * per_iter + j)
            return carry
        lax.fori_loop(0, d // per_iter, body, 0)

    for h in range(HEADS_PER_GROUP):
        o_ref[:, h * HEAD_DIM:(h + 1) * HEAD_DIM] = o_scr[h].astype(BF16)
    l_ref[...] = l_scr[...]
    prev_scr[...] = qkv_ref[:, (nsub - 1) * BLOCK:nsub * BLOCK, GROUP_WIDTH:]


def _attn_group(qkv, dilation):
    sub_len = qkv.shape[1]
    s = sub_len * dilation
    step_tokens = min(ATTN_TOKENS[dilation], s)
    qb = step_tokens // dilation
    nsub = qb // BLOCK
    cur = pl.BlockSpec((dilation, qb, 3 * GROUP_WIDTH), lambda i: (0, i, 0))
    return pl.pallas_call(
        functools.partial(_attn_kernel, d=dilation, nsub=nsub),
        grid=(s // step_tokens,),
        in_specs=[cur],
        out_specs=[pl.BlockSpec((step_tokens, GROUP_WIDTH), lambda i: (i, 0)),
                   pl.BlockSpec((step_tokens, HEAD_DIM), lambda i: (i, 0))],
        out_shape=[jax.ShapeDtypeStruct((s, GROUP_WIDTH), BF16), jax.ShapeDtypeStruct((s, HEAD_DIM), F32)],
        scratch_shapes=[pltpu.VMEM((HEADS_PER_GROUP, step_tokens, HEAD_DIM), F32),
                        pltpu.VMEM((step_tokens, HEAD_DIM), F32),
                        pltpu.VMEM((dilation, BLOCK, 2 * GROUP_WIDTH), BF16)],
        compiler_params=_params("arbitrary"),
        name=f"attn_d{dilation}",
    )(qkv)


def _discretize(lr, li, log_dt):
    dt = jnp.exp(log_dt)
    mag = jnp.exp(lr * dt)
    bar_re = mag * jnp.cos(li * dt)
    bar_im = mag * jnp.sin(li * dt)
    nr = bar_re - 1.0
    ni = bar_im
    den = lr * lr + li * li
    return bar_re, bar_im, (nr * lr + ni * li) / den, (ni * lr - nr * li) / den


def _expand_block_diag(compact, n_blocks):
    rows, b = compact.shape
    a = rows // n_blocks
    wide = n_blocks * b
    src_lane = lax.broadcasted_iota(jnp.int32, (b, wide), 0)
    dst_lane = lax.broadcasted_iota(jnp.int32, (b, wide), 1)
    tiled = _bdot(compact, (dst_lane % b == src_lane).astype(BF16))
    row_block = lax.broadcasted_iota(jnp.int32, (rows, wide), 0) // a
    col_block = lax.broadcasted_iota(jnp.int32, (rows, wide), 1) // b
    return jnp.where(row_block == col_block, tiled, 0.0).astype(BF16)


def _ssm_prep_kernel(lr_ref, li_ref, logdt_ref, lr_rep_ref, li_rep_ref, logdt_rep_ref, b_re_ref, b_im_ref,
                     c_re_ref, c_im_ref, w_in_ref,
                     bmat_ref, cre_ref, cim_ref, e_re_ref, e_im_ref, einv_re_ref, einv_im_ref, lam_ref,
                     w_in_bf_ref):
    w_in_bf_ref[...] = w_in_ref[...].astype(BF16)
    lr = lr_ref[...]
    li = li_ref[...]
    dt = jnp.exp(logdt_ref[...])

    @pl.when(pl.program_id(0) == 0)
    def _():
        _, _, z_re, z_im = _discretize(lr_rep_ref[...], li_rep_ref[...], logdt_rep_ref[...])
        b_re = b_re_ref[...]
        b_im = b_im_ref[...]
        bb_re = (z_re * b_re - z_im * b_im).astype(BF16)
        bb_im = (z_re * b_im + z_im * b_re).astype(BF16)
        for hf in range(SSM_HALVES):
            rows = slice(hf * HALF_WIDTH, (hf + 1) * HALF_WIDTH)
            bmat_ref[rows, :HALF_STATES] = _expand_block_diag(bb_re[rows], HALF_GROUPS)
            bmat_ref[rows, HALF_STATES:] = _expand_block_diag(bb_im[rows], HALF_GROUPS)
            srows = slice(hf * HALF_STATES, (hf + 1) * HALF_STATES)
            cre_ref[srows, :] = _expand_block_diag(c_re_ref[srows, :].astype(BF16), HALF_GROUPS)
            cim_ref[srows, :] = _expand_block_diag(c_im_ref[srows, :].astype(BF16), HALF_GROUPS)
        whole = float(SSM_CHUNK)
        chunk_mag = jnp.exp(whole * (lr * dt))
        lam_ref[0:1, :] = chunk_mag * jnp.cos(whole * (li * dt))
        lam_ref[1:2, :] = chunk_mag * jnp.sin(whole * (li * dt))

    rows_per_step = e_re_ref.shape[0]
    t = (lax.broadcasted_iota(jnp.int32, (rows_per_step, N_STATES), 0)
         + pl.program_id(0) * rows_per_step).astype(F32)
    grow = jnp.exp(t * (lr * dt))
    theta = t * (li * dt)
    c = jnp.cos(theta)
    s = jnp.sin(theta)
    e_re_ref[...] = (grow * c).astype(BF16)
    e_im_ref[...] = (grow * s).astype(BF16)
    shrink = jnp.exp(-t * (lr * dt))
    einv_re_ref[...] = (shrink * c).astype(BF16)
    einv_im_ref[...] = (-(shrink * s)).astype(BF16)


def _ssm_prep(a_re, a_im, log_dt, b_re, b_im, c_re, c_im, w_in):
    row = lambda v: v.reshape(1, N_STATES)
    rep = lambda v: jnp.repeat(v, SSM_GROUP, axis=0)
    logdt_gp = jnp.broadcast_to(log_dt[:, None], (SSM_GROUPS, SSM_STATE))
    chan_state = lambda b: jnp.swapaxes(b, 1, 2).reshape(SSM_WIDTH, SSM_STATE)
    state_chan = lambda c: jnp.swapaxes(c, 1, 2).reshape(N_STATES, SSM_GROUP)
    params = (row(a_re), row(a_im), row(logdt_gp), rep(a_re), rep(a_im), rep(logdt_gp),
              chan_state(b_re), chan_state(b_im), state_chan(c_re), state_chan(c_im))
    whole = lambda shape: pl.BlockSpec(shape, lambda i: (0,) * len(shape))
    slab = lambda arr: pl.BlockSpec((arr.shape[0] // PREP_STEPS, arr.shape[1]), lambda i: (i, 0))
    tab = jax.ShapeDtypeStruct((SSM_CHUNK, N_STATES), BF16)
    cmat = jax.ShapeDtypeStruct((N_STATES, HALF_WIDTH), BF16)
    bmat = jax.ShapeDtypeStruct((SSM_WIDTH, 2 * HALF_STATES), BF16)
    lam = jax.ShapeDtypeStruct((2, N_STATES), F32)
    return pl.pallas_call(
        _ssm_prep_kernel,
        grid=(PREP_STEPS,),
        in_specs=[whole(p.shape) for p in params] + [slab(w_in)],
        out_specs=[whole(bmat.shape), whole(cmat.shape), whole(cmat.shape), slab(tab), slab(tab), slab(tab),
                   slab(tab), whole(lam.shape), slab(w_in)],
        out_shape=[bmat, cmat, cmat, tab, tab, tab, tab, lam, jax.ShapeDtypeStruct(w_in.shape, BF16)],
        compiler_params=_params("arbitrary"),
        name="ssm_prep",
    )(*params, w_in)


def _gelu_tanh(x):
    return 0.5 * x * (1.0 + jnp.tanh(math.sqrt(2.0 / math.pi) * (x + 0.044715 * (x * x * x))))


def _ssm_kernel(u_ref, bmat_ref, cre_ref, cim_ref, tri_ref, e_re_ref, e_im_ref, einv_re_ref, einv_im_ref,
                lam_ref, d_ref, y_ref, h_re_scr, h_im_scr, carry_scr):
    @pl.when(pl.program_id(0) == 0)
    def _():
        carry_scr[...] = jnp.zeros_like(carry_scr)

    u = u_ref[...]
    ub = u.astype(BF16)
    for hf in range(SSM_HALVES):
        chans = slice(hf * HALF_WIDTH, (hf + 1) * HALF_WIDTH)
        y = d_ref[:, chans] * u[:, chans]
        for blk in range(HALF_STATES // STATE_BLOCK):
            local = slice(blk * STATE_BLOCK, (blk + 1) * STATE_BLOCK)
            local_im = slice(HALF_STATES + blk * STATE_BLOCK, HALF_STATES + (blk + 1) * STATE_BLOCK)
            states = slice(hf * HALF_STATES + blk * STATE_BLOCK, hf * HALF_STATES + (blk + 1) * STATE_BLOCK)
            bu_re_all = _bdot(ub[:, chans], bmat_ref[chans, local])
            bu_im_all = _bdot(ub[:, chans], bmat_ref[chans, local_im])
            lam_re = lam_ref[0:1, states]
            lam_im = lam_ref[1:2, states]
            carry_re = carry_scr[0:1, states]
            carry_im = carry_scr[1:2, states]
            for c in range(u.shape[0] // SSM_CHUNK):
                rows = slice(c * SSM_CHUNK, (c + 1) * SSM_CHUNK)
                bu_re = bu_re_all[rows].astype(BF16)
                bu_im = bu_im_all[rows].astype(BF16)
                einv_re = einv_re_ref[:, states]
                einv_im = einv_im_ref[:, states]
                x = jnp.concatenate([bu_re * einv_re - bu_im * einv_im,
                                     bu_re * einv_im + bu_im * einv_re], axis=1)
                a = _bdot(tri_ref[...], x)
                a_re = a[:, :STATE_BLOCK] + carry_re
                a_im = a[:, STATE_BLOCK:] + carry_im
                e_re = e_re_ref[:, states]
                e_im = e_im_ref[:, states]
                a_re_b = a_re.astype(BF16)
                a_im_b = a_im.astype(BF16)
                h_re_scr[rows, :] = e_re * a_re_b - e_im * a_im_b
                h_im_scr[rows, :] = e_re * a_im_b + e_im * a_re_b
                last_re = a_re[SSM_CHUNK - 1:SSM_CHUNK, :]
                last_im = a_im[SSM_CHUNK - 1:SSM_CHUNK, :]
                carry_re = lam_re * last_re - lam_im * last_im
                carry_im = lam_re * last_im + lam_im * last_re
            carry_scr[0:1, states] = carry_re
            carry_scr[1:2, states] = carry_im
            y = y + _bdot(h_re_scr[...], cre_ref[states, :]) - _bdot(h_im_scr[...], cim_ref[states, :])
        y_ref[:, chans] = _gelu_tanh(y).astype(BF16)


def _ssm(u, bmat, cre, cim, tri, e_re, e_im, einv_re, einv_im, lam, d_row):
    s = u.shape[0]
    tm = min(SSM_ROWS, s)
    row = pl.BlockSpec((tm, SSM_WIDTH), lambda i: (i, 0))
    consts = (bmat, cre, cim, tri, e_re, e_im, einv_re, einv_im, lam, d_row)
    return pl.pallas_call(
        _ssm_kernel,
        grid=(s // tm,),
        in_specs=[row] + [_resident(c.shape) for c in consts],
        out_specs=row,
        out_shape=jax.ShapeDtypeStruct((s, SSM_WIDTH), BF16),
        scratch_shapes=[pltpu.VMEM((tm, STATE_BLOCK), BF16), pltpu.VMEM((tm, STATE_BLOCK), BF16),
                        pltpu.VMEM((2, N_STATES), F32)],
        compiler_params=_params("arbitrary"),
        name="ssm",
    )(u, *consts)


def _mix_kernel(*refs, n_cast):
    x_ref, o0_ref, o1_ref, o2_ref, l0_ref, l1_ref, l2_ref, y_ref, gates_ref = refs[:9]
    wap_ref, wa_ref, wb_ref, wout_ref = refs[9:13]
    cast_in, h_ref, cast_out = refs[13:13 + n_cast], refs[13 + n_cast], refs[14 + n_cast:]
    _cast_blocks(cast_in, cast_out)
    for rb in range(x_ref.shape[0] // MIX_SUB_ROWS):
        rows = slice(rb * MIX_SUB_ROWS, (rb + 1) * MIX_SUB_ROWS)
        ls = (l0_ref[rows, :], l1_ref[rows, :], l2_ref[rows, :])
        m_max = jnp.maximum(jnp.maximum(ls[0], ls[1]), ls[2])
        es = [jnp.exp2(l - m_max) for l in ls]
        dens = [pltpu.roll(l, HEAD_DIM - STAT_LANES // 2, 1) for l in ls]
        inv = 1.0 / (es[0] * dens[0] + es[1] * dens[1] + es[2] * dens[2])
        heads = []
        for h in range(HEADS_PER_GROUP):
            hs = slice(h * HEAD_DIM, (h + 1) * HEAD_DIM)
            col = slice(h * STAT_LANES, h * STAT_LANES + 1)
            heads.append(sum((e * inv)[:, col] * o_ref[rows, hs].astype(F32)
                             for e, o_ref in zip(es, (o0_ref, o1_ref, o2_ref))))
        attn = jnp.concatenate(heads, axis=1).astype(BF16)
        attn_d = _bdot(attn, wap_ref[...])
        y = y_ref[rows, :]
        ssm_d = _bdot(y, wa_ref[...]) * jax.nn.sigmoid(_bdot(y, wb_ref[...]))
        mix = (gates_ref[rows, :D_MODEL].astype(F32) * attn_d
               + gates_ref[rows, D_MODEL:].astype(F32) * ssm_d)
        h_ref[rows, :] = x_ref[rows, :] + _bdot(mix.astype(BF16), wout_ref[...])


def _mix(x, os, ls, y, gates, wap, wa, wb, wout, later_weights):
    s = x.shape[0]
    tm = min(MIX_ROWS, s)
    row = lambda w: pl.BlockSpec((tm, w), lambda i: (i, 0))
    weights = (wap, wa, wb, wout)
    cast_specs, cast_shapes = _cast_plan(later_weights, s // tm)
    return pl.pallas_call(
        functools.partial(_mix_kernel, n_cast=len(later_weights)),
        grid=(s // tm,),
        in_specs=[row(D_MODEL)] + [row(GROUP_WIDTH)] * 3 + [row(HEAD_DIM)] * 3
        + [row(SSM_WIDTH), row(2 * D_MODEL)] + [_resident(w.shape) for w in weights] + cast_specs,
        out_specs=[row(D_MODEL)] + cast_specs,
        out_shape=[jax.ShapeDtypeStruct((s, D_MODEL), F32)] + cast_shapes,
        compiler_params=_params("parallel"),
        name="mix",
    )(x, *os, *ls, y, gates, *weights, *later_weights)


def _ffn_kernel(h_ref, p_ref, gffn_ref, wg_ref, wu_ref, wd_ref, wpg_ref, wpp_ref, gfin_ref,
                out_ref, acc_scr):
    for rb in range(h_ref.shape[0] // FFN_SUB_ROWS):
        rows = slice(rb * FFN_SUB_ROWS, (rb + 1) * FFN_SUB_ROWS)
        h = h_ref[rows, :]
        n2 = _rmsnorm(h, gffn_ref[...]).astype(BF16)
        for idx, (c0, width) in enumerate(FFN_CHUNKS):
            gate = _bdot(n2, wg_ref[:, c0:c0 + width])
            up = _bdot(n2, wu_ref[:, c0:c0 + width])
            act = (gate * jax.nn.sigmoid(gate) * up).astype(BF16)
            part = _bdot(act, wd_ref[c0:c0 + width, :])
            if idx == 0:
                acc_scr[rows, :] = h + part
            else:
                acc_scr[rows, :] += part
        h2 = acc_scr[rows, :]
        ple = (jax.nn.sigmoid(_bdot(h2.astype(BF16), wpg_ref[...]))
               * _bdot(p_ref[rows, :].astype(BF16), wpp_ref[...]))
        out_ref[rows, :] = _rmsnorm(h2 + ple, gfin_ref[...])


def _ffn(h, p, g_ffn, wg, wu, wd, wpg, wpp, g_final):
    s = h.shape[0]
    tm = min(FFN_ROWS, s)
    row = lambda w: pl.BlockSpec((tm, w), lambda i: (i, 0))
    consts = (g_ffn, wg, wu, wd, wpg, wpp, g_final)
    return pl.pallas_call(
        _ffn_kernel,
        grid=(s // tm,),
        in_specs=[row(D_MODEL), row(PLE_DIM)] + [_resident(c.shape) for c in consts],
        out_specs=row(D_MODEL),
        out_shape=jax.ShapeDtypeStruct((s, D_MODEL), F32),
        scratch_shapes=[pltpu.VMEM((tm, D_MODEL), F32)],
        compiler_params=_params("parallel"),
        name="ffn",
    )(h, p, *consts)


def _layer(x, p, positions, g_mix, w_in, a_re, a_im, log_dt, b_re, b_im, c_re, c_im, d_skip,
           w_attn_proj, w_glu_a, w_glu_b, w_out, g_ffn, w_ffn_gate, w_ffn_up, w_ffn_down,
           w_ple_gate, w_ple_proj, g_final):
    row = lambda v: v.reshape(1, -1).astype(F32)

    inv_freq = ROPE_THETA ** (-jnp.arange(ROPE_HALF, dtype=F32) * 2.0 / ROPE_DIM)
    bmat, cre, cim, e_re, e_im, einv_re, einv_im, lam, w_in_bf = _ssm_prep(
        a_re, a_im, log_dt, b_re, b_im, c_re, c_im, w_in)
    outs = _proj(x, positions.reshape(1, -1), row(g_mix), inv_freq.reshape(ROPE_HALF, 1), w_in_bf,
                 (w_attn_proj, w_glu_a, w_glu_b, w_out, w_ffn_gate, w_ffn_up, w_ple_gate))
    qkvs, (u, gates) = outs[:N_GROUPS], outs[N_GROUPS:N_GROUPS + 2]
    wap, wga, wgb, wout, wfg, wfu, wpg = outs[N_GROUPS + 2:]

    attn_os, attn_ls = zip(*[_attn_group(qkvs[g], d) for g, d in enumerate(ATTN_DILATIONS)])

    tri = jnp.tril(jnp.ones((SSM_CHUNK, SSM_CHUNK), F32)).astype(BF16)
    y = _ssm(u, bmat, cre, cim, tri, e_re, e_im, einv_re, einv_im, lam, row(d_skip))

    h1, wfd, wpp = _mix(x, attn_os, attn_ls, y, gates, wap, wga, wgb, wout, (w_ffn_down, w_ple_proj))
    return _ffn(h1, p, row(g_ffn), wfg, wfu, wfd, wpg, wpp, row(g_final))


def kernel(x, p, positions, g_mix, w_in, a_re, a_im, log_dt, b_re, b_im, c_re, c_im, d_skip,
           w_attn_proj, w_glu_a, w_glu_b, w_out, g_ffn, w_ffn_gate, w_ffn_up, w_ffn_down,
           w_ple_gate, w_ple_proj, g_final):
    batch, depth = x.shape[0], p.shape[0]
    assert batch == 1 and depth == 1, "kernel supports the stated BATCH=1, DEPTH=1 problem"
    out = _layer(x[0], p[0, 0], positions[0], g_mix[0], w_in[0], a_re[0], a_im[0], log_dt[0],
                 b_re[0], b_im[0], c_re[0], c_im[0], d_skip[0], w_attn_proj[0], w_glu_a[0],
                 w_glu_b[0], w_out[0], g_ffn[0], w_ffn_gate[0], w_ffn_up[0], w_ffn_down[0],
                 w_ple_gate[0], w_ple_proj[0], g_final)
    return out[None]
```

```python
import functools
import math

import jax
import jax.numpy as jnp
from jax import lax
from jax.experimental import pallas as pl
from jax.experimental.pallas import tpu as pltpu

F32 = jnp.float32
BF16 = jnp.bfloat16

D_MODEL = 1024
HEAD_DIM = 128
HEADS_PER_GROUP = 4
GROUP_WIDTH = HEADS_PER_GROUP * HEAD_DIM
ATTN_DILATIONS = (1, 4, 16)
N_GROUPS = len(ATTN_DILATIONS)
QK_WIDTH = N_GROUPS * GROUP_WIDTH
BLOCK = 128
ROPE_THETA = 500000.0
ROPE_DIM = HEAD_DIM // 4
ROPE_HALF = ROPE_DIM // 2
SSM_WIDTH = 512
SSM_GROUP = 16
SSM_GROUPS = SSM_WIDTH // SSM_GROUP
SSM_STATE = 64
N_STATES = SSM_GROUPS * SSM_STATE
SSM_HALVES = 2
HALF_GROUPS = SSM_GROUPS // SSM_HALVES
HALF_WIDTH = SSM_WIDTH // SSM_HALVES
HALF_STATES = N_STATES // SSM_HALVES
STATE_BLOCK = 256
D_FF = 2816
PLE_DIM = 256
EPS = 1e-6
MASK_VALUE = -1e30

V7X_VMEM_LIMIT_BYTES = 56 * 1024 * 1024
BF16_SUBLANES = 16

PROJ_ROWS = 512
PROJ_SUB_ROWS = 256
ATTN_TOKENS = {1: 1024, 4: 2048, 16: 2048}
ATTN_TILES_PER_BODY = 32
STAT_LANES = HEAD_DIM // HEADS_PER_GROUP
SSM_CHUNK = 128
SSM_ROWS = 512
PREP_STEPS = 8
POST_ROWS = 512
POST_SUB_ROWS = 256
FFN_CHUNKS = ((0, 1024), (1024, 1024), (2048, 768))


def _resident(shape):
    return pl.BlockSpec(shape, lambda *_: (0,) * len(shape), pipeline_mode=pl.Buffered(1))


def _params(*semantics):
    return pltpu.CompilerParams(dimension_semantics=semantics,
                                vmem_limit_bytes=V7X_VMEM_LIMIT_BYTES)


def _rmsnorm(x, g):
    return (x * lax.rsqrt(jnp.mean(x * x, axis=-1, keepdims=True) + EPS)) * g


def _bdot(a, b):
    return jnp.dot(a, b, preferred_element_type=F32)


def _cast_plan(weights, steps):
    specs = []
    for w in weights:
        per_slab = 1 if (w.shape[0] // steps) % BF16_SUBLANES == 0 else 2
        slab_rows = w.shape[0] * per_slab // steps
        assert slab_rows % BF16_SUBLANES == 0 and slab_rows * steps == w.shape[0] * per_slab
        specs.append(pl.BlockSpec((slab_rows, w.shape[1]), lambda i, per_slab=per_slab: (i // per_slab, 0)))
    shapes = [jax.ShapeDtypeStruct(w.shape, BF16) for w in weights]
    return specs, shapes


def _cast_blocks(in_refs, out_refs):
    for src, dst in zip(in_refs, out_refs):
        dst[...] = src[...].astype(BF16)


def _proj_kernel(*refs, n_cast):
    x_ref, pos_ref, g_ref, invf_ref, w_ref = refs[:5]
    cast_in, refs = refs[5:5 + n_cast], refs[5 + n_cast:]
    qkv0, qkv1, qkv2, u_ref, gates_ref = refs[:5]
    cast_out, (n_scr, nperm_scr, cos_scr, sin_scr) = refs[5:5 + n_cast], refs[5 + n_cast:]
    _cast_blocks(cast_in, cast_out)
    rows = PROJ_SUB_ROWS
    lane = lax.broadcasted_iota(jnp.int32, (rows, HEAD_DIM), 1)
    first_half = lane < ROPE_HALF
    scale = math.log2(math.e) / math.sqrt(HEAD_DIM)

    for sb in range(x_ref.shape[0] // rows):
        base = sb * rows
        tile_rows = slice(base, base + rows)
        xn = _rmsnorm(x_ref[tile_rows, :], g_ref[...])
        n = xn.astype(BF16)
        for c in range(D_MODEL // HEAD_DIM):
            n_scr[c, tile_rows, :] = xn[:, c * HEAD_DIM:(c + 1) * HEAD_DIM]

        ang = invf_ref[...] * pos_ref[:, tile_rows].astype(F32)
        cos_t = jnp.cos(ang)
        sin_t = jnp.sin(ang)
        rest = (HEAD_DIM - ROPE_DIM, rows)
        cos_scr[tile_rows, :] = jnp.concatenate([cos_t, cos_t, jnp.ones(rest, F32)], axis=0).T
        sin_scr[tile_rows, :] = jnp.concatenate([-sin_t, sin_t, jnp.zeros(rest, F32)], axis=0).T

        for g, (d, qkv_ref) in enumerate(zip(ATTN_DILATIONS, (qkv0, qkv1, qkv2))):
            def residue_major(ref_2d):
                if d == 1:
                    return ref_2d[tile_rows, :]
                return jnp.concatenate(
                    [ref_2d[pl.ds(base + r, rows // d, stride=d), :] for r in range(d)], axis=0)

            cos = residue_major(cos_scr)
            sin_signed = residue_major(sin_scr)
            if d == 1:
                ng = n
            else:
                for c in range(D_MODEL // HEAD_DIM):
                    nperm_scr[tile_rows, c * HEAD_DIM:(c + 1) * HEAD_DIM] = (
                        residue_major(n_scr.at[c]).astype(BF16))
                ng = nperm_scr[tile_rows, :]

            def rotary(t):
                partner = jnp.where(first_half,
                                    pltpu.roll(t, HEAD_DIM - ROPE_HALF, 1),
                                    pltpu.roll(t, ROPE_HALF, 1))
                return t * cos + partner * sin_signed

            c0 = g * GROUP_WIDTH
            zq = _bdot(ng, w_ref[:, c0:c0 + GROUP_WIDTH])
            zk = _bdot(ng, w_ref[:, QK_WIDTH + c0:QK_WIDTH + c0 + GROUP_WIDTH])
            zv = _bdot(ng, w_ref[:, 2 * QK_WIDTH + c0:2 * QK_WIDTH + c0 + GROUP_WIDTH])
            sub_rows = slice(base // d, (base + rows) // d)
            for h in range(HEADS_PER_GROUP):
                hs = slice(h * HEAD_DIM, (h + 1) * HEAD_DIM)
                ks = slice(GROUP_WIDTH + h * HEAD_DIM, GROUP_WIDTH + (h + 1) * HEAD_DIM)
                qkv_ref[:, sub_rows, hs] = (
                    (rotary(zq[:, hs]) * scale).astype(BF16).reshape(d, rows // d, HEAD_DIM))
                qkv_ref[:, sub_rows, ks] = rotary(zk[:, hs]).astype(BF16).reshape(d, rows // d, HEAD_DIM)
            qkv_ref[:, sub_rows, 2 * GROUP_WIDTH:] = zv.astype(BF16).reshape(d, rows // d, GROUP_WIDTH)
        o1 = 3 * QK_WIDTH
        u_ref[tile_rows, :] = _bdot(n, w_ref[:, o1:o1 + SSM_WIDTH])
        o2 = o1 + SSM_WIDTH
        gates_ref[tile_rows, :] = jax.nn.sigmoid(_bdot(n, w_ref[:, o2:o2 + 2 * D_MODEL])).astype(BF16)


def _proj(x, pos_row, g_mix, inv_freq_col, w_in, later_weights):
    s = x.shape[0]
    tm = min(PROJ_ROWS, s)
    row = lambda w: pl.BlockSpec((tm, w), lambda i: (i, 0))
    cast_specs, cast_shapes = _cast_plan(later_weights, s // tm)
    grp_specs, grp_shapes = [], []
    for d in ATTN_DILATIONS:
        grp_specs.append(pl.BlockSpec((d, tm // d, 3 * GROUP_WIDTH), lambda i: (0, i, 0)))
        grp_shapes.append(jax.ShapeDtypeStruct((d, s // d, 3 * GROUP_WIDTH), BF16))
    return pl.pallas_call(
        functools.partial(_proj_kernel, n_cast=len(later_weights)),
        grid=(s // tm,),
        in_specs=[row(D_MODEL), pl.BlockSpec((1, tm), lambda i: (0, i)), _resident((1, D_MODEL)),
                  _resident((ROPE_HALF, 1)), _resident(w_in.shape)] + cast_specs,
        out_specs=grp_specs + [row(SSM_WIDTH), row(2 * D_MODEL)] + cast_specs,
        out_shape=grp_shapes + [jax.ShapeDtypeStruct((s, SSM_WIDTH), F32),
                                jax.ShapeDtypeStruct((s, 2 * D_MODEL), BF16)] + cast_shapes,
        scratch_shapes=[pltpu.VMEM((D_MODEL // HEAD_DIM, tm, HEAD_DIM), F32), pltpu.VMEM((tm, D_MODEL), BF16),
                        pltpu.VMEM((tm, HEAD_DIM), F32), pltpu.VMEM((tm, HEAD_DIM), F32)],
        compiler_params=_params("arbitrary"),
        name="proj",
    )(x, pos_row, g_mix, inv_freq_col, w_in, *later_weights)


def _attn_kernel(qkv_ref, o_ref, l_ref, o_scr, l_scr, prev_scr, *, d, nsub):
    not_first_block = pl.program_id(0) > 0

    @pl.when(pl.program_id(0) == 0)
    def _():
        prev_scr[...] = jnp.zeros_like(prev_scr)

    qi = lax.broadcasted_iota(jnp.int32, (BLOCK, 2 * BLOCK), 0)
    kj = lax.broadcasted_iota(jnp.int32, (BLOCK, 2 * BLOCK), 1)
    rel = BLOCK + qi - kj
    band = (rel >= 0) & (rel <= BLOCK)
    band_first = band & ((kj >= BLOCK) | not_first_block)
    lane = lax.broadcasted_iota(jnp.int32, (BLOCK, HEAD_DIM), 1)

    def residue(r):
        for b in range(nsub):
            rows = slice(b * BLOCK, (b + 1) * BLOCK)
            mask = band_first if b == 0 else band
            token_rows = pl.ds(b * BLOCK * d + r, BLOCK, stride=d) if d > 1 else rows
            for h in range(HEADS_PER_GROUP):
                hs = slice(h * HEAD_DIM, (h + 1) * HEAD_DIM)
                ks = slice(GROUP_WIDTH + h * HEAD_DIM, GROUP_WIDTH + (h + 1) * HEAD_DIM)
                vs = slice(2 * GROUP_WIDTH + h * HEAD_DIM, 2 * GROUP_WIDTH + (h + 1) * HEAD_DIM)
                q = qkv_ref[r, rows, hs]
                if b == 0:
                    pk = slice(h * HEAD_DIM, (h + 1) * HEAD_DIM)
                    pv = slice(GROUP_WIDTH + h * HEAD_DIM, GROUP_WIDTH + (h + 1) * HEAD_DIM)
                    kw = jnp.concatenate([prev_scr[r, :, pk], qkv_ref[r, rows, ks]], axis=0)
                    vw = jnp.concatenate([prev_scr[r, :, pv], qkv_ref[r, rows, vs]], axis=0)
                else:
                    win = slice((b - 1) * BLOCK, (b + 1) * BLOCK)
                    kw = qkv_ref[r, win, ks]
                    vw = qkv_ref[r, win, vs]
                s = lax.dot_general(q, kw, (((1,), (1,)), ((), ())), preferred_element_type=F32)
                s = jnp.where(mask, s, MASK_VALUE)
                m = jnp.max(s, axis=-1, keepdims=True)
                p = jnp.exp2(s - m)
                den = jnp.sum(p, axis=-1, keepdims=True)
                o_scr[h, token_rows, :] = _bdot(p.astype(BF16), vw)
                m_b = jnp.broadcast_to(m, (BLOCK, HEAD_DIM))
                den_b = jnp.broadcast_to(den, (BLOCK, HEAD_DIM))
                if h > 0:
                    m_b = jnp.where(lane >= h * STAT_LANES, m_b, stats)
                stats = jnp.where(lane >= h * STAT_LANES + STAT_LANES // 2, den_b, m_b)
            l_scr[token_rows, :] = stats

    per_iter = max(1, min(d, ATTN_TILES_PER_BODY // (nsub * HEADS_PER_GROUP)))
    if per_iter == d:
        for r in range(d):
            residue(r)
    else:
        def body(i, carry):
            for j in range(per_iter):
                residue(i * per_iter + j)
            return carry
        lax.fori_loop(0, d // per_iter, body, 0)

    for h in range(HEADS_PER_GROUP):
        o_ref[:, h * HEAD_DIM:(h + 1) * HEAD_DIM] = o_scr[h].astype(BF16)
    l_ref[...] = l_scr[...]
    prev_scr[...] = qkv_ref[:, (nsub - 1) * BLOCK:nsub * BLOCK, GROUP_WIDTH:]


def _attn_group(qkv, dilation):
    sub_len = qkv.shape[1]
    s = sub_len * dilation
    step_tokens = min(ATTN_TOKENS[dilation], s)
    qb = step_tokens // dilation
    nsub = qb // BLOCK
    cur = pl.BlockSpec((dilation, qb, 3 * GROUP_WIDTH), lambda i: (0, i, 0))
    return pl.pallas_call(
        functools.partial(_attn_kernel, d=dilation, nsub=nsub),
        grid=(s // step_tokens,),
        in_specs=[cur],
        out_specs=[pl.BlockSpec((step_tokens, GROUP_WIDTH), lambda i: (i, 0)),
                   pl.BlockSpec((step_tokens, HEAD_DIM), lambda i: (i, 0))],
        out_shape=[jax.ShapeDtypeStruct((s, GROUP_WIDTH), BF16), jax.ShapeDtypeStruct((s, HEAD_DIM), F32)],
        scratch_shapes=[pltpu.VMEM((HEADS_PER_GROUP, step_tokens, HEAD_DIM), F32),
                        pltpu.VMEM((step_tokens, HEAD_DIM), F32),
                        pltpu.VMEM((dilation, BLOCK, 2 * GROUP_WIDTH), BF16)],
        compiler_params=_params("arbitrary"),
        name=f"attn_d{dilation}",
    )(qkv)


def _discretize(lr, li, log_dt):
    dt = jnp.exp(log_dt)
    mag = jnp.exp(lr * dt)
    bar_re = mag * jnp.cos(li * dt)
    bar_im = mag * jnp.sin(li * dt)
    nr = bar_re - 1.0
    ni = bar_im
    den = lr * lr + li * li
    return bar_re, bar_im, (nr * lr + ni * li) / den, (ni * lr - nr * li) / den


def _expand_block_diag(compact, n_blocks):
    rows, b = compact.shape
    a = rows // n_blocks
    wide = n_blocks * b
    src_lane = lax.broadcasted_iota(jnp.int32, (b, wide), 0)
    dst_lane = lax.broadcasted_iota(jnp.int32, (b, wide), 1)
    tiled = _bdot(compact, (dst_lane % b == src_lane).astype(BF16))
    row_block = lax.broadcasted_iota(jnp.int32, (rows, wide), 0) // a
    col_block = lax.broadcasted_iota(jnp.int32, (rows, wide), 1) // b
    return jnp.where(row_block == col_block, tiled, 0.0).astype(BF16)


def _ssm_prep_kernel(lr_ref, li_ref, logdt_ref, lr_rep_ref, li_rep_ref, logdt_rep_ref, b_re_ref, b_im_ref,
                     c_re_ref, c_im_ref, w_in_ref,
                     bmat_ref, cre_ref, cim_ref, e_re_ref, e_im_ref, einv_re_ref, einv_im_ref, lam_ref,
                     w_in_bf_ref):
    w_in_bf_ref[...] = w_in_ref[...].astype(BF16)
    lr = lr_ref[...]
    li = li_ref[...]
    dt = jnp.exp(logdt_ref[...])

    @pl.when(pl.program_id(0) == 0)
    def _():
        _, _, z_re, z_im = _discretize(lr_rep_ref[...], li_rep_ref[...], logdt_rep_ref[...])
        b_re = b_re_ref[...]
        b_im = b_im_ref[...]
        bb_re = (z_re * b_re - z_im * b_im).astype(BF16)
        bb_im = (z_re * b_im + z_im * b_re).astype(BF16)
        for hf in range(SSM_HALVES):
            rows = slice(hf * HALF_WIDTH, (hf + 1) * HALF_WIDTH)
            bmat_ref[rows, :HALF_STATES] = _expand_block_diag(bb_re[rows], HALF_GROUPS)
            bmat_ref[rows, HALF_STATES:] = _expand_block_diag(bb_im[rows], HALF_GROUPS)
            srows = slice(hf * HALF_STATES, (hf + 1) * HALF_STATES)
            cre_ref[srows, :] = _expand_block_diag(c_re_ref[srows, :].astype(BF16), HALF_GROUPS)
            cim_ref[srows, :] = _expand_block_diag(c_im_ref[srows, :].astype(BF16), HALF_GROUPS)
        whole = float(SSM_CHUNK)
        chunk_mag = jnp.exp(whole * (lr * dt))
        lam_ref[0:1, :] = chunk_mag * jnp.cos(whole * (li * dt))
        lam_ref[1:2, :] = chunk_mag * jnp.sin(whole * (li * dt))

    rows_per_step = e_re_ref.shape[0]
    t = (lax.broadcasted_iota(jnp.int32, (rows_per_step, N_STATES), 0)
         + pl.program_id(0) * rows_per_step).astype(F32)
    grow = jnp.exp(t * (lr * dt))
    theta = t * (li * dt)
    c = jnp.cos(theta)
    s = jnp.sin(theta)
    e_re_ref[...] = (grow * c).astype(BF16)
    e_im_ref[...] = (grow * s).astype(BF16)
    shrink = jnp.exp(-t * (lr * dt))
    einv_re_ref[...] = (shrink * c).astype(BF16)
    einv_im_ref[...] = (-(shrink * s)).astype(BF16)


def _ssm_prep(a_re, a_im, log_dt, b_re, b_im, c_re, c_im, w_in):
    row = lambda v: v.reshape(1, N_STATES)
    rep = lambda v: jnp.repeat(v, SSM_GROUP, axis=0)
    logdt_gp = jnp.broadcast_to(log_dt[:, None], (SSM_GROUPS, SSM_STATE))
    chan_state = lambda b: jnp.swapaxes(b, 1, 2).reshape(SSM_WIDTH, SSM_STATE)
    state_chan = lambda c: jnp.swapaxes(c, 1, 2).reshape(N_STATES, SSM_GROUP)
    params = (row(a_re), row(a_im), row(logdt_gp), rep(a_re), rep(a_im), rep(logdt_gp),
              chan_state(b_re), chan_state(b_im), state_chan(c_re), state_chan(c_im))
    whole = lambda shape: pl.BlockSpec(shape, lambda i: (0,) * len(shape))
    slab = lambda arr: pl.BlockSpec((arr.shape[0] // PREP_STEPS, arr.shape[1]), lambda i: (i, 0))
    tab = jax.ShapeDtypeStruct((SSM_CHUNK, N_STATES), BF16)
    cmat = jax.ShapeDtypeStruct((N_STATES, HALF_WIDTH), BF16)
    bmat = jax.ShapeDtypeStruct((SSM_WIDTH, 2 * HALF_STATES), BF16)
    lam = jax.ShapeDtypeStruct((2, N_STATES), F32)
    return pl.pallas_call(
        _ssm_prep_kernel,
        grid=(PREP_STEPS,),
        in_specs=[whole(p.shape) for p in params] + [slab(w_in)],
        out_specs=[whole(bmat.shape), whole(cmat.shape), whole(cmat.shape), slab(tab), slab(tab), slab(tab),
                   slab(tab), whole(lam.shape), slab(w_in)],
        out_shape=[bmat, cmat, cmat, tab, tab, tab, tab, lam, jax.ShapeDtypeStruct(w_in.shape, BF16)],
        compiler_params=_params("arbitrary"),
        name="ssm_prep",
    )(*params, w_in)


def _gelu_tanh(x):
    return 0.5 * x * (1.0 + jnp.tanh(math.sqrt(2.0 / math.pi) * (x + 0.044715 * (x * x * x))))


def _ssm_kernel(u_ref, bmat_ref, cre_ref, cim_ref, tri_ref, e_re_ref, e_im_ref, einv_re_ref, einv_im_ref,
                lam_ref, d_ref, y_ref, h_re_scr, h_im_scr, carry_scr):
    @pl.when(pl.program_id(0) == 0)
    def _():
        carry_scr[...] = jnp.zeros_like(carry_scr)

    u = u_ref[...]
    ub = u.astype(BF16)
    for hf in range(SSM_HALVES):
        chans = slice(hf * HALF_WIDTH, (hf + 1) * HALF_WIDTH)
        y = d_ref[:, chans] * u[:, chans]
        for blk in range(HALF_STATES // STATE_BLOCK):
            local = slice(blk * STATE_BLOCK, (blk + 1) * STATE_BLOCK)
            local_im = slice(HALF_STATES + blk * STATE_BLOCK, HALF_STATES + (blk + 1) * STATE_BLOCK)
            states = slice(hf * HALF_STATES + blk * STATE_BLOCK, hf * HALF_STATES + (blk + 1) * STATE_BLOCK)
            bu_re_all = _bdot(ub[:, chans], bmat_ref[chans, local])
            bu_im_all = _bdot(ub[:, chans], bmat_ref[chans, local_im])
            lam_re = lam_ref[0:1, states]
            lam_im = lam_ref[1:2, states]
            carry_re = carry_scr[0:1, states]
            carry_im = carry_scr[1:2, states]
            for c in range(u.shape[0] // SSM_CHUNK):
                rows = slice(c * SSM_CHUNK, (c + 1) * SSM_CHUNK)
                bu_re = bu_re_all[rows].astype(BF16)
                bu_im = bu_im_all[rows].astype(BF16)
                einv_re = einv_re_ref[:, states]
                einv_im = einv_im_ref[:, states]
                x = jnp.concatenate([bu_re * einv_re - bu_im * einv_im,
                                     bu_re * einv_im + bu_im * einv_re], axis=1)
                a = _bdot(tri_ref[...], x)
                a_re = a[:, :STATE_BLOCK] + carry_re
                a_im = a[:, STATE_BLOCK:] + carry_im
                e_re = e_re_ref[:, states]
                e_im = e_im_ref[:, states]
                a_re_b = a_re.astype(BF16)
                a_im_b = a_im.astype(BF16)
                h_re_scr[rows, :] = e_re * a_re_b - e_im * a_im_b
                h_im_scr[rows, :] = e_re * a_im_b + e_im * a_re_b
                last_re = a_re[SSM_CHUNK - 1:SSM_CHUNK, :]
                last_im = a_im[SSM_CHUNK - 1:SSM_CHUNK, :]
                carry_re = lam_re * last_re - lam_im * last_im
                carry_im = lam_re * last_im + lam_im * last_re
            carry_scr[0:1, states] = carry_re
            carry_scr[1:2, states] = carry_im
            y = y + _bdot(h_re_scr[...], cre_ref[states, :]) - _bdot(h_im_scr[...], cim_ref[states, :])
        y_ref[:, chans] = _gelu_tanh(y).astype(BF16)


def _ssm(u, bmat, cre, cim, tri, e_re, e_im, einv_re, einv_im, lam, d_row):
    s = u.shape[0]
    tm = min(SSM_ROWS, s)
    row = pl.BlockSpec((tm, SSM_WIDTH), lambda i: (i, 0))
    consts = (bmat, cre, cim, tri, e_re, e_im, einv_re, einv_im, lam, d_row)
    return pl.pallas_call(
        _ssm_kernel,
        grid=(s // tm,),
        in_specs=[row] + [_resident(c.shape) for c in consts],
        out_specs=row,
        out_shape=jax.ShapeDtypeStruct((s, SSM_WIDTH), BF16),
        scratch_shapes=[pltpu.VMEM((tm, STATE_BLOCK), BF16), pltpu.VMEM((tm, STATE_BLOCK), BF16),
                        pltpu.VMEM((2, N_STATES), F32)],
        compiler_params=_params("arbitrary"),
        name="ssm",
    )(u, *consts)


def _post_kernel(x_ref, o0_ref, o1_ref, o2_ref, l0_ref, l1_ref, l2_ref, y_ref, gates_ref, p_ref,
                 wap_ref, wa_ref, wb_ref, wout_ref, gffn_ref, wg_ref, wu_ref, wd_ref, wpg_ref, wpp_ref, gfin_ref,
                 out_ref, acc_scr):
    for rb in range(x_ref.shape[0] // POST_SUB_ROWS):
        rows = slice(rb * POST_SUB_ROWS, (rb + 1) * POST_SUB_ROWS)
        ls = (l0_ref[rows, :], l1_ref[rows, :], l2_ref[rows, :])
        m_max = jnp.maximum(jnp.maximum(ls[0], ls[1]), ls[2])
        es = [jnp.exp2(l - m_max) for l in ls]
        dens = [pltpu.roll(l, HEAD_DIM - STAT_LANES // 2, 1) for l in ls]
        inv = 1.0 / (es[0] * dens[0] + es[1] * dens[1] + es[2] * dens[2])
        heads = []
        for h in range(HEADS_PER_GROUP):
            hs = slice(h * HEAD_DIM, (h + 1) * HEAD_DIM)
            col = slice(h * STAT_LANES, h * STAT_LANES + 1)
            heads.append(sum((e * inv)[:, col] * o_ref[rows, hs].astype(F32)
                             for e, o_ref in zip(es, (o0_ref, o1_ref, o2_ref))))
        attn = jnp.concatenate(heads, axis=1).astype(BF16)
        attn_d = _bdot(attn, wap_ref[...])
        y = y_ref[rows, :]
        ssm_d = _bdot(y, wa_ref[...]) * jax.nn.sigmoid(_bdot(y, wb_ref[...]))
        mix = (gates_ref[rows, :D_MODEL].astype(F32) * attn_d
               + gates_ref[rows, D_MODEL:].astype(F32) * ssm_d)
        h = x_ref[rows, :] + _bdot(mix.astype(BF16), wout_ref[...])

        n2 = _rmsnorm(h, gffn_ref[...]).astype(BF16)
        for idx, (c0, width) in enumerate(FFN_CHUNKS):
            gate = _bdot(n2, wg_ref[:, c0:c0 + width])
            up = _bdot(n2, wu_ref[:, c0:c0 + width])
            act = (gate * jax.nn.sigmoid(gate) * up).astype(BF16)
            part = _bdot(act, wd_ref[c0:c0 + width, :])
            if idx == 0:
                acc_scr[rows, :] = h + part
            else:
                acc_scr[rows, :] += part
        h2 = acc_scr[rows, :]
        ple = (jax.nn.sigmoid(_bdot(h2.astype(BF16), wpg_ref[...]))
               * _bdot(p_ref[rows, :].astype(BF16), wpp_ref[...]))
        out_ref[rows, :] = _rmsnorm(h2 + ple, gfin_ref[...])


def _post(x, os, ls, y, gates, p, consts):
    s = x.shape[0]
    tm = min(POST_ROWS, s)
    row = lambda w: pl.BlockSpec((tm, w), lambda i: (i, 0))
    return pl.pallas_call(
        _post_kernel,
        grid=(s // tm,),
        in_specs=[row(D_MODEL)] + [row(GROUP_WIDTH)] * 3 + [row(HEAD_DIM)] * 3
        + [row(SSM_WIDTH), row(2 * D_MODEL), row(PLE_DIM)] + [_resident(c.shape) for c in consts],
        out_specs=row(D_MODEL),
        out_shape=jax.ShapeDtypeStruct((s, D_MODEL), F32),
        scratch_shapes=[pltpu.VMEM((tm, D_MODEL), F32)],
        compiler_params=_params("parallel"),
        name="post",
    )(x, *os, *ls, y, gates, p, *consts)


def _layer(x, p, positions, g_mix, w_in, a_re, a_im, log_dt, b_re, b_im, c_re, c_im, d_skip,
           w_attn_proj, w_glu_a, w_glu_b, w_out, g_ffn, w_ffn_gate, w_ffn_up, w_ffn_down,
           w_ple_gate, w_ple_proj, g_final):
    row = lambda v: v.reshape(1, -1).astype(F32)

    inv_freq = ROPE_THETA ** (-jnp.arange(ROPE_HALF, dtype=F32) * 2.0 / ROPE_DIM)
    bmat, cre, cim, e_re, e_im, einv_re, einv_im, lam, w_in_bf = _ssm_prep(
        a_re, a_im, log_dt, b_re, b_im, c_re, c_im, w_in)
    outs = _proj(x, positions.reshape(1, -1), row(g_mix), inv_freq.reshape(ROPE_HALF, 1), w_in_bf,
                 (w_attn_proj, w_glu_a, w_glu_b, w_out, w_ffn_gate, w_ffn_up, w_ffn_down, w_ple_gate, w_ple_proj))
    qkvs, (u, gates) = outs[:N_GROUPS], outs[N_GROUPS:N_GROUPS + 2]
    wap, wga, wgb, wout, wfg, wfu, wfd, wpg, wpp = outs[N_GROUPS + 2:]

    attn_os, attn_ls = zip(*[_attn_group(qkvs[g], d) for g, d in enumerate(ATTN_DILATIONS)])

    tri = jnp.tril(jnp.ones((SSM_CHUNK, SSM_CHUNK), F32)).astype(BF16)
    y = _ssm(u, bmat, cre, cim, tri, e_re, e_im, einv_re, einv_im, lam, row(d_skip))

    return _post(x, attn_os, attn_ls, y, gates, p,
                 (wap, wga, wgb, wout, row(g_ffn), wfg, wfu, wfd, wpg, wpp, row(g_final)))


def kernel(x, p, positions, g_mix, w_in, a_re, a_im, log_dt, b_re, b_im, c_re, c_im, d_skip,
           w_attn_proj, w_glu_a, w_glu_b, w_out, g_ffn, w_ffn_gate, w_ffn_up, w_ffn_down,
           w_ple_gate, w_ple_proj, g_final):
    batch, depth = x.shape[0], p.shape[0]
    assert batch == 1 and depth == 1, "kernel supports the stated BATCH=1, DEPTH=1 problem"
    out = _layer(x[0], p[0, 0], positions[0], g_mix[0], w_in[0], a_re[0], a_im[0], log_dt[0],
                 b_re[0], b_im[0], c_re[0], c_im[0], d_skip[0], w_attn_proj[0], w_glu_a[0],
                 w_glu_b[0], w_out[0], g_ffn[0], w_ffn_gate[0], w_ffn_up[0], w_ffn_down[0],
                 w_ple_gate[0], w_ple_proj[0], g_final)
    return out[None]
```

```python
import functools
import math

import jax
import jax.numpy as jnp
from jax import lax
from jax.experimental import pallas as pl
from jax.experimental.pallas import tpu as pltpu

F32 = jnp.float32
BF16 = jnp.bfloat16

D_MODEL = 1024
HEAD_DIM = 128
HEADS_PER_GROUP = 4
GROUP_WIDTH = HEADS_PER_GROUP * HEAD_DIM
ATTN_DILATIONS = (1, 4, 16)
N_GROUPS = len(ATTN_DILATIONS)
QK_WIDTH = N_GROUPS * GROUP_WIDTH
BLOCK = 128
ROPE_THETA = 500000.0
ROPE_DIM = HEAD_DIM // 4
ROPE_HALF = ROPE_DIM // 2
SSM_WIDTH = 512
SSM_GROUP = 16
SSM_GROUPS = SSM_WIDTH // SSM_GROUP
SSM_STATE = 64
N_STATES = SSM_GROUPS * SSM_STATE
SSM_HALVES = 2
HALF_GROUPS = SSM_GROUPS // SSM_HALVES
HALF_WIDTH = SSM_WIDTH // SSM_HALVES
HALF_STATES = N_STATES // SSM_HALVES
STATE_BLOCK = 256
D_FF = 2816
PLE_DIM = 256
EPS = 1e-6
MASK_VALUE = -1e30

V7X_VMEM_LIMIT_BYTES = 56 * 1024 * 1024

PROJ_ROWS = 512
PROJ_SUB_ROWS = 256
ATTN_TOKENS = {1: 1024, 4: 2048, 16: 2048}
ATTN_TILES_PER_BODY = 32
STAT_LANES = HEAD_DIM // HEADS_PER_GROUP
SSM_CHUNK = 128
SSM_ROWS = 512
PREP_STEPS = 8
MIX_ROWS = 1024
MIX_SUB_ROWS = 256
FFN_ROWS = 512
FFN_SUB_ROWS = 256
FFN_CHUNKS = ((0, 1024), (1024, 1024), (2048, 768))


def _resident(shape):
    return pl.BlockSpec(shape, lambda *_: (0,) * len(shape), pipeline_mode=pl.Buffered(1))


def _params(*semantics):
    return pltpu.CompilerParams(dimension_semantics=semantics,
                                vmem_limit_bytes=V7X_VMEM_LIMIT_BYTES)


def _rmsnorm(x, g):
    return (x * lax.rsqrt(jnp.mean(x * x, axis=-1, keepdims=True) + EPS)) * g


def _bdot(a, b):
    return jnp.dot(a, b, preferred_element_type=F32)


def _cast_plan(weights, steps):
    specs = [pl.BlockSpec((w.shape[0] // steps, w.shape[1]), lambda i: (i, 0)) for w in weights]
    shapes = [jax.ShapeDtypeStruct(w.shape, BF16) for w in weights]
    return specs, shapes


def _cast_blocks(in_refs, out_refs):
    for src, dst in zip(in_refs, out_refs):
        dst[...] = src[...].astype(BF16)


def _proj_kernel(*refs, n_cast):
    x_ref, pos_ref, g_ref, invf_ref, w_ref = refs[:5]
    cast_in, refs = refs[5:5 + n_cast], refs[5 + n_cast:]
    qkv0, qkv1, qkv2, u_ref, gates_ref = refs[:5]
    cast_out, (n_scr, nperm_scr, cos_scr, sin_scr) = refs[5:5 + n_cast], refs[5 + n_cast:]
    _cast_blocks(cast_in, cast_out)
    rows = PROJ_SUB_ROWS
    lane = lax.broadcasted_iota(jnp.int32, (rows, HEAD_DIM), 1)
    first_half = lane < ROPE_HALF
    scale = math.log2(math.e) / math.sqrt(HEAD_DIM)

    for sb in range(x_ref.shape[0] // rows):
        base = sb * rows
        tile_rows = slice(base, base + rows)
        xn = _rmsnorm(x_ref[tile_rows, :], g_ref[...])
        n = xn.astype(BF16)
        for c in range(D_MODEL // HEAD_DIM):
            n_scr[c, tile_rows, :] = xn[:, c * HEAD_DIM:(c + 1) * HEAD_DIM]

        ang = invf_ref[...] * pos_ref[:, tile_rows].astype(F32)
        cos_t = jnp.cos(ang)
        sin_t = jnp.sin(ang)
        rest = (HEAD_DIM - ROPE_DIM, rows)
        cos_scr[tile_rows, :] = jnp.concatenate([cos_t, cos_t, jnp.ones(rest, F32)], axis=0).T
        sin_scr[tile_rows, :] = jnp.concatenate([-sin_t, sin_t, jnp.zeros(rest, F32)], axis=0).T

        for g, (d, qkv_ref) in enumerate(zip(ATTN_DILATIONS, (qkv0, qkv1, qkv2))):
            def residue_major(ref_2d):
                if d == 1:
                    return ref_2d[tile_rows, :]
                return jnp.concatenate(
                    [ref_2d[pl.ds(base + r, rows // d, stride=d), :] for r in range(d)], axis=0)

            cos = residue_major(cos_scr)
            sin_signed = residue_major(sin_scr)
            if d == 1:
                ng = n
            else:
                for c in range(D_MODEL // HEAD_DIM):
                    nperm_scr[tile_rows, c * HEAD_DIM:(c + 1) * HEAD_DIM] = (
                        residue_major(n_scr.at[c]).astype(BF16))
                ng = nperm_scr[tile_rows, :]

            def rotary(t):
                partner = jnp.where(first_half,
                                    pltpu.roll(t, HEAD_DIM - ROPE_HALF, 1),
                                    pltpu.roll(t, ROPE_HALF, 1))
                return t * cos + partner * sin_signed

            c0 = g * GROUP_WIDTH
            zq = _bdot(ng, w_ref[:, c0:c0 + GROUP_WIDTH])
            zk = _bdot(ng, w_ref[:, QK_WIDTH + c0:QK_WIDTH + c0 + GROUP_WIDTH])
            zv = _bdot(ng, w_ref[:, 2 * QK_WIDTH + c0:2 * QK_WIDTH + c0 + GROUP_WIDTH])
            sub_rows = slice(base // d, (base + rows) // d)
            for h in range(HEADS_PER_GROUP):
                hs = slice(h * HEAD_DIM, (h + 1) * HEAD_DIM)
                ks = slice(GROUP_WIDTH + h * HEAD_DIM, GROUP_WIDTH + (h + 1) * HEAD_DIM)
                qkv_ref[:, sub_rows, hs] = (
                    (rotary(zq[:, hs]) * scale).astype(BF16).reshape(d, rows // d, HEAD_DIM))
                qkv_ref[:, sub_rows, ks] = rotary(zk[:, hs]).astype(BF16).reshape(d, rows // d, HEAD_DIM)
            qkv_ref[:, sub_rows, 2 * GROUP_WIDTH:] = zv.astype(BF16).reshape(d, rows // d, GROUP_WIDTH)
        o1 = 3 * QK_WIDTH
        u_ref[tile_rows, :] = _bdot(n, w_ref[:, o1:o1 + SSM_WIDTH])
        o2 = o1 + SSM_WIDTH
        gates_ref[tile_rows, :] = jax.nn.sigmoid(_bdot(n, w_ref[:, o2:o2 + 2 * D_MODEL])).astype(BF16)


def _proj(x, pos_row, g_mix, inv_freq_col, w_in, later_weights):
    s = x.shape[0]
    tm = min(PROJ_ROWS, s)
    row = lambda w: pl.BlockSpec((tm, w), lambda i: (i, 0))
    cast_specs, cast_shapes = _cast_plan(later_weights, s // tm)
    grp_specs, grp_shapes = [], []
    for d in ATTN_DILATIONS:
        grp_specs.append(pl.BlockSpec((d, tm // d, 3 * GROUP_WIDTH), lambda i: (0, i, 0)))
        grp_shapes.append(jax.ShapeDtypeStruct((d, s // d, 3 * GROUP_WIDTH), BF16))
    return pl.pallas_call(
        functools.partial(_proj_kernel, n_cast=len(later_weights)),
        grid=(s // tm,),
        in_specs=[row(D_MODEL), pl.BlockSpec((1, tm), lambda i: (0, i)), _resident((1, D_MODEL)),
                  _resident((ROPE_HALF, 1)), _resident(w_in.shape)] + cast_specs,
        out_specs=grp_specs + [row(SSM_WIDTH), row(2 * D_MODEL)] + cast_specs,
        out_shape=grp_shapes + [jax.ShapeDtypeStruct((s, SSM_WIDTH), F32),
                                jax.ShapeDtypeStruct((s, 2 * D_MODEL), BF16)] + cast_shapes,
        scratch_shapes=[pltpu.VMEM((D_MODEL // HEAD_DIM, tm, HEAD_DIM), F32), pltpu.VMEM((tm, D_MODEL), BF16),
                        pltpu.VMEM((tm, HEAD_DIM), F32), pltpu.VMEM((tm, HEAD_DIM), F32)],
        compiler_params=_params("parallel"),
        name="proj",
    )(x, pos_row, g_mix, inv_freq_col, w_in, *later_weights)


def _attn_kernel(qkv_ref, o_ref, l_ref, prev_scr, *scatter_scr, d, nsub):
    not_first_block = pl.program_id(0) > 0
    if d > 1:
        o_scr, l_scr = scatter_scr

    @pl.when(pl.program_id(0) == 0)
    def _():
        prev_scr[...] = jnp.zeros_like(prev_scr)

    qi = lax.broadcasted_iota(jnp.int32, (BLOCK, 2 * BLOCK), 0)
    kj = lax.broadcasted_iota(jnp.int32, (BLOCK, 2 * BLOCK), 1)
    rel = BLOCK + qi - kj
    band = (rel >= 0) & (rel <= BLOCK)
    band_first = band & ((kj >= BLOCK) | not_first_block)
    lane = lax.broadcasted_iota(jnp.int32, (BLOCK, HEAD_DIM), 1)

    def residue(r):
        for b in range(nsub):
            rows = slice(b * BLOCK, (b + 1) * BLOCK)
            mask = band_first if b == 0 else band
            token_rows = pl.ds(b * BLOCK * d + r, BLOCK, stride=d) if d > 1 else rows
            for h in range(HEADS_PER_GROUP):
                hs = slice(h * HEAD_DIM, (h + 1) * HEAD_DIM)
                ks = slice(GROUP_WIDTH + h * HEAD_DIM, GROUP_WIDTH + (h + 1) * HEAD_DIM)
                vs = slice(2 * GROUP_WIDTH + h * HEAD_DIM, 2 * GROUP_WIDTH + (h + 1) * HEAD_DIM)
                q = qkv_ref[r, rows, hs]
                if b == 0:
                    prev_k = slice(h * HEAD_DIM, (h + 1) * HEAD_DIM)
                    prev_v = slice(GROUP_WIDTH + h * HEAD_DIM, GROUP_WIDTH + (h + 1) * HEAD_DIM)
                    kw = jnp.concatenate([prev_scr[r, :, prev_k], qkv_ref[r, rows, ks]], axis=0)
                    vw = jnp.concatenate([prev_scr[r, :, prev_v], qkv_ref[r, rows, vs]], axis=0)
                else:
                    win = slice((b - 1) * BLOCK, (b + 1) * BLOCK)
                    kw = qkv_ref[r, win, ks]
                    vw = qkv_ref[r, win, vs]
                s = lax.dot_general(q, kw, (((1,), (1,)), ((), ())), preferred_element_type=F32)
                s = jnp.where(mask, s, MASK_VALUE)
                m = jnp.max(s, axis=-1, keepdims=True)
                p = jnp.exp2(s - m)
                den = jnp.sum(p, axis=-1, keepdims=True)
                pv = _bdot(p.astype(BF16), vw)
                if d > 1:
                    o_scr[h, token_rows, :] = pv
                else:
                    o_ref[rows, hs] = pv.astype(BF16)
                m_b = jnp.broadcast_to(m, (BLOCK, HEAD_DIM))
                den_b = jnp.broadcast_to(den, (BLOCK, HEAD_DIM))
                if h > 0:
                    m_b = jnp.where(lane >= h * STAT_LANES, m_b, stats)
                stats = jnp.where(lane >= h * STAT_LANES + STAT_LANES // 2, den_b, m_b)
            if d > 1:
                l_scr[token_rows, :] = stats
            else:
                l_ref[rows, :] = stats

    per_iter = max(1, min(d, ATTN_TILES_PER_BODY // (nsub * HEADS_PER_GROUP)))
    if per_iter == d:
        for r in range(d):
            residue(r)
    else:
        def body(i, carry):
            for j in range(per_iter):
                residue(i * per_iter + j)
            return carry
        lax.fori_loop(0, d // per_iter, body, 0)

    if d > 1:
        for h in range(HEADS_PER_GROUP):
            o_ref[:, h * HEAD_DIM:(h + 1) * HEAD_DIM] = o_scr[h].astype(BF16)
        l_ref[...] = l_scr[...]
    prev_scr[...] = qkv_ref[:, (nsub - 1) * BLOCK:nsub * BLOCK, GROUP_WIDTH:]


def _attn_group(qkv, dilation):
    sub_len = qkv.shape[1]
    s = sub_len * dilation
    step_tokens = min(ATTN_TOKENS[dilation], s)
    qb = step_tokens // dilation
    nsub = qb // BLOCK
    cur = pl.BlockSpec((dilation, qb, 3 * GROUP_WIDTH), lambda i: (0, i, 0))
    scatter_scratch = [] if dilation == 1 else [
        pltpu.VMEM((HEADS_PER_GROUP, step_tokens, HEAD_DIM), F32), pltpu.VMEM((step_tokens, HEAD_DIM), F32)]
    return pl.pallas_call(
        functools.partial(_attn_kernel, d=dilation, nsub=nsub),
        grid=(s // step_tokens,),
        in_specs=[cur],
        out_specs=[pl.BlockSpec((step_tokens, GROUP_WIDTH), lambda i: (i, 0)),
                   pl.BlockSpec((step_tokens, HEAD_DIM), lambda i: (i, 0))],
        out_shape=[jax.ShapeDtypeStruct((s, GROUP_WIDTH), BF16), jax.ShapeDtypeStruct((s, HEAD_DIM), F32)],
        scratch_shapes=[pltpu.VMEM((dilation, BLOCK, 2 * GROUP_WIDTH), BF16)] + scatter_scratch,
        compiler_params=_params("arbitrary"),
        name=f"attn_d{dilation}",
    )(qkv)


def _discretize(lr, li, log_dt):
    dt = jnp.exp(log_dt)
    mag = jnp.exp(lr * dt)
    bar_re = mag * jnp.cos(li * dt)
    bar_im = mag * jnp.sin(li * dt)
    nr = bar_re - 1.0
    ni = bar_im
    den = lr * lr + li * li
    return bar_re, bar_im, (nr * lr + ni * li) / den, (ni * lr - nr * li) / den


def _expand_block_diag(compact, n_blocks):
    rows, b = compact.shape
    a = rows // n_blocks
    wide = n_blocks * b
    src_lane = lax.broadcasted_iota(jnp.int32, (b, wide), 0)
    dst_lane = lax.broadcasted_iota(jnp.int32, (b, wide), 1)
    tiled = _bdot(compact, (dst_lane % b == src_lane).astype(BF16))
    row_block = lax.broadcasted_iota(jnp.int32, (rows, wide), 0) // a
    col_block = lax.broadcasted_iota(jnp.int32, (rows, wide), 1) // b
    return jnp.where(row_block == col_block, tiled, 0.0).astype(BF16)


def _ssm_prep_kernel(lr_ref, li_ref, logdt_ref, lr_rep_ref, li_rep_ref, logdt_rep_ref, b_re_ref, b_im_ref,
                     c_re_ref, c_im_ref, w_in_ref,
                     bmat_ref, cre_ref, cim_ref, e_re_ref, e_im_ref, einv_re_ref, einv_im_ref, lam_ref,
                     w_in_bf_ref):
    w_in_bf_ref[...] = w_in_ref[...].astype(BF16)
    lr = lr_ref[...]
    li = li_ref[...]
    dt = jnp.exp(logdt_ref[...])

    @pl.when(pl.program_id(0) == 0)
    def _():
        _, _, z_re, z_im = _discretize(lr_rep_ref[...], li_rep_ref[...], logdt_rep_ref[...])
        b_re = b_re_ref[...]
        b_im = b_im_ref[...]
        bb_re = (z_re * b_re - z_im * b_im).astype(BF16)
        bb_im = (z_re * b_im + z_im * b_re).astype(BF16)
        for hf in range(SSM_HALVES):
            rows = slice(hf * HALF_WIDTH, (hf + 1) * HALF_WIDTH)
            bmat_ref[rows, :HALF_STATES] = _expand_block_diag(bb_re[rows], HALF_GROUPS)
            bmat_ref[rows, HALF_STATES:] = _expand_block_diag(bb_im[rows], HALF_GROUPS)
            srows = slice(hf * HALF_STATES, (hf + 1) * HALF_STATES)
            cre_ref[srows, :] = _expand_block_diag(c_re_ref[srows, :].astype(BF16), HALF_GROUPS)
            cim_ref[srows, :] = _expand_block_diag(c_im_ref[srows, :].astype(BF16), HALF_GROUPS)
        whole = float(SSM_CHUNK)
        chunk_mag = jnp.exp(whole * (lr * dt))
        lam_ref[0:1, :] = chunk_mag * jnp.cos(whole * (li * dt))
        lam_ref[1:2, :] = chunk_mag * jnp.sin(whole * (li * dt))

    rows_per_step = e_re_ref.shape[0]
    t = (lax.broadcasted_iota(jnp.int32, (rows_per_step, N_STATES), 0)
         + pl.program_id(0) * rows_per_step).astype(F32)
    grow = jnp.exp(t * (lr * dt))
    theta = t * (li * dt)
    c = jnp.cos(theta)
    s = jnp.sin(theta)
    e_re_ref[...] = (grow * c).astype(BF16)
    e_im_ref[...] = (grow * s).astype(BF16)
    shrink = jnp.exp(-t * (lr * dt))
    einv_re_ref[...] = (shrink * c).astype(BF16)
    einv_im_ref[...] = (-(shrink * s)).astype(BF16)


def _ssm_prep(a_re, a_im, log_dt, b_re, b_im, c_re, c_im, w_in):
    row = lambda v: v.reshape(1, N_STATES)
    rep = lambda v: jnp.repeat(v, SSM_GROUP, axis=0)
    logdt_gp = jnp.broadcast_to(log_dt[:, None], (SSM_GROUPS, SSM_STATE))
    chan_state = lambda b: jnp.swapaxes(b, 1, 2).reshape(SSM_WIDTH, SSM_STATE)
    state_chan = lambda c: jnp.swapaxes(c, 1, 2).reshape(N_STATES, SSM_GROUP)
    params = (row(a_re), row(a_im), row(logdt_gp), rep(a_re), rep(a_im), rep(logdt_gp),
              chan_state(b_re), chan_state(b_im), state_chan(c_re), state_chan(c_im))
    whole = lambda shape: pl.BlockSpec(shape, lambda i: (0,) * len(shape))
    slab = lambda arr: pl.BlockSpec((arr.shape[0] // PREP_STEPS, arr.shape[1]), lambda i: (i, 0))
    tab = jax.ShapeDtypeStruct((SSM_CHUNK, N_STATES), BF16)
    cmat = jax.ShapeDtypeStruct((N_STATES, HALF_WIDTH), BF16)
    bmat = jax.ShapeDtypeStruct((SSM_WIDTH, 2 * HALF_STATES), BF16)
    lam = jax.ShapeDtypeStruct((2, N_STATES), F32)
    return pl.pallas_call(
        _ssm_prep_kernel,
        grid=(PREP_STEPS,),
        in_specs=[whole(p.shape) for p in params] + [slab(w_in)],
        out_specs=[whole(bmat.shape), whole(cmat.shape), whole(cmat.shape), slab(tab), slab(tab), slab(tab),
                   slab(tab), whole(lam.shape), slab(w_in)],
        out_shape=[bmat, cmat, cmat, tab, tab, tab, tab, lam, jax.ShapeDtypeStruct(w_in.shape, BF16)],
        compiler_params=_params("arbitrary"),
        name="ssm_prep",
    )(*params, w_in)


def _gelu_tanh(x):
    return 0.5 * x * (1.0 + jnp.tanh(math.sqrt(2.0 / math.pi) * (x + 0.044715 * (x * x * x))))


def _ssm_kernel(u_ref, bmat_ref, cre_ref, cim_ref, tri_ref, e_re_ref, e_im_ref, einv_re_ref, einv_im_ref,
                lam_ref, d_ref, y_ref, h_re_scr, h_im_scr, carry_scr):
    @pl.when(pl.program_id(0) == 0)
    def _():
        carry_scr[...] = jnp.zeros_like(carry_scr)

    u = u_ref[...]
    ub = u.astype(BF16)
    for hf in range(SSM_HALVES):
        chans = slice(hf * HALF_WIDTH, (hf + 1) * HALF_WIDTH)
        y = d_ref[:, chans] * u[:, chans]
        for blk in range(HALF_STATES // STATE_BLOCK):
            local = slice(blk * STATE_BLOCK, (blk + 1) * STATE_BLOCK)
            local_im = slice(HALF_STATES + blk * STATE_BLOCK, HALF_STATES + (blk + 1) * STATE_BLOCK)
            states = slice(hf * HALF_STATES + blk * STATE_BLOCK, hf * HALF_STATES + (blk + 1) * STATE_BLOCK)
            bu_re_all = _bdot(ub[:, chans], bmat_ref[chans, local])
            bu_im_all = _bdot(ub[:, chans], bmat_ref[chans, local_im])
            lam_re = lam_ref[0:1, states]
            lam_im = lam_ref[1:2, states]
            carry_re = carry_scr[0:1, states]
            carry_im = carry_scr[1:2, states]
            for c in range(u.shape[0] // SSM_CHUNK):
                rows = slice(c * SSM_CHUNK, (c + 1) * SSM_CHUNK)
                bu_re = bu_re_all[rows].astype(BF16)
                bu_im = bu_im_all[rows].astype(BF16)
                einv_re = einv_re_ref[:, states]
                einv_im = einv_im_ref[:, states]
                x = jnp.concatenate([bu_re * einv_re - bu_im * einv_im,
                                     bu_re * einv_im + bu_im * einv_re], axis=1)
                a = _bdot(tri_ref[...], x)
                a_re = a[:, :STATE_BLOCK] + carry_re
                a_im = a[:, STATE_BLOCK:] + carry_im
                e_re = e_re_ref[:, states]
                e_im = e_im_ref[:, states]
                a_re_b = a_re.astype(BF16)
                a_im_b = a_im.astype(BF16)
                h_re_scr[rows, :] = e_re * a_re_b - e_im * a_im_b
                h_im_scr[rows, :] = e_re * a_im_b + e_im * a_re_b
                last_re = a_re[SSM_CHUNK - 1:SSM_CHUNK, :]
                last_im = a_im[SSM_CHUNK - 1:SSM_CHUNK, :]
                carry_re = lam_re * last_re - lam_im * last_im
                carry_im = lam_re * last_im + lam_im * last_re
            carry_scr[0:1, states] = carry_re
            carry_scr[1:2, states] = carry_im
            y = y + _bdot(h_re_scr[...], cre_ref[states, :]) - _bdot(h_im_scr[...], cim_ref[states, :])
        y_ref[:, chans] = _gelu_tanh(y).astype(BF16)


def _ssm(u, bmat, cre, cim, tri, e_re, e_im, einv_re, einv_im, lam, d_row):
    s = u.shape[0]
    tm = min(SSM_ROWS, s)
    row = pl.BlockSpec((tm, SSM_WIDTH), lambda i: (i, 0))
    consts = (bmat, cre, cim, tri, e_re, e_im, einv_re, einv_im, lam, d_row)
    return pl.pallas_call(
        _ssm_kernel,
        grid=(s // tm,),
        in_specs=[row] + [_resident(c.shape) for c in consts],
        out_specs=row,
        out_shape=jax.ShapeDtypeStruct((s, SSM_WIDTH), BF16),
        scratch_shapes=[pltpu.VMEM((tm, STATE_BLOCK), BF16), pltpu.VMEM((tm, STATE_BLOCK), BF16),
                        pltpu.VMEM((2, N_STATES), F32)],
        compiler_params=_params("arbitrary"),
        name="ssm",
    )(u, *consts)


def _mix_kernel(*refs, n_cast):
    x_ref, o0_ref, o1_ref, o2_ref, l0_ref, l1_ref, l2_ref, y_ref, gates_ref = refs[:9]
    wap_ref, wa_ref, wb_ref, wout_ref = refs[9:13]
    cast_in, h_ref, cast_out = refs[13:13 + n_cast], refs[13 + n_cast], refs[14 + n_cast:]
    _cast_blocks(cast_in, cast_out)
    for rb in range(x_ref.shape[0] // MIX_SUB_ROWS):
        rows = slice(rb * MIX_SUB_ROWS, (rb + 1) * MIX_SUB_ROWS)
        ls = (l0_ref[rows, :], l1_ref[rows, :], l2_ref[rows, :])
        m_max = jnp.maximum(jnp.maximum(ls[0], ls[1]), ls[2])
        es = [jnp.exp2(l - m_max) for l in ls]
        dens = [pltpu.roll(l, HEAD_DIM - STAT_LANES // 2, 1) for l in ls]
        inv = 1.0 / (es[0] * dens[0] + es[1] * dens[1] + es[2] * dens[2])
        heads = []
        for h in range(HEADS_PER_GROUP):
            hs = slice(h * HEAD_DIM, (h + 1) * HEAD_DIM)
            col = slice(h * STAT_LANES, h * STAT_LANES + 1)
            heads.append(sum((e * inv)[:, col] * o_ref[rows, hs].astype(F32)
                             for e, o_ref in zip(es, (o0_ref, o1_ref, o2_ref))))
        attn = jnp.concatenate(heads, axis=1).astype(BF16)
        attn_d = _bdot(attn, wap_ref[...])
        y = y_ref[rows, :]
        ssm_d = _bdot(y, wa_ref[...]) * jax.nn.sigmoid(_bdot(y, wb_ref[...]))
        mix = (gates_ref[rows, :D_MODEL].astype(F32) * attn_d
               + gates_ref[rows, D_MODEL:].astype(F32) * ssm_d)
        h_ref[rows, :] = x_ref[rows, :] + _bdot(mix.astype(BF16), wout_ref[...])


def _mix(x, os, ls, y, gates, wap, wa, wb, wout, later_weights):
    s = x.shape[0]
    tm = min(MIX_ROWS, s)
    row = lambda w: pl.BlockSpec((tm, w), lambda i: (i, 0))
    weights = (wap, wa, wb, wout)
    cast_specs, cast_shapes = _cast_plan(later_weights, s // tm)
    return pl.pallas_call(
        functools.partial(_mix_kernel, n_cast=len(later_weights)),
        grid=(s // tm,),
        in_specs=[row(D_MODEL)] + [row(GROUP_WIDTH)] * 3 + [row(HEAD_DIM)] * 3
        + [row(SSM_WIDTH), row(2 * D_MODEL)] + [_resident(w.shape) for w in weights] + cast_specs,
        out_specs=[row(D_MODEL)] + cast_specs,
        out_shape=[jax.ShapeDtypeStruct((s, D_MODEL), F32)] + cast_shapes,
        compiler_params=_params("parallel"),
        name="mix",
    )(x, *os, *ls, y, gates, *weights, *later_weights)


def _ffn_kernel(h_ref, p_ref, gffn_ref, wg_ref, wu_ref, wd_ref, wpg_ref, wpp_ref, gfin_ref,
                out_ref, acc_scr):
    for rb in range(h_ref.shape[0] // FFN_SUB_ROWS):
        rows = slice(rb * FFN_SUB_ROWS, (rb + 1) * FFN_SUB_ROWS)
        h = h_ref[rows, :]
        n2 = _rmsnorm(h, gffn_ref[...]).astype(BF16)
        for idx, (c0, width) in enumerate(FFN_CHUNKS):
            gate = _bdot(n2, wg_ref[:, c0:c0 + width])
            up = _bdot(n2, wu_ref[:, c0:c0 + width])
            act = (gate * jax.nn.sigmoid(gate) * up).astype(BF16)
            part = _bdot(act, wd_ref[c0:c0 + width, :])
            if idx == 0:
                acc_scr[rows, :] = h + part
            else:
                acc_scr[rows, :] += part
        h2 = acc_scr[rows, :]
        ple = (jax.nn.sigmoid(_bdot(h2.astype(BF16), wpg_ref[...]))
               * _bdot(p_ref[rows, :].astype(BF16), wpp_ref[...]))
        out_ref[rows, :] = _rmsnorm(h2 + ple, gfin_ref[...])


def _ffn(h, p, g_ffn, wg, wu, wd, wpg, wpp, g_final):
    s = h.shape[0]
    tm = min(FFN_ROWS, s)
    row = lambda w: pl.BlockSpec((tm, w), lambda i: (i, 0))
    consts = (g_ffn, wg, wu, wd, wpg, wpp, g_final)
    return pl.pallas_call(
        _ffn_kernel,
        grid=(s // tm,),
        in_specs=[row(D_MODEL), row(PLE_DIM)] + [_resident(c.shape) for c in consts],
        out_specs=row(D_MODEL),
        out_shape=jax.ShapeDtypeStruct((s, D_MODEL), F32),
        scratch_shapes=[pltpu.VMEM((tm, D_MODEL), F32)],
        compiler_params=_params("parallel"),
        name="ffn",
    )(h, p, *consts)


def _layer(x, p, positions, g_mix, w_in, a_re, a_im, log_dt, b_re, b_im, c_re, c_im, d_skip,
           w_attn_proj, w_glu_a, w_glu_b, w_out, g_ffn, w_ffn_gate, w_ffn_up, w_ffn_down,
           w_ple_gate, w_ple_proj, g_final):
    row = lambda v: v.reshape(1, -1).astype(F32)

    inv_freq = ROPE_THETA ** (-jnp.arange(ROPE_HALF, dtype=F32) * 2.0 / ROPE_DIM)
    bmat, cre, cim, e_re, e_im, einv_re, einv_im, lam, w_in_bf = _ssm_prep(
        a_re, a_im, log_dt, b_re, b_im, c_re, c_im, w_in)
    outs = _proj(x, positions.reshape(1, -1), row(g_mix), inv_freq.reshape(ROPE_HALF, 1), w_in_bf,
                 (w_attn_proj, w_glu_a, w_glu_b, w_out, w_ffn_gate, w_ffn_up, w_ple_gate))
    qkvs, (u, gates) = outs[:N_GROUPS], outs[N_GROUPS:N_GROUPS + 2]
    wap, wga, wgb, wout, wfg, wfu, wpg = outs[N_GROUPS + 2:]

    attn_os, attn_ls = zip(*[_attn_group(qkvs[g], d) for g, d in enumerate(ATTN_DILATIONS)])

    tri = jnp.tril(jnp.ones((SSM_CHUNK, SSM_CHUNK), F32)).astype(BF16)
    y = _ssm(u, bmat, cre, cim, tri, e_re, e_im, einv_re, einv_im, lam, row(d_skip))

    h1, wfd, wpp = _mix(x, attn_os, attn_ls, y, gates, wap, wga, wgb, wout, (w_ffn_down, w_ple_proj))
    return _ffn(h1, p, row(g_ffn), wfg, wfu, wfd, wpg, wpp, row(g_final))


def kernel(x, p, positions, g_mix, w_in, a_re, a_im, log_dt, b_re, b_im, c_re, c_im, d_skip,
           w_attn_proj, w_glu_a, w_glu_b, w_out, g_ffn, w_ffn_gate, w_ffn_up, w_ffn_down,
           w_ple_gate, w_ple_proj, g_final):
    batch, depth = x.shape[0], p.shape[0]
    assert batch == 1 and depth == 1, "kernel supports the stated BATCH=1, DEPTH=1 problem"
    out = _layer(x[0], p[0, 0], positions[0], g_mix[0], w_in[0], a_re[0], a_im[0], log_dt[0],
                 b_re[0], b_im[0], c_re[0], c_im[0], d_skip[0], w_attn_proj[0], w_glu_a[0],
                 w_glu_b[0], w_out[0], g_ffn[0], w_ffn_gate[0], w_ffn_up[0], w_ffn_down[0],
                 w_ple_gate[0], w_ple_proj[0], g_final)
    return out[None]
```

```python
import functools
import math

import jax
import jax.numpy as jnp
from jax import lax
from jax.experimental import pallas as pl
from jax.experimental.pallas import tpu as pltpu

F32 = jnp.float32
BF16 = jnp.bfloat16

D_MODEL = 1024
HEAD_DIM = 128
HEADS_PER_GROUP = 4
GROUP_WIDTH = HEADS_PER_GROUP * HEAD_DIM
ATTN_DILATIONS = (1, 4, 16)
N_GROUPS = len(ATTN_DILATIONS)
QK_WIDTH = N_GROUPS * GROUP_WIDTH
BLOCK = 128
ROPE_THETA = 500000.0
ROPE_DIM = HEAD_DIM // 4
ROPE_HALF = ROPE_DIM // 2
SSM_WIDTH = 512
SSM_GROUP = 16
SSM_GROUPS = SSM_WIDTH // SSM_GROUP
SSM_STATE = 64
N_STATES = SSM_GROUPS * SSM_STATE
SSM_HALVES = 2
HALF_GROUPS = SSM_GROUPS // SSM_HALVES
HALF_WIDTH = SSM_WIDTH // SSM_HALVES
HALF_STATES = N_STATES // SSM_HALVES
STATE_BLOCK = 256
D_FF = 2816
PLE_DIM = 256
EPS = 1e-6
MASK_VALUE = -1e30

V7X_VMEM_LIMIT_BYTES = 56 * 1024 * 1024

PROJ_ROWS = 512
PROJ_SUB_ROWS = 256
ATTN_TOKENS = {1: 2048, 4: 2048, 16: 2048}
ATTN_TILES_PER_BODY = {1: 64, 4: 64, 16: 32}
STAT_LANES = HEAD_DIM // HEADS_PER_GROUP
SSM_CHUNK = 128
SSM_ROWS = 512
PREP_STEPS = 8
MIX_ROWS = 1024
MIX_SUB_ROWS = 256
FFN_ROWS = 1024
FFN_SUB_ROWS = 256
FFN_CHUNKS = ((0, 1024), (1024, 1024), (2048, 768))


def _resident(shape):
    return pl.BlockSpec(shape, lambda *_: (0,) * len(shape), pipeline_mode=pl.Buffered(1))


def _params(*semantics):
    return pltpu.CompilerParams(dimension_semantics=semantics,
                                vmem_limit_bytes=V7X_VMEM_LIMIT_BYTES)


def _rmsnorm(x, g):
    return (x * lax.rsqrt(jnp.mean(x * x, axis=-1, keepdims=True) + EPS)) * g


def _bdot(a, b):
    return jnp.dot(a, b, preferred_element_type=F32)


def _cast_plan(weights, steps):
    specs = [pl.BlockSpec((w.shape[0] // steps, w.shape[1]), lambda i: (i, 0)) for w in weights]
    shapes = [jax.ShapeDtypeStruct(w.shape, BF16) for w in weights]
    return specs, shapes


def _cast_blocks(in_refs, out_refs):
    for src, dst in zip(in_refs, out_refs):
        dst[...] = src[...].astype(BF16)


def _proj_kernel(*refs, n_cast):
    x_ref, pos_ref, g_ref, invf_ref, w_ref = refs[:5]
    cast_in, refs = refs[5:5 + n_cast], refs[5 + n_cast:]
    qkv0, qkv1, qkv2, u_ref, gates_ref = refs[:5]
    cast_out, (n_scr, nperm_scr, cos_scr, sin_scr) = refs[5:5 + n_cast], refs[5 + n_cast:]
    _cast_blocks(cast_in, cast_out)
    rows = PROJ_SUB_ROWS
    lane = lax.broadcasted_iota(jnp.int32, (rows, HEAD_DIM), 1)
    first_half = lane < ROPE_HALF
    scale = math.log2(math.e) / math.sqrt(HEAD_DIM)

    for sb in range(x_ref.shape[0] // rows):
        base = sb * rows
        tile_rows = slice(base, base + rows)
        xn = _rmsnorm(x_ref[tile_rows, :], g_ref[...])
        n = xn.astype(BF16)
        for c in range(D_MODEL // HEAD_DIM):
            n_scr[c, tile_rows, :] = xn[:, c * HEAD_DIM:(c + 1) * HEAD_DIM]

        ang = invf_ref[...] * pos_ref[:, tile_rows].astype(F32)
        cos_t = jnp.cos(ang)
        sin_t = jnp.sin(ang)
        rest = (HEAD_DIM - ROPE_DIM, rows)
        cos_scr[tile_rows, :] = jnp.concatenate([cos_t, cos_t, jnp.ones(rest, F32)], axis=0).T
        sin_scr[tile_rows, :] = jnp.concatenate([-sin_t, sin_t, jnp.zeros(rest, F32)], axis=0).T

        for g, (d, qkv_ref) in enumerate(zip(ATTN_DILATIONS, (qkv0, qkv1, qkv2))):
            def residue_major(ref_2d):
                if d == 1:
                    return ref_2d[tile_rows, :]
                return jnp.concatenate(
                    [ref_2d[pl.ds(base + r, rows // d, stride=d), :] for r in range(d)], axis=0)

            cos = residue_major(cos_scr)
            sin_signed = residue_major(sin_scr)
            if d == 1:
                ng = n
            else:
                for c in range(D_MODEL // HEAD_DIM):
                    nperm_scr[tile_rows, c * HEAD_DIM:(c + 1) * HEAD_DIM] = (
                        residue_major(n_scr.at[c]).astype(BF16))
                ng = nperm_scr[tile_rows, :]

            def rotary(t):
                partner = jnp.where(first_half,
                                    pltpu.roll(t, HEAD_DIM - ROPE_HALF, 1),
                                    pltpu.roll(t, ROPE_HALF, 1))
                return t * cos + partner * sin_signed

            c0 = g * GROUP_WIDTH
            zq = _bdot(ng, w_ref[:, c0:c0 + GROUP_WIDTH])
            zk = _bdot(ng, w_ref[:, QK_WIDTH + c0:QK_WIDTH + c0 + GROUP_WIDTH])
            zv = _bdot(ng, w_ref[:, 2 * QK_WIDTH + c0:2 * QK_WIDTH + c0 + GROUP_WIDTH])
            sub_rows = slice(base // d, (base + rows) // d)
            for h in range(HEADS_PER_GROUP):
                hs = slice(h * HEAD_DIM, (h + 1) * HEAD_DIM)
                ks = slice(GROUP_WIDTH + h * HEAD_DIM, GROUP_WIDTH + (h + 1) * HEAD_DIM)
                qkv_ref[:, sub_rows, hs] = (
                    (rotary(zq[:, hs]) * scale).astype(BF16).reshape(d, rows // d, HEAD_DIM))
                qkv_ref[:, sub_rows, ks] = rotary(zk[:, hs]).astype(BF16).reshape(d, rows // d, HEAD_DIM)
            qkv_ref[:, sub_rows, 2 * GROUP_WIDTH:] = zv.astype(BF16).reshape(d, rows // d, GROUP_WIDTH)
        o1 = 3 * QK_WIDTH
        u_ref[tile_rows, :] = _bdot(n, w_ref[:, o1:o1 + SSM_WIDTH])
        o2 = o1 + SSM_WIDTH
        gates_ref[tile_rows, :] = jax.nn.sigmoid(_bdot(n, w_ref[:, o2:o2 + 2 * D_MODEL])).astype(BF16)


def _proj(x, pos_row, g_mix, inv_freq_col, w_in, later_weights):
    s = x.shape[0]
    tm = min(PROJ_ROWS, s)
    row = lambda w: pl.BlockSpec((tm, w), lambda i: (i, 0))
    cast_specs, cast_shapes = _cast_plan(later_weights, s // tm)
    grp_specs, grp_shapes = [], []
    for d in ATTN_DILATIONS:
        grp_specs.append(pl.BlockSpec((d, tm // d, 3 * GROUP_WIDTH), lambda i: (0, i, 0)))
        grp_shapes.append(jax.ShapeDtypeStruct((d, s // d, 3 * GROUP_WIDTH), BF16))
    return pl.pallas_call(
        functools.partial(_proj_kernel, n_cast=len(later_weights)),
        grid=(s // tm,),
        in_specs=[row(D_MODEL), pl.BlockSpec((1, tm), lambda i: (0, i)), _resident((1, D_MODEL)),
                  _resident((ROPE_HALF, 1)), _resident(w_in.shape)] + cast_specs,
        out_specs=grp_specs + [row(SSM_WIDTH), row(2 * D_MODEL)] + cast_specs,
        out_shape=grp_shapes + [jax.ShapeDtypeStruct((s, SSM_WIDTH), F32),
                                jax.ShapeDtypeStruct((s, 2 * D_MODEL), BF16)] + cast_shapes,
        scratch_shapes=[pltpu.VMEM((D_MODEL // HEAD_DIM, tm, HEAD_DIM), F32), pltpu.VMEM((tm, D_MODEL), BF16),
                        pltpu.VMEM((tm, HEAD_DIM), F32), pltpu.VMEM((tm, HEAD_DIM), F32)],
        compiler_params=_params("parallel"),
        name="proj",
    )(x, pos_row, g_mix, inv_freq_col, w_in, *later_weights)


def _attn_kernel(qkv_ref, o_ref, l_ref, prev_scr, *scatter_scr, d, nsub):
    not_first_block = pl.program_id(0) > 0
    if d > 1:
        o_scr, l_scr = scatter_scr

    @pl.when(pl.program_id(0) == 0)
    def _():
        prev_scr[...] = jnp.zeros_like(prev_scr)

    qi = lax.broadcasted_iota(jnp.int32, (BLOCK, 2 * BLOCK), 0)
    kj = lax.broadcasted_iota(jnp.int32, (BLOCK, 2 * BLOCK), 1)
    rel = BLOCK + qi - kj
    band = (rel >= 0) & (rel <= BLOCK)
    band_first = band & ((kj >= BLOCK) | not_first_block)
    lane = lax.broadcasted_iota(jnp.int32, (BLOCK, HEAD_DIM), 1)

    def residue(r):
        for b in range(nsub):
            rows = slice(b * BLOCK, (b + 1) * BLOCK)
            mask = band_first if b == 0 else band
            token_rows = pl.ds(b * BLOCK * d + r, BLOCK, stride=d) if d > 1 else rows
            for h in range(HEADS_PER_GROUP):
                hs = slice(h * HEAD_DIM, (h + 1) * HEAD_DIM)
                ks = slice(GROUP_WIDTH + h * HEAD_DIM, GROUP_WIDTH + (h + 1) * HEAD_DIM)
                vs = slice(2 * GROUP_WIDTH + h * HEAD_DIM, 2 * GROUP_WIDTH + (h + 1) * HEAD_DIM)
                q = qkv_ref[r, rows, hs]
                if b == 0:
                    prev_k = slice(h * HEAD_DIM, (h + 1) * HEAD_DIM)
                    prev_v = slice(GROUP_WIDTH + h * HEAD_DIM, GROUP_WIDTH + (h + 1) * HEAD_DIM)
                    kw = jnp.concatenate([prev_scr[r, :, prev_k], qkv_ref[r, rows, ks]], axis=0)
                    vw = jnp.concatenate([prev_scr[r, :, prev_v], qkv_ref[r, rows, vs]], axis=0)
                else:
                    win = slice((b - 1) * BLOCK, (b + 1) * BLOCK)
                    kw = qkv_ref[r, win, ks]
                    vw = qkv_ref[r, win, vs]
                s = lax.dot_general(q, kw, (((1,), (1,)), ((), ())), preferred_element_type=F32)
                s = jnp.where(mask, s, MASK_VALUE)
                m = jnp.max(s, axis=-1, keepdims=True)
                p = jnp.exp2(s - m)
                den = jnp.sum(p, axis=-1, keepdims=True)
                pv = _bdot(p.astype(BF16), vw)
                if d > 1:
                    o_scr[h, token_rows, :] = pv
                else:
                    o_ref[rows, hs] = pv.astype(BF16)
                m_b = jnp.broadcast_to(m, (BLOCK, HEAD_DIM))
                den_b = jnp.broadcast_to(den, (BLOCK, HEAD_DIM))
                if h > 0:
                    m_b = jnp.where(lane >= h * STAT_LANES, m_b, stats)
                stats = jnp.where(lane >= h * STAT_LANES + STAT_LANES // 2, den_b, m_b)
            if d > 1:
                l_scr[token_rows, :] = stats
            else:
                l_ref[rows, :] = stats

    per_iter = max(1, min(d, ATTN_TILES_PER_BODY[d] // (nsub * HEADS_PER_GROUP)))
    if per_iter == d:
        for r in range(d):
            residue(r)
    else:
        def body(i, carry):
            for j in range(per_iter):
                residue(i * per_iter + j)
            return carry
        lax.fori_loop(0, d // per_iter, body, 0)

    if d > 1:
        for h in range(HEADS_PER_GROUP):
            o_ref[:, h * HEAD_DIM:(h + 1) * HEAD_DIM] = o_scr[h].astype(BF16)
        l_ref[...] = l_scr[...]
    prev_scr[...] = qkv_ref[:, (nsub - 1) * BLOCK:nsub * BLOCK, GROUP_WIDTH:]


def _attn_group(qkv, dilation):
    sub_len = qkv.shape[1]
    s = sub_len * dilation
    step_tokens = min(ATTN_TOKENS[dilation], s)
    qb = step_tokens // dilation
    nsub = qb // BLOCK
    cur = pl.BlockSpec((dilation, qb, 3 * GROUP_WIDTH), lambda i: (0, i, 0))
    scatter_scratch = [] if dilation == 1 else [
        pltpu.VMEM((HEADS_PER_GROUP, step_tokens, HEAD_DIM), F32), pltpu.VMEM((step_tokens, HEAD_DIM), F32)]
    return pl.pallas_call(
        functools.partial(_attn_kernel, d=dilation, nsub=nsub),
        grid=(s // step_tokens,),
        in_specs=[cur],
        out_specs=[pl.BlockSpec((step_tokens, GROUP_WIDTH), lambda i: (i, 0)),
                   pl.BlockSpec((step_tokens, HEAD_DIM), lambda i: (i, 0))],
        out_shape=[jax.ShapeDtypeStruct((s, GROUP_WIDTH), BF16), jax.ShapeDtypeStruct((s, HEAD_DIM), F32)],
        scratch_shapes=[pltpu.VMEM((dilation, BLOCK, 2 * GROUP_WIDTH), BF16)] + scatter_scratch,
        compiler_params=_params("arbitrary"),
        name=f"attn_d{dilation}",
    )(qkv)


def _discretize(lr, li, log_dt):
    dt = jnp.exp(log_dt)
    mag = jnp.exp(lr * dt)
    bar_re = mag * jnp.cos(li * dt)
    bar_im = mag * jnp.sin(li * dt)
    nr = bar_re - 1.0
    ni = bar_im
    den = lr * lr + li * li
    return bar_re, bar_im, (nr * lr + ni * li) / den, (ni * lr - nr * li) / den


def _expand_block_diag(compact, n_blocks):
    rows, b = compact.shape
    a = rows // n_blocks
    wide = n_blocks * b
    src_lane = lax.broadcasted_iota(jnp.int32, (b, wide), 0)
    dst_lane = lax.broadcasted_iota(jnp.int32, (b, wide), 1)
    tiled = _bdot(compact, (dst_lane % b == src_lane).astype(BF16))
    row_block = lax.broadcasted_iota(jnp.int32, (rows, wide), 0) // a
    col_block = lax.broadcasted_iota(jnp.int32, (rows, wide), 1) // b
    return jnp.where(row_block == col_block, tiled, 0.0).astype(BF16)


def _ssm_prep_kernel(lr_ref, li_ref, logdt_ref, lr_rep_ref, li_rep_ref, logdt_rep_ref, b_re_ref, b_im_ref,
                     c_re_ref, c_im_ref, w_in_ref,
                     bmat_ref, cre_ref, cim_ref, e_re_ref, e_im_ref, einv_re_ref, einv_im_ref, lam_ref,
                     w_in_bf_ref):
    w_in_bf_ref[...] = w_in_ref[...].astype(BF16)
    lr = lr_ref[...]
    li = li_ref[...]
    dt = jnp.exp(logdt_ref[...])

    @pl.when(pl.program_id(0) == 0)
    def _():
        _, _, z_re, z_im = _discretize(lr_rep_ref[...], li_rep_ref[...], logdt_rep_ref[...])
        b_re = b_re_ref[...]
        b_im = b_im_ref[...]
        bb_re = (z_re * b_re - z_im * b_im).astype(BF16)
        bb_im = (z_re * b_im + z_im * b_re).astype(BF16)
        for hf in range(SSM_HALVES):
            rows = slice(hf * HALF_WIDTH, (hf + 1) * HALF_WIDTH)
            bmat_ref[rows, :HALF_STATES] = _expand_block_diag(bb_re[rows], HALF_GROUPS)
            bmat_ref[rows, HALF_STATES:] = _expand_block_diag(bb_im[rows], HALF_GROUPS)
            srows = slice(hf * HALF_STATES, (hf + 1) * HALF_STATES)
            cre_ref[srows, :] = _expand_block_diag(c_re_ref[srows, :].astype(BF16), HALF_GROUPS)
            cim_ref[srows, :] = _expand_block_diag(c_im_ref[srows, :].astype(BF16), HALF_GROUPS)
        whole = float(SSM_CHUNK)
        chunk_mag = jnp.exp(whole * (lr * dt))
        lam_ref[0:1, :] = chunk_mag * jnp.cos(whole * (li * dt))
        lam_ref[1:2, :] = chunk_mag * jnp.sin(whole * (li * dt))

    rows_per_step = e_re_ref.shape[0]
    t = (lax.broadcasted_iota(jnp.int32, (rows_per_step, N_STATES), 0)
         + pl.program_id(0) * rows_per_step).astype(F32)
    grow = jnp.exp(t * (lr * dt))
    theta = t * (li * dt)
    c = jnp.cos(theta)
    s = jnp.sin(theta)
    e_re_ref[...] = (grow * c).astype(BF16)
    e_im_ref[...] = (grow * s).astype(BF16)
    shrink = jnp.exp(-t * (lr * dt))
    einv_re_ref[...] = (shrink * c).astype(BF16)
    einv_im_ref[...] = (-(shrink * s)).astype(BF16)


def _ssm_prep(a_re, a_im, log_dt, b_re, b_im, c_re, c_im, w_in):
    row = lambda v: v.reshape(1, N_STATES)
    rep = lambda v: jnp.repeat(v, SSM_GROUP, axis=0)
    logdt_gp = jnp.broadcast_to(log_dt[:, None], (SSM_GROUPS, SSM_STATE))
    chan_state = lambda b: jnp.swapaxes(b, 1, 2).reshape(SSM_WIDTH, SSM_STATE)
    state_chan = lambda c: jnp.swapaxes(c, 1, 2).reshape(N_STATES, SSM_GROUP)
    params = (row(a_re), row(a_im), row(logdt_gp), rep(a_re), rep(a_im), rep(logdt_gp),
              chan_state(b_re), chan_state(b_im), state_chan(c_re), state_chan(c_im))
    whole = lambda shape: pl.BlockSpec(shape, lambda i: (0,) * len(shape))
    slab = lambda arr: pl.BlockSpec((arr.shape[0] // PREP_STEPS, arr.shape[1]), lambda i: (i, 0))
    tab = jax.ShapeDtypeStruct((SSM_CHUNK, N_STATES), BF16)
    cmat = jax.ShapeDtypeStruct((N_STATES, HALF_WIDTH), BF16)
    bmat = jax.ShapeDtypeStruct((SSM_WIDTH, 2 * HALF_STATES), BF16)
    lam = jax.ShapeDtypeStruct((2, N_STATES), F32)
    return pl.pallas_call(
        _ssm_prep_kernel,
        grid=(PREP_STEPS,),
        in_specs=[whole(p.shape) for p in params] + [slab(w_in)],
        out_specs=[whole(bmat.shape), whole(cmat.shape), whole(cmat.shape), slab(tab), slab(tab), slab(tab),
                   slab(tab), whole(lam.shape), slab(w_in)],
        out_shape=[bmat, cmat, cmat, tab, tab, tab, tab, lam, jax.ShapeDtypeStruct(w_in.shape, BF16)],
        compiler_params=_params("arbitrary"),
        name="ssm_prep",
    )(*params, w_in)


def _gelu_tanh(x):
    return 0.5 * x * (1.0 + jnp.tanh(math.sqrt(2.0 / math.pi) * (x + 0.044715 * (x * x * x))))


def _ssm_kernel(u_ref, bmat_ref, cre_ref, cim_ref, tri_ref, e_re_ref, e_im_ref, einv_re_ref, einv_im_ref,
                lam_ref, d_ref, y_ref, h_re_scr, h_im_scr, carry_scr):
    @pl.when(pl.program_id(0) == 0)
    def _():
        carry_scr[...] = jnp.zeros_like(carry_scr)

    u = u_ref[...]
    ub = u.astype(BF16)
    for hf in range(SSM_HALVES):
        chans = slice(hf * HALF_WIDTH, (hf + 1) * HALF_WIDTH)
        y = d_ref[:, chans] * u[:, chans]
        for blk in range(HALF_STATES // STATE_BLOCK):
            local = slice(blk * STATE_BLOCK, (blk + 1) * STATE_BLOCK)
            local_im = slice(HALF_STATES + blk * STATE_BLOCK, HALF_STATES + (blk + 1) * STATE_BLOCK)
            states = slice(hf * HALF_STATES + blk * STATE_BLOCK, hf * HALF_STATES + (blk + 1) * STATE_BLOCK)
            bu_re_all = _bdot(ub[:, chans], bmat_ref[chans, local])
            bu_im_all = _bdot(ub[:, chans], bmat_ref[chans, local_im])
            lam_re = lam_ref[0:1, states]
            lam_im = lam_ref[1:2, states]
            carry_re = carry_scr[0:1, states]
            carry_im = carry_scr[1:2, states]
            for c in range(u.shape[0] // SSM_CHUNK):
                rows = slice(c * SSM_CHUNK, (c + 1) * SSM_CHUNK)
                bu_re = bu_re_all[rows].astype(BF16)
                bu_im = bu_im_all[rows].astype(BF16)
                einv_re = einv_re_ref[:, states]
                einv_im = einv_im_ref[:, states]
                x = jnp.concatenate([bu_re * einv_re - bu_im * einv_im,
                                     bu_re * einv_im + bu_im * einv_re], axis=1)
                a = _bdot(tri_ref[...], x)
                a_re = a[:, :STATE_BLOCK] + carry_re
                a_im = a[:, STATE_BLOCK:] + carry_im
                e_re = e_re_ref[:, states]
                e_im = e_im_ref[:, states]
                a_re_b = a_re.astype(BF16)
                a_im_b = a_im.astype(BF16)
                h_re_scr[rows, :] = e_re * a_re_b - e_im * a_im_b
                h_im_scr[rows, :] = e_re * a_im_b + e_im * a_re_b
                last_re = a_re[SSM_CHUNK - 1:SSM_CHUNK, :]
                last_im = a_im[SSM_CHUNK - 1:SSM_CHUNK, :]
                carry_re = lam_re * last_re - lam_im * last_im
                carry_im = lam_re * last_im + lam_im * last_re
            carry_scr[0:1, states] = carry_re
            carry_scr[1:2, states] = carry_im
            y = y + _bdot(h_re_scr[...], cre_ref[states, :]) - _bdot(h_im_scr[...], cim_ref[states, :])
        y_ref[:, chans] = _gelu_tanh(y).astype(BF16)


def _ssm(u, bmat, cre, cim, tri, e_re, e_im, einv_re, einv_im, lam, d_row):
    s = u.shape[0]
    tm = min(SSM_ROWS, s)
    row = pl.BlockSpec((tm, SSM_WIDTH), lambda i: (i, 0))
    consts = (bmat, cre, cim, tri, e_re, e_im, einv_re, einv_im, lam, d_row)
    return pl.pallas_call(
        _ssm_kernel,
        grid=(s // tm,),
        in_specs=[row] + [_resident(c.shape) for c in consts],
        out_specs=row,
        out_shape=jax.ShapeDtypeStruct((s, SSM_WIDTH), BF16),
        scratch_shapes=[pltpu.VMEM((tm, STATE_BLOCK), BF16), pltpu.VMEM((tm, STATE_BLOCK), BF16),
                        pltpu.VMEM((2, N_STATES), F32)],
        compiler_params=_params("arbitrary"),
        name="ssm",
    )(u, *consts)


def _mix_kernel(*refs, n_cast):
    x_ref, o0_ref, o1_ref, o2_ref, l0_ref, l1_ref, l2_ref, y_ref, gates_ref = refs[:9]
    wap_ref, wa_ref, wb_ref, wout_ref = refs[9:13]
    cast_in, h_ref, cast_out = refs[13:13 + n_cast], refs[13 + n_cast], refs[14 + n_cast:]
    _cast_blocks(cast_in, cast_out)
    for rb in range(x_ref.shape[0] // MIX_SUB_ROWS):
        rows = slice(rb * MIX_SUB_ROWS, (rb + 1) * MIX_SUB_ROWS)
        ls = (l0_ref[rows, :], l1_ref[rows, :], l2_ref[rows, :])
        m_max = jnp.maximum(jnp.maximum(ls[0], ls[1]), ls[2])
        es = [jnp.exp2(l - m_max) for l in ls]
        dens = [pltpu.roll(l, HEAD_DIM - STAT_LANES // 2, 1) for l in ls]
        inv = 1.0 / (es[0] * dens[0] + es[1] * dens[1] + es[2] * dens[2])
        heads = []
        for h in range(HEADS_PER_GROUP):
            hs = slice(h * HEAD_DIM, (h + 1) * HEAD_DIM)
            col = slice(h * STAT_LANES, h * STAT_LANES + 1)
            heads.append(sum((e * inv)[:, col] * o_ref[rows, hs].astype(F32)
                             for e, o_ref in zip(es, (o0_ref, o1_ref, o2_ref))))
        attn = jnp.concatenate(heads, axis=1).astype(BF16)
        attn_d = _bdot(attn, wap_ref[...])
        y = y_ref[rows, :]
        ssm_d = _bdot(y, wa_ref[...]) * jax.nn.sigmoid(_bdot(y, wb_ref[...]))
        mix = (gates_ref[rows, :D_MODEL].astype(F32) * attn_d
               + gates_ref[rows, D_MODEL:].astype(F32) * ssm_d)
        h_ref[rows, :] = x_ref[rows, :] + _bdot(mix.astype(BF16), wout_ref[...])


def _mix(x, os, ls, y, gates, wap, wa, wb, wout, later_weights):
    s = x.shape[0]
    tm = min(MIX_ROWS, s)
    row = lambda w: pl.BlockSpec((tm, w), lambda i: (i, 0))
    weights = (wap, wa, wb, wout)
    cast_specs, cast_shapes = _cast_plan(later_weights, s // tm)
    return pl.pallas_call(
        functools.partial(_mix_kernel, n_cast=len(later_weights)),
        grid=(s // tm,),
        in_specs=[row(D_MODEL)] + [row(GROUP_WIDTH)] * 3 + [row(HEAD_DIM)] * 3
        + [row(SSM_WIDTH), row(2 * D_MODEL)] + [_resident(w.shape) for w in weights] + cast_specs,
        out_specs=[row(D_MODEL)] + cast_specs,
        out_shape=[jax.ShapeDtypeStruct((s, D_MODEL), F32)] + cast_shapes,
        compiler_params=_params("parallel"),
        name="mix",
    )(x, *os, *ls, y, gates, *weights, *later_weights)


def _ffn_kernel(h_ref, p_ref, gffn_ref, wg_ref, wu_ref, wd_ref, wpg_ref, wpp_ref, gfin_ref,
                out_ref, acc_scr):
    for rb in range(h_ref.shape[0] // FFN_SUB_ROWS):
        rows = slice(rb * FFN_SUB_ROWS, (rb + 1) * FFN_SUB_ROWS)
        h = h_ref[rows, :]
        n2 = _rmsnorm(h, gffn_ref[...]).astype(BF16)
        for idx, (c0, width) in enumerate(FFN_CHUNKS):
            gate = _bdot(n2, wg_ref[:, c0:c0 + width])
            up = _bdot(n2, wu_ref[:, c0:c0 + width])
            act = (gate * jax.nn.sigmoid(gate) * up).astype(BF16)
            part = _bdot(act, wd_ref[c0:c0 + width, :])
            if idx == 0:
                acc_scr[rows, :] = h + part
            else:
                acc_scr[rows, :] += part
        h2 = acc_scr[rows, :]
        ple = (jax.nn.sigmoid(_bdot(h2.astype(BF16), wpg_ref[...]))
               * _bdot(p_ref[rows, :].astype(BF16), wpp_ref[...]))
        out_ref[rows, :] = _rmsnorm(h2 + ple, gfin_ref[...])


def _ffn(h, p, g_ffn, wg, wu, wd, wpg, wpp, g_final):
    s = h.shape[0]
    tm = min(FFN_ROWS, s)
    row = lambda w: pl.BlockSpec((tm, w), lambda i: (i, 0))
    consts = (g_ffn, wg, wu, wd, wpg, wpp, g_final)
    return pl.pallas_call(
        _ffn_kernel,
        grid=(s // tm,),
        in_specs=[row(D_MODEL), row(PLE_DIM)] + [_resident(c.shape) for c in consts],
        out_specs=row(D_MODEL),
        out_shape=jax.ShapeDtypeStruct((s, D_MODEL), F32),
        scratch_shapes=[pltpu.VMEM((tm, D_MODEL), F32)],
        compiler_params=_params("parallel"),
        name="ffn",
    )(h, p, *consts)


def _layer(x, p, positions, g_mix, w_in, a_re, a_im, log_dt, b_re, b_im, c_re, c_im, d_skip,
           w_attn_proj, w_glu_a, w_glu_b, w_out, g_ffn, w_ffn_gate, w_ffn_up, w_ffn_down,
           w_ple_gate, w_ple_proj, g_final):
    row = lambda v: v.reshape(1, -1).astype(F32)

    inv_freq = ROPE_THETA ** (-jnp.arange(ROPE_HALF, dtype=F32) * 2.0 / ROPE_DIM)
    bmat, cre, cim, e_re, e_im, einv_re, einv_im, lam, w_in_bf = _ssm_prep(
        a_re, a_im, log_dt, b_re, b_im, c_re, c_im, w_in)
    outs = _proj(x, positions.reshape(1, -1), row(g_mix), inv_freq.reshape(ROPE_HALF, 1), w_in_bf,
                 (w_attn_proj, w_glu_a, w_glu_b, w_out, w_ffn_gate, w_ffn_up, w_ple_gate))
    qkvs, (u, gates) = outs[:N_GROUPS], outs[N_GROUPS:N_GROUPS + 2]
    wap, wga, wgb, wout, wfg, wfu, wpg = outs[N_GROUPS + 2:]

    attn_os, attn_ls = zip(*[_attn_group(qkvs[g], d) for g, d in enumerate(ATTN_DILATIONS)])

    tri = jnp.tril(jnp.ones((SSM_CHUNK, SSM_CHUNK), F32)).astype(BF16)
    y = _ssm(u, bmat, cre, cim, tri, e_re, e_im, einv_re, einv_im, lam, row(d_skip))

    h1, wfd, wpp = _mix(x, attn_os, attn_ls, y, gates, wap, wga, wgb, wout, (w_ffn_down, w_ple_proj))
    return _ffn(h1, p, row(g_ffn), wfg, wfu, wfd, wpg, wpp, row(g_final))


def kernel(x, p, positions, g_mix, w_in, a_re, a_im, log_dt, b_re, b_im, c_re, c_im, d_skip,
           w_attn_proj, w_glu_a, w_glu_b, w_out, g_ffn, w_ffn_gate, w_ffn_up, w_ffn_down,
           w_ple_gate, w_ple_proj, g_final):
    batch, depth = x.shape[0], p.shape[0]
    assert batch == 1 and depth == 1, "kernel supports the stated BATCH=1, DEPTH=1 problem"
    out = _layer(x[0], p[0, 0], positions[0], g_mix[0], w_in[0], a_re[0], a_im[0], log_dt[0],
                 b_re[0], b_im[0], c_re[0], c_im[0], d_skip[0], w_attn_proj[0], w_glu_a[0],
                 w_glu_b[0], w_out[0], g_ffn[0], w_ffn_gate[0], w_ffn_up[0], w_ffn_down[0],
                 w_ple_gate[0], w_ple_proj[0], g_final)
    return out[None]
```

```python
import functools
import math

import jax
import jax.numpy as jnp
from jax import lax
from jax.experimental import pallas as pl
from jax.experimental.pallas import tpu as pltpu

F32 = jnp.float32
BF16 = jnp.bfloat16

D_MODEL = 1024
HEAD_DIM = 128
HEADS_PER_GROUP = 4
GROUP_WIDTH = HEADS_PER_GROUP * HEAD_DIM
ATTN_DILATIONS = (1, 4, 16)
N_GROUPS = len(ATTN_DILATIONS)
QK_WIDTH = N_GROUPS * GROUP_WIDTH
BLOCK = 128
ROPE_THETA = 500000.0
ROPE_DIM = HEAD_DIM // 4
ROPE_HALF = ROPE_DIM // 2
SSM_WIDTH = 512
SSM_GROUP = 16
SSM_GROUPS = SSM_WIDTH // SSM_GROUP
SSM_STATE = 64
N_STATES = SSM_GROUPS * SSM_STATE
SSM_HALVES = 2
HALF_GROUPS = SSM_GROUPS // SSM_HALVES
HALF_WIDTH = SSM_WIDTH // SSM_HALVES
HALF_STATES = N_STATES // SSM_HALVES
STATE_BLOCK = 256
D_FF = 2816
PLE_DIM = 256
EPS = 1e-6
MASK_VALUE = -1e30

V7X_VMEM_LIMIT_BYTES = 56 * 1024 * 1024

PROJ_ROWS = 512
PROJ_SUB_ROWS = 256
ATTN_TOKENS = 2048
STAT_LANES = HEAD_DIM // HEADS_PER_GROUP
SSM_CHUNK = 128
SSM_ROWS = 512
PREP_STEPS = 8
MIX_ROWS = 1024
MIX_SUB_ROWS = 256
FFN_ROWS = 1024
FFN_SUB_ROWS = 256
FFN_CHUNKS = ((0, 1024), (1024, 1024), (2048, 768))


def _resident(shape):
    return pl.BlockSpec(shape, lambda *_: (0,) * len(shape), pipeline_mode=pl.Buffered(1))


def _params(*semantics):
    return pltpu.CompilerParams(dimension_semantics=semantics,
                                vmem_limit_bytes=V7X_VMEM_LIMIT_BYTES)


def _rmsnorm(x, g):
    return (x * lax.rsqrt(jnp.mean(x * x, axis=-1, keepdims=True) + EPS)) * g


def _bdot(a, b):
    return jnp.dot(a, b, preferred_element_type=F32)


def _cast_plan(weights, steps):
    specs = [pl.BlockSpec((w.shape[0] // steps, w.shape[1]), lambda i: (i, 0)) for w in weights]
    shapes = [jax.ShapeDtypeStruct(w.shape, BF16) for w in weights]
    return specs, shapes


def _cast_blocks(in_refs, out_refs):
    for src, dst in zip(in_refs, out_refs):
        dst[...] = src[...].astype(BF16)


def _proj_kernel(*refs, n_cast):
    x_ref, pos_ref, g_ref, invf_ref, w_ref = refs[:5]
    cast_in, refs = refs[5:5 + n_cast], refs[5 + n_cast:]
    qkv0, qkv1, qkv2, u_ref, gates_ref = refs[:5]
    cast_out, (n_scr, nperm_scr, cos_scr, sin_scr) = refs[5:5 + n_cast], refs[5 + n_cast:]
    _cast_blocks(cast_in, cast_out)
    rows = PROJ_SUB_ROWS
    lane = lax.broadcasted_iota(jnp.int32, (rows, HEAD_DIM), 1)
    first_half = lane < ROPE_HALF
    scale = math.log2(math.e) / math.sqrt(HEAD_DIM)

    for sb in range(x_ref.shape[0] // rows):
        base = sb * rows
        tile_rows = slice(base, base + rows)
        xn = _rmsnorm(x_ref[tile_rows, :], g_ref[...])
        n = xn.astype(BF16)
        for c in range(D_MODEL // HEAD_DIM):
            n_scr[c, tile_rows, :] = xn[:, c * HEAD_DIM:(c + 1) * HEAD_DIM]

        ang = invf_ref[...] * pos_ref[:, tile_rows].astype(F32)
        cos_t = jnp.cos(ang)
        sin_t = jnp.sin(ang)
        rest = (HEAD_DIM - ROPE_DIM, rows)
        cos_scr[tile_rows, :] = jnp.concatenate([cos_t, cos_t, jnp.ones(rest, F32)], axis=0).T
        sin_scr[tile_rows, :] = jnp.concatenate([-sin_t, sin_t, jnp.zeros(rest, F32)], axis=0).T

        for g, (d, qkv_ref) in enumerate(zip(ATTN_DILATIONS, (qkv0, qkv1, qkv2))):
            def residue_major(ref_2d):
                if d == 1:
                    return ref_2d[tile_rows, :]
                return jnp.concatenate(
                    [ref_2d[pl.ds(base + r, rows // d, stride=d), :] for r in range(d)], axis=0)

            cos = residue_major(cos_scr)
            sin_signed = residue_major(sin_scr)
            if d == 1:
                ng = n
            else:
                for c in range(D_MODEL // HEAD_DIM):
                    nperm_scr[tile_rows, c * HEAD_DIM:(c + 1) * HEAD_DIM] = (
                        residue_major(n_scr.at[c]).astype(BF16))
                ng = nperm_scr[tile_rows, :]

            def rotary(t):
                partner = jnp.where(first_half,
                                    pltpu.roll(t, HEAD_DIM - ROPE_HALF, 1),
                                    pltpu.roll(t, ROPE_HALF, 1))
                return t * cos + partner * sin_signed

            c0 = g * GROUP_WIDTH
            zq = _bdot(ng, w_ref[:, c0:c0 + GROUP_WIDTH])
            zk = _bdot(ng, w_ref[:, QK_WIDTH + c0:QK_WIDTH + c0 + GROUP_WIDTH])
            zv = _bdot(ng, w_ref[:, 2 * QK_WIDTH + c0:2 * QK_WIDTH + c0 + GROUP_WIDTH])
            sub_rows = slice(base // d, (base + rows) // d)
            for h in range(HEADS_PER_GROUP):
                hs = slice(h * HEAD_DIM, (h + 1) * HEAD_DIM)
                ks = slice(GROUP_WIDTH + h * HEAD_DIM, GROUP_WIDTH + (h + 1) * HEAD_DIM)
                qkv_ref[:, sub_rows, hs] = (
                    (rotary(zq[:, hs]) * scale).astype(BF16).reshape(d, rows // d, HEAD_DIM))
                qkv_ref[:, sub_rows, ks] = rotary(zk[:, hs]).astype(BF16).reshape(d, rows // d, HEAD_DIM)
            qkv_ref[:, sub_rows, 2 * GROUP_WIDTH:] = zv.astype(BF16).reshape(d, rows // d, GROUP_WIDTH)
        o1 = 3 * QK_WIDTH
        u_ref[tile_rows, :] = _bdot(n, w_ref[:, o1:o1 + SSM_WIDTH])
        o2 = o1 + SSM_WIDTH
        gates_ref[tile_rows, :] = jax.nn.sigmoid(_bdot(n, w_ref[:, o2:o2 + 2 * D_MODEL])).astype(BF16)


def _proj(x, pos_row, g_mix, inv_freq_col, w_in, later_weights):
    s = x.shape[0]
    tm = min(PROJ_ROWS, s)
    row = lambda w: pl.BlockSpec((tm, w), lambda i: (i, 0))
    cast_specs, cast_shapes = _cast_plan(later_weights, s // tm)
    grp_specs, grp_shapes = [], []
    for d in ATTN_DILATIONS:
        grp_specs.append(pl.BlockSpec((d, tm // d, 3 * GROUP_WIDTH), lambda i: (0, i, 0)))
        grp_shapes.append(jax.ShapeDtypeStruct((d, s // d, 3 * GROUP_WIDTH), BF16))
    return pl.pallas_call(
        functools.partial(_proj_kernel, n_cast=len(later_weights)),
        grid=(s // tm,),
        in_specs=[row(D_MODEL), pl.BlockSpec((1, tm), lambda i: (0, i)), _resident((1, D_MODEL)),
                  _resident((ROPE_HALF, 1)), _resident(w_in.shape)] + cast_specs,
        out_specs=grp_specs + [row(SSM_WIDTH), row(2 * D_MODEL)] + cast_specs,
        out_shape=grp_shapes + [jax.ShapeDtypeStruct((s, SSM_WIDTH), F32),
                                jax.ShapeDtypeStruct((s, 2 * D_MODEL), BF16)] + cast_shapes,
        scratch_shapes=[pltpu.VMEM((D_MODEL // HEAD_DIM, tm, HEAD_DIM), F32), pltpu.VMEM((tm, D_MODEL), BF16),
                        pltpu.VMEM((tm, HEAD_DIM), F32), pltpu.VMEM((tm, HEAD_DIM), F32)],
        compiler_params=_params("parallel"),
        name="proj",
    )(x, pos_row, g_mix, inv_freq_col, w_in, *later_weights)


def _attn_kernel(qkv_ref, o_ref, l_ref, prev_scr, *scatter_scr, d, nsub):
    not_first_block = pl.program_id(0) > 0
    if d > 1:
        o_scr, l_scr = scatter_scr

    @pl.when(pl.program_id(0) == 0)
    def _():
        prev_scr[...] = jnp.zeros_like(prev_scr)

    qi = lax.broadcasted_iota(jnp.int32, (BLOCK, 2 * BLOCK), 0)
    kj = lax.broadcasted_iota(jnp.int32, (BLOCK, 2 * BLOCK), 1)
    rel = BLOCK + qi - kj
    band = (rel >= 0) & (rel <= BLOCK)
    band_first = band & ((kj >= BLOCK) | not_first_block)
    lane = lax.broadcasted_iota(jnp.int32, (BLOCK, HEAD_DIM), 1)

    def residue(r):
        for b in range(nsub):
            rows = slice(b * BLOCK, (b + 1) * BLOCK)
            mask = band_first if b == 0 else band
            token_rows = pl.ds(b * BLOCK * d + r, BLOCK, stride=d) if d > 1 else rows
            for h in range(HEADS_PER_GROUP):
                hs = slice(h * HEAD_DIM, (h + 1) * HEAD_DIM)
                ks = slice(GROUP_WIDTH + h * HEAD_DIM, GROUP_WIDTH + (h + 1) * HEAD_DIM)
                vs = slice(2 * GROUP_WIDTH + h * HEAD_DIM, 2 * GROUP_WIDTH + (h + 1) * HEAD_DIM)
                q = qkv_ref[r, rows, hs]
                if b == 0:
                    prev_k = slice(h * HEAD_DIM, (h + 1) * HEAD_DIM)
                    prev_v = slice(GROUP_WIDTH + h * HEAD_DIM, GROUP_WIDTH + (h + 1) * HEAD_DIM)
                    kw = jnp.concatenate([prev_scr[r, :, prev_k], qkv_ref[r, rows, ks]], axis=0)
                    vw = jnp.concatenate([prev_scr[r, :, prev_v], qkv_ref[r, rows, vs]], axis=0)
                else:
                    win = slice((b - 1) * BLOCK, (b + 1) * BLOCK)
                    kw = qkv_ref[r, win, ks]
                    vw = qkv_ref[r, win, vs]
                s = lax.dot_general(q, kw, (((1,), (1,)), ((), ())), preferred_element_type=F32)
                s = jnp.where(mask, s, MASK_VALUE)
                m = jnp.max(s, axis=-1, keepdims=True)
                p = jnp.exp2(s - m)
                den = jnp.sum(p, axis=-1, keepdims=True)
                pv = _bdot(p.astype(BF16), vw)
                if d > 1:
                    o_scr[h, token_rows, :] = pv
                else:
                    o_ref[rows, hs] = pv.astype(BF16)
                m_b = jnp.broadcast_to(m, (BLOCK, HEAD_DIM))
                den_b = jnp.broadcast_to(den, (BLOCK, HEAD_DIM))
                if h > 0:
                    m_b = jnp.where(lane >= h * STAT_LANES, m_b, stats)
                stats = jnp.where(lane >= h * STAT_LANES + STAT_LANES // 2, den_b, m_b)
            if d > 1:
                l_scr[token_rows, :] = stats
            else:
                l_ref[rows, :] = stats

    for r in range(d):
        residue(r)

    if d > 1:
        for h in range(HEADS_PER_GROUP):
            o_ref[:, h * HEAD_DIM:(h + 1) * HEAD_DIM] = o_scr[h].astype(BF16)
        l_ref[...] = l_scr[...]
    prev_scr[...] = qkv_ref[:, (nsub - 1) * BLOCK:nsub * BLOCK, GROUP_WIDTH:]


def _attn_group(qkv, dilation):
    sub_len = qkv.shape[1]
    s = sub_len * dilation
    step_tokens = min(ATTN_TOKENS, s)
    qb = step_tokens // dilation
    nsub = qb // BLOCK
    cur = pl.BlockSpec((dilation, qb, 3 * GROUP_WIDTH), lambda i: (0, i, 0))
    scatter_scratch = [] if dilation == 1 else [
        pltpu.VMEM((HEADS_PER_GROUP, step_tokens, HEAD_DIM), F32), pltpu.VMEM((step_tokens, HEAD_DIM), F32)]
    return pl.pallas_call(
        functools.partial(_attn_kernel, d=dilation, nsub=nsub),
        grid=(s // step_tokens,),
        in_specs=[cur],
        out_specs=[pl.BlockSpec((step_tokens, GROUP_WIDTH), lambda i: (i, 0)),
                   pl.BlockSpec((step_tokens, HEAD_DIM), lambda i: (i, 0))],
        out_shape=[jax.ShapeDtypeStruct((s, GROUP_WIDTH), BF16), jax.ShapeDtypeStruct((s, HEAD_DIM), F32)],
        scratch_shapes=[pltpu.VMEM((dilation, BLOCK, 2 * GROUP_WIDTH), BF16)] + scatter_scratch,
        compiler_params=_params("arbitrary"),
        name=f"attn_d{dilation}",
    )(qkv)


def _discretize(lr, li, log_dt):
    dt = jnp.exp(log_dt)
    mag = jnp.exp(lr * dt)
    bar_re = mag * jnp.cos(li * dt)
    bar_im = mag * jnp.sin(li * dt)
    nr = bar_re - 1.0
    ni = bar_im
    den = lr * lr + li * li
    return bar_re, bar_im, (nr * lr + ni * li) / den, (ni * lr - nr * li) / den


def _expand_block_diag(compact, n_blocks):
    rows, b = compact.shape
    a = rows // n_blocks
    wide = n_blocks * b
    src_lane = lax.broadcasted_iota(jnp.int32, (b, wide), 0)
    dst_lane = lax.broadcasted_iota(jnp.int32, (b, wide), 1)
    tiled = _bdot(compact, (dst_lane % b == src_lane).astype(BF16))
    row_block = lax.broadcasted_iota(jnp.int32, (rows, wide), 0) // a
    col_block = lax.broadcasted_iota(jnp.int32, (rows, wide), 1) // b
    return jnp.where(row_block == col_block, tiled, 0.0).astype(BF16)


def _ssm_prep_kernel(lr_ref, li_ref, logdt_ref, lr_rep_ref, li_rep_ref, logdt_rep_ref, b_re_ref, b_im_ref,
                     c_re_ref, c_im_ref, w_in_ref,
                     bmat_ref, cre_ref, cim_ref, e_re_ref, e_im_ref, einv_re_ref, einv_im_ref, lam_ref,
                     w_in_bf_ref):
    w_in_bf_ref[...] = w_in_ref[...].astype(BF16)
    lr = lr_ref[...]
    li = li_ref[...]
    dt = jnp.exp(logdt_ref[...])

    @pl.when(pl.program_id(0) == 0)
    def _():
        _, _, z_re, z_im = _discretize(lr_rep_ref[...], li_rep_ref[...], logdt_rep_ref[...])
        b_re = b_re_ref[...]
        b_im = b_im_ref[...]
        bb_re = (z_re * b_re - z_im * b_im).astype(BF16)
        bb_im = (z_re * b_im + z_im * b_re).astype(BF16)
        for hf in range(SSM_HALVES):
            rows = slice(hf * HALF_WIDTH, (hf + 1) * HALF_WIDTH)
            bmat_ref[rows, :HALF_STATES] = _expand_block_diag(bb_re[rows], HALF_GROUPS)
            bmat_ref[rows, HALF_STATES:] = _expand_block_diag(bb_im[rows], HALF_GROUPS)
            srows = slice(hf * HALF_STATES, (hf + 1) * HALF_STATES)
            cre_ref[srows, :] = _expand_block_diag(c_re_ref[srows, :].astype(BF16), HALF_GROUPS)
            cim_ref[srows, :] = _expand_block_diag(c_im_ref[srows, :].astype(BF16), HALF_GROUPS)
        whole = float(SSM_CHUNK)
        chunk_mag = jnp.exp(whole * (lr * dt))
        lam_ref[0:1, :] = chunk_mag * jnp.cos(whole * (li * dt))
        lam_ref[1:2, :] = chunk_mag * jnp.sin(whole * (li * dt))

    rows_per_step = e_re_ref.shape[0]
    t = (lax.broadcasted_iota(jnp.int32, (rows_per_step, N_STATES), 0)
         + pl.program_id(0) * rows_per_step).astype(F32)
    grow = jnp.exp(t * (lr * dt))
    theta = t * (li * dt)
    c = jnp.cos(theta)
    s = jnp.sin(theta)
    e_re_ref[...] = (grow * c).astype(BF16)
    e_im_ref[...] = (grow * s).astype(BF16)
    shrink = jnp.exp(-t * (lr * dt))
    einv_re_ref[...] = (shrink * c).astype(BF16)
    einv_im_ref[...] = (-(shrink * s)).astype(BF16)


def _ssm_prep(a_re, a_im, log_dt, b_re, b_im, c_re, c_im, w_in):
    row = lambda v: v.reshape(1, N_STATES)
    rep = lambda v: jnp.repeat(v, SSM_GROUP, axis=0)
    logdt_gp = jnp.broadcast_to(log_dt[:, None], (SSM_GROUPS, SSM_STATE))
    chan_state = lambda b: jnp.swapaxes(b, 1, 2).reshape(SSM_WIDTH, SSM_STATE)
    state_chan = lambda c: jnp.swapaxes(c, 1, 2).reshape(N_STATES, SSM_GROUP)
    params = (row(a_re), row(a_im), row(logdt_gp), rep(a_re), rep(a_im), rep(logdt_gp),
              chan_state(b_re), chan_state(b_im), state_chan(c_re), state_chan(c_im))
    whole = lambda shape: pl.BlockSpec(shape, lambda i: (0,) * len(shape))
    slab = lambda arr: pl.BlockSpec((arr.shape[0] // PREP_STEPS, arr.shape[1]), lambda i: (i, 0))
    tab = jax.ShapeDtypeStruct((SSM_CHUNK, N_STATES), BF16)
    cmat = jax.ShapeDtypeStruct((N_STATES, HALF_WIDTH), BF16)
    bmat = jax.ShapeDtypeStruct((SSM_WIDTH, 2 * HALF_STATES), BF16)
    lam = jax.ShapeDtypeStruct((2, N_STATES), F32)
    return pl.pallas_call(
        _ssm_prep_kernel,
        grid=(PREP_STEPS,),
        in_specs=[whole(p.shape) for p in params] + [slab(w_in)],
        out_specs=[whole(bmat.shape), whole(cmat.shape), whole(cmat.shape), slab(tab), slab(tab), slab(tab),
                   slab(tab), whole(lam.shape), slab(w_in)],
        out_shape=[bmat, cmat, cmat, tab, tab, tab, tab, lam, jax.ShapeDtypeStruct(w_in.shape, BF16)],
        compiler_params=_params("arbitrary"),
        name="ssm_prep",
    )(*params, w_in)


def _gelu_tanh(x):
    return 0.5 * x * (1.0 + jnp.tanh(math.sqrt(2.0 / math.pi) * (x + 0.044715 * (x * x * x))))


def _ssm_kernel(u_ref, bmat_ref, cre_ref, cim_ref, tri_ref, e_re_ref, e_im_ref, einv_re_ref, einv_im_ref,
                lam_ref, d_ref, y_ref, h_re_scr, h_im_scr, carry_scr):
    @pl.when(pl.program_id(0) == 0)
    def _():
        carry_scr[...] = jnp.zeros_like(carry_scr)

    u = u_ref[...]
    ub = u.astype(BF16)
    for hf in range(SSM_HALVES):
        chans = slice(hf * HALF_WIDTH, (hf + 1) * HALF_WIDTH)
        y = d_ref[:, chans] * u[:, chans]
        for blk in range(HALF_STATES // STATE_BLOCK):
            local = slice(blk * STATE_BLOCK, (blk + 1) * STATE_BLOCK)
            local_im = slice(HALF_STATES + blk * STATE_BLOCK, HALF_STATES + (blk + 1) * STATE_BLOCK)
            states = slice(hf * HALF_STATES + blk * STATE_BLOCK, hf * HALF_STATES + (blk + 1) * STATE_BLOCK)
            bu_re_all = _bdot(ub[:, chans], bmat_ref[chans, local])
            bu_im_all = _bdot(ub[:, chans], bmat_ref[chans, local_im])
            lam_re = lam_ref[0:1, states]
            lam_im = lam_ref[1:2, states]
            carry_re = carry_scr[0:1, states]
            carry_im = carry_scr[1:2, states]
            for c in range(u.shape[0] // SSM_CHUNK):
                rows = slice(c * SSM_CHUNK, (c + 1) * SSM_CHUNK)
                bu_re = bu_re_all[rows].astype(BF16)
                bu_im = bu_im_all[rows].astype(BF16)
                einv_re = einv_re_ref[:, states]
                einv_im = einv_im_ref[:, states]
                x = jnp.concatenate([bu_re * einv_re - bu_im * einv_im,
                                     bu_re * einv_im + bu_im * einv_re], axis=1)
                a = _bdot(tri_ref[...], x)
                a_re = a[:, :STATE_BLOCK] + carry_re
                a_im = a[:, STATE_BLOCK:] + carry_im
                e_re = e_re_ref[:, states]
                e_im = e_im_ref[:, states]
                a_re_b = a_re.astype(BF16)
                a_im_b = a_im.astype(BF16)
                h_re_scr[rows, :] = e_re * a_re_b - e_im * a_im_b
                h_im_scr[rows, :] = e_re * a_im_b + e_im * a_re_b
                last_re = a_re[SSM_CHUNK - 1:SSM_CHUNK, :]
                last_im = a_im[SSM_CHUNK - 1:SSM_CHUNK, :]
                carry_re = lam_re * last_re - lam_im * last_im
                carry_im = lam_re * last_im + lam_im * last_re
            carry_scr[0:1, states] = carry_re
            carry_scr[1:2, states] = carry_im
            y = y + _bdot(h_re_scr[...], cre_ref[states, :]) - _bdot(h_im_scr[...], cim_ref[states, :])
        y_ref[:, chans] = _gelu_tanh(y).astype(BF16)


def _ssm(u, bmat, cre, cim, tri, e_re, e_im, einv_re, einv_im, lam, d_row):
    s = u.shape[0]
    tm = min(SSM_ROWS, s)
    row = pl.BlockSpec((tm, SSM_WIDTH), lambda i: (i, 0))
    consts = (bmat, cre, cim, tri, e_re, e_im, einv_re, einv_im, lam, d_row)
    return pl.pallas_call(
        _ssm_kernel,
        grid=(s // tm,),
        in_specs=[row] + [_resident(c.shape) for c in consts],
        out_specs=row,
        out_shape=jax.ShapeDtypeStruct((s, SSM_WIDTH), BF16),
        scratch_shapes=[pltpu.VMEM((tm, STATE_BLOCK), BF16), pltpu.VMEM((tm, STATE_BLOCK), BF16),
                        pltpu.VMEM((2, N_STATES), F32)],
        compiler_params=_params("arbitrary"),
        name="ssm",
    )(u, *consts)


def _mix_kernel(*refs, n_cast):
    x_ref, o0_ref, o1_ref, o2_ref, l0_ref, l1_ref, l2_ref, y_ref, gates_ref = refs[:9]
    wap_ref, wa_ref, wb_ref, wout_ref = refs[9:13]
    cast_in, h_ref, cast_out = refs[13:13 + n_cast], refs[13 + n_cast], refs[14 + n_cast:]
    _cast_blocks(cast_in, cast_out)
    for rb in range(x_ref.shape[0] // MIX_SUB_ROWS):
        rows = slice(rb * MIX_SUB_ROWS, (rb + 1) * MIX_SUB_ROWS)
        ls = (l0_ref[rows, :], l1_ref[rows, :], l2_ref[rows, :])
        m_max = jnp.maximum(jnp.maximum(ls[0], ls[1]), ls[2])
        es = [jnp.exp2(l - m_max) for l in ls]
        dens = [pltpu.roll(l, HEAD_DIM - STAT_LANES // 2, 1) for l in ls]
        inv = 1.0 / (es[0] * dens[0] + es[1] * dens[1] + es[2] * dens[2])
        heads = []
        for h in range(HEADS_PER_GROUP):
            hs = slice(h * HEAD_DIM, (h + 1) * HEAD_DIM)
            col = slice(h * STAT_LANES, h * STAT_LANES + 1)
            heads.append(sum((e * inv)[:, col] * o_ref[rows, hs].astype(F32)
                             for e, o_ref in zip(es, (o0_ref, o1_ref, o2_ref))))
        attn = jnp.concatenate(heads, axis=1).astype(BF16)
        attn_d = _bdot(attn, wap_ref[...])
        y = y_ref[rows, :]
        ssm_d = _bdot(y, wa_ref[...]) * jax.nn.sigmoid(_bdot(y, wb_ref[...]))
        mix = (gates_ref[rows, :D_MODEL].astype(F32) * attn_d
               + gates_ref[rows, D_MODEL:].astype(F32) * ssm_d)
        h_ref[rows, :] = x_ref[rows, :] + _bdot(mix.astype(BF16), wout_ref[...])


def _mix(x, os, ls, y, gates, wap, wa, wb, wout, later_weights):
    s = x.shape[0]
    tm = min(MIX_ROWS, s)
    row = lambda w: pl.BlockSpec((tm, w), lambda i: (i, 0))
    weights = (wap, wa, wb, wout)
    cast_specs, cast_shapes = _cast_plan(later_weights, s // tm)
    return pl.pallas_call(
        functools.partial(_mix_kernel, n_cast=len(later_weights)),
        grid=(s // tm,),
        in_specs=[row(D_MODEL)] + [row(GROUP_WIDTH)] * 3 + [row(HEAD_DIM)] * 3
        + [row(SSM_WIDTH), row(2 * D_MODEL)] + [_resident(w.shape) for w in weights] + cast_specs,
        out_specs=[row(D_MODEL)] + cast_specs,
        out_shape=[jax.ShapeDtypeStruct((s, D_MODEL), F32)] + cast_shapes,
        compiler_params=_params("parallel"),
        name="mix",
    )(x, *os, *ls, y, gates, *weights, *later_weights)


def _ffn_kernel(h_ref, p_ref, gffn_ref, wg_ref, wu_ref, wd_ref, wpg_ref, wpp_ref, gfin_ref,
                out_ref, acc_scr):
    for rb in range(h_ref.shape[0] // FFN_SUB_ROWS):
        rows = slice(rb * FFN_SUB_ROWS, (rb + 1) * FFN_SUB_ROWS)
        h = h_ref[rows, :]
        n2 = _rmsnorm(h, gffn_ref[...]).astype(BF16)
        for idx, (c0, width) in enumerate(FFN_CHUNKS):
            gate = _bdot(n2, wg_ref[:, c0:c0 + width])
            up = _bdot(n2, wu_ref[:, c0:c0 + width])
            act = (gate * jax.nn.sigmoid(gate) * up).astype(BF16)
            part = _bdot(act, wd_ref[c0:c0 + width, :])
            if idx == 0:
                acc_scr[rows, :] = h + part
            else:
                acc_scr[rows, :] += part
        h2 = acc_scr[rows, :]
        ple = (jax.nn.sigmoid(_bdot(h2.astype(BF16), wpg_ref[...]))
               * _bdot(p_ref[rows, :].astype(BF16), wpp_ref[...]))
        out_ref[rows, :] = _rmsnorm(h2 + ple, gfin_ref[...])


def _ffn(h, p, g_ffn, wg, wu, wd, wpg, wpp, g_final):
    s = h.shape[0]
    tm = min(FFN_ROWS, s)
    row = lambda w: pl.BlockSpec((tm, w), lambda i: (i, 0))
    consts = (g_ffn, wg, wu, wd, wpg, wpp, g_final)
    return pl.pallas_call(
        _ffn_kernel,
        grid=(s // tm,),
        in_specs=[row(D_MODEL), row(PLE_DIM)] + [_resident(c.shape) for c in consts],
        out_specs=row(D_MODEL),
        out_shape=jax.ShapeDtypeStruct((s, D_MODEL), F32),
        scratch_shapes=[pltpu.VMEM((tm, D_MODEL), F32)],
        compiler_params=_params("parallel"),
        name="ffn",
    )(h, p, *consts)


def _layer(x, p, positions, g_mix, w_in, a_re, a_im, log_dt, b_re, b_im, c_re, c_im, d_skip,
           w_attn_proj, w_glu_a, w_glu_b, w_out, g_ffn, w_ffn_gate, w_ffn_up, w_ffn_down,
           w_ple_gate, w_ple_proj, g_final):
    row = lambda v: v.reshape(1, -1).astype(F32)

    inv_freq = ROPE_THETA ** (-jnp.arange(ROPE_HALF, dtype=F32) * 2.0 / ROPE_DIM)
    bmat, cre, cim, e_re, e_im, einv_re, einv_im, lam, w_in_bf = _ssm_prep(
        a_re, a_im, log_dt, b_re, b_im, c_re, c_im, w_in)
    outs = _proj(x, positions.reshape(1, -1), row(g_mix), inv_freq.reshape(ROPE_HALF, 1), w_in_bf,
                 (w_attn_proj, w_glu_a, w_glu_b, w_out, w_ffn_gate, w_ffn_up, w_ple_gate))
    qkvs, (u, gates) = outs[:N_GROUPS], outs[N_GROUPS:N_GROUPS + 2]
    wap, wga, wgb, wout, wfg, wfu, wpg = outs[N_GROUPS + 2:]

    attn_os, attn_ls = zip(*[_attn_group(qkvs[g], d) for g, d in enumerate(ATTN_DILATIONS)])

    tri = jnp.tril(jnp.ones((SSM_CHUNK, SSM_CHUNK), F32)).astype(BF16)
    y = _ssm(u, bmat, cre, cim, tri, e_re, e_im, einv_re, einv_im, lam, row(d_skip))

    h1, wfd, wpp = _mix(x, attn_os, attn_ls, y, gates, wap, wga, wgb, wout, (w_ffn_down, w_ple_proj))
    return _ffn(h1, p, row(g_ffn), wfg, wfu, wfd, wpg, wpp, row(g_final))


def kernel(x, p, positions, g_mix, w_in, a_re, a_im, log_dt, b_re, b_im, c_re, c_im, d_skip,
           w_attn_proj, w_glu_a, w_glu_b, w_out, g_ffn, w_ffn_gate, w_ffn_up, w_ffn_down,
           w_ple_gate, w_ple_proj, g_final):
    batch, depth = x.shape[0], p.shape[0]
    assert batch == 1 and depth == 1, "kernel supports the stated BATCH=1, DEPTH=1 problem"
    out = _layer(x[0], p[0, 0], positions[0], g_mix[0], w_in[0], a_re[0], a_im[0], log_dt[0],
                 b_re[0], b_im[0], c_re[0], c_im[0], d_skip[0], w_attn_proj[0], w_glu_a[0],
                 w_glu_b[0], w_out[0], g_ffn[0], w_ffn_gate[0], w_ffn_up[0], w_ffn_down[0],
                 w_ple_gate[0], w_ple_proj[0], g_final)
    return out[None]
```

```python
import functools
import math

import jax
import jax.numpy as jnp
from jax import lax
from jax.experimental import pallas as pl
from jax.experimental.pallas import tpu as pltpu

F32 = jnp.float32
BF16 = jnp.bfloat16

D_MODEL = 1024
HEAD_DIM = 128
HEADS_PER_GROUP = 4
GROUP_WIDTH = HEADS_PER_GROUP * HEAD_DIM
ATTN_DILATIONS = (1, 4, 16)
N_GROUPS = len(ATTN_DILATIONS)
QK_WIDTH = N_GROUPS * GROUP_WIDTH
BLOCK = 128
ROPE_THETA = 500000.0
ROPE_DIM = HEAD_DIM // 4
ROPE_HALF = ROPE_DIM // 2
SSM_WIDTH = 512
SSM_GROUP = 16
SSM_GROUPS = SSM_WIDTH // SSM_GROUP
SSM_STATE = 64
N_STATES = SSM_GROUPS * SSM_STATE
SSM_HALVES = 2
HALF_GROUPS = SSM_GROUPS // SSM_HALVES
HALF_WIDTH = SSM_WIDTH // SSM_HALVES
HALF_STATES = N_STATES // SSM_HALVES
STATE_BLOCK = 256
D_FF = 2816
PLE_DIM = 256
EPS = 1e-6
MASK_VALUE = -1e30

V7X_VMEM_LIMIT_BYTES = 56 * 1024 * 1024

PROJ_ROWS = 512
PROJ_SUB_ROWS = 256
ATTN_TOKENS = 2048
STAT_LANES = HEAD_DIM // HEADS_PER_GROUP
SCATTER_STRIDE = 4
SSM_CHUNK = 128
SSM_ROWS = 512
PREP_STEPS = 8
MIX_ROWS = 1024
MIX_SUB_ROWS = 256
FFN_ROWS = 1024
FFN_SUB_ROWS = 256
FFN_CHUNKS = ((0, 1024), (1024, 1024), (2048, 768))


def _resident(shape):
    return pl.BlockSpec(shape, lambda *_: (0,) * len(shape), pipeline_mode=pl.Buffered(1))


def _params(*semantics):
    return pltpu.CompilerParams(dimension_semantics=semantics,
                                vmem_limit_bytes=V7X_VMEM_LIMIT_BYTES)


def _rmsnorm(x, g):
    return (x * lax.rsqrt(jnp.mean(x * x, axis=-1, keepdims=True) + EPS)) * g


def _bdot(a, b):
    return jnp.dot(a, b, preferred_element_type=F32)


def _cast_plan(weights, steps):
    specs = [pl.BlockSpec((w.shape[0] // steps, w.shape[1]), lambda i: (i, 0)) for w in weights]
    shapes = [jax.ShapeDtypeStruct(w.shape, BF16) for w in weights]
    return specs, shapes


def _cast_blocks(in_refs, out_refs):
    for src, dst in zip(in_refs, out_refs):
        dst[...] = src[...].astype(BF16)


def _proj_kernel(*refs, n_cast):
    x_ref, pos_ref, g_ref, invf_ref, w_ref = refs[:5]
    cast_in, refs = refs[5:5 + n_cast], refs[5 + n_cast:]
    qkv0, qkv1, qkv2, u_ref, gates_ref = refs[:5]
    cast_out, (n_scr, nperm_scr, cos_scr, sin_scr) = refs[5:5 + n_cast], refs[5 + n_cast:]
    _cast_blocks(cast_in, cast_out)
    rows = PROJ_SUB_ROWS
    lane = lax.broadcasted_iota(jnp.int32, (rows, HEAD_DIM), 1)
    first_half = lane < ROPE_HALF
    scale = math.log2(math.e) / math.sqrt(HEAD_DIM)

    for sb in range(x_ref.shape[0] // rows):
        base = sb * rows
        tile_rows = slice(base, base + rows)
        xn = _rmsnorm(x_ref[tile_rows, :], g_ref[...])
        n = xn.astype(BF16)
        for c in range(D_MODEL // HEAD_DIM):
            n_scr[c, tile_rows, :] = xn[:, c * HEAD_DIM:(c + 1) * HEAD_DIM]

        ang = invf_ref[...] * pos_ref[:, tile_rows].astype(F32)
        cos_t = jnp.cos(ang)
        sin_t = jnp.sin(ang)
        rest = (HEAD_DIM - ROPE_DIM, rows)
        cos_scr[tile_rows, :] = jnp.concatenate([cos_t, cos_t, jnp.ones(rest, F32)], axis=0).T
        sin_scr[tile_rows, :] = jnp.concatenate([-sin_t, sin_t, jnp.zeros(rest, F32)], axis=0).T

        for g, (d, qkv_ref) in enumerate(zip(ATTN_DILATIONS, (qkv0, qkv1, qkv2))):
            def residue_major(ref_2d):
                if d == 1:
                    return ref_2d[tile_rows, :]
                return jnp.concatenate(
                    [ref_2d[pl.ds(base + r, rows // d, stride=d), :] for r in range(d)], axis=0)

            cos = residue_major(cos_scr)
            sin_signed = residue_major(sin_scr)
            if d == 1:
                ng = n
            else:
                for c in range(D_MODEL // HEAD_DIM):
                    nperm_scr[tile_rows, c * HEAD_DIM:(c + 1) * HEAD_DIM] = (
                        residue_major(n_scr.at[c]).astype(BF16))
                ng = nperm_scr[tile_rows, :]

            def rotary(t):
                partner = jnp.where(first_half,
                                    pltpu.roll(t, HEAD_DIM - ROPE_HALF, 1),
                                    pltpu.roll(t, ROPE_HALF, 1))
                return t * cos + partner * sin_signed

            c0 = g * GROUP_WIDTH
            zq = _bdot(ng, w_ref[:, c0:c0 + GROUP_WIDTH])
            zk = _bdot(ng, w_ref[:, QK_WIDTH + c0:QK_WIDTH + c0 + GROUP_WIDTH])
            zv = _bdot(ng, w_ref[:, 2 * QK_WIDTH + c0:2 * QK_WIDTH + c0 + GROUP_WIDTH])
            sub_rows = slice(base // d, (base + rows) // d)
            for h in range(HEADS_PER_GROUP):
                hs = slice(h * HEAD_DIM, (h + 1) * HEAD_DIM)
                ks = slice(GROUP_WIDTH + h * HEAD_DIM, GROUP_WIDTH + (h + 1) * HEAD_DIM)
                qkv_ref[:, sub_rows, hs] = (
                    (rotary(zq[:, hs]) * scale).astype(BF16).reshape(d, rows // d, HEAD_DIM))
                qkv_ref[:, sub_rows, ks] = rotary(zk[:, hs]).astype(BF16).reshape(d, rows // d, HEAD_DIM)
            qkv_ref[:, sub_rows, 2 * GROUP_WIDTH:] = zv.astype(BF16).reshape(d, rows // d, GROUP_WIDTH)
        o1 = 3 * QK_WIDTH
        u_ref[tile_rows, :] = _bdot(n, w_ref[:, o1:o1 + SSM_WIDTH])
        o2 = o1 + SSM_WIDTH
        gates_ref[tile_rows, :] = jax.nn.sigmoid(_bdot(n, w_ref[:, o2:o2 + 2 * D_MODEL])).astype(BF16)


def _proj(x, pos_row, g_mix, inv_freq_col, w_in, later_weights):
    s = x.shape[0]
    tm = min(PROJ_ROWS, s)
    row = lambda w: pl.BlockSpec((tm, w), lambda i: (i, 0))
    cast_specs, cast_shapes = _cast_plan(later_weights, s // tm)
    grp_specs, grp_shapes = [], []
    for d in ATTN_DILATIONS:
        grp_specs.append(pl.BlockSpec((d, tm // d, 3 * GROUP_WIDTH), lambda i: (0, i, 0)))
        grp_shapes.append(jax.ShapeDtypeStruct((d, s // d, 3 * GROUP_WIDTH), BF16))
    return pl.pallas_call(
        functools.partial(_proj_kernel, n_cast=len(later_weights)),
        grid=(s // tm,),
        in_specs=[row(D_MODEL), pl.BlockSpec((1, tm), lambda i: (0, i)), _resident((1, D_MODEL)),
                  _resident((ROPE_HALF, 1)), _resident(w_in.shape)] + cast_specs,
        out_specs=grp_specs + [row(SSM_WIDTH), row(2 * D_MODEL)] + cast_specs,
        out_shape=grp_shapes + [jax.ShapeDtypeStruct((s, SSM_WIDTH), F32),
                                jax.ShapeDtypeStruct((s, 2 * D_MODEL), BF16)] + cast_shapes,
        scratch_shapes=[pltpu.VMEM((D_MODEL // HEAD_DIM, tm, HEAD_DIM), F32), pltpu.VMEM((tm, D_MODEL), BF16),
                        pltpu.VMEM((tm, HEAD_DIM), F32), pltpu.VMEM((tm, HEAD_DIM), F32)],
        compiler_params=_params("parallel"),
        name="proj",
    )(x, pos_row, g_mix, inv_freq_col, w_in, *later_weights)


def _attn_kernel(qkv_ref, o_ref, l_ref, prev_scr, *scatter_scr, d, nsub):
    not_first_block = pl.program_id(0) > 0
    two_pass = d > SCATTER_STRIDE
    if two_pass:
        o_scr, l_scr, part_o, part_l = scatter_scr
    elif d > 1:
        o_scr, l_scr = scatter_scr

    @pl.when(pl.program_id(0) == 0)
    def _():
        prev_scr[...] = jnp.zeros_like(prev_scr)

    qi = lax.broadcasted_iota(jnp.int32, (BLOCK, 2 * BLOCK), 0)
    kj = lax.broadcasted_iota(jnp.int32, (BLOCK, 2 * BLOCK), 1)
    rel = BLOCK + qi - kj
    band = (rel >= 0) & (rel <= BLOCK)
    band_first = band & ((kj >= BLOCK) | not_first_block)
    lane = lax.broadcasted_iota(jnp.int32, (BLOCK, HEAD_DIM), 1)

    def residue(r):
        for b in range(nsub):
            rows = slice(b * BLOCK, (b + 1) * BLOCK)
            mask = band_first if b == 0 else band
            token_rows = pl.ds(b * BLOCK * d + r, BLOCK, stride=d) if d > 1 else rows
            if two_pass:
                r_hi, r_lo = divmod(r, SCATTER_STRIDE)
                part_rows = pl.ds(b * BLOCK * (d // SCATTER_STRIDE) + r_hi, BLOCK, stride=d // SCATTER_STRIDE)
            for h in range(HEADS_PER_GROUP):
                hs = slice(h * HEAD_DIM, (h + 1) * HEAD_DIM)
                ks = slice(GROUP_WIDTH + h * HEAD_DIM, GROUP_WIDTH + (h + 1) * HEAD_DIM)
                vs = slice(2 * GROUP_WIDTH + h * HEAD_DIM, 2 * GROUP_WIDTH + (h + 1) * HEAD_DIM)
                q = qkv_ref[r, rows, hs]
                if b == 0:
                    prev_k = slice(h * HEAD_DIM, (h + 1) * HEAD_DIM)
                    prev_v = slice(GROUP_WIDTH + h * HEAD_DIM, GROUP_WIDTH + (h + 1) * HEAD_DIM)
                    kw = jnp.concatenate([prev_scr[r, :, prev_k], qkv_ref[r, rows, ks]], axis=0)
                    vw = jnp.concatenate([prev_scr[r, :, prev_v], qkv_ref[r, rows, vs]], axis=0)
                else:
                    win = slice((b - 1) * BLOCK, (b + 1) * BLOCK)
                    kw = qkv_ref[r, win, ks]
                    vw = qkv_ref[r, win, vs]
                s = lax.dot_general(q, kw, (((1,), (1,)), ((), ())), preferred_element_type=F32)
                s = jnp.where(mask, s, MASK_VALUE)
                m = jnp.max(s, axis=-1, keepdims=True)
                p = jnp.exp2(s - m)
                den = jnp.sum(p, axis=-1, keepdims=True)
                pv = _bdot(p.astype(BF16), vw)
                if two_pass:
                    part_o[r_lo, h, part_rows, :] = pv
                elif d > 1:
                    o_scr[h, token_rows, :] = pv
                else:
                    o_ref[rows, hs] = pv.astype(BF16)
                m_b = jnp.broadcast_to(m, (BLOCK, HEAD_DIM))
                den_b = jnp.broadcast_to(den, (BLOCK, HEAD_DIM))
                if h > 0:
                    m_b = jnp.where(lane >= h * STAT_LANES, m_b, stats)
                stats = jnp.where(lane >= h * STAT_LANES + STAT_LANES // 2, den_b, m_b)
            if two_pass:
                part_l[r_lo, part_rows, :] = stats
            elif d > 1:
                l_scr[token_rows, :] = stats
            else:
                l_ref[rows, :] = stats

    for r in range(d):
        residue(r)

    if two_pass:
        part_len = part_l.shape[1]
        for r_lo in range(SCATTER_STRIDE):
            token_rows = pl.ds(r_lo, part_len, stride=SCATTER_STRIDE)
            for h in range(HEADS_PER_GROUP):
                o_scr[h, token_rows, :] = part_o[r_lo, h]
            l_scr[token_rows, :] = part_l[r_lo]
    if d > 1:
        for h in range(HEADS_PER_GROUP):
            o_ref[:, h * HEAD_DIM:(h + 1) * HEAD_DIM] = o_scr[h].astype(BF16)
        l_ref[...] = l_scr[...]
    prev_scr[...] = qkv_ref[:, (nsub - 1) * BLOCK:nsub * BLOCK, GROUP_WIDTH:]


def _attn_group(qkv, dilation):
    sub_len = qkv.shape[1]
    s = sub_len * dilation
    step_tokens = min(ATTN_TOKENS, s)
    qb = step_tokens // dilation
    nsub = qb // BLOCK
    cur = pl.BlockSpec((dilation, qb, 3 * GROUP_WIDTH), lambda i: (0, i, 0))
    scatter_scratch = [] if dilation == 1 else [
        pltpu.VMEM((HEADS_PER_GROUP, step_tokens, HEAD_DIM), F32), pltpu.VMEM((step_tokens, HEAD_DIM), F32)]
    if dilation > SCATTER_STRIDE:
        part = step_tokens // SCATTER_STRIDE
        scatter_scratch += [pltpu.VMEM((SCATTER_STRIDE, HEADS_PER_GROUP, part, HEAD_DIM), F32),
                            pltpu.VMEM((SCATTER_STRIDE, part, HEAD_DIM), F32)]
    return pl.pallas_call(
        functools.partial(_attn_kernel, d=dilation, nsub=nsub),
        grid=(s // step_tokens,),
        in_specs=[cur],
        out_specs=[pl.BlockSpec((step_tokens, GROUP_WIDTH), lambda i: (i, 0)),
                   pl.BlockSpec((step_tokens, HEAD_DIM), lambda i: (i, 0))],
        out_shape=[jax.ShapeDtypeStruct((s, GROUP_WIDTH), BF16), jax.ShapeDtypeStruct((s, HEAD_DIM), F32)],
        scratch_shapes=[pltpu.VMEM((dilation, BLOCK, 2 * GROUP_WIDTH), BF16)] + scatter_scratch,
        compiler_params=_params("arbitrary"),
        name=f"attn_d{dilation}",
    )(qkv)


def _discretize(lr, li, log_dt):
    dt = jnp.exp(log_dt)
    mag = jnp.exp(lr * dt)
    bar_re = mag * jnp.cos(li * dt)
    bar_im = mag * jnp.sin(li * dt)
    nr = bar_re - 1.0
    ni = bar_im
    den = lr * lr + li * li
    return bar_re, bar_im, (nr * lr + ni * li) / den, (ni * lr - nr * li) / den


def _expand_block_diag(compact, n_blocks):
    rows, b = compact.shape
    a = rows // n_blocks
    wide = n_blocks * b
    src_lane = lax.broadcasted_iota(jnp.int32, (b, wide), 0)
    dst_lane = lax.broadcasted_iota(jnp.int32, (b, wide), 1)
    tiled = _bdot(compact, (dst_lane % b == src_lane).astype(BF16))
    row_block = lax.broadcasted_iota(jnp.int32, (rows, wide), 0) // a
    col_block = lax.broadcasted_iota(jnp.int32, (rows, wide), 1) // b
    return jnp.where(row_block == col_block, tiled, 0.0).astype(BF16)


def _ssm_prep_kernel(lr_ref, li_ref, logdt_ref, lr_rep_ref, li_rep_ref, logdt_rep_ref, b_re_ref, b_im_ref,
                     c_re_ref, c_im_ref, w_in_ref,
                     bmat_ref, cre_ref, cim_ref, e_re_ref, e_im_ref, einv_re_ref, einv_im_ref, lam_ref,
                     w_in_bf_ref):
    w_in_bf_ref[...] = w_in_ref[...].astype(BF16)
    lr = lr_ref[...]
    li = li_ref[...]
    dt = jnp.exp(logdt_ref[...])

    @pl.when(pl.program_id(0) == 0)
    def _():
        _, _, z_re, z_im = _discretize(lr_rep_ref[...], li_rep_ref[...], logdt_rep_ref[...])
        b_re = b_re_ref[...]
        b_im = b_im_ref[...]
        bb_re = (z_re * b_re - z_im * b_im).astype(BF16)
        bb_im = (z_re * b_im + z_im * b_re).astype(BF16)
        for hf in range(SSM_HALVES):
            rows = slice(hf * HALF_WIDTH, (hf + 1) * HALF_WIDTH)
            bmat_ref[rows, :HALF_STATES] = _expand_block_diag(bb_re[rows], HALF_GROUPS)
            bmat_ref[rows, HALF_STATES:] = _expand_block_diag(bb_im[rows], HALF_GROUPS)
            srows = slice(hf * HALF_STATES, (hf + 1) * HALF_STATES)
            cre_ref[srows, :] = _expand_block_diag(c_re_ref[srows, :].astype(BF16), HALF_GROUPS)
            cim_ref[srows, :] = _expand_block_diag(c_im_ref[srows, :].astype(BF16), HALF_GROUPS)
        whole = float(SSM_CHUNK)
        chunk_mag = jnp.exp(whole * (lr * dt))
        lam_ref[0:1, :] = chunk_mag * jnp.cos(whole * (li * dt))
        lam_ref[1:2, :] = chunk_mag * jnp.sin(whole * (li * dt))

    rows_per_step = e_re_ref.shape[0]
    t = (lax.broadcasted_iota(jnp.int32, (rows_per_step, N_STATES), 0)
         + pl.program_id(0) * rows_per_step).astype(F32)
    grow = jnp.exp(t * (lr * dt))
    theta = t * (li * dt)
    c = jnp.cos(theta)
    s = jnp.sin(theta)
    e_re_ref[...] = (grow * c).astype(BF16)
    e_im_ref[...] = (grow * s).astype(BF16)
    shrink = jnp.exp(-t * (lr * dt))
    einv_re_ref[...] = (shrink * c).astype(BF16)
    einv_im_ref[...] = (-(shrink * s)).astype(BF16)


def _ssm_prep(a_re, a_im, log_dt, b_re, b_im, c_re, c_im, w_in):
    row = lambda v: v.reshape(1, N_STATES)
    rep = lambda v: jnp.repeat(v, SSM_GROUP, axis=0)
    logdt_gp = jnp.broadcast_to(log_dt[:, None], (SSM_GROUPS, SSM_STATE))
    chan_state = lambda b: jnp.swapaxes(b, 1, 2).reshape(SSM_WIDTH, SSM_STATE)
    state_chan = lambda c: jnp.swapaxes(c, 1, 2).reshape(N_STATES, SSM_GROUP)
    params = (row(a_re), row(a_im), row(logdt_gp), rep(a_re), rep(a_im), rep(logdt_gp),
              chan_state(b_re), chan_state(b_im), state_chan(c_re), state_chan(c_im))
    whole = lambda shape: pl.BlockSpec(shape, lambda i: (0,) * len(shape))
    slab = lambda arr: pl.BlockSpec((arr.shape[0] // PREP_STEPS, arr.shape[1]), lambda i: (i, 0))
    tab = jax.ShapeDtypeStruct((SSM_CHUNK, N_STATES), BF16)
    cmat = jax.ShapeDtypeStruct((N_STATES, HALF_WIDTH), BF16)
    bmat = jax.ShapeDtypeStruct((SSM_WIDTH, 2 * HALF_STATES), BF16)
    lam = jax.ShapeDtypeStruct((2, N_STATES), F32)
    return pl.pallas_call(
        _ssm_prep_kernel,
        grid=(PREP_STEPS,),
        in_specs=[whole(p.shape) for p in params] + [slab(w_in)],
        out_specs=[whole(bmat.shape), whole(cmat.shape), whole(cmat.shape), slab(tab), slab(tab), slab(tab),
                   slab(tab), whole(lam.shape), slab(w_in)],
        out_shape=[bmat, cmat, cmat, tab, tab, tab, tab, lam, jax.ShapeDtypeStruct(w_in.shape, BF16)],
        compiler_params=_params("arbitrary"),
        name="ssm_prep",
    )(*params, w_in)


def _gelu_tanh(x):
    return 0.5 * x * (1.0 + jnp.tanh(math.sqrt(2.0 / math.pi) * (x + 0.044715 * (x * x * x))))


def _ssm_kernel(u_ref, bmat_ref, cre_ref, cim_ref, tri_ref, e_re_ref, e_im_ref, einv_re_ref, einv_im_ref,
                lam_ref, d_ref, y_ref, h_re_scr, h_im_scr, carry_scr):
    @pl.when(pl.program_id(0) == 0)
    def _():
        carry_scr[...] = jnp.zeros_like(carry_scr)

    u = u_ref[...]
    ub = u.astype(BF16)
    for hf in range(SSM_HALVES):
        chans = slice(hf * HALF_WIDTH, (hf + 1) * HALF_WIDTH)
        y = d_ref[:, chans] * u[:, chans]
        for blk in range(HALF_STATES // STATE_BLOCK):
            local = slice(blk * STATE_BLOCK, (blk + 1) * STATE_BLOCK)
            local_im = slice(HALF_STATES + blk * STATE_BLOCK, HALF_STATES + (blk + 1) * STATE_BLOCK)
            states = slice(hf * HALF_STATES + blk * STATE_BLOCK, hf * HALF_STATES + (blk + 1) * STATE_BLOCK)
            bu_re_all = _bdot(ub[:, chans], bmat_ref[chans, local])
            bu_im_all = _bdot(ub[:, chans], bmat_ref[chans, local_im])
            lam_re = lam_ref[0:1, states]
            lam_im = lam_ref[1:2, states]
            carry_re = carry_scr[0:1, states]
            carry_im = carry_scr[1:2, states]
            for c in range(u.shape[0] // SSM_CHUNK):
                rows = slice(c * SSM_CHUNK, (c + 1) * SSM_CHUNK)
                bu_re = bu_re_all[rows].astype(BF16)
                bu_im = bu_im_all[rows].astype(BF16)
                einv_re = einv_re_ref[:, states]
                einv_im = einv_im_ref[:, states]
                x = jnp.concatenate([bu_re * einv_re - bu_im * einv_im,
                                     bu_re * einv_im + bu_im * einv_re], axis=1)
                a = _bdot(tri_ref[...], x)
                a_re = a[:, :STATE_BLOCK] + carry_re
                a_im = a[:, STATE_BLOCK:] + carry_im
                e_re = e_re_ref[:, states]
                e_im = e_im_ref[:, states]
                a_re_b = a_re.astype(BF16)
                a_im_b = a_im.astype(BF16)
                h_re_scr[rows, :] = e_re * a_re_b - e_im * a_im_b
                h_im_scr[rows, :] = e_re * a_im_b + e_im * a_re_b
                last_re = a_re[SSM_CHUNK - 1:SSM_CHUNK, :]
                last_im = a_im[SSM_CHUNK - 1:SSM_CHUNK, :]
                carry_re = lam_re * last_re - lam_im * last_im
                carry_im = lam_re * last_im + lam_im * last_re
            carry_scr[0:1, states] = carry_re
            carry_scr[1:2, states] = carry_im
            y = y + _bdot(h_re_scr[...], cre_ref[states, :]) - _bdot(h_im_scr[...], cim_ref[states, :])
        y_ref[:, chans] = _gelu_tanh(y).astype(BF16)


def _ssm(u, bmat, cre, cim, tri, e_re, e_im, einv_re, einv_im, lam, d_row):
    s = u.shape[0]
    tm = min(SSM_ROWS, s)
    row = pl.BlockSpec((tm, SSM_WIDTH), lambda i: (i, 0))
    consts = (bmat, cre, cim, tri, e_re, e_im, einv_re, einv_im, lam, d_row)
    return pl.pallas_call(
        _ssm_kernel,
        grid=(s // tm,),
        in_specs=[row] + [_resident(c.shape) for c in consts],
        out_specs=row,
        out_shape=jax.ShapeDtypeStruct((s, SSM_WIDTH), BF16),
        scratch_shapes=[pltpu.VMEM((tm, STATE_BLOCK), BF16), pltpu.VMEM((tm, STATE_BLOCK), BF16),
                        pltpu.VMEM((2, N_STATES), F32)],
        compiler_params=_params("arbitrary"),
        name="ssm",
    )(u, *consts)


def _mix_kernel(*refs, n_cast):
    x_ref, o0_ref, o1_ref, o2_ref, l0_ref, l1_ref, l2_ref, y_ref, gates_ref = refs[:9]
    wap_ref, wa_ref, wb_ref, wout_ref = refs[9:13]
    cast_in, h_ref, cast_out = refs[13:13 + n_cast], refs[13 + n_cast], refs[14 + n_cast:]
    _cast_blocks(cast_in, cast_out)
    for rb in range(x_ref.shape[0] // MIX_SUB_ROWS):
        rows = slice(rb * MIX_SUB_ROWS, (rb + 1) * MIX_SUB_ROWS)
        ls = (l0_ref[rows, :], l1_ref[rows, :], l2_ref[rows, :])
        m_max = jnp.maximum(jnp.maximum(ls[0], ls[1]), ls[2])
        es = [jnp.exp2(l - m_max) for l in ls]
        dens = [pltpu.roll(l, HEAD_DIM - STAT_LANES // 2, 1) for l in ls]
        inv = 1.0 / (es[0] * dens[0] + es[1] * dens[1] + es[2] * dens[2])
        heads = []
        for h in range(HEADS_PER_GROUP):
            hs = slice(h * HEAD_DIM, (h + 1) * HEAD_DIM)
            col = slice(h * STAT_LANES, h * STAT_LANES + 1)
            heads.append(sum((e * inv)[:, col] * o_ref[rows, hs].astype(F32)
                             for e, o_ref in zip(es, (o0_ref, o1_ref, o2_ref))))
        attn = jnp.concatenate(heads, axis=1).astype(BF16)
        attn_d = _bdot(attn, wap_ref[...])
        y = y_ref[rows, :]
        ssm_d = _bdot(y, wa_ref[...]) * jax.nn.sigmoid(_bdot(y, wb_ref[...]))
        mix = (gates_ref[rows, :D_MODEL].astype(F32) * attn_d
               + gates_ref[rows, D_MODEL:].astype(F32) * ssm_d)
        h_ref[rows, :] = x_ref[rows, :] + _bdot(mix.astype(BF16), wout_ref[...])


def _mix(x, os, ls, y, gates, wap, wa, wb, wout, later_weights):
    s = x.shape[0]
    tm = min(MIX_ROWS, s)
    row = lambda w: pl.BlockSpec((tm, w), lambda i: (i, 0))
    weights = (wap, wa, wb, wout)
    cast_specs, cast_shapes = _cast_plan(later_weights, s // tm)
    return pl.pallas_call(
        functools.partial(_mix_kernel, n_cast=len(later_weights)),
        grid=(s // tm,),
        in_specs=[row(D_MODEL)] + [row(GROUP_WIDTH)] * 3 + [row(HEAD_DIM)] * 3
        + [row(SSM_WIDTH), row(2 * D_MODEL)] + [_resident(w.shape) for w in weights] + cast_specs,
        out_specs=[row(D_MODEL)] + cast_specs,
        out_shape=[jax.ShapeDtypeStruct((s, D_MODEL), F32)] + cast_shapes,
        compiler_params=_params("parallel"),
        name="mix",
    )(x, *os, *ls, y, gates, *weights, *later_weights)


def _ffn_kernel(h_ref, p_ref, gffn_ref, wg_ref, wu_ref, wd_ref, wpg_ref, wpp_ref, gfin_ref,
                out_ref, acc_scr):
    for rb in range(h_ref.shape[0] // FFN_SUB_ROWS):
        rows = slice(rb * FFN_SUB_ROWS, (rb + 1) * FFN_SUB_ROWS)
        h = h_ref[rows, :]
        n2 = _rmsnorm(h, gffn_ref[...]).astype(BF16)
        for idx, (c0, width) in enumerate(FFN_CHUNKS):
            gate = _bdot(n2, wg_ref[:, c0:c0 + width])
            up = _bdot(n2, wu_ref[:, c0:c0 + width])
            act = (gate * jax.nn.sigmoid(gate) * up).astype(BF16)
            part = _bdot(act, wd_ref[c0:c0 + width, :])
            if idx == 0:
                acc_scr[rows, :] = h + part
            else:
                acc_scr[rows, :] += part
        h2 = acc_scr[rows, :]
        ple = (jax.nn.sigmoid(_bdot(h2.astype(BF16), wpg_ref[...]))
               * _bdot(p_ref[rows, :].astype(BF16), wpp_ref[...]))
        out_ref[rows, :] = _rmsnorm(h2 + ple, gfin_ref[...])


def _ffn(h, p, g_ffn, wg, wu, wd, wpg, wpp, g_final):
    s = h.shape[0]
    tm = min(FFN_ROWS, s)
    row = lambda w: pl.BlockSpec((tm, w), lambda i: (i, 0))
    consts = (g_ffn, wg, wu, wd, wpg, wpp, g_final)
    return pl.pallas_call(
        _ffn_kernel,
        grid=(s // tm,),
        in_specs=[row(D_MODEL), row(PLE_DIM)] + [_resident(c.shape) for c in consts],
        out_specs=row(D_MODEL),
        out_shape=jax.ShapeDtypeStruct((s, D_MODEL), F32),
        scratch_shapes=[pltpu.VMEM((tm, D_MODEL), F32)],
        compiler_params=_params("parallel"),
        name="ffn",
    )(h, p, *consts)


def _layer(x, p, positions, g_mix, w_in, a_re, a_im, log_dt, b_re, b_im, c_re, c_im, d_skip,
           w_attn_proj, w_glu_a, w_glu_b, w_out, g_ffn, w_ffn_gate, w_ffn_up, w_ffn_down,
           w_ple_gate, w_ple_proj, g_final):
    row = lambda v: v.reshape(1, -1).astype(F32)

    inv_freq = ROPE_THETA ** (-jnp.arange(ROPE_HALF, dtype=F32) * 2.0 / ROPE_DIM)
    bmat, cre, cim, e_re, e_im, einv_re, einv_im, lam, w_in_bf = _ssm_prep(
        a_re, a_im, log_dt, b_re, b_im, c_re, c_im, w_in)
    outs = _proj(x, positions.reshape(1, -1), row(g_mix), inv_freq.reshape(ROPE_HALF, 1), w_in_bf,
                 (w_attn_proj, w_glu_a, w_glu_b, w_out, w_ffn_gate, w_ffn_up, w_ple_gate))
    qkvs, (u, gates) = outs[:N_GROUPS], outs[N_GROUPS:N_GROUPS + 2]
    wap, wga, wgb, wout, wfg, wfu, wpg = outs[N_GROUPS + 2:]

    attn_os, attn_ls = zip(*[_attn_group(qkvs[g], d) for g, d in enumerate(ATTN_DILATIONS)])

    tri = jnp.tril(jnp.ones((SSM_CHUNK, SSM_CHUNK), F32)).astype(BF16)
    y = _ssm(u, bmat, cre, cim, tri, e_re, e_im, einv_re, einv_im, lam, row(d_skip))

    h1, wfd, wpp = _mix(x, attn_os, attn_ls, y, gates, wap, wga, wgb, wout, (w_ffn_down, w_ple_proj))
    return _ffn(h1, p, row(g_ffn), wfg, wfu, wfd, wpg, wpp, row(g_final))


def kernel(x, p, positions, g_mix, w_in, a_re, a_im, log_dt, b_re, b_im, c_re, c_im, d_skip,
           w_attn_proj, w_glu_a, w_glu_b, w_out, g_ffn, w_ffn_gate, w_ffn_up, w_ffn_down,
           w_ple_gate, w_ple_proj, g_final):
    batch, depth = x.shape[0], p.shape[0]
    assert batch == 1 and depth == 1, "kernel supports the stated BATCH=1, DEPTH=1 problem"
    out = _layer(x[0], p[0, 0], positions[0], g_mix[0], w_in[0], a_re[0], a_im[0], log_dt[0],
                 b_re[0], b_im[0], c_re[0], c_im[0], d_skip[0], w_attn_proj[0], w_glu_a[0],
                 w_glu_b[0], w_out[0], g_ffn[0], w_ffn_gate[0], w_ffn_up[0], w_ffn_down[0],
                 w_ple_gate[0], w_ple_proj[0], g_final)
    return out[None]
```

```python
import functools
import math

import jax
import jax.numpy as jnp
from jax import lax
from jax.experimental import pallas as pl
from jax.experimental.pallas import tpu as pltpu

F32 = jnp.float32
BF16 = jnp.bfloat16

D_MODEL = 1024
HEAD_DIM = 128
HEADS_PER_GROUP = 4
GROUP_WIDTH = HEADS_PER_GROUP * HEAD_DIM
ATTN_DILATIONS = (1, 4, 16)
N_GROUPS = len(ATTN_DILATIONS)
QK_WIDTH = N_GROUPS * GROUP_WIDTH
BLOCK = 128
ROPE_THETA = 500000.0
ROPE_DIM = HEAD_DIM // 4
ROPE_HALF = ROPE_DIM // 2
SSM_WIDTH = 512
SSM_GROUP = 16
SSM_GROUPS = SSM_WIDTH // SSM_GROUP
SSM_STATE = 64
N_STATES = SSM_GROUPS * SSM_STATE
SSM_HALVES = 2
HALF_GROUPS = SSM_GROUPS // SSM_HALVES
HALF_WIDTH = SSM_WIDTH // SSM_HALVES
HALF_STATES = N_STATES // SSM_HALVES
STATE_BLOCK = 256
D_FF = 2816
PLE_DIM = 256
EPS = 1e-6
MASK_VALUE = -1e30

V7X_VMEM_LIMIT_BYTES = 56 * 1024 * 1024

PROJ_ROWS = 512
PROJ_SUB_ROWS = 256
ATTN_TOKENS = 2048
STAT_LANES = HEAD_DIM // HEADS_PER_GROUP
SCATTER_STRIDE = 4
SSM_CHUNK = 128
SSM_ROWS = 512
PREP_STEPS = 8
MIX_ROWS = 1024
MIX_SUB_ROWS = 256
FFN_ROWS = 1024
FFN_SUB_ROWS = 256
FFN_CHUNKS = ((0, 1024), (1024, 1024), (2048, 768))


def _resident(shape):
    return pl.BlockSpec(shape, lambda *_: (0,) * len(shape), pipeline_mode=pl.Buffered(1))


def _params(*semantics):
    return pltpu.CompilerParams(dimension_semantics=semantics,
                                vmem_limit_bytes=V7X_VMEM_LIMIT_BYTES)


def _rmsnorm(x, g):
    return (x * lax.rsqrt(jnp.mean(x * x, axis=-1, keepdims=True) + EPS)) * g


def _bdot(a, b):
    return jnp.dot(a, b, preferred_element_type=F32)


def _cast_plan(weights, steps):
    specs = [pl.BlockSpec((w.shape[0] // steps, w.shape[1]), lambda i: (i, 0)) for w in weights]
    shapes = [jax.ShapeDtypeStruct(w.shape, BF16) for w in weights]
    return specs, shapes


def _cast_blocks(in_refs, out_refs):
    for src, dst in zip(in_refs, out_refs):
        dst[...] = src[...].astype(BF16)


def _proj_kernel(*refs, n_cast):
    x_ref, pos_ref, g_ref, invf_ref, w_ref = refs[:5]
    cast_in, refs = refs[5:5 + n_cast], refs[5 + n_cast:]
    qkv0, qkv1, qkv2, u_ref, gates_ref = refs[:5]
    cast_out, (n_scr, nperm_scr, cos_scr, sin_scr) = refs[5:5 + n_cast], refs[5 + n_cast:]
    _cast_blocks(cast_in, cast_out)
    rows = PROJ_SUB_ROWS
    lane = lax.broadcasted_iota(jnp.int32, (rows, HEAD_DIM), 1)
    first_half = lane < ROPE_HALF
    scale = math.log2(math.e) / math.sqrt(HEAD_DIM)

    for sb in range(x_ref.shape[0] // rows):
        base = sb * rows
        tile_rows = slice(base, base + rows)
        xn = _rmsnorm(x_ref[tile_rows, :], g_ref[...])
        n = xn.astype(BF16)
        for c in range(D_MODEL // HEAD_DIM):
            n_scr[c, tile_rows, :] = xn[:, c * HEAD_DIM:(c + 1) * HEAD_DIM]

        ang = invf_ref[...] * pos_ref[:, tile_rows].astype(F32)
        cos_t = jnp.cos(ang)
        sin_t = jnp.sin(ang)
        rest = (HEAD_DIM - ROPE_DIM, rows)
        cos_scr[tile_rows, :] = jnp.concatenate([cos_t, cos_t, jnp.ones(rest, F32)], axis=0).T
        sin_scr[tile_rows, :] = jnp.concatenate([-sin_t, sin_t, jnp.zeros(rest, F32)], axis=0).T

        for g, (d, qkv_ref) in enumerate(zip(ATTN_DILATIONS, (qkv0, qkv1, qkv2))):
            def residue_major(ref_2d):
                if d == 1:
                    return ref_2d[tile_rows, :]
                return jnp.concatenate(
                    [ref_2d[pl.ds(base + r, rows // d, stride=d), :] for r in range(d)], axis=0)

            cos = residue_major(cos_scr)
            sin_signed = residue_major(sin_scr)
            if d == 1:
                ng = n
            else:
                for c in range(D_MODEL // HEAD_DIM):
                    nperm_scr[tile_rows, c * HEAD_DIM:(c + 1) * HEAD_DIM] = (
                        residue_major(n_scr.at[c]).astype(BF16))
                ng = nperm_scr[tile_rows, :]

            def rotary(t):
                partner = jnp.where(first_half,
                                    pltpu.roll(t, HEAD_DIM - ROPE_HALF, 1),
                                    pltpu.roll(t, ROPE_HALF, 1))
                return t * cos + partner * sin_signed

            c0 = g * GROUP_WIDTH
            zq = _bdot(ng, w_ref[:, c0:c0 + GROUP_WIDTH])
            zk = _bdot(ng, w_ref[:, QK_WIDTH + c0:QK_WIDTH + c0 + GROUP_WIDTH])
            zv = _bdot(ng, w_ref[:, 2 * QK_WIDTH + c0:2 * QK_WIDTH + c0 + GROUP_WIDTH])
            sub_rows = slice(base // d, (base + rows) // d)
            for h in range(HEADS_PER_GROUP):
                hs = slice(h * HEAD_DIM, (h + 1) * HEAD_DIM)
                ks = slice(GROUP_WIDTH + h * HEAD_DIM, GROUP_WIDTH + (h + 1) * HEAD_DIM)
                qkv_ref[:, sub_rows, hs] = (
                    (rotary(zq[:, hs]) * scale).astype(BF16).reshape(d, rows // d, HEAD_DIM))
                qkv_ref[:, sub_rows, ks] = rotary(zk[:, hs]).astype(BF16).reshape(d, rows // d, HEAD_DIM)
            qkv_ref[:, sub_rows, 2 * GROUP_WIDTH:] = zv.astype(BF16).reshape(d, rows // d, GROUP_WIDTH)
        o1 = 3 * QK_WIDTH
        u_ref[tile_rows, :] = _bdot(n, w_ref[:, o1:o1 + SSM_WIDTH])
        o2 = o1 + SSM_WIDTH
        gates_ref[tile_rows, :] = jax.nn.sigmoid(_bdot(n, w_ref[:, o2:o2 + 2 * D_MODEL])).astype(BF16)


def _proj(x, pos_row, g_mix, inv_freq_col, w_in, later_weights):
    s = x.shape[0]
    tm = min(PROJ_ROWS, s)
    row = lambda w: pl.BlockSpec((tm, w), lambda i: (i, 0))
    cast_specs, cast_shapes = _cast_plan(later_weights, s // tm)
    grp_specs, grp_shapes = [], []
    for d in ATTN_DILATIONS:
        grp_specs.append(pl.BlockSpec((d, tm // d, 3 * GROUP_WIDTH), lambda i: (0, i, 0)))
        grp_shapes.append(jax.ShapeDtypeStruct((d, s // d, 3 * GROUP_WIDTH), BF16))
    return pl.pallas_call(
        functools.partial(_proj_kernel, n_cast=len(later_weights)),
        grid=(s // tm,),
        in_specs=[row(D_MODEL), pl.BlockSpec((1, tm), lambda i: (0, i)), _resident((1, D_MODEL)),
                  _resident((ROPE_HALF, 1)), _resident(w_in.shape)] + cast_specs,
        out_specs=grp_specs + [row(SSM_WIDTH), row(2 * D_MODEL)] + cast_specs,
        out_shape=grp_shapes + [jax.ShapeDtypeStruct((s, SSM_WIDTH), F32),
                                jax.ShapeDtypeStruct((s, 2 * D_MODEL), BF16)] + cast_shapes,
        scratch_shapes=[pltpu.VMEM((D_MODEL // HEAD_DIM, tm, HEAD_DIM), F32), pltpu.VMEM((tm, D_MODEL), BF16),
                        pltpu.VMEM((tm, HEAD_DIM), F32), pltpu.VMEM((tm, HEAD_DIM), F32)],
        compiler_params=_params("parallel"),
        name="proj",
    )(x, pos_row, g_mix, inv_freq_col, w_in, *later_weights)


def _attn_kernel(qkv_ref, prev_k_ref, prev_v_ref, o_ref, l_ref, *scatter_scr, d, nsub):
    not_first_block = pl.program_id(0) > 0
    two_pass = d > SCATTER_STRIDE
    if two_pass:
        o_scr, l_scr, part_o, part_l = scatter_scr
    elif d > 1:
        o_scr, l_scr = scatter_scr

    qi = lax.broadcasted_iota(jnp.int32, (BLOCK, 2 * BLOCK), 0)
    kj = lax.broadcasted_iota(jnp.int32, (BLOCK, 2 * BLOCK), 1)
    rel = BLOCK + qi - kj
    band = (rel >= 0) & (rel <= BLOCK)
    band_first = band & ((kj >= BLOCK) | not_first_block)
    lane = lax.broadcasted_iota(jnp.int32, (BLOCK, HEAD_DIM), 1)

    def residue(r):
        for b in range(nsub):
            rows = slice(b * BLOCK, (b + 1) * BLOCK)
            mask = band_first if b == 0 else band
            token_rows = pl.ds(b * BLOCK * d + r, BLOCK, stride=d) if d > 1 else rows
            if two_pass:
                r_hi, r_lo = divmod(r, SCATTER_STRIDE)
                part_rows = pl.ds(b * BLOCK * (d // SCATTER_STRIDE) + r_hi, BLOCK, stride=d // SCATTER_STRIDE)
            for h in range(HEADS_PER_GROUP):
                hs = slice(h * HEAD_DIM, (h + 1) * HEAD_DIM)
                ks = slice(GROUP_WIDTH + h * HEAD_DIM, GROUP_WIDTH + (h + 1) * HEAD_DIM)
                vs = slice(2 * GROUP_WIDTH + h * HEAD_DIM, 2 * GROUP_WIDTH + (h + 1) * HEAD_DIM)
                q = qkv_ref[r, rows, hs]
                if b == 0:
                    kw = jnp.concatenate([prev_k_ref[r, :, hs], qkv_ref[r, rows, ks]], axis=0)
                    vw = jnp.concatenate([prev_v_ref[r, :, hs], qkv_ref[r, rows, vs]], axis=0)
                else:
                    win = slice((b - 1) * BLOCK, (b + 1) * BLOCK)
                    kw = qkv_ref[r, win, ks]
                    vw = qkv_ref[r, win, vs]
                s = lax.dot_general(q, kw, (((1,), (1,)), ((), ())), preferred_element_type=F32)
                s = jnp.where(mask, s, MASK_VALUE)
                m = jnp.max(s, axis=-1, keepdims=True)
                p = jnp.exp2(s - m)
                den = jnp.sum(p, axis=-1, keepdims=True)
                pv = _bdot(p.astype(BF16), vw)
                if two_pass:
                    part_o[r_lo, h, part_rows, :] = pv
                elif d > 1:
                    o_scr[h, token_rows, :] = pv
                else:
                    o_ref[rows, hs] = pv.astype(BF16)
                m_b = jnp.broadcast_to(m, (BLOCK, HEAD_DIM))
                den_b = jnp.broadcast_to(den, (BLOCK, HEAD_DIM))
                if h > 0:
                    m_b = jnp.where(lane >= h * STAT_LANES, m_b, stats)
                stats = jnp.where(lane >= h * STAT_LANES + STAT_LANES // 2, den_b, m_b)
            if two_pass:
                part_l[r_lo, part_rows, :] = stats
            elif d > 1:
                l_scr[token_rows, :] = stats
            else:
                l_ref[rows, :] = stats

    for r in range(d):
        residue(r)

    if two_pass:
        part_len = part_l.shape[1]
        for r_lo in range(SCATTER_STRIDE):
            token_rows = pl.ds(r_lo, part_len, stride=SCATTER_STRIDE)
            for h in range(HEADS_PER_GROUP):
                o_scr[h, token_rows, :] = part_o[r_lo, h]
            l_scr[token_rows, :] = part_l[r_lo]
    if d > 1:
        for h in range(HEADS_PER_GROUP):
            o_ref[:, h * HEAD_DIM:(h + 1) * HEAD_DIM] = o_scr[h].astype(BF16)
        l_ref[...] = l_scr[...]


def _attn_group(qkv, dilation):
    sub_len = qkv.shape[1]
    s = sub_len * dilation
    step_tokens = min(ATTN_TOKENS, s)
    qb = step_tokens // dilation
    nsub = qb // BLOCK
    cur = pl.BlockSpec((dilation, qb, 3 * GROUP_WIDTH), lambda i: (0, i, 0))
    prev = lambda col: pl.BlockSpec((dilation, BLOCK, GROUP_WIDTH),
                                    lambda i: (0, jnp.maximum(i * nsub - 1, 0), col))
    scatter_scratch = [] if dilation == 1 else [
        pltpu.VMEM((HEADS_PER_GROUP, step_tokens, HEAD_DIM), F32), pltpu.VMEM((step_tokens, HEAD_DIM), F32)]
    if dilation > SCATTER_STRIDE:
        part = step_tokens // SCATTER_STRIDE
        scatter_scratch += [pltpu.VMEM((SCATTER_STRIDE, HEADS_PER_GROUP, part, HEAD_DIM), F32),
                            pltpu.VMEM((SCATTER_STRIDE, part, HEAD_DIM), F32)]
    return pl.pallas_call(
        functools.partial(_attn_kernel, d=dilation, nsub=nsub),
        grid=(s // step_tokens,),
        in_specs=[cur, prev(1), prev(2)],
        out_specs=[pl.BlockSpec((step_tokens, GROUP_WIDTH), lambda i: (i, 0)),
                   pl.BlockSpec((step_tokens, HEAD_DIM), lambda i: (i, 0))],
        out_shape=[jax.ShapeDtypeStruct((s, GROUP_WIDTH), BF16), jax.ShapeDtypeStruct((s, HEAD_DIM), F32)],
        scratch_shapes=scatter_scratch,
        compiler_params=_params("parallel"),
        name=f"attn_d{dilation}",
    )(qkv, qkv, qkv)


def _discretize(lr, li, log_dt):
    dt = jnp.exp(log_dt)
    mag = jnp.exp(lr * dt)
    bar_re = mag * jnp.cos(li * dt)
    bar_im = mag * jnp.sin(li * dt)
    nr = bar_re - 1.0
    ni = bar_im
    den = lr * lr + li * li
    return bar_re, bar_im, (nr * lr + ni * li) / den, (ni * lr - nr * li) / den


def _expand_block_diag(compact, n_blocks):
    rows, b = compact.shape
    a = rows // n_blocks
    wide = n_blocks * b
    src_lane = lax.broadcasted_iota(jnp.int32, (b, wide), 0)
    dst_lane = lax.broadcasted_iota(jnp.int32, (b, wide), 1)
    tiled = _bdot(compact, (dst_lane % b == src_lane).astype(BF16))
    row_block = lax.broadcasted_iota(jnp.int32, (rows, wide), 0) // a
    col_block = lax.broadcasted_iota(jnp.int32, (rows, wide), 1) // b
    return jnp.where(row_block == col_block, tiled, 0.0).astype(BF16)


def _ssm_prep_kernel(lr_ref, li_ref, logdt_ref, lr_rep_ref, li_rep_ref, logdt_rep_ref, b_re_ref, b_im_ref,
                     c_re_ref, c_im_ref, w_in_ref,
                     bmat_ref, cre_ref, cim_ref, e_re_ref, e_im_ref, einv_re_ref, einv_im_ref, lam_ref,
                     w_in_bf_ref):
    w_in_bf_ref[...] = w_in_ref[...].astype(BF16)
    lr = lr_ref[...]
    li = li_ref[...]
    dt = jnp.exp(logdt_ref[...])

    @pl.when(pl.program_id(0) == 0)
    def _():
        _, _, z_re, z_im = _discretize(lr_rep_ref[...], li_rep_ref[...], logdt_rep_ref[...])
        b_re = b_re_ref[...]
        b_im = b_im_ref[...]
        bb_re = (z_re * b_re - z_im * b_im).astype(BF16)
        bb_im = (z_re * b_im + z_im * b_re).astype(BF16)
        for hf in range(SSM_HALVES):
            rows = slice(hf * HALF_WIDTH, (hf + 1) * HALF_WIDTH)
            bmat_ref[rows, :HALF_STATES] = _expand_block_diag(bb_re[rows], HALF_GROUPS)
            bmat_ref[rows, HALF_STATES:] = _expand_block_diag(bb_im[rows], HALF_GROUPS)
            srows = slice(hf * HALF_STATES, (hf + 1) * HALF_STATES)
            cre_ref[srows, :] = _expand_block_diag(c_re_ref[srows, :].astype(BF16), HALF_GROUPS)
            cim_ref[srows, :] = _expand_block_diag(c_im_ref[srows, :].astype(BF16), HALF_GROUPS)
        whole = float(SSM_CHUNK)
        chunk_mag = jnp.exp(whole * (lr * dt))
        lam_ref[0:1, :] = chunk_mag * jnp.cos(whole * (li * dt))
        lam_ref[1:2, :] = chunk_mag * jnp.sin(whole * (li * dt))

    rows_per_step = e_re_ref.shape[0]
    t = (lax.broadcasted_iota(jnp.int32, (rows_per_step, N_STATES), 0)
         + pl.program_id(0) * rows_per_step).astype(F32)
    grow = jnp.exp(t * (lr * dt))
    theta = t * (li * dt)
    c = jnp.cos(theta)
    s = jnp.sin(theta)
    e_re_ref[...] = (grow * c).astype(BF16)
    e_im_ref[...] = (grow * s).astype(BF16)
    shrink = jnp.exp(-t * (lr * dt))
    einv_re_ref[...] = (shrink * c).astype(BF16)
    einv_im_ref[...] = (-(shrink * s)).astype(BF16)


def _ssm_prep(a_re, a_im, log_dt, b_re, b_im, c_re, c_im, w_in):
    row = lambda v: v.reshape(1, N_STATES)
    rep = lambda v: jnp.repeat(v, SSM_GROUP, axis=0)
    logdt_gp = jnp.broadcast_to(log_dt[:, None], (SSM_GROUPS, SSM_STATE))
    chan_state = lambda b: jnp.swapaxes(b, 1, 2).reshape(SSM_WIDTH, SSM_STATE)
    state_chan = lambda c: jnp.swapaxes(c, 1, 2).reshape(N_STATES, SSM_GROUP)
    params = (row(a_re), row(a_im), row(logdt_gp), rep(a_re), rep(a_im), rep(logdt_gp),
              chan_state(b_re), chan_state(b_im), state_chan(c_re), state_chan(c_im))
    whole = lambda shape: pl.BlockSpec(shape, lambda i: (0,) * len(shape))
    slab = lambda arr: pl.BlockSpec((arr.shape[0] // PREP_STEPS, arr.shape[1]), lambda i: (i, 0))
    tab = jax.ShapeDtypeStruct((SSM_CHUNK, N_STATES), BF16)
    cmat = jax.ShapeDtypeStruct((N_STATES, HALF_WIDTH), BF16)
    bmat = jax.ShapeDtypeStruct((SSM_WIDTH, 2 * HALF_STATES), BF16)
    lam = jax.ShapeDtypeStruct((2, N_STATES), F32)
    return pl.pallas_call(
        _ssm_prep_kernel,
        grid=(PREP_STEPS,),
        in_specs=[whole(p.shape) for p in params] + [slab(w_in)],
        out_specs=[whole(bmat.shape), whole(cmat.shape), whole(cmat.shape), slab(tab), slab(tab), slab(tab),
                   slab(tab), whole(lam.shape), slab(w_in)],
        out_shape=[bmat, cmat, cmat, tab, tab, tab, tab, lam, jax.ShapeDtypeStruct(w_in.shape, BF16)],
        compiler_params=_params("arbitrary"),
        name="ssm_prep",
    )(*params, w_in)


def _gelu_tanh(x):
    return 0.5 * x * (1.0 + jnp.tanh(math.sqrt(2.0 / math.pi) * (x + 0.044715 * (x * x * x))))


def _ssm_kernel(u_ref, bmat_ref, cre_ref, cim_ref, tri_ref, e_re_ref, e_im_ref, einv_re_ref, einv_im_ref,
                lam_ref, d_ref, y_ref, h_re_scr, h_im_scr, carry_scr):
    @pl.when(pl.program_id(0) == 0)
    def _():
        carry_scr[...] = jnp.zeros_like(carry_scr)

    u = u_ref[...]
    ub = u.astype(BF16)
    for hf in range(SSM_HALVES):
        chans = slice(hf * HALF_WIDTH, (hf + 1) * HALF_WIDTH)
        y = d_ref[:, chans] * u[:, chans]
        for blk in range(HALF_STATES // STATE_BLOCK):
            local = slice(blk * STATE_BLOCK, (blk + 1) * STATE_BLOCK)
            local_im = slice(HALF_STATES + blk * STATE_BLOCK, HALF_STATES + (blk + 1) * STATE_BLOCK)
            states = slice(hf * HALF_STATES + blk * STATE_BLOCK, hf * HALF_STATES + (blk + 1) * STATE_BLOCK)
            bu_re_all = _bdot(ub[:, chans], bmat_ref[chans, local])
            bu_im_all = _bdot(ub[:, chans], bmat_ref[chans, local_im])
            lam_re = lam_ref[0:1, states]
            lam_im = lam_ref[1:2, states]
            carry_re = carry_scr[0:1, states]
            carry_im = carry_scr[1:2, states]
            for c in range(u.shape[0] // SSM_CHUNK):
                rows = slice(c * SSM_CHUNK, (c + 1) * SSM_CHUNK)
                bu_re = bu_re_all[rows].astype(BF16)
                bu_im = bu_im_all[rows].astype(BF16)
                einv_re = einv_re_ref[:, states]
                einv_im = einv_im_ref[:, states]
                x = jnp.concatenate([bu_re * einv_re - bu_im * einv_im,
                                     bu_re * einv_im + bu_im * einv_re], axis=1)
                a = _bdot(tri_ref[...], x)
                a_re = a[:, :STATE_BLOCK] + carry_re
                a_im = a[:, STATE_BLOCK:] + carry_im
                e_re = e_re_ref[:, states]
                e_im = e_im_ref[:, states]
                a_re_b = a_re.astype(BF16)
                a_im_b = a_im.astype(BF16)
                h_re_scr[rows, :] = e_re * a_re_b - e_im * a_im_b
                h_im_scr[rows, :] = e_re * a_im_b + e_im * a_re_b
                last_re = a_re[SSM_CHUNK - 1:SSM_CHUNK, :]
                last_im = a_im[SSM_CHUNK - 1:SSM_CHUNK, :]
                carry_re = lam_re * last_re - lam_im * last_im
                carry_im = lam_re * last_im + lam_im * last_re
            carry_scr[0:1, states] = carry_re
            carry_scr[1:2, states] = carry_im
            y = y + _bdot(h_re_scr[...], cre_ref[states, :]) - _bdot(h_im_scr[...], cim_ref[states, :])
        y_ref[:, chans] = _gelu_tanh(y).astype(BF16)


def _ssm(u, bmat, cre, cim, tri, e_re, e_im, einv_re, einv_im, lam, d_row):
    s = u.shape[0]
    tm = min(SSM_ROWS, s)
    row = pl.BlockSpec((tm, SSM_WIDTH), lambda i: (i, 0))
    consts = (bmat, cre, cim, tri, e_re, e_im, einv_re, einv_im, lam, d_row)
    return pl.pallas_call(
        _ssm_kernel,
        grid=(s // tm,),
        in_specs=[row] + [_resident(c.shape) for c in consts],
        out_specs=row,
        out_shape=jax.ShapeDtypeStruct((s, SSM_WIDTH), BF16),
        scratch_shapes=[pltpu.VMEM((tm, STATE_BLOCK), BF16), pltpu.VMEM((tm, STATE_BLOCK), BF16),
                        pltpu.VMEM((2, N_STATES), F32)],
        compiler_params=_params("arbitrary"),
        name="ssm",
    )(u, *consts)


def _mix_kernel(*refs, n_cast):
    x_ref, o0_ref, o1_ref, o2_ref, l0_ref, l1_ref, l2_ref, y_ref, gates_ref = refs[:9]
    wap_ref, wa_ref, wb_ref, wout_ref = refs[9:13]
    cast_in, h_ref, cast_out = refs[13:13 + n_cast], refs[13 + n_cast], refs[14 + n_cast:]
    _cast_blocks(cast_in, cast_out)
    for rb in range(x_ref.shape[0] // MIX_SUB_ROWS):
        rows = slice(rb * MIX_SUB_ROWS, (rb + 1) * MIX_SUB_ROWS)
        ls = (l0_ref[rows, :], l1_ref[rows, :], l2_ref[rows, :])
        m_max = jnp.maximum(jnp.maximum(ls[0], ls[1]), ls[2])
        es = [jnp.exp2(l - m_max) for l in ls]
        dens = [pltpu.roll(l, HEAD_DIM - STAT_LANES // 2, 1) for l in ls]
        inv = 1.0 / (es[0] * dens[0] + es[1] * dens[1] + es[2] * dens[2])
        heads = []
        for h in range(HEADS_PER_GROUP):
            hs = slice(h * HEAD_DIM, (h + 1) * HEAD_DIM)
            col = slice(h * STAT_LANES, h * STAT_LANES + 1)
            heads.append(sum((e * inv)[:, col] * o_ref[rows, hs].astype(F32)
                             for e, o_ref in zip(es, (o0_ref, o1_ref, o2_ref))))
        attn = jnp.concatenate(heads, axis=1).astype(BF16)
        attn_d = _bdot(attn, wap_ref[...])
        y = y_ref[rows, :]
        ssm_d = _bdot(y, wa_ref[...]) * jax.nn.sigmoid(_bdot(y, wb_ref[...]))
        mix = (gates_ref[rows, :D_MODEL].astype(F32) * attn_d
               + gates_ref[rows, D_MODEL:].astype(F32) * ssm_d)
        h_ref[rows, :] = x_ref[rows, :] + _bdot(mix.astype(BF16), wout_ref[...])


def _mix(x, os, ls, y, gates, wap, wa, wb, wout, later_weights):
    s = x.shape[0]
    tm = min(MIX_ROWS, s)
    row = lambda w: pl.BlockSpec((tm, w), lambda i: (i, 0))
    weights = (wap, wa, wb, wout)
    cast_specs, cast_shapes = _cast_plan(later_weights, s // tm)
    return pl.pallas_call(
        functools.partial(_mix_kernel, n_cast=len(later_weights)),
        grid=(s // tm,),
        in_specs=[row(D_MODEL)] + [row(GROUP_WIDTH)] * 3 + [row(HEAD_DIM)] * 3
        + [row(SSM_WIDTH), row(2 * D_MODEL)] + [_resident(w.shape) for w in weights] + cast_specs,
        out_specs=[row(D_MODEL)] + cast_specs,
        out_shape=[jax.ShapeDtypeStruct((s, D_MODEL), F32)] + cast_shapes,
        compiler_params=_params("parallel"),
        name="mix",
    )(x, *os, *ls, y, gates, *weights, *later_weights)


def _ffn_kernel(h_ref, p_ref, gffn_ref, wg_ref, wu_ref, wd_ref, wpg_ref, wpp_ref, gfin_ref,
                out_ref, acc_scr):
    for rb in range(h_ref.shape[0] // FFN_SUB_ROWS):
        rows = slice(rb * FFN_SUB_ROWS, (rb + 1) * FFN_SUB_ROWS)
        h = h_ref[rows, :]
        n2 = _rmsnorm(h, gffn_ref[...]).astype(BF16)
        for idx, (c0, width) in enumerate(FFN_CHUNKS):
            gate = _bdot(n2, wg_ref[:, c0:c0 + width])
            up = _bdot(n2, wu_ref[:, c0:c0 + width])
            act = (gate * jax.nn.sigmoid(gate) * up).astype(BF16)
            part = _bdot(act, wd_ref[c0:c0 + width, :])
            if idx == 0:
                acc_scr[rows, :] = h + part
            else:
                acc_scr[rows, :] += part
        h2 = acc_scr[rows, :]
        ple = (jax.nn.sigmoid(_bdot(h2.astype(BF16), wpg_ref[...]))
               * _bdot(p_ref[rows, :].astype(BF16), wpp_ref[...]))
        out_ref[rows, :] = _rmsnorm(h2 + ple, gfin_ref[...])


def _ffn(h, p, g_ffn, wg, wu, wd, wpg, wpp, g_final):
    s = h.shape[0]
    tm = min(FFN_ROWS, s)
    row = lambda w: pl.BlockSpec((tm, w), lambda i: (i, 0))
    consts = (g_ffn, wg, wu, wd, wpg, wpp, g_final)
    return pl.pallas_call(
        _ffn_kernel,
        grid=(s // tm,),
        in_specs=[row(D_MODEL), row(PLE_DIM)] + [_resident(c.shape) for c in consts],
        out_specs=row(D_MODEL),
        out_shape=jax.ShapeDtypeStruct((s, D_MODEL), F32),
        scratch_shapes=[pltpu.VMEM((tm, D_MODEL), F32)],
        compiler_params=_params("parallel"),
        name="ffn",
    )(h, p, *consts)


def _layer(x, p, positions, g_mix, w_in, a_re, a_im, log_dt, b_re, b_im, c_re, c_im, d_skip,
           w_attn_proj, w_glu_a, w_glu_b, w_out, g_ffn, w_ffn_gate, w_ffn_up, w_ffn_down,
           w_ple_gate, w_ple_proj, g_final):
    row = lambda v: v.reshape(1, -1).astype(F32)

    inv_freq = ROPE_THETA ** (-jnp.arange(ROPE_HALF, dtype=F32) * 2.0 / ROPE_DIM)
    bmat, cre, cim, e_re, e_im, einv_re, einv_im, lam, w_in_bf = _ssm_prep(
        a_re, a_im, log_dt, b_re, b_im, c_re, c_im, w_in)
    outs = _proj(x, positions.reshape(1, -1), row(g_mix), inv_freq.reshape(ROPE_HALF, 1), w_in_bf,
                 (w_attn_proj, w_glu_a, w_glu_b, w_out, w_ffn_gate, w_ffn_up, w_ple_gate))
    qkvs, (u, gates) = outs[:N_GROUPS], outs[N_GROUPS:N_GROUPS + 2]
    wap, wga, wgb, wout, wfg, wfu, wpg = outs[N_GROUPS + 2:]

    attn_os, attn_ls = zip(*[_attn_group(qkvs[g], d) for g, d in enumerate(ATTN_DILATIONS)])

    tri = jnp.tril(jnp.ones((SSM_CHUNK, SSM_CHUNK), F32)).astype(BF16)
    y = _ssm(u, bmat, cre, cim, tri, e_re, e_im, einv_re, einv_im, lam, row(d_skip))

    h1, wfd, wpp = _mix(x, attn_os, attn_ls, y, gates, wap, wga, wgb, wout, (w_ffn_down, w_ple_proj))
    return _ffn(h1, p, row(g_ffn), wfg, wfu, wfd, wpg, wpp, row(g_final))


def kernel(x, p, positions, g_mix, w_in, a_re, a_im, log_dt, b_re, b_im, c_re, c_im, d_skip,
           w_attn_proj, w_glu_a, w_glu_b, w_out, g_ffn, w_ffn_gate, w_ffn_up, w_ffn_down,
           w_ple_gate, w_ple_proj, g_final):
    batch, depth = x.shape[0], p.shape[0]
    assert batch == 1 and depth == 1, "kernel supports the stated BATCH=1, DEPTH=1 problem"
    out = _layer(x[0], p[0, 0], positions[0], g_mix[0], w_in[0], a_re[0], a_im[0], log_dt[0],
                 b_re[0], b_im[0], c_re[0], c_im[0], d_skip[0], w_attn_proj[0], w_glu_a[0],
                 w_glu_b[0], w_out[0], g_ffn[0], w_ffn_gate[0], w_ffn_up[0], w_ffn_down[0],
                 w_ple_gate[0], w_ple_proj[0], g_final)
    return out[None]
```

```python
import functools
import math

import jax
import jax.numpy as jnp
from jax import lax
from jax.experimental import pallas as pl
from jax.experimental.pallas import tpu as pltpu

F32 = jnp.float32
BF16 = jnp.bfloat16

D_MODEL = 1024
HEAD_DIM = 128
HEADS_PER_GROUP = 4
GROUP_WIDTH = HEADS_PER_GROUP * HEAD_DIM
ATTN_DILATIONS = (1, 4, 16)
N_GROUPS = len(ATTN_DILATIONS)
QK_WIDTH = N_GROUPS * GROUP_WIDTH
BLOCK = 128
ROPE_THETA = 500000.0
ROPE_DIM = HEAD_DIM // 4
ROPE_HALF = ROPE_DIM // 2
SSM_WIDTH = 512
SSM_GROUP = 16
SSM_GROUPS = SSM_WIDTH // SSM_GROUP
SSM_STATE = 64
N_STATES = SSM_GROUPS * SSM_STATE
SSM_HALVES = 2
HALF_GROUPS = SSM_GROUPS // SSM_HALVES
HALF_WIDTH = SSM_WIDTH // SSM_HALVES
HALF_STATES = N_STATES // SSM_HALVES
STATE_BLOCK = 256
D_FF = 2816
PLE_DIM = 256
EPS = 1e-6
MASK_VALUE = -1e30

V7X_VMEM_LIMIT_BYTES = 56 * 1024 * 1024

PROJ_ROWS = 512
PROJ_SUB_ROWS = 256
ATTN_TOKENS = 2048
STAT_LANES = HEAD_DIM // HEADS_PER_GROUP
SCATTER_STRIDE = 4
SSM_CHUNK = 64
SSM_ROWS = 512
PREP_STEPS = 4
MIX_ROWS = 1024
MIX_SUB_ROWS = 256
FFN_ROWS = 1024
FFN_SUB_ROWS = 256
FFN_CHUNKS = ((0, 1024), (1024, 1024), (2048, 768))


def _resident(shape):
    return pl.BlockSpec(shape, lambda *_: (0,) * len(shape), pipeline_mode=pl.Buffered(1))


def _params(*semantics):
    return pltpu.CompilerParams(dimension_semantics=semantics,
                                vmem_limit_bytes=V7X_VMEM_LIMIT_BYTES)


def _rmsnorm(x, g):
    return (x * lax.rsqrt(jnp.mean(x * x, axis=-1, keepdims=True) + EPS)) * g


def _bdot(a, b):
    return jnp.dot(a, b, preferred_element_type=F32)


def _cast_plan(weights, steps):
    specs = [pl.BlockSpec((w.shape[0] // steps, w.shape[1]), lambda i: (i, 0)) for w in weights]
    shapes = [jax.ShapeDtypeStruct(w.shape, BF16) for w in weights]
    return specs, shapes


def _cast_blocks(in_refs, out_refs):
    for src, dst in zip(in_refs, out_refs):
        dst[...] = src[...].astype(BF16)


def _proj_kernel(*refs, n_cast):
    x_ref, pos_ref, g_ref, invf_ref, w_ref = refs[:5]
    cast_in, refs = refs[5:5 + n_cast], refs[5 + n_cast:]
    qkv0, qkv1, qkv2, u_ref, gates_ref = refs[:5]
    cast_out, (n_scr, nperm_scr, cos_scr, sin_scr) = refs[5:5 + n_cast], refs[5 + n_cast:]
    _cast_blocks(cast_in, cast_out)
    rows = PROJ_SUB_ROWS
    lane = lax.broadcasted_iota(jnp.int32, (rows, HEAD_DIM), 1)
    first_half = lane < ROPE_HALF
    scale = math.log2(math.e) / math.sqrt(HEAD_DIM)

    for sb in range(x_ref.shape[0] // rows):
        base = sb * rows
        tile_rows = slice(base, base + rows)
        xn = _rmsnorm(x_ref[tile_rows, :], g_ref[...])
        n = xn.astype(BF16)
        for c in range(D_MODEL // HEAD_DIM):
            n_scr[c, tile_rows, :] = xn[:, c * HEAD_DIM:(c + 1) * HEAD_DIM]

        ang = invf_ref[...] * pos_ref[:, tile_rows].astype(F32)
        cos_t = jnp.cos(ang)
        sin_t = jnp.sin(ang)
        rest = (HEAD_DIM - ROPE_DIM, rows)
        cos_scr[tile_rows, :] = jnp.concatenate([cos_t, cos_t, jnp.ones(rest, F32)], axis=0).T
        sin_scr[tile_rows, :] = jnp.concatenate([-sin_t, sin_t, jnp.zeros(rest, F32)], axis=0).T

        for g, (d, qkv_ref) in enumerate(zip(ATTN_DILATIONS, (qkv0, qkv1, qkv2))):
            def residue_major(ref_2d):
                if d == 1:
                    return ref_2d[tile_rows, :]
                return jnp.concatenate(
                    [ref_2d[pl.ds(base + r, rows // d, stride=d), :] for r in range(d)], axis=0)

            cos = residue_major(cos_scr)
            sin_signed = residue_major(sin_scr)
            if d == 1:
                ng = n
            else:
                for c in range(D_MODEL // HEAD_DIM):
                    nperm_scr[tile_rows, c * HEAD_DIM:(c + 1) * HEAD_DIM] = (
                        residue_major(n_scr.at[c]).astype(BF16))
                ng = nperm_scr[tile_rows, :]

            def rotary(t):
                partner = jnp.where(first_half,
                                    pltpu.roll(t, HEAD_DIM - ROPE_HALF, 1),
                                    pltpu.roll(t, ROPE_HALF, 1))
                return t * cos + partner * sin_signed

            c0 = g * GROUP_WIDTH
            zq = _bdot(ng, w_ref[:, c0:c0 + GROUP_WIDTH])
            zk = _bdot(ng, w_ref[:, QK_WIDTH + c0:QK_WIDTH + c0 + GROUP_WIDTH])
            zv = _bdot(ng, w_ref[:, 2 * QK_WIDTH + c0:2 * QK_WIDTH + c0 + GROUP_WIDTH])
            sub_rows = slice(base // d, (base + rows) // d)
            for h in range(HEADS_PER_GROUP):
                hs = slice(h * HEAD_DIM, (h + 1) * HEAD_DIM)
                ks = slice(GROUP_WIDTH + h * HEAD_DIM, GROUP_WIDTH + (h + 1) * HEAD_DIM)
                qkv_ref[:, sub_rows, hs] = (
                    (rotary(zq[:, hs]) * scale).astype(BF16).reshape(d, rows // d, HEAD_DIM))
                qkv_ref[:, sub_rows, ks] = rotary(zk[:, hs]).astype(BF16).reshape(d, rows // d, HEAD_DIM)
            qkv_ref[:, sub_rows, 2 * GROUP_WIDTH:] = zv.astype(BF16).reshape(d, rows // d, GROUP_WIDTH)
        o1 = 3 * QK_WIDTH
        u_ref[tile_rows, :] = _bdot(n, w_ref[:, o1:o1 + SSM_WIDTH])
        o2 = o1 + SSM_WIDTH
        gates_ref[tile_rows, :] = jax.nn.sigmoid(_bdot(n, w_ref[:, o2:o2 + 2 * D_MODEL])).astype(BF16)


def _proj(x, pos_row, g_mix, inv_freq_col, w_in, later_weights):
    s = x.shape[0]
    tm = min(PROJ_ROWS, s)
    row = lambda w: pl.BlockSpec((tm, w), lambda i: (i, 0))
    cast_specs, cast_shapes = _cast_plan(later_weights, s // tm)
    grp_specs, grp_shapes = [], []
    for d in ATTN_DILATIONS:
        grp_specs.append(pl.BlockSpec((d, tm // d, 3 * GROUP_WIDTH), lambda i: (0, i, 0)))
        grp_shapes.append(jax.ShapeDtypeStruct((d, s // d, 3 * GROUP_WIDTH), BF16))
    return pl.pallas_call(
        functools.partial(_proj_kernel, n_cast=len(later_weights)),
        grid=(s // tm,),
        in_specs=[row(D_MODEL), pl.BlockSpec((1, tm), lambda i: (0, i)), _resident((1, D_MODEL)),
                  _resident((ROPE_HALF, 1)), _resident(w_in.shape)] + cast_specs,
        out_specs=grp_specs + [row(SSM_WIDTH), row(2 * D_MODEL)] + cast_specs,
        out_shape=grp_shapes + [jax.ShapeDtypeStruct((s, SSM_WIDTH), F32),
                                jax.ShapeDtypeStruct((s, 2 * D_MODEL), BF16)] + cast_shapes,
        scratch_shapes=[pltpu.VMEM((D_MODEL // HEAD_DIM, tm, HEAD_DIM), F32), pltpu.VMEM((tm, D_MODEL), BF16),
                        pltpu.VMEM((tm, HEAD_DIM), F32), pltpu.VMEM((tm, HEAD_DIM), F32)],
        compiler_params=_params("parallel"),
        name="proj",
    )(x, pos_row, g_mix, inv_freq_col, w_in, *later_weights)


def _attn_kernel(qkv_ref, o_ref, l_ref, prev_scr, *scatter_scr, d, nsub):
    not_first_block = pl.program_id(0) > 0
    two_pass = d > SCATTER_STRIDE
    if two_pass:
        o_scr, l_scr, part_o, part_l = scatter_scr
    elif d > 1:
        o_scr, l_scr = scatter_scr

    @pl.when(pl.program_id(0) == 0)
    def _():
        prev_scr[...] = jnp.zeros_like(prev_scr)

    qi = lax.broadcasted_iota(jnp.int32, (BLOCK, 2 * BLOCK), 0)
    kj = lax.broadcasted_iota(jnp.int32, (BLOCK, 2 * BLOCK), 1)
    rel = BLOCK + qi - kj
    band = (rel >= 0) & (rel <= BLOCK)
    band_first = band & ((kj >= BLOCK) | not_first_block)
    lane = lax.broadcasted_iota(jnp.int32, (BLOCK, HEAD_DIM), 1)

    def residue(r):
        for b in range(nsub):
            rows = slice(b * BLOCK, (b + 1) * BLOCK)
            mask = band_first if b == 0 else band
            token_rows = pl.ds(b * BLOCK * d + r, BLOCK, stride=d) if d > 1 else rows
            if two_pass:
                r_hi, r_lo = divmod(r, SCATTER_STRIDE)
                part_rows = pl.ds(b * BLOCK * (d // SCATTER_STRIDE) + r_hi, BLOCK, stride=d // SCATTER_STRIDE)
            for h in range(HEADS_PER_GROUP):
                hs = slice(h * HEAD_DIM, (h + 1) * HEAD_DIM)
                ks = slice(GROUP_WIDTH + h * HEAD_DIM, GROUP_WIDTH + (h + 1) * HEAD_DIM)
                vs = slice(2 * GROUP_WIDTH + h * HEAD_DIM, 2 * GROUP_WIDTH + (h + 1) * HEAD_DIM)
                q = qkv_ref[r, rows, hs]
                if b == 0:
                    prev_k = slice(h * HEAD_DIM, (h + 1) * HEAD_DIM)
                    prev_v = slice(GROUP_WIDTH + h * HEAD_DIM, GROUP_WIDTH + (h + 1) * HEAD_DIM)
                    kw = jnp.concatenate([prev_scr[r, :, prev_k], qkv_ref[r, rows, ks]], axis=0)
                    vw = jnp.concatenate([prev_scr[r, :, prev_v], qkv_ref[r, rows, vs]], axis=0)
                else:
                    win = slice((b - 1) * BLOCK, (b + 1) * BLOCK)
                    kw = qkv_ref[r, win, ks]
                    vw = qkv_ref[r, win, vs]
                s = lax.dot_general(q, kw, (((1,), (1,)), ((), ())), preferred_element_type=F32)
                s = jnp.where(mask, s, MASK_VALUE)
                m = jnp.max(s, axis=-1, keepdims=True)
                p = jnp.exp2(s - m)
                den = jnp.sum(p, axis=-1, keepdims=True)
                pv = _bdot(p.astype(BF16), vw)
                if two_pass:
                    part_o[r_lo, h, part_rows, :] = pv
                elif d > 1:
                    o_scr[h, token_rows, :] = pv
                else:
                    o_ref[rows, hs] = pv.astype(BF16)
                m_b = jnp.broadcast_to(m, (BLOCK, HEAD_DIM))
                den_b = jnp.broadcast_to(den, (BLOCK, HEAD_DIM))
                if h > 0:
                    m_b = jnp.where(lane >= h * STAT_LANES, m_b, stats)
                stats = jnp.where(lane >= h * STAT_LANES + STAT_LANES // 2, den_b, m_b)
            if two_pass:
                part_l[r_lo, part_rows, :] = stats
            elif d > 1:
                l_scr[token_rows, :] = stats
            else:
                l_ref[rows, :] = stats

    for r in range(d):
        residue(r)

    if two_pass:
        part_len = part_l.shape[1]
        for r_lo in range(SCATTER_STRIDE):
            token_rows = pl.ds(r_lo, part_len, stride=SCATTER_STRIDE)
            for h in range(HEADS_PER_GROUP):
                o_scr[h, token_rows, :] = part_o[r_lo, h]
            l_scr[token_rows, :] = part_l[r_lo]
    if d > 1:
        for h in range(HEADS_PER_GROUP):
            o_ref[:, h * HEAD_DIM:(h + 1) * HEAD_DIM] = o_scr[h].astype(BF16)
        l_ref[...] = l_scr[...]
    prev_scr[...] = qkv_ref[:, (nsub - 1) * BLOCK:nsub * BLOCK, GROUP_WIDTH:]


def _attn_group(qkv, dilation):
    sub_len = qkv.shape[1]
    s = sub_len * dilation
    step_tokens = min(ATTN_TOKENS, s)
    qb = step_tokens // dilation
    nsub = qb // BLOCK
    cur = pl.BlockSpec((dilation, qb, 3 * GROUP_WIDTH), lambda i: (0, i, 0))
    scatter_scratch = [] if dilation == 1 else [
        pltpu.VMEM((HEADS_PER_GROUP, step_tokens, HEAD_DIM), F32), pltpu.VMEM((step_tokens, HEAD_DIM), F32)]
    if dilation > SCATTER_STRIDE:
        part = step_tokens // SCATTER_STRIDE
        scatter_scratch += [pltpu.VMEM((SCATTER_STRIDE, HEADS_PER_GROUP, part, HEAD_DIM), F32),
                            pltpu.VMEM((SCATTER_STRIDE, part, HEAD_DIM), F32)]
    return pl.pallas_call(
        functools.partial(_attn_kernel, d=dilation, nsub=nsub),
        grid=(s // step_tokens,),
        in_specs=[cur],
        out_specs=[pl.BlockSpec((step_tokens, GROUP_WIDTH), lambda i: (i, 0)),
                   pl.BlockSpec((step_tokens, HEAD_DIM), lambda i: (i, 0))],
        out_shape=[jax.ShapeDtypeStruct((s, GROUP_WIDTH), BF16), jax.ShapeDtypeStruct((s, HEAD_DIM), F32)],
        scratch_shapes=[pltpu.VMEM((dilation, BLOCK, 2 * GROUP_WIDTH), BF16)] + scatter_scratch,
        compiler_params=_params("arbitrary"),
        name=f"attn_d{dilation}",
    )(qkv)


def _discretize(lr, li, log_dt):
    dt = jnp.exp(log_dt)
    mag = jnp.exp(lr * dt)
    bar_re = mag * jnp.cos(li * dt)
    bar_im = mag * jnp.sin(li * dt)
    nr = bar_re - 1.0
    ni = bar_im
    den = lr * lr + li * li
    return bar_re, bar_im, (nr * lr + ni * li) / den, (ni * lr - nr * li) / den


def _expand_block_diag(compact, n_blocks):
    rows, b = compact.shape
    a = rows // n_blocks
    wide = n_blocks * b
    src_lane = lax.broadcasted_iota(jnp.int32, (b, wide), 0)
    dst_lane = lax.broadcasted_iota(jnp.int32, (b, wide), 1)
    tiled = _bdot(compact, (dst_lane % b == src_lane).astype(BF16))
    row_block = lax.broadcasted_iota(jnp.int32, (rows, wide), 0) // a
    col_block = lax.broadcasted_iota(jnp.int32, (rows, wide), 1) // b
    return jnp.where(row_block == col_block, tiled, 0.0).astype(BF16)


def _ssm_prep_kernel(lr_ref, li_ref, logdt_ref, lr_rep_ref, li_rep_ref, logdt_rep_ref, b_re_ref, b_im_ref,
                     c_re_ref, c_im_ref, w_in_ref,
                     bmat_ref, cre_ref, cim_ref, e_re_ref, e_im_ref, einv_re_ref, einv_im_ref, lam_ref,
                     w_in_bf_ref):
    w_in_bf_ref[...] = w_in_ref[...].astype(BF16)
    lr = lr_ref[...]
    li = li_ref[...]
    dt = jnp.exp(logdt_ref[...])

    @pl.when(pl.program_id(0) == 0)
    def _():
        _, _, z_re, z_im = _discretize(lr_rep_ref[...], li_rep_ref[...], logdt_rep_ref[...])
        b_re = b_re_ref[...]
        b_im = b_im_ref[...]
        bb_re = (z_re * b_re - z_im * b_im).astype(BF16)
        bb_im = (z_re * b_im + z_im * b_re).astype(BF16)
        for hf in range(SSM_HALVES):
            rows = slice(hf * HALF_WIDTH, (hf + 1) * HALF_WIDTH)
            bmat_ref[rows, :HALF_STATES] = _expand_block_diag(bb_re[rows], HALF_GROUPS)
            bmat_ref[rows, HALF_STATES:] = _expand_block_diag(bb_im[rows], HALF_GROUPS)
            srows = slice(hf * HALF_STATES, (hf + 1) * HALF_STATES)
            cre_ref[srows, :] = _expand_block_diag(c_re_ref[srows, :].astype(BF16), HALF_GROUPS)
            cim_ref[srows, :] = _expand_block_diag(c_im_ref[srows, :].astype(BF16), HALF_GROUPS)
        whole = float(SSM_CHUNK)
        chunk_mag = jnp.exp(whole * (lr * dt))
        lam_ref[0:1, :] = chunk_mag * jnp.cos(whole * (li * dt))
        lam_ref[1:2, :] = chunk_mag * jnp.sin(whole * (li * dt))

    rows_per_step = e_re_ref.shape[0]
    t = (lax.broadcasted_iota(jnp.int32, (rows_per_step, N_STATES), 0)
         + pl.program_id(0) * rows_per_step).astype(F32)
    grow = jnp.exp(t * (lr * dt))
    theta = t * (li * dt)
    c = jnp.cos(theta)
    s = jnp.sin(theta)
    e_re_ref[...] = (grow * c).astype(BF16)
    e_im_ref[...] = (grow * s).astype(BF16)
    shrink = jnp.exp(-t * (lr * dt))
    einv_re_ref[...] = (shrink * c).astype(BF16)
    einv_im_ref[...] = (-(shrink * s)).astype(BF16)


def _ssm_prep(a_re, a_im, log_dt, b_re, b_im, c_re, c_im, w_in):
    row = lambda v: v.reshape(1, N_STATES)
    rep = lambda v: jnp.repeat(v, SSM_GROUP, axis=0)
    logdt_gp = jnp.broadcast_to(log_dt[:, None], (SSM_GROUPS, SSM_STATE))
    chan_state = lambda b: jnp.swapaxes(b, 1, 2).reshape(SSM_WIDTH, SSM_STATE)
    state_chan = lambda c: jnp.swapaxes(c, 1, 2).reshape(N_STATES, SSM_GROUP)
    params = (row(a_re), row(a_im), row(logdt_gp), rep(a_re), rep(a_im), rep(logdt_gp),
              chan_state(b_re), chan_state(b_im), state_chan(c_re), state_chan(c_im))
    whole = lambda shape: pl.BlockSpec(shape, lambda i: (0,) * len(shape))
    slab = lambda arr: pl.BlockSpec((arr.shape[0] // PREP_STEPS, arr.shape[1]), lambda i: (i, 0))
    tab = jax.ShapeDtypeStruct((SSM_CHUNK, N_STATES), BF16)
    cmat = jax.ShapeDtypeStruct((N_STATES, HALF_WIDTH), BF16)
    bmat = jax.ShapeDtypeStruct((SSM_WIDTH, 2 * HALF_STATES), BF16)
    lam = jax.ShapeDtypeStruct((2, N_STATES), F32)
    return pl.pallas_call(
        _ssm_prep_kernel,
        grid=(PREP_STEPS,),
        in_specs=[whole(p.shape) for p in params] + [slab(w_in)],
        out_specs=[whole(bmat.shape), whole(cmat.shape), whole(cmat.shape), slab(tab), slab(tab), slab(tab),
                   slab(tab), whole(lam.shape), slab(w_in)],
        out_shape=[bmat, cmat, cmat, tab, tab, tab, tab, lam, jax.ShapeDtypeStruct(w_in.shape, BF16)],
        compiler_params=_params("arbitrary"),
        name="ssm_prep",
    )(*params, w_in)


def _gelu_tanh(x):
    return 0.5 * x * (1.0 + jnp.tanh(math.sqrt(2.0 / math.pi) * (x + 0.044715 * (x * x * x))))


def _ssm_kernel(u_ref, bmat_ref, cre_ref, cim_ref, tri_ref, e_re_ref, e_im_ref, einv_re_ref, einv_im_ref,
                lam_ref, d_ref, y_ref, h_re_scr, h_im_scr, carry_scr):
    @pl.when(pl.program_id(0) == 0)
    def _():
        carry_scr[...] = jnp.zeros_like(carry_scr)

    u = u_ref[...]
    ub = u.astype(BF16)
    for hf in range(SSM_HALVES):
        chans = slice(hf * HALF_WIDTH, (hf + 1) * HALF_WIDTH)
        y = d_ref[:, chans] * u[:, chans]
        for blk in range(HALF_STATES // STATE_BLOCK):
            local = slice(blk * STATE_BLOCK, (blk + 1) * STATE_BLOCK)
            local_im = slice(HALF_STATES + blk * STATE_BLOCK, HALF_STATES + (blk + 1) * STATE_BLOCK)
            states = slice(hf * HALF_STATES + blk * STATE_BLOCK, hf * HALF_STATES + (blk + 1) * STATE_BLOCK)
            bu_re_all = _bdot(ub[:, chans], bmat_ref[chans, local])
            bu_im_all = _bdot(ub[:, chans], bmat_ref[chans, local_im])
            lam_re = lam_ref[0:1, states]
            lam_im = lam_ref[1:2, states]
            carry_re = carry_scr[0:1, states]
            carry_im = carry_scr[1:2, states]
            for c in range(u.shape[0] // SSM_CHUNK):
                rows = slice(c * SSM_CHUNK, (c + 1) * SSM_CHUNK)
                bu_re = bu_re_all[rows].astype(BF16)
                bu_im = bu_im_all[rows].astype(BF16)
                einv_re = einv_re_ref[:, states]
                einv_im = einv_im_ref[:, states]
                x = jnp.concatenate([bu_re * einv_re - bu_im * einv_im,
                                     bu_re * einv_im + bu_im * einv_re], axis=1)
                a = _bdot(tri_ref[...], x)
                a_re = a[:, :STATE_BLOCK] + carry_re
                a_im = a[:, STATE_BLOCK:] + carry_im
                e_re = e_re_ref[:, states]
                e_im = e_im_ref[:, states]
                a_re_b = a_re.astype(BF16)
                a_im_b = a_im.astype(BF16)
                h_re_scr[rows, :] = e_re * a_re_b - e_im * a_im_b
                h_im_scr[rows, :] = e_re * a_im_b + e_im * a_re_b
                last_re = a_re[SSM_CHUNK - 1:SSM_CHUNK, :]
                last_im = a_im[SSM_CHUNK - 1:SSM_CHUNK, :]
                carry_re = lam_re * last_re - lam_im * last_im
                carry_im = lam_re * last_im + lam_im * last_re
            carry_scr[0:1, states] = carry_re
            carry_scr[1:2, states] = carry_im
            y = y + _bdot(h_re_scr[...], cre_ref[states, :]) - _bdot(h_im_scr[...], cim_ref[states, :])
        y_ref[:, chans] = _gelu_tanh(y).astype(BF16)


def _ssm(u, bmat, cre, cim, tri, e_re, e_im, einv_re, einv_im, lam, d_row):
    s = u.shape[0]
    tm = min(SSM_ROWS, s)
    row = pl.BlockSpec((tm, SSM_WIDTH), lambda i: (i, 0))
    consts = (bmat, cre, cim, tri, e_re, e_im, einv_re, einv_im, lam, d_row)
    return pl.pallas_call(
        _ssm_kernel,
        grid=(s // tm,),
        in_specs=[row] + [_resident(c.shape) for c in consts],
        out_specs=row,
        out_shape=jax.ShapeDtypeStruct((s, SSM_WIDTH), BF16),
        scratch_shapes=[pltpu.VMEM((tm, STATE_BLOCK), BF16), pltpu.VMEM((tm, STATE_BLOCK), BF16),
                        pltpu.VMEM((2, N_STATES), F32)],
        compiler_params=_params("arbitrary"),
        name="ssm",
    )(u, *consts)


def _mix_kernel(*refs, n_cast):
    x_ref, o0_ref, o1_ref, o2_ref, l0_ref, l1_ref, l2_ref, y_ref, gates_ref = refs[:9]
    wap_ref, wa_ref, wb_ref, wout_ref = refs[9:13]
    cast_in, h_ref, cast_out = refs[13:13 + n_cast], refs[13 + n_cast], refs[14 + n_cast:]
    _cast_blocks(cast_in, cast_out)
    for rb in range(x_ref.shape[0] // MIX_SUB_ROWS):
        rows = slice(rb * MIX_SUB_ROWS, (rb + 1) * MIX_SUB_ROWS)
        ls = (l0_ref[rows, :], l1_ref[rows, :], l2_ref[rows, :])
        m_max = jnp.maximum(jnp.maximum(ls[0], ls[1]), ls[2])
        es = [jnp.exp2(l - m_max) for l in ls]
        dens = [pltpu.roll(l, HEAD_DIM - STAT_LANES // 2, 1) for l in ls]
        inv = 1.0 / (es[0] * dens[0] + es[1] * dens[1] + es[2] * dens[2])
        heads = []
        for h in range(HEADS_PER_GROUP):
            hs = slice(h * HEAD_DIM, (h + 1) * HEAD_DIM)
            col = slice(h * STAT_LANES, h * STAT_LANES + 1)
            heads.append(sum((e * inv)[:, col] * o_ref[rows, hs].astype(F32)
                             for e, o_ref in zip(es, (o0_ref, o1_ref, o2_ref))))
        attn = jnp.concatenate(heads, axis=1).astype(BF16)
        attn_d = _bdot(attn, wap_ref[...])
        y = y_ref[rows, :]
        ssm_d = _bdot(y, wa_ref[...]) * jax.nn.sigmoid(_bdot(y, wb_ref[...]))
        mix = (gates_ref[rows, :D_MODEL].astype(F32) * attn_d
               + gates_ref[rows, D_MODEL:].astype(F32) * ssm_d)
        h_ref[rows, :] = x_ref[rows, :] + _bdot(mix.astype(BF16), wout_ref[...])


def _mix(x, os, ls, y, gates, wap, wa, wb, wout, later_weights):
    s = x.shape[0]
    tm = min(MIX_ROWS, s)
    row = lambda w: pl.BlockSpec((tm, w), lambda i: (i, 0))
    weights = (wap, wa, wb, wout)
    cast_specs, cast_shapes = _cast_plan(later_weights, s // tm)
    return pl.pallas_call(
        functools.partial(_mix_kernel, n_cast=len(later_weights)),
        grid=(s // tm,),
        in_specs=[row(D_MODEL)] + [row(GROUP_WIDTH)] * 3 + [row(HEAD_DIM)] * 3
        + [row(SSM_WIDTH), row(2 * D_MODEL)] + [_resident(w.shape) for w in weights] + cast_specs,
        out_specs=[row(D_MODEL)] + cast_specs,
        out_shape=[jax.ShapeDtypeStruct((s, D_MODEL), F32)] + cast_shapes,
        compiler_params=_params("parallel"),
        name="mix",
    )(x, *os, *ls, y, gates, *weights, *later_weights)


def _ffn_kernel(h_ref, p_ref, gffn_ref, wg_ref, wu_ref, wd_ref, wpg_ref, wpp_ref, gfin_ref,
                out_ref, acc_scr):
    for rb in range(h_ref.shape[0] // FFN_SUB_ROWS):
        rows = slice(rb * FFN_SUB_ROWS, (rb + 1) * FFN_SUB_ROWS)
        h = h_ref[rows, :]
        n2 = _rmsnorm(h, gffn_ref[...]).astype(BF16)
        for idx, (c0, width) in enumerate(FFN_CHUNKS):
            gate = _bdot(n2, wg_ref[:, c0:c0 + width])
            up = _bdot(n2, wu_ref[:, c0:c0 + width])
            act = (gate * jax.nn.sigmoid(gate) * up).astype(BF16)
            part = _bdot(act, wd_ref[c0:c0 + width, :])
            if idx == 0:
                acc_scr[rows, :] = h + part
            else:
                acc_scr[rows, :] += part
        h2 = acc_scr[rows, :]
        ple = (jax.nn.sigmoid(_bdot(h2.astype(BF16), wpg_ref[...]))
               * _bdot(p_ref[rows, :].astype(BF16), wpp_ref[...]))
        out_ref[rows, :] = _rmsnorm(h2 + ple, gfin_ref[...])


def _ffn(h, p, g_ffn, wg, wu, wd, wpg, wpp, g_final):
    s = h.shape[0]
    tm = min(FFN_ROWS, s)
    row = lambda w: pl.BlockSpec((tm, w), lambda i: (i, 0))
    consts = (g_ffn, wg, wu, wd, wpg, wpp, g_final)
    return pl.pallas_call(
        _ffn_kernel,
        grid=(s // tm,),
        in_specs=[row(D_MODEL), row(PLE_DIM)] + [_resident(c.shape) for c in consts],
        out_specs=row(D_MODEL),
        out_shape=jax.ShapeDtypeStruct((s, D_MODEL), F32),
        scratch_shapes=[pltpu.VMEM((tm, D_MODEL), F32)],
        compiler_params=_params("parallel"),
        name="ffn",
    )(h, p, *consts)


def _layer(x, p, positions, g_mix, w_in, a_re, a_im, log_dt, b_re, b_im, c_re, c_im, d_skip,
           w_attn_proj, w_glu_a, w_glu_b, w_out, g_ffn, w_ffn_gate, w_ffn_up, w_ffn_down,
           w_ple_gate, w_ple_proj, g_final):
    row = lambda v: v.reshape(1, -1).astype(F32)

    inv_freq = ROPE_THETA ** (-jnp.arange(ROPE_HALF, dtype=F32) * 2.0 / ROPE_DIM)
    bmat, cre, cim, e_re, e_im, einv_re, einv_im, lam, w_in_bf = _ssm_prep(
        a_re, a_im, log_dt, b_re, b_im, c_re, c_im, w_in)
    outs = _proj(x, positions.reshape(1, -1), row(g_mix), inv_freq.reshape(ROPE_HALF, 1), w_in_bf,
                 (w_attn_proj, w_glu_a, w_glu_b, w_out, w_ffn_gate, w_ffn_up, w_ple_gate))
    qkvs, (u, gates) = outs[:N_GROUPS], outs[N_GROUPS:N_GROUPS + 2]
    wap, wga, wgb, wout, wfg, wfu, wpg = outs[N_GROUPS + 2:]

    attn_os, attn_ls = zip(*[_attn_group(qkvs[g], d) for g, d in enumerate(ATTN_DILATIONS)])

    tri = jnp.tril(jnp.ones((SSM_CHUNK, SSM_CHUNK), F32)).astype(BF16)
    y = _ssm(u, bmat, cre, cim, tri, e_re, e_im, einv_re, einv_im, lam, row(d_skip))

    h1, wfd, wpp = _mix(x, attn_os, attn_ls, y, gates, wap, wga, wgb, wout, (w_ffn_down, w_ple_proj))
    return _ffn(h1, p, row(g_ffn), wfg, wfu, wfd, wpg, wpp, row(g_final))


def kernel(x, p, positions, g_mix, w_in, a_re, a_im, log_dt, b_re, b_im, c_re, c_im, d_skip,
           w_attn_proj, w_glu_a, w_glu_b, w_out, g_ffn, w_ffn_gate, w_ffn_up, w_ffn_down,
           w_ple_gate, w_ple_proj, g_final):
    batch, depth = x.shape[0], p.shape[0]
    assert batch == 1 and depth == 1, "kernel supports the stated BATCH=1, DEPTH=1 problem"
    out = _layer(x[0], p[0, 0], positions[0], g_mix[0], w_in[0], a_re[0], a_im[0], log_dt[0],
                 b_re[0], b_im[0], c_re[0], c_im[0], d_skip[0], w_attn_proj[0], w_glu_a[0],
                 w_glu_b[0], w_out[0], g_ffn[0], w_ffn_gate[0], w_ffn_up[0], w_ffn_down[0],
                 w_ple_gate[0], w_ple_proj[0], g_final)
    return out[None]
```

```python
import functools
import math

import jax
import jax.numpy as jnp
from jax import lax
from jax.experimental import pallas as pl
from jax.experimental.pallas import tpu as pltpu

F32 = jnp.float32
BF16 = jnp.bfloat16

D_MODEL = 1024
HEAD_DIM = 128
HEADS_PER_GROUP = 4
GROUP_WIDTH = HEADS_PER_GROUP * HEAD_DIM
ATTN_DILATIONS = (1, 4, 16)
N_GROUPS = len(ATTN_DILATIONS)
QK_WIDTH = N_GROUPS * GROUP_WIDTH
BLOCK = 128
ROPE_THETA = 500000.0
ROPE_DIM = HEAD_DIM // 4
ROPE_HALF = ROPE_DIM // 2
SSM_WIDTH = 512
SSM_GROUP = 16
SSM_GROUPS = SSM_WIDTH // SSM_GROUP
SSM_STATE = 64
N_STATES = SSM_GROUPS * SSM_STATE
SSM_HALVES = 2
HALF_GROUPS = SSM_GROUPS // SSM_HALVES
HALF_WIDTH = SSM_WIDTH // SSM_HALVES
HALF_STATES = N_STATES // SSM_HALVES
STATE_BLOCK = 256
D_FF = 2816
PLE_DIM = 256
EPS = 1e-6
MASK_VALUE = -1e30

V7X_VMEM_LIMIT_BYTES = 56 * 1024 * 1024

PROJ_ROWS = 512
PROJ_SUB_ROWS = 256
ATTN_TOKENS = 2048
STAT_LANES = HEAD_DIM // HEADS_PER_GROUP
SCATTER_STRIDE = 4
SSM_CHUNK = 32
SSM_ROWS = 512
PREP_STEPS = 2
MIX_ROWS = 1024
MIX_SUB_ROWS = 256
FFN_ROWS = 1024
FFN_SUB_ROWS = 256
FFN_CHUNKS = ((0, 1024), (1024, 1024), (2048, 768))


def _resident(shape):
    return pl.BlockSpec(shape, lambda *_: (0,) * len(shape), pipeline_mode=pl.Buffered(1))


def _params(*semantics):
    return pltpu.CompilerParams(dimension_semantics=semantics,
                                vmem_limit_bytes=V7X_VMEM_LIMIT_BYTES)


def _rmsnorm(x, g):
    return (x * lax.rsqrt(jnp.mean(x * x, axis=-1, keepdims=True) + EPS)) * g


def _bdot(a, b):
    return jnp.dot(a, b, preferred_element_type=F32)


def _cast_plan(weights, steps):
    specs = [pl.BlockSpec((w.shape[0] // steps, w.shape[1]), lambda i: (i, 0)) for w in weights]
    shapes = [jax.ShapeDtypeStruct(w.shape, BF16) for w in weights]
    return specs, shapes


def _cast_blocks(in_refs, out_refs):
    for src, dst in zip(in_refs, out_refs):
        dst[...] = src[...].astype(BF16)


def _proj_kernel(*refs, n_cast):
    x_ref, pos_ref, g_ref, invf_ref, w_ref = refs[:5]
    cast_in, refs = refs[5:5 + n_cast], refs[5 + n_cast:]
    qkv0, qkv1, qkv2, u_ref, gates_ref = refs[:5]
    cast_out, (n_scr, nperm_scr, cos_scr, sin_scr) = refs[5:5 + n_cast], refs[5 + n_cast:]
    _cast_blocks(cast_in, cast_out)
    rows = PROJ_SUB_ROWS
    lane = lax.broadcasted_iota(jnp.int32, (rows, HEAD_DIM), 1)
    first_half = lane < ROPE_HALF
    scale = math.log2(math.e) / math.sqrt(HEAD_DIM)

    for sb in range(x_ref.shape[0] // rows):
        base = sb * rows
        tile_rows = slice(base, base + rows)
        xn = _rmsnorm(x_ref[tile_rows, :], g_ref[...])
        n = xn.astype(BF16)
        for c in range(D_MODEL // HEAD_DIM):
            n_scr[c, tile_rows, :] = xn[:, c * HEAD_DIM:(c + 1) * HEAD_DIM]

        ang = invf_ref[...] * pos_ref[:, tile_rows].astype(F32)
        cos_t = jnp.cos(ang)
        sin_t = jnp.sin(ang)
        rest = (HEAD_DIM - ROPE_DIM, rows)
        cos_scr[tile_rows, :] = jnp.concatenate([cos_t, cos_t, jnp.ones(rest, F32)], axis=0).T
        sin_scr[tile_rows, :] = jnp.concatenate([-sin_t, sin_t, jnp.zeros(rest, F32)], axis=0).T

        for g, (d, qkv_ref) in enumerate(zip(ATTN_DILATIONS, (qkv0, qkv1, qkv2))):
            def residue_major(ref_2d):
                if d == 1:
                    return ref_2d[tile_rows, :]
                return jnp.concatenate(
                    [ref_2d[pl.ds(base + r, rows // d, stride=d), :] for r in range(d)], axis=0)

            cos = residue_major(cos_scr)
            sin_signed = residue_major(sin_scr)
            if d == 1:
                ng = n
            else:
                for c in range(D_MODEL // HEAD_DIM):
                    nperm_scr[tile_rows, c * HEAD_DIM:(c + 1) * HEAD_DIM] = (
                        residue_major(n_scr.at[c]).astype(BF16))
                ng = nperm_scr[tile_rows, :]

            def rotary(t):
                partner = jnp.where(first_half,
                                    pltpu.roll(t, HEAD_DIM - ROPE_HALF, 1),
                                    pltpu.roll(t, ROPE_HALF, 1))
                return t * cos + partner * sin_signed

            c0 = g * GROUP_WIDTH
            zq = _bdot(ng, w_ref[:, c0:c0 + GROUP_WIDTH])
            zk = _bdot(ng, w_ref[:, QK_WIDTH + c0:QK_WIDTH + c0 + GROUP_WIDTH])
            zv = _bdot(ng, w_ref[:, 2 * QK_WIDTH + c0:2 * QK_WIDTH + c0 + GROUP_WIDTH])
            sub_rows = slice(base // d, (base + rows) // d)
            for h in range(HEADS_PER_GROUP):
                hs = slice(h * HEAD_DIM, (h + 1) * HEAD_DIM)
                ks = slice(GROUP_WIDTH + h * HEAD_DIM, GROUP_WIDTH + (h + 1) * HEAD_DIM)
                qkv_ref[:, sub_rows, hs] = (
                    (rotary(zq[:, hs]) * scale).astype(BF16).reshape(d, rows // d, HEAD_DIM))
                qkv_ref[:, sub_rows, ks] = rotary(zk[:, hs]).astype(BF16).reshape(d, rows // d, HEAD_DIM)
            qkv_ref[:, sub_rows, 2 * GROUP_WIDTH:] = zv.astype(BF16).reshape(d, rows // d, GROUP_WIDTH)
        o1 = 3 * QK_WIDTH
        u_ref[tile_rows, :] = _bdot(n, w_ref[:, o1:o1 + SSM_WIDTH])
        o2 = o1 + SSM_WIDTH
        gates_ref[tile_rows, :] = jax.nn.sigmoid(_bdot(n, w_ref[:, o2:o2 + 2 * D_MODEL])).astype(BF16)


def _proj(x, pos_row, g_mix, inv_freq_col, w_in, later_weights):
    s = x.shape[0]
    tm = min(PROJ_ROWS, s)
    row = lambda w: pl.BlockSpec((tm, w), lambda i: (i, 0))
    cast_specs, cast_shapes = _cast_plan(later_weights, s // tm)
    grp_specs, grp_shapes = [], []
    for d in ATTN_DILATIONS:
        grp_specs.append(pl.BlockSpec((d, tm // d, 3 * GROUP_WIDTH), lambda i: (0, i, 0)))
        grp_shapes.append(jax.ShapeDtypeStruct((d, s // d, 3 * GROUP_WIDTH), BF16))
    return pl.pallas_call(
        functools.partial(_proj_kernel, n_cast=len(later_weights)),
        grid=(s // tm,),
        in_specs=[row(D_MODEL), pl.BlockSpec((1, tm), lambda i: (0, i)), _resident((1, D_MODEL)),
                  _resident((ROPE_HALF, 1)), _resident(w_in.shape)] + cast_specs,
        out_specs=grp_specs + [row(SSM_WIDTH), row(2 * D_MODEL)] + cast_specs,
        out_shape=grp_shapes + [jax.ShapeDtypeStruct((s, SSM_WIDTH), F32),
                                jax.ShapeDtypeStruct((s, 2 * D_MODEL), BF16)] + cast_shapes,
        scratch_shapes=[pltpu.VMEM((D_MODEL // HEAD_DIM, tm, HEAD_DIM), F32), pltpu.VMEM((tm, D_MODEL), BF16),
                        pltpu.VMEM((tm, HEAD_DIM), F32), pltpu.VMEM((tm, HEAD_DIM), F32)],
        compiler_params=_params("parallel"),
        name="proj",
    )(x, pos_row, g_mix, inv_freq_col, w_in, *later_weights)


def _attn_kernel(qkv_ref, o_ref, l_ref, prev_scr, *scatter_scr, d, nsub):
    not_first_block = pl.program_id(0) > 0
    two_pass = d > SCATTER_STRIDE
    if two_pass:
        o_scr, l_scr, part_o, part_l = scatter_scr
    elif d > 1:
        o_scr, l_scr = scatter_scr

    @pl.when(pl.program_id(0) == 0)
    def _():
        prev_scr[...] = jnp.zeros_like(prev_scr)

    qi = lax.broadcasted_iota(jnp.int32, (BLOCK, 2 * BLOCK), 0)
    kj = lax.broadcasted_iota(jnp.int32, (BLOCK, 2 * BLOCK), 1)
    rel = BLOCK + qi - kj
    band = (rel >= 0) & (rel <= BLOCK)
    band_first = band & ((kj >= BLOCK) | not_first_block)
    lane = lax.broadcasted_iota(jnp.int32, (BLOCK, HEAD_DIM), 1)

    def residue(r):
        for b in range(nsub):
            rows = slice(b * BLOCK, (b + 1) * BLOCK)
            mask = band_first if b == 0 else band
            token_rows = pl.ds(b * BLOCK * d + r, BLOCK, stride=d) if d > 1 else rows
            if two_pass:
                r_hi, r_lo = divmod(r, SCATTER_STRIDE)
                part_rows = pl.ds(b * BLOCK * (d // SCATTER_STRIDE) + r_hi, BLOCK, stride=d // SCATTER_STRIDE)
            for h in range(HEADS_PER_GROUP):
                hs = slice(h * HEAD_DIM, (h + 1) * HEAD_DIM)
                ks = slice(GROUP_WIDTH + h * HEAD_DIM, GROUP_WIDTH + (h + 1) * HEAD_DIM)
                vs = slice(2 * GROUP_WIDTH + h * HEAD_DIM, 2 * GROUP_WIDTH + (h + 1) * HEAD_DIM)
                q = qkv_ref[r, rows, hs]
                if b == 0:
                    prev_k = slice(h * HEAD_DIM, (h + 1) * HEAD_DIM)
                    prev_v = slice(GROUP_WIDTH + h * HEAD_DIM, GROUP_WIDTH + (h + 1) * HEAD_DIM)
                    kw = jnp.concatenate([prev_scr[r, :, prev_k], qkv_ref[r, rows, ks]], axis=0)
                    vw = jnp.concatenate([prev_scr[r, :, prev_v], qkv_ref[r, rows, vs]], axis=0)
                else:
                    win = slice((b - 1) * BLOCK, (b + 1) * BLOCK)
                    kw = qkv_ref[r, win, ks]
                    vw = qkv_ref[r, win, vs]
                s = lax.dot_general(q, kw, (((1,), (1,)), ((), ())), preferred_element_type=F32)
                s = jnp.where(mask, s, MASK_VALUE)
                m = jnp.max(s, axis=-1, keepdims=True)
                p = jnp.exp2(s - m)
                den = jnp.sum(p, axis=-1, keepdims=True)
                pv = _bdot(p.astype(BF16), vw)
                if two_pass:
                    part_o[r_lo, h, part_rows, :] = pv
                elif d > 1:
                    o_scr[h, token_rows, :] = pv
                else:
                    o_ref[rows, hs] = pv.astype(BF16)
                m_b = jnp.broadcast_to(m, (BLOCK, HEAD_DIM))
                den_b = jnp.broadcast_to(den, (BLOCK, HEAD_DIM))
                if h > 0:
                    m_b = jnp.where(lane >= h * STAT_LANES, m_b, stats)
                stats = jnp.where(lane >= h * STAT_LANES + STAT_LANES // 2, den_b, m_b)
            if two_pass:
                part_l[r_lo, part_rows, :] = stats
            elif d > 1:
                l_scr[token_rows, :] = stats
            else:
                l_ref[rows, :] = stats

    for r in range(d):
        residue(r)

    if two_pass:
        part_len = part_l.shape[1]
        for r_lo in range(SCATTER_STRIDE):
            token_rows = pl.ds(r_lo, part_len, stride=SCATTER_STRIDE)
            for h in range(HEADS_PER_GROUP):
                o_scr[h, token_rows, :] = part_o[r_lo, h]
            l_scr[token_rows, :] = part_l[r_lo]
    if d > 1:
        for h in range(HEADS_PER_GROUP):
            o_ref[:, h * HEAD_DIM:(h + 1) * HEAD_DIM] = o_scr[h].astype(BF16)
        l_ref[...] = l_scr[...]
    prev_scr[...] = qkv_ref[:, (nsub - 1) * BLOCK:nsub * BLOCK, GROUP_WIDTH:]


def _attn_group(qkv, dilation):
    sub_len = qkv.shape[1]
    s = sub_len * dilation
    step_tokens = min(ATTN_TOKENS, s)
    qb = step_tokens // dilation
    nsub = qb // BLOCK
    cur = pl.BlockSpec((dilation, qb, 3 * GROUP_WIDTH), lambda i: (0, i, 0))
    scatter_scratch = [] if dilation == 1 else [
        pltpu.VMEM((HEADS_PER_GROUP, step_tokens, HEAD_DIM), F32), pltpu.VMEM((step_tokens, HEAD_DIM), F32)]
    if dilation > SCATTER_STRIDE:
        part = step_tokens // SCATTER_STRIDE
        scatter_scratch += [pltpu.VMEM((SCATTER_STRIDE, HEADS_PER_GROUP, part, HEAD_DIM), F32),
                            pltpu.VMEM((SCATTER_STRIDE, part, HEAD_DIM), F32)]
    return pl.pallas_call(
        functools.partial(_attn_kernel, d=dilation, nsub=nsub),
        grid=(s // step_tokens,),
        in_specs=[cur],
        out_specs=[pl.BlockSpec((step_tokens, GROUP_WIDTH), lambda i: (i, 0)),
                   pl.BlockSpec((step_tokens, HEAD_DIM), lambda i: (i, 0))],
        out_shape=[jax.ShapeDtypeStruct((s, GROUP_WIDTH), BF16), jax.ShapeDtypeStruct((s, HEAD_DIM), F32)],
        scratch_shapes=[pltpu.VMEM((dilation, BLOCK, 2 * GROUP_WIDTH), BF16)] + scatter_scratch,
        compiler_params=_params("arbitrary"),
        name=f"attn_d{dilation}",
    )(qkv)


def _discretize(lr, li, log_dt):
    dt = jnp.exp(log_dt)
    mag = jnp.exp(lr * dt)
    bar_re = mag * jnp.cos(li * dt)
    bar_im = mag * jnp.sin(li * dt)
    nr = bar_re - 1.0
    ni = bar_im
    den = lr * lr + li * li
    return bar_re, bar_im, (nr * lr + ni * li) / den, (ni * lr - nr * li) / den


def _expand_block_diag(compact, n_blocks):
    rows, b = compact.shape
    a = rows // n_blocks
    wide = n_blocks * b
    src_lane = lax.broadcasted_iota(jnp.int32, (b, wide), 0)
    dst_lane = lax.broadcasted_iota(jnp.int32, (b, wide), 1)
    tiled = _bdot(compact, (dst_lane % b == src_lane).astype(BF16))
    row_block = lax.broadcasted_iota(jnp.int32, (rows, wide), 0) // a
    col_block = lax.broadcasted_iota(jnp.int32, (rows, wide), 1) // b
    return jnp.where(row_block == col_block, tiled, 0.0).astype(BF16)


def _ssm_prep_kernel(lr_ref, li_ref, logdt_ref, lr_rep_ref, li_rep_ref, logdt_rep_ref, b_re_ref, b_im_ref,
                     c_re_ref, c_im_ref, w_in_ref,
                     bmat_ref, cre_ref, cim_ref, e_re_ref, e_im_ref, einv_re_ref, einv_im_ref, lam_ref,
                     w_in_bf_ref):
    w_in_bf_ref[...] = w_in_ref[...].astype(BF16)
    lr = lr_ref[...]
    li = li_ref[...]
    dt = jnp.exp(logdt_ref[...])

    @pl.when(pl.program_id(0) == 0)
    def _():
        _, _, z_re, z_im = _discretize(lr_rep_ref[...], li_rep_ref[...], logdt_rep_ref[...])
        b_re = b_re_ref[...]
        b_im = b_im_ref[...]
        bb_re = (z_re * b_re - z_im * b_im).astype(BF16)
        bb_im = (z_re * b_im + z_im * b_re).astype(BF16)
        for hf in range(SSM_HALVES):
            rows = slice(hf * HALF_WIDTH, (hf + 1) * HALF_WIDTH)
            bmat_ref[rows, :HALF_STATES] = _expand_block_diag(bb_re[rows], HALF_GROUPS)
            bmat_ref[rows, HALF_STATES:] = _expand_block_diag(bb_im[rows], HALF_GROUPS)
            srows = slice(hf * HALF_STATES, (hf + 1) * HALF_STATES)
            cre_ref[srows, :] = _expand_block_diag(c_re_ref[srows, :].astype(BF16), HALF_GROUPS)
            cim_ref[srows, :] = _expand_block_diag(c_im_ref[srows, :].astype(BF16), HALF_GROUPS)
        whole = float(SSM_CHUNK)
        chunk_mag = jnp.exp(whole * (lr * dt))
        lam_ref[0:1, :] = chunk_mag * jnp.cos(whole * (li * dt))
        lam_ref[1:2, :] = chunk_mag * jnp.sin(whole * (li * dt))

    rows_per_step = e_re_ref.shape[0]
    t = (lax.broadcasted_iota(jnp.int32, (rows_per_step, N_STATES), 0)
         + pl.program_id(0) * rows_per_step).astype(F32)
    grow = jnp.exp(t * (lr * dt))
    theta = t * (li * dt)
    c = jnp.cos(theta)
    s = jnp.sin(theta)
    e_re_ref[...] = (grow * c).astype(BF16)
    e_im_ref[...] = (grow * s).astype(BF16)
    shrink = jnp.exp(-t * (lr * dt))
    einv_re_ref[...] = (shrink * c).astype(BF16)
    einv_im_ref[...] = (-(shrink * s)).astype(BF16)


def _ssm_prep(a_re, a_im, log_dt, b_re, b_im, c_re, c_im, w_in):
    row = lambda v: v.reshape(1, N_STATES)
    rep = lambda v: jnp.repeat(v, SSM_GROUP, axis=0)
    logdt_gp = jnp.broadcast_to(log_dt[:, None], (SSM_GROUPS, SSM_STATE))
    chan_state = lambda b: jnp.swapaxes(b, 1, 2).reshape(SSM_WIDTH, SSM_STATE)
    state_chan = lambda c: jnp.swapaxes(c, 1, 2).reshape(N_STATES, SSM_GROUP)
    params = (row(a_re), row(a_im), row(logdt_gp), rep(a_re), rep(a_im), rep(logdt_gp),
              chan_state(b_re), chan_state(b_im), state_chan(c_re), state_chan(c_im))
    whole = lambda shape: pl.BlockSpec(shape, lambda i: (0,) * len(shape))
    slab = lambda arr: pl.BlockSpec((arr.shape[0] // PREP_STEPS, arr.shape[1]), lambda i: (i, 0))
    tab = jax.ShapeDtypeStruct((SSM_CHUNK, N_STATES), BF16)
    cmat = jax.ShapeDtypeStruct((N_STATES, HALF_WIDTH), BF16)
    bmat = jax.ShapeDtypeStruct((SSM_WIDTH, 2 * HALF_STATES), BF16)
    lam = jax.ShapeDtypeStruct((2, N_STATES), F32)
    return pl.pallas_call(
        _ssm_prep_kernel,
        grid=(PREP_STEPS,),
        in_specs=[whole(p.shape) for p in params] + [slab(w_in)],
        out_specs=[whole(bmat.shape), whole(cmat.shape), whole(cmat.shape), slab(tab), slab(tab), slab(tab),
                   slab(tab), whole(lam.shape), slab(w_in)],
        out_shape=[bmat, cmat, cmat, tab, tab, tab, tab, lam, jax.ShapeDtypeStruct(w_in.shape, BF16)],
        compiler_params=_params("arbitrary"),
        name="ssm_prep",
    )(*params, w_in)


def _gelu_tanh(x):
    return 0.5 * x * (1.0 + jnp.tanh(math.sqrt(2.0 / math.pi) * (x + 0.044715 * (x * x * x))))


def _ssm_kernel(u_ref, bmat_ref, cre_ref, cim_ref, tri_ref, e_re_ref, e_im_ref, einv_re_ref, einv_im_ref,
                lam_ref, d_ref, y_ref, h_re_scr, h_im_scr, carry_scr):
    @pl.when(pl.program_id(0) == 0)
    def _():
        carry_scr[...] = jnp.zeros_like(carry_scr)

    u = u_ref[...]
    ub = u.astype(BF16)
    for hf in range(SSM_HALVES):
        chans = slice(hf * HALF_WIDTH, (hf + 1) * HALF_WIDTH)
        y = d_ref[:, chans] * u[:, chans]
        for blk in range(HALF_STATES // STATE_BLOCK):
            local = slice(blk * STATE_BLOCK, (blk + 1) * STATE_BLOCK)
            local_im = slice(HALF_STATES + blk * STATE_BLOCK, HALF_STATES + (blk + 1) * STATE_BLOCK)
            states = slice(hf * HALF_STATES + blk * STATE_BLOCK, hf * HALF_STATES + (blk + 1) * STATE_BLOCK)
            bu_re_all = _bdot(ub[:, chans], bmat_ref[chans, local])
            bu_im_all = _bdot(ub[:, chans], bmat_ref[chans, local_im])
            lam_re = lam_ref[0:1, states]
            lam_im = lam_ref[1:2, states]
            carry_re = carry_scr[0:1, states]
            carry_im = carry_scr[1:2, states]
            for c in range(u.shape[0] // SSM_CHUNK):
                rows = slice(c * SSM_CHUNK, (c + 1) * SSM_CHUNK)
                bu_re = bu_re_all[rows].astype(BF16)
                bu_im = bu_im_all[rows].astype(BF16)
                einv_re = einv_re_ref[:, states]
                einv_im = einv_im_ref[:, states]
                x = jnp.concatenate([bu_re * einv_re - bu_im * einv_im,
                                     bu_re * einv_im + bu_im * einv_re], axis=1)
                a = _bdot(tri_ref[...], x)
                a_re = a[:, :STATE_BLOCK] + carry_re
                a_im = a[:, STATE_BLOCK:] + carry_im
                e_re = e_re_ref[:, states]
                e_im = e_im_ref[:, states]
                a_re_b = a_re.astype(BF16)
                a_im_b = a_im.astype(BF16)
                h_re_scr[rows, :] = e_re * a_re_b - e_im * a_im_b
                h_im_scr[rows, :] = e_re * a_im_b + e_im * a_re_b
                last_re = a_re[SSM_CHUNK - 1:SSM_CHUNK, :]
                last_im = a_im[SSM_CHUNK - 1:SSM_CHUNK, :]
                carry_re = lam_re * last_re - lam_im * last_im
                carry_im = lam_re * last_im + lam_im * last_re
            carry_scr[0:1, states] = carry_re
            carry_scr[1:2, states] = carry_im
            y = y + _bdot(h_re_scr[...], cre_ref[states, :]) - _bdot(h_im_scr[...], cim_ref[states, :])
        y_ref[:, chans] = _gelu_tanh(y).astype(BF16)


def _ssm(u, bmat, cre, cim, tri, e_re, e_im, einv_re, einv_im, lam, d_row):
    s = u.shape[0]
    tm = min(SSM_ROWS, s)
    row = pl.BlockSpec((tm, SSM_WIDTH), lambda i: (i, 0))
    consts = (bmat, cre, cim, tri, e_re, e_im, einv_re, einv_im, lam, d_row)
    return pl.pallas_call(
        _ssm_kernel,
        grid=(s // tm,),
        in_specs=[row] + [_resident(c.shape) for c in consts],
        out_specs=row,
        out_shape=jax.ShapeDtypeStruct((s, SSM_WIDTH), BF16),
        scratch_shapes=[pltpu.VMEM((tm, STATE_BLOCK), BF16), pltpu.VMEM((tm, STATE_BLOCK), BF16),
                        pltpu.VMEM((2, N_STATES), F32)],
        compiler_params=_params("arbitrary"),
        name="ssm",
    )(u, *consts)


def _mix_kernel(*refs, n_cast):
    x_ref, o0_ref, o1_ref, o2_ref, l0_ref, l1_ref, l2_ref, y_ref, gates_ref = refs[:9]
    wap_ref, wa_ref, wb_ref, wout_ref = refs[9:13]
    cast_in, h_ref, cast_out = refs[13:13 + n_cast], refs[13 + n_cast], refs[14 + n_cast:]
    _cast_blocks(cast_in, cast_out)
    for rb in range(x_ref.shape[0] // MIX_SUB_ROWS):
        rows = slice(rb * MIX_SUB_ROWS, (rb + 1) * MIX_SUB_ROWS)
        ls = (l0_ref[rows, :], l1_ref[rows, :], l2_ref[rows, :])
        m_max = jnp.maximum(jnp.maximum(ls[0], ls[1]), ls[2])
        es = [jnp.exp2(l - m_max) for l in ls]
        dens = [pltpu.roll(l, HEAD_DIM - STAT_LANES // 2, 1) for l in ls]
        inv = 1.0 / (es[0] * dens[0] + es[1] * dens[1] + es[2] * dens[2])
        heads = []
        for h in range(HEADS_PER_GROUP):
            hs = slice(h * HEAD_DIM, (h + 1) * HEAD_DIM)
            col = slice(h * STAT_LANES, h * STAT_LANES + 1)
            heads.append(sum((e * inv)[:, col] * o_ref[rows, hs].astype(F32)
                             for e, o_ref in zip(es, (o0_ref, o1_ref, o2_ref))))
        attn = jnp.concatenate(heads, axis=1).astype(BF16)
        attn_d = _bdot(attn, wap_ref[...])
        y = y_ref[rows, :]
        ssm_d = _bdot(y, wa_ref[...]) * jax.nn.sigmoid(_bdot(y, wb_ref[...]))
        mix = (gates_ref[rows, :D_MODEL].astype(F32) * attn_d
               + gates_ref[rows, D_MODEL:].astype(F32) * ssm_d)
        h_ref[rows, :] = x_ref[rows, :] + _bdot(mix.astype(BF16), wout_ref[...])


def _mix(x, os, ls, y, gates, wap, wa, wb, wout, later_weights):
    s = x.shape[0]
    tm = min(MIX_ROWS, s)
    row = lambda w: pl.BlockSpec((tm, w), lambda i: (i, 0))
    weights = (wap, wa, wb, wout)
    cast_specs, cast_shapes = _cast_plan(later_weights, s // tm)
    return pl.pallas_call(
        functools.partial(_mix_kernel, n_cast=len(later_weights)),
        grid=(s // tm,),
        in_specs=[row(D_MODEL)] + [row(GROUP_WIDTH)] * 3 + [row(HEAD_DIM)] * 3
        + [row(SSM_WIDTH), row(2 * D_MODEL)] + [_resident(w.shape) for w in weights] + cast_specs,
        out_specs=[row(D_MODEL)] + cast_specs,
        out_shape=[jax.ShapeDtypeStruct((s, D_MODEL), F32)] + cast_shapes,
        compiler_params=_params("parallel"),
        name="mix",
    )(x, *os, *ls, y, gates, *weights, *later_weights)


def _ffn_kernel(h_ref, p_ref, gffn_ref, wg_ref, wu_ref, wd_ref, wpg_ref, wpp_ref, gfin_ref,
                out_ref, acc_scr):
    for rb in range(h_ref.shape[0] // FFN_SUB_ROWS):
        rows = slice(rb * FFN_SUB_ROWS, (rb + 1) * FFN_SUB_ROWS)
        h = h_ref[rows, :]
        n2 = _rmsnorm(h, gffn_ref[...]).astype(BF16)
        for idx, (c0, width) in enumerate(FFN_CHUNKS):
            gate = _bdot(n2, wg_ref[:, c0:c0 + width])
            up = _bdot(n2, wu_ref[:, c0:c0 + width])
            act = (gate * jax.nn.sigmoid(gate) * up).astype(BF16)
            part = _bdot(act, wd_ref[c0:c0 + width, :])
            if idx == 0:
                acc_scr[rows, :] = h + part
            else:
                acc_scr[rows, :] += part
        h2 = acc_scr[rows, :]
        ple = (jax.nn.sigmoid(_bdot(h2.astype(BF16), wpg_ref[...]))
               * _bdot(p_ref[rows, :].astype(BF16), wpp_ref[...]))
        out_ref[rows, :] = _rmsnorm(h2 + ple, gfin_ref[...])


def _ffn(h, p, g_ffn, wg, wu, wd, wpg, wpp, g_final):
    s = h.shape[0]
    tm = min(FFN_ROWS, s)
    row = lambda w: pl.BlockSpec((tm, w), lambda i: (i, 0))
    consts = (g_ffn, wg, wu, wd, wpg, wpp, g_final)
    return pl.pallas_call(
        _ffn_kernel,
        grid=(s // tm,),
        in_specs=[row(D_MODEL), row(PLE_DIM)] + [_resident(c.shape) for c in consts],
        out_specs=row(D_MODEL),
        out_shape=jax.ShapeDtypeStruct((s, D_MODEL), F32),
        scratch_shapes=[pltpu.VMEM((tm, D_MODEL), F32)],
        compiler_params=_params("parallel"),
        name="ffn",
    )(h, p, *consts)


def _layer(x, p, positions, g_mix, w_in, a_re, a_im, log_dt, b_re, b_im, c_re, c_im, d_skip,
           w_attn_proj, w_glu_a, w_glu_b, w_out, g_ffn, w_ffn_gate, w_ffn_up, w_ffn_down,
           w_ple_gate, w_ple_proj, g_final):
    row = lambda v: v.reshape(1, -1).astype(F32)

    inv_freq = ROPE_THETA ** (-jnp.arange(ROPE_HALF, dtype=F32) * 2.0 / ROPE_DIM)
    bmat, cre, cim, e_re, e_im, einv_re, einv_im, lam, w_in_bf = _ssm_prep(
        a_re, a_im, log_dt, b_re, b_im, c_re, c_im, w_in)
    outs = _proj(x, positions.reshape(1, -1), row(g_mix), inv_freq.reshape(ROPE_HALF, 1), w_in_bf,
                 (w_attn_proj, w_glu_a, w_glu_b, w_out, w_ffn_gate, w_ffn_up, w_ple_gate))
    qkvs, (u, gates) = outs[:N_GROUPS], outs[N_GROUPS:N_GROUPS + 2]
    wap, wga, wgb, wout, wfg, wfu, wpg = outs[N_GROUPS + 2:]

    attn_os, attn_ls = zip(*[_attn_group(qkvs[g], d) for g, d in enumerate(ATTN_DILATIONS)])

    tri = jnp.tril(jnp.ones((SSM_CHUNK, SSM_CHUNK), F32)).astype(BF16)
    y = _ssm(u, bmat, cre, cim, tri, e_re, e_im, einv_re, einv_im, lam, row(d_skip))

    h1, wfd, wpp = _mix(x, attn_os, attn_ls, y, gates, wap, wga, wgb, wout, (w_ffn_down, w_ple_proj))
    return _ffn(h1, p, row(g_ffn), wfg, wfu, wfd, wpg, wpp, row(g_final))


def kernel(x, p, positions, g_mix, w_in, a_re, a_im, log_dt, b_re, b_im, c_re, c_im, d_skip,
           w_attn_proj, w_glu_a, w_glu_b, w_out, g_ffn, w_ffn_gate, w_ffn_up, w_ffn_down,
           w_ple_gate, w_ple_proj, g_final):
    batch, depth = x.shape[0], p.shape[0]
    assert batch == 1 and depth == 1, "kernel supports the stated BATCH=1, DEPTH=1 problem"
    out = _layer(x[0], p[0, 0], positions[0], g_mix[0], w_in[0], a_re[0], a_im[0], log_dt[0],
                 b_re[0], b_im[0], c_re[0], c_im[0], d_skip[0], w_attn_proj[0], w_glu_a[0],
                 w_glu_b[0], w_out[0], g_ffn[0], w_ffn_gate[0], w_ffn_up[0], w_ffn_down[0],
                 w_ple_gate[0], w_ple_proj[0], g_final)
    return out[None]
```

```python
import functools
import math

import jax
import jax.numpy as jnp
from jax import lax
from jax.experimental import pallas as pl
from jax.experimental.pallas import tpu as pltpu

F32 = jnp.float32
BF16 = jnp.bfloat16

D_MODEL = 1024
HEAD_DIM = 128
HEADS_PER_GROUP = 4
GROUP_WIDTH = HEADS_PER_GROUP * HEAD_DIM
ATTN_DILATIONS = (1, 4, 16)
N_GROUPS = len(ATTN_DILATIONS)
QK_WIDTH = N_GROUPS * GROUP_WIDTH
BLOCK = 128
ROPE_THETA = 500000.0
ROPE_DIM = HEAD_DIM // 4
ROPE_HALF = ROPE_DIM // 2
SSM_WIDTH = 512
SSM_GROUP = 16
SSM_GROUPS = SSM_WIDTH // SSM_GROUP
SSM_STATE = 64
N_STATES = SSM_GROUPS * SSM_STATE
SSM_HALVES = 2
HALF_GROUPS = SSM_GROUPS // SSM_HALVES
HALF_WIDTH = SSM_WIDTH // SSM_HALVES
HALF_STATES = N_STATES // SSM_HALVES
STATE_BLOCK = 256
D_FF = 2816
PLE_DIM = 256
EPS = 1e-6
MASK_VALUE = -1e30

V7X_VMEM_LIMIT_BYTES = 56 * 1024 * 1024

PROJ_ROWS = 512
PROJ_SUB_ROWS = 256
ATTN_TOKENS = 2048
STAT_LANES = HEAD_DIM // HEADS_PER_GROUP
SCATTER_STRIDE = 4
SSM_CHUNK = 64
SSM_ROWS = 512
PREP_STEPS = 2
MIX_ROWS = 1024
MIX_SUB_ROWS = 256
FFN_ROWS = 1024
FFN_SUB_ROWS = 256
FFN_CHUNKS = ((0, 1024), (1024, 1024), (2048, 768))


def _resident(shape):
    return pl.BlockSpec(shape, lambda *_: (0,) * len(shape), pipeline_mode=pl.Buffered(1))


def _params(*semantics):
    return pltpu.CompilerParams(dimension_semantics=semantics,
                                vmem_limit_bytes=V7X_VMEM_LIMIT_BYTES)


def _rmsnorm(x, g):
    return (x * lax.rsqrt(jnp.mean(x * x, axis=-1, keepdims=True) + EPS)) * g


def _bdot(a, b):
    return jnp.dot(a, b, preferred_element_type=F32)


def _cast_plan(weights, steps):
    specs = [pl.BlockSpec((w.shape[0] // steps, w.shape[1]), lambda i: (i, 0)) for w in weights]
    shapes = [jax.ShapeDtypeStruct(w.shape, BF16) for w in weights]
    return specs, shapes


def _cast_blocks(in_refs, out_refs):
    for src, dst in zip(in_refs, out_refs):
        dst[...] = src[...].astype(BF16)


def _proj_kernel(*refs, n_cast):
    x_ref, pos_ref, g_ref, invf_ref, w_ref = refs[:5]
    cast_in, refs = refs[5:5 + n_cast], refs[5 + n_cast:]
    qkv0, qkv1, qkv2, u_ref, gates_ref = refs[:5]
    cast_out, (n_scr, nperm_scr, cos_scr, sin_scr) = refs[5:5 + n_cast], refs[5 + n_cast:]
    _cast_blocks(cast_in, cast_out)
    rows = PROJ_SUB_ROWS
    lane = lax.broadcasted_iota(jnp.int32, (rows, HEAD_DIM), 1)
    first_half = lane < ROPE_HALF
    scale = math.log2(math.e) / math.sqrt(HEAD_DIM)

    for sb in range(x_ref.shape[0] // rows):
        base = sb * rows
        tile_rows = slice(base, base + rows)
        xn = _rmsnorm(x_ref[tile_rows, :], g_ref[...])
        n = xn.astype(BF16)
        for c in range(D_MODEL // HEAD_DIM):
            n_scr[c, tile_rows, :] = xn[:, c * HEAD_DIM:(c + 1) * HEAD_DIM]

        ang = invf_ref[...] * pos_ref[:, tile_rows].astype(F32)
        cos_t = jnp.cos(ang)
        sin_t = jnp.sin(ang)
        rest = (HEAD_DIM - ROPE_DIM, rows)
        cos_scr[tile_rows, :] = jnp.concatenate([cos_t, cos_t, jnp.ones(rest, F32)], axis=0).T
        sin_scr[tile_rows, :] = jnp.concatenate([-sin_t, sin_t, jnp.zeros(rest, F32)], axis=0).T

        for g, (d, qkv_ref) in enumerate(zip(ATTN_DILATIONS, (qkv0, qkv1, qkv2))):
            def residue_major(ref_2d):
                if d == 1:
                    return ref_2d[tile_rows, :]
                return jnp.concatenate(
                    [ref_2d[pl.ds(base + r, rows // d, stride=d), :] for r in range(d)], axis=0)

            cos = residue_major(cos_scr)
            sin_signed = residue_major(sin_scr)
            if d == 1:
                ng = n
            else:
                for c in range(D_MODEL // HEAD_DIM):
                    nperm_scr[tile_rows, c * HEAD_DIM:(c + 1) * HEAD_DIM] = (
                        residue_major(n_scr.at[c]).astype(BF16))
                ng = nperm_scr[tile_rows, :]

            def rotary(t):
                partner = jnp.where(first_half,
                                    pltpu.roll(t, HEAD_DIM - ROPE_HALF, 1),
                                    pltpu.roll(t, ROPE_HALF, 1))
                return t * cos + partner * sin_signed

            c0 = g * GROUP_WIDTH
            zq = _bdot(ng, w_ref[:, c0:c0 + GROUP_WIDTH])
            zk = _bdot(ng, w_ref[:, QK_WIDTH + c0:QK_WIDTH + c0 + GROUP_WIDTH])
            zv = _bdot(ng, w_ref[:, 2 * QK_WIDTH + c0:2 * QK_WIDTH + c0 + GROUP_WIDTH])
            sub_rows = slice(base // d, (base + rows) // d)
            for h in range(HEADS_PER_GROUP):
                hs = slice(h * HEAD_DIM, (h + 1) * HEAD_DIM)
                ks = slice(GROUP_WIDTH + h * HEAD_DIM, GROUP_WIDTH + (h + 1) * HEAD_DIM)
                qkv_ref[:, sub_rows, hs] = (
                    (rotary(zq[:, hs]) * scale).astype(BF16).reshape(d, rows // d, HEAD_DIM))
                qkv_ref[:, sub_rows, ks] = rotary(zk[:, hs]).astype(BF16).reshape(d, rows // d, HEAD_DIM)
            qkv_ref[:, sub_rows, 2 * GROUP_WIDTH:] = zv.astype(BF16).reshape(d, rows // d, GROUP_WIDTH)
        o1 = 3 * QK_WIDTH
        u_ref[tile_rows, :] = _bdot(n, w_ref[:, o1:o1 + SSM_WIDTH])
        o2 = o1 + SSM_WIDTH
        gates_ref[tile_rows, :] = jax.nn.sigmoid(_bdot(n, w_ref[:, o2:o2 + 2 * D_MODEL])).astype(BF16)


def _proj(x, pos_row, g_mix, inv_freq_col, w_in, later_weights):
    s = x.shape[0]
    tm = min(PROJ_ROWS, s)
    row = lambda w: pl.BlockSpec((tm, w), lambda i: (i, 0))
    cast_specs, cast_shapes = _cast_plan(later_weights, s // tm)
    grp_specs, grp_shapes = [], []
    for d in ATTN_DILATIONS:
        grp_specs.append(pl.BlockSpec((d, tm // d, 3 * GROUP_WIDTH), lambda i: (0, i, 0)))
        grp_shapes.append(jax.ShapeDtypeStruct((d, s // d, 3 * GROUP_WIDTH), BF16))
    return pl.pallas_call(
        functools.partial(_proj_kernel, n_cast=len(later_weights)),
        grid=(s // tm,),
        in_specs=[row(D_MODEL), pl.BlockSpec((1, tm), lambda i: (0, i)), _resident((1, D_MODEL)),
                  _resident((ROPE_HALF, 1)), _resident(w_in.shape)] + cast_specs,
        out_specs=grp_specs + [row(SSM_WIDTH), row(2 * D_MODEL)] + cast_specs,
        out_shape=grp_shapes + [jax.ShapeDtypeStruct((s, SSM_WIDTH), F32),
                                jax.ShapeDtypeStruct((s, 2 * D_MODEL), BF16)] + cast_shapes,
        scratch_shapes=[pltpu.VMEM((D_MODEL // HEAD_DIM, tm, HEAD_DIM), F32), pltpu.VMEM((tm, D_MODEL), BF16),
                        pltpu.VMEM((tm, HEAD_DIM), F32), pltpu.VMEM((tm, HEAD_DIM), F32)],
        compiler_params=_params("parallel"),
        name="proj",
    )(x, pos_row, g_mix, inv_freq_col, w_in, *later_weights)


def _attn_kernel(qkv_ref, o_ref, l_ref, prev_scr, *scatter_scr, d, nsub):
    not_first_block = pl.program_id(0) > 0
    two_pass = d > SCATTER_STRIDE
    if two_pass:
        o_scr, l_scr, part_o, part_l = scatter_scr
    elif d > 1:
        o_scr, l_scr = scatter_scr

    @pl.when(pl.program_id(0) == 0)
    def _():
        prev_scr[...] = jnp.zeros_like(prev_scr)

    qi = lax.broadcasted_iota(jnp.int32, (BLOCK, 2 * BLOCK), 0)
    kj = lax.broadcasted_iota(jnp.int32, (BLOCK, 2 * BLOCK), 1)
    rel = BLOCK + qi - kj
    band = (rel >= 0) & (rel <= BLOCK)
    band_first = band & ((kj >= BLOCK) | not_first_block)
    lane = lax.broadcasted_iota(jnp.int32, (BLOCK, HEAD_DIM), 1)

    def residue(r):
        for b in range(nsub):
            rows = slice(b * BLOCK, (b + 1) * BLOCK)
            mask = band_first if b == 0 else band
            token_rows = pl.ds(b * BLOCK * d + r, BLOCK, stride=d) if d > 1 else rows
            if two_pass:
                r_hi, r_lo = divmod(r, SCATTER_STRIDE)
                part_rows = pl.ds(b * BLOCK * (d // SCATTER_STRIDE) + r_hi, BLOCK, stride=d // SCATTER_STRIDE)
            for h in range(HEADS_PER_GROUP):
                hs = slice(h * HEAD_DIM, (h + 1) * HEAD_DIM)
                ks = slice(GROUP_WIDTH + h * HEAD_DIM, GROUP_WIDTH + (h + 1) * HEAD_DIM)
                vs = slice(2 * GROUP_WIDTH + h * HEAD_DIM, 2 * GROUP_WIDTH + (h + 1) * HEAD_DIM)
                q = qkv_ref[r, rows, hs]
                if b == 0:
                    prev_k = slice(h * HEAD_DIM, (h + 1) * HEAD_DIM)
                    prev_v = slice(GROUP_WIDTH + h * HEAD_DIM, GROUP_WIDTH + (h + 1) * HEAD_DIM)
                    kw = jnp.concatenate([prev_scr[r, :, prev_k], qkv_ref[r, rows, ks]], axis=0)
                    vw = jnp.concatenate([prev_scr[r, :, prev_v], qkv_ref[r, rows, vs]], axis=0)
                else:
                    win = slice((b - 1) * BLOCK, (b + 1) * BLOCK)
                    kw = qkv_ref[r, win, ks]
                    vw = qkv_ref[r, win, vs]
                s = lax.dot_general(q, kw, (((1,), (1,)), ((), ())), preferred_element_type=F32)
                s = jnp.where(mask, s, MASK_VALUE)
                m = jnp.max(s, axis=-1, keepdims=True)
                p = jnp.exp2(s - m)
                den = jnp.sum(p, axis=-1, keepdims=True)
                pv = _bdot(p.astype(BF16), vw)
                if two_pass:
                    part_o[r_lo, h, part_rows, :] = pv
                elif d > 1:
                    o_scr[h, token_rows, :] = pv
                else:
                    o_ref[rows, hs] = pv.astype(BF16)
                m_b = jnp.broadcast_to(m, (BLOCK, HEAD_DIM))
                den_b = jnp.broadcast_to(den, (BLOCK, HEAD_DIM))
                if h > 0:
                    m_b = jnp.where(lane >= h * STAT_LANES, m_b, stats)
                stats = jnp.where(lane >= h * STAT_LANES + STAT_LANES // 2, den_b, m_b)
            if two_pass:
                part_l[r_lo, part_rows, :] = stats
            elif d > 1:
                l_scr[token_rows, :] = stats
            else:
                l_ref[rows, :] = stats

    for r in range(d):
        residue(r)

    if two_pass:
        part_len = part_l.shape[1]
        for r_lo in range(SCATTER_STRIDE):
            token_rows = pl.ds(r_lo, part_len, stride=SCATTER_STRIDE)
            for h in range(HEADS_PER_GROUP):
                o_scr[h, token_rows, :] = part_o[r_lo, h]
            l_scr[token_rows, :] = part_l[r_lo]
    if d > 1:
        for h in range(HEADS_PER_GROUP):
            o_ref[:, h * HEAD_DIM:(h + 1) * HEAD_DIM] = o_scr[h].astype(BF16)
        l_ref[...] = l_scr[...]
    prev_scr[...] = qkv_ref[:, (nsub - 1) * BLOCK:nsub * BLOCK, GROUP_WIDTH:]


def _attn_group(qkv, dilation):
    sub_len = qkv.shape[1]
    s = sub_len * dilation
    step_tokens = min(ATTN_TOKENS, s)
    qb = step_tokens // dilation
    nsub = qb // BLOCK
    cur = pl.BlockSpec((dilation, qb, 3 * GROUP_WIDTH), lambda i: (0, i, 0))
    scatter_scratch = [] if dilation == 1 else [
        pltpu.VMEM((HEADS_PER_GROUP, step_tokens, HEAD_DIM), F32), pltpu.VMEM((step_tokens, HEAD_DIM), F32)]
    if dilation > SCATTER_STRIDE:
        part = step_tokens // SCATTER_STRIDE
        scatter_scratch += [pltpu.VMEM((SCATTER_STRIDE, HEADS_PER_GROUP, part, HEAD_DIM), F32),
                            pltpu.VMEM((SCATTER_STRIDE, part, HEAD_DIM), F32)]
    return pl.pallas_call(
        functools.partial(_attn_kernel, d=dilation, nsub=nsub),
        grid=(s // step_tokens,),
        in_specs=[cur],
        out_specs=[pl.BlockSpec((step_tokens, GROUP_WIDTH), lambda i: (i, 0)),
                   pl.BlockSpec((step_tokens, HEAD_DIM), lambda i: (i, 0))],
        out_shape=[jax.ShapeDtypeStruct((s, GROUP_WIDTH), BF16), jax.ShapeDtypeStruct((s, HEAD_DIM), F32)],
        scratch_shapes=[pltpu.VMEM((dilation, BLOCK, 2 * GROUP_WIDTH), BF16)] + scatter_scratch,
        compiler_params=_params("arbitrary"),
        name=f"attn_d{dilation}",
    )(qkv)


def _discretize(lr, li, log_dt):
    dt = jnp.exp(log_dt)
    mag = jnp.exp(lr * dt)
    bar_re = mag * jnp.cos(li * dt)
    bar_im = mag * jnp.sin(li * dt)
    nr = bar_re - 1.0
    ni = bar_im
    den = lr * lr + li * li
    return bar_re, bar_im, (nr * lr + ni * li) / den, (ni * lr - nr * li) / den


def _expand_block_diag(compact, n_blocks):
    rows, b = compact.shape
    a = rows // n_blocks
    wide = n_blocks * b
    src_lane = lax.broadcasted_iota(jnp.int32, (b, wide), 0)
    dst_lane = lax.broadcasted_iota(jnp.int32, (b, wide), 1)
    tiled = _bdot(compact, (dst_lane % b == src_lane).astype(BF16))
    row_block = lax.broadcasted_iota(jnp.int32, (rows, wide), 0) // a
    col_block = lax.broadcasted_iota(jnp.int32, (rows, wide), 1) // b
    return jnp.where(row_block == col_block, tiled, 0.0).astype(BF16)


def _ssm_prep_kernel(lr_ref, li_ref, logdt_ref, lr_rep_ref, li_rep_ref, logdt_rep_ref, b_re_ref, b_im_ref,
                     c_re_ref, c_im_ref, w_in_ref,
                     bmat_ref, cre_ref, cim_ref, e_re_ref, e_im_ref, einv_re_ref, einv_im_ref, lam_ref,
                     w_in_bf_ref):
    w_in_bf_ref[...] = w_in_ref[...].astype(BF16)
    lr = lr_ref[...]
    li = li_ref[...]
    dt = jnp.exp(logdt_ref[...])

    @pl.when(pl.program_id(0) == 0)
    def _():
        _, _, z_re, z_im = _discretize(lr_rep_ref[...], li_rep_ref[...], logdt_rep_ref[...])
        b_re = b_re_ref[...]
        b_im = b_im_ref[...]
        bb_re = (z_re * b_re - z_im * b_im).astype(BF16)
        bb_im = (z_re * b_im + z_im * b_re).astype(BF16)
        for hf in range(SSM_HALVES):
            rows = slice(hf * HALF_WIDTH, (hf + 1) * HALF_WIDTH)
            bmat_ref[rows, :HALF_STATES] = _expand_block_diag(bb_re[rows], HALF_GROUPS)
            bmat_ref[rows, HALF_STATES:] = _expand_block_diag(bb_im[rows], HALF_GROUPS)
            srows = slice(hf * HALF_STATES, (hf + 1) * HALF_STATES)
            cre_ref[srows, :] = _expand_block_diag(c_re_ref[srows, :].astype(BF16), HALF_GROUPS)
            cim_ref[srows, :] = _expand_block_diag(c_im_ref[srows, :].astype(BF16), HALF_GROUPS)
        whole = float(SSM_CHUNK)
        chunk_mag = jnp.exp(whole * (lr * dt))
        lam_ref[0:1, :] = chunk_mag * jnp.cos(whole * (li * dt))
        lam_ref[1:2, :] = chunk_mag * jnp.sin(whole * (li * dt))

    rows_per_step = e_re_ref.shape[0]
    t = (lax.broadcasted_iota(jnp.int32, (rows_per_step, N_STATES), 0)
         + pl.program_id(0) * rows_per_step).astype(F32)
    grow = jnp.exp(t * (lr * dt))
    theta = t * (li * dt)
    c = jnp.cos(theta)
    s = jnp.sin(theta)
    e_re_ref[...] = (grow * c).astype(BF16)
    e_im_ref[...] = (grow * s).astype(BF16)
    shrink = jnp.exp(-t * (lr * dt))
    einv_re_ref[...] = (shrink * c).astype(BF16)
    einv_im_ref[...] = (-(shrink * s)).astype(BF16)


def _ssm_prep(a_re, a_im, log_dt, b_re, b_im, c_re, c_im, w_in):
    row = lambda v: v.reshape(1, N_STATES)
    rep = lambda v: jnp.repeat(v, SSM_GROUP, axis=0)
    logdt_gp = jnp.broadcast_to(log_dt[:, None], (SSM_GROUPS, SSM_STATE))
    chan_state = lambda b: jnp.swapaxes(b, 1, 2).reshape(SSM_WIDTH, SSM_STATE)
    state_chan = lambda c: jnp.swapaxes(c, 1, 2).reshape(N_STATES, SSM_GROUP)
    params = (row(a_re), row(a_im), row(logdt_gp), rep(a_re), rep(a_im), rep(logdt_gp),
              chan_state(b_re), chan_state(b_im), state_chan(c_re), state_chan(c_im))
    whole = lambda shape: pl.BlockSpec(shape, lambda i: (0,) * len(shape))
    slab = lambda arr: pl.BlockSpec((arr.shape[0] // PREP_STEPS, arr.shape[1]), lambda i: (i, 0))
    tab = jax.ShapeDtypeStruct((SSM_CHUNK, N_STATES), BF16)
    cmat = jax.ShapeDtypeStruct((N_STATES, HALF_WIDTH), BF16)
    bmat = jax.ShapeDtypeStruct((SSM_WIDTH, 2 * HALF_STATES), BF16)
    lam = jax.ShapeDtypeStruct((2, N_STATES), F32)
    return pl.pallas_call(
        _ssm_prep_kernel,
        grid=(PREP_STEPS,),
        in_specs=[whole(p.shape) for p in params] + [slab(w_in)],
        out_specs=[whole(bmat.shape), whole(cmat.shape), whole(cmat.shape), slab(tab), slab(tab), slab(tab),
                   slab(tab), whole(lam.shape), slab(w_in)],
        out_shape=[bmat, cmat, cmat, tab, tab, tab, tab, lam, jax.ShapeDtypeStruct(w_in.shape, BF16)],
        compiler_params=_params("arbitrary"),
        name="ssm_prep",
    )(*params, w_in)


def _gelu_tanh(x):
    return 0.5 * x * (1.0 + jnp.tanh(math.sqrt(2.0 / math.pi) * (x + 0.044715 * (x * x * x))))


def _ssm_kernel(u_ref, bmat_ref, cre_ref, cim_ref, tri_ref, e_re_ref, e_im_ref, einv_re_ref, einv_im_ref,
                lam_ref, d_ref, y_ref, h_re_scr, h_im_scr, carry_scr):
    @pl.when(pl.program_id(0) == 0)
    def _():
        carry_scr[...] = jnp.zeros_like(carry_scr)

    u = u_ref[...]
    ub = u.astype(BF16)
    for hf in range(SSM_HALVES):
        chans = slice(hf * HALF_WIDTH, (hf + 1) * HALF_WIDTH)
        y = d_ref[:, chans] * u[:, chans]
        for blk in range(HALF_STATES // STATE_BLOCK):
            local = slice(blk * STATE_BLOCK, (blk + 1) * STATE_BLOCK)
            local_im = slice(HALF_STATES + blk * STATE_BLOCK, HALF_STATES + (blk + 1) * STATE_BLOCK)
            states = slice(hf * HALF_STATES + blk * STATE_BLOCK, hf * HALF_STATES + (blk + 1) * STATE_BLOCK)
            bu_re_all = _bdot(ub[:, chans], bmat_ref[chans, local])
            bu_im_all = _bdot(ub[:, chans], bmat_ref[chans, local_im])
            lam_re = lam_ref[0:1, states]
            lam_im = lam_ref[1:2, states]
            carry_re = carry_scr[0:1, states]
            carry_im = carry_scr[1:2, states]
            for c in range(u.shape[0] // SSM_CHUNK):
                rows = slice(c * SSM_CHUNK, (c + 1) * SSM_CHUNK)
                bu_re = bu_re_all[rows].astype(BF16)
                bu_im = bu_im_all[rows].astype(BF16)
                einv_re = einv_re_ref[:, states]
                einv_im = einv_im_ref[:, states]
                x = jnp.concatenate([bu_re * einv_re - bu_im * einv_im,
                                     bu_re * einv_im + bu_im * einv_re], axis=1)
                a = _bdot(tri_ref[...], x)
                a_re = a[:, :STATE_BLOCK] + carry_re
                a_im = a[:, STATE_BLOCK:] + carry_im
                e_re = e_re_ref[:, states]
                e_im = e_im_ref[:, states]
                a_re_b = a_re.astype(BF16)
                a_im_b = a_im.astype(BF16)
                h_re_scr[rows, :] = e_re * a_re_b - e_im * a_im_b
                h_im_scr[rows, :] = e_re * a_im_b + e_im * a_re_b
                last_re = a_re[SSM_CHUNK - 1:SSM_CHUNK, :]
                last_im = a_im[SSM_CHUNK - 1:SSM_CHUNK, :]
                carry_re = lam_re * last_re - lam_im * last_im
                carry_im = lam_re * last_im + lam_im * last_re
            carry_scr[0:1, states] = carry_re
            carry_scr[1:2, states] = carry_im
            y = y + _bdot(h_re_scr[...], cre_ref[states, :]) - _bdot(h_im_scr[...], cim_ref[states, :])
        y_ref[:, chans] = _gelu_tanh(y).astype(BF16)


def _ssm(u, bmat, cre, cim, tri, e_re, e_im, einv_re, einv_im, lam, d_row):
    s = u.shape[0]
    tm = min(SSM_ROWS, s)
    row = pl.BlockSpec((tm, SSM_WIDTH), lambda i: (i, 0))
    consts = (bmat, cre, cim, tri, e_re, e_im, einv_re, einv_im, lam, d_row)
    return pl.pallas_call(
        _ssm_kernel,
        grid=(s // tm,),
        in_specs=[row] + [_resident(c.shape) for c in consts],
        out_specs=row,
        out_shape=jax.ShapeDtypeStruct((s, SSM_WIDTH), BF16),
        scratch_shapes=[pltpu.VMEM((tm, STATE_BLOCK), BF16), pltpu.VMEM((tm, STATE_BLOCK), BF16),
                        pltpu.VMEM((2, N_STATES), F32)],
        compiler_params=_params("arbitrary"),
        name="ssm",
    )(u, *consts)


def _mix_kernel(*refs, n_cast):
    x_ref, o0_ref, o1_ref, o2_ref, l0_ref, l1_ref, l2_ref, y_ref, gates_ref = refs[:9]
    wap_ref, wa_ref, wb_ref, wout_ref = refs[9:13]
    cast_in, h_ref, cast_out = refs[13:13 + n_cast], refs[13 + n_cast], refs[14 + n_cast:]
    _cast_blocks(cast_in, cast_out)
    for rb in range(x_ref.shape[0] // MIX_SUB_ROWS):
        rows = slice(rb * MIX_SUB_ROWS, (rb + 1) * MIX_SUB_ROWS)
        ls = (l0_ref[rows, :], l1_ref[rows, :], l2_ref[rows, :])
        m_max = jnp.maximum(jnp.maximum(ls[0], ls[1]), ls[2])
        es = [jnp.exp2(l - m_max) for l in ls]
        dens = [pltpu.roll(l, HEAD_DIM - STAT_LANES // 2, 1) for l in ls]
        inv = 1.0 / (es[0] * dens[0] + es[1] * dens[1] + es[2] * dens[2])
        heads = []
        for h in range(HEADS_PER_GROUP):
            hs = slice(h * HEAD_DIM, (h + 1) * HEAD_DIM)
            col = slice(h * STAT_LANES, h * STAT_LANES + 1)
            heads.append(sum((e * inv)[:, col] * o_ref[rows, hs].astype(F32)
                             for e, o_ref in zip(es, (o0_ref, o1_ref, o2_ref))))
        attn = jnp.concatenate(heads, axis=1).astype(BF16)
        attn_d = _bdot(attn, wap_ref[...])
        y = y_ref[rows, :]
        ssm_d = _bdot(y, wa_ref[...]) * jax.nn.sigmoid(_bdot(y, wb_ref[...]))
        mix = (gates_ref[rows, :D_MODEL].astype(F32) * attn_d
               + gates_ref[rows, D_MODEL:].astype(F32) * ssm_d)
        h_ref[rows, :] = x_ref[rows, :] + _bdot(mix.astype(BF16), wout_ref[...])


def _mix(x, os, ls, y, gates, wap, wa, wb, wout, later_weights):
    s = x.shape[0]
    tm = min(MIX_ROWS, s)
    row = lambda w: pl.BlockSpec((tm, w), lambda i: (i, 0))
    weights = (wap, wa, wb, wout)
    cast_specs, cast_shapes = _cast_plan(later_weights, s // tm)
    return pl.pallas_call(
        functools.partial(_mix_kernel, n_cast=len(later_weights)),
        grid=(s // tm,),
        in_specs=[row(D_MODEL)] + [row(GROUP_WIDTH)] * 3 + [row(HEAD_DIM)] * 3
        + [row(SSM_WIDTH), row(2 * D_MODEL)] + [_resident(w.shape) for w in weights] + cast_specs,
        out_specs=[row(D_MODEL)] + cast_specs,
        out_shape=[jax.ShapeDtypeStruct((s, D_MODEL), F32)] + cast_shapes,
        compiler_params=_params("parallel"),
        name="mix",
    )(x, *os, *ls, y, gates, *weights, *later_weights)


def _ffn_kernel(h_ref, p_ref, gffn_ref, wg_ref, wu_ref, wd_ref, wpg_ref, wpp_ref, gfin_ref,
                out_ref, acc_scr):
    for rb in range(h_ref.shape[0] // FFN_SUB_ROWS):
        rows = slice(rb * FFN_SUB_ROWS, (rb + 1) * FFN_SUB_ROWS)
        h = h_ref[rows, :]
        n2 = _rmsnorm(h, gffn_ref[...]).astype(BF16)
        for idx, (c0, width) in enumerate(FFN_CHUNKS):
            gate = _bdot(n2, wg_ref[:, c0:c0 + width])
            up = _bdot(n2, wu_ref[:, c0:c0 + width])
            act = (gate * jax.nn.sigmoid(gate) * up).astype(BF16)
            part = _bdot(act, wd_ref[c0:c0 + width, :])
            if idx == 0:
                acc_scr[rows, :] = h + part
            else:
                acc_scr[rows, :] += part
        h2 = acc_scr[rows, :]
        ple = (jax.nn.sigmoid(_bdot(h2.astype(BF16), wpg_ref[...]))
               * _bdot(p_ref[rows, :].astype(BF16), wpp_ref[...]))
        out_ref[rows, :] = _rmsnorm(h2 + ple, gfin_ref[...])


def _ffn(h, p, g_ffn, wg, wu, wd, wpg, wpp, g_final):
    s = h.shape[0]
    tm = min(FFN_ROWS, s)
    row = lambda w: pl.BlockSpec((tm, w), lambda i: (i, 0))
    consts = (g_ffn, wg, wu, wd, wpg, wpp, g_final)
    return pl.pallas_call(
        _ffn_kernel,
        grid=(s // tm,),
        in_specs=[row(D_MODEL), row(PLE_DIM)] + [_resident(c.shape) for c in consts],
        out_specs=row(D_MODEL),
        out_shape=jax.ShapeDtypeStruct((s, D_MODEL), F32),
        scratch_shapes=[pltpu.VMEM((tm, D_MODEL), F32)],
        compiler_params=_params("parallel"),
        name="ffn",
    )(h, p, *consts)


def _layer(x, p, positions, g_mix, w_in, a_re, a_im, log_dt, b_re, b_im, c_re, c_im, d_skip,
           w_attn_proj, w_glu_a, w_glu_b, w_out, g_ffn, w_ffn_gate, w_ffn_up, w_ffn_down,
           w_ple_gate, w_ple_proj, g_final):
    row = lambda v: v.reshape(1, -1).astype(F32)

    inv_freq = ROPE_THETA ** (-jnp.arange(ROPE_HALF, dtype=F32) * 2.0 / ROPE_DIM)
    bmat, cre, cim, e_re, e_im, einv_re, einv_im, lam, w_in_bf = _ssm_prep(
        a_re, a_im, log_dt, b_re, b_im, c_re, c_im, w_in)
    outs = _proj(x, positions.reshape(1, -1), row(g_mix), inv_freq.reshape(ROPE_HALF, 1), w_in_bf,
                 (w_attn_proj, w_glu_a, w_glu_b, w_out, w_ffn_gate, w_ffn_up, w_ple_gate))
    qkvs, (u, gates) = outs[:N_GROUPS], outs[N_GROUPS:N_GROUPS + 2]
    wap, wga, wgb, wout, wfg, wfu, wpg = outs[N_GROUPS + 2:]

    attn_os, attn_ls = zip(*[_attn_group(qkvs[g], d) for g, d in enumerate(ATTN_DILATIONS)])

    tri = jnp.tril(jnp.ones((SSM_CHUNK, SSM_CHUNK), F32)).astype(BF16)
    y = _ssm(u, bmat, cre, cim, tri, e_re, e_im, einv_re, einv_im, lam, row(d_skip))

    h1, wfd, wpp = _mix(x, attn_os, attn_ls, y, gates, wap, wga, wgb, wout, (w_ffn_down, w_ple_proj))
    return _ffn(h1, p, row(g_ffn), wfg, wfu, wfd, wpg, wpp, row(g_final))


def kernel(x, p, positions, g_mix, w_in, a_re, a_im, log_dt, b_re, b_im, c_re, c_im, d_skip,
           w_attn_proj, w_glu_a, w_glu_b, w_out, g_ffn, w_ffn_gate, w_ffn_up, w_ffn_down,
           w_ple_gate, w_ple_proj, g_final):
    batch, depth = x.shape[0], p.shape[0]
    assert batch == 1 and depth == 1, "kernel supports the stated BATCH=1, DEPTH=1 problem"
    out = _layer(x[0], p[0, 0], positions[0], g_mix[0], w_in[0], a_re[0], a_im[0], log_dt[0],
                 b_re[0], b_im[0], c_re[0], c_im[0], d_skip[0], w_attn_proj[0], w_glu_a[0],
                 w_glu_b[0], w_out[0], g_ffn[0], w_ffn_gate[0], w_ffn_up[0], w_ffn_down[0],
                 w_ple_gate[0], w_ple_proj[0], g_final)
    return out[None]
```

```python
import functools
import math

import jax
import jax.numpy as jnp
from jax import lax
from jax.experimental import pallas as pl
from jax.experimental.pallas import tpu as pltpu

F32 = jnp.float32
BF16 = jnp.bfloat16

D_MODEL = 1024
HEAD_DIM = 128
HEADS_PER_GROUP = 4
GROUP_WIDTH = HEADS_PER_GROUP * HEAD_DIM
ATTN_DILATIONS = (1, 4, 16)
N_GROUPS = len(ATTN_DILATIONS)
QK_WIDTH = N_GROUPS * GROUP_WIDTH
BLOCK = 128
ROPE_THETA = 500000.0
ROPE_DIM = HEAD_DIM // 4
ROPE_HALF = ROPE_DIM // 2
SSM_WIDTH = 512
SSM_GROUP = 16
SSM_GROUPS = SSM_WIDTH // SSM_GROUP
SSM_STATE = 64
N_STATES = SSM_GROUPS * SSM_STATE
SSM_HALVES = 2
HALF_GROUPS = SSM_GROUPS // SSM_HALVES
HALF_WIDTH = SSM_WIDTH // SSM_HALVES
HALF_STATES = N_STATES // SSM_HALVES
STATE_BLOCK = 256
D_FF = 2816
PLE_DIM = 256
EPS = 1e-6
MASK_VALUE = -1e30

V7X_VMEM_LIMIT_BYTES = 56 * 1024 * 1024

PROJ_ROWS = 512
PROJ_SUB_ROWS = 256
ATTN_TOKENS = 2048
STAT_LANES = HEAD_DIM // HEADS_PER_GROUP
SCATTER_STRIDE = 4
SSM_CHUNK = 64
SSM_ROWS = 512
PREP_STEPS = 2
MIX_ROWS = 1024
MIX_SUB_ROWS = 256
FFN_ROWS = 1024
FFN_SUB_ROWS = 256
FFN_CHUNKS = ((0, 1024), (1024, 1024), (2048, 768))


def _resident(shape):
    return pl.BlockSpec(shape, lambda *_: (0,) * len(shape), pipeline_mode=pl.Buffered(1))


def _params(*semantics):
    return pltpu.CompilerParams(dimension_semantics=semantics,
                                vmem_limit_bytes=V7X_VMEM_LIMIT_BYTES)


def _rmsnorm(x, g):
    return (x * lax.rsqrt(jnp.mean(x * x, axis=-1, keepdims=True) + EPS)) * g


def _bdot(a, b):
    return jnp.dot(a, b, preferred_element_type=F32)


def _cast_plan(weights, steps):
    specs = [pl.BlockSpec((w.shape[0] // steps, w.shape[1]), lambda i: (i, 0)) for w in weights]
    shapes = [jax.ShapeDtypeStruct(w.shape, BF16) for w in weights]
    return specs, shapes


def _cast_blocks(in_refs, out_refs):
    for src, dst in zip(in_refs, out_refs):
        dst[...] = src[...].astype(BF16)


def _proj_kernel(*refs, n_cast):
    x_ref, pos_ref, g_ref, invf_ref, w_ref = refs[:5]
    cast_in, refs = refs[5:5 + n_cast], refs[5 + n_cast:]
    qkv0, qkv1, qkv2, u_ref, gates_ref = refs[:5]
    cast_out, (n_scr, nperm_scr, cos_scr, sin_scr) = refs[5:5 + n_cast], refs[5 + n_cast:]
    _cast_blocks(cast_in, cast_out)
    rows = PROJ_SUB_ROWS
    lane = lax.broadcasted_iota(jnp.int32, (rows, HEAD_DIM), 1)
    first_half = lane < ROPE_HALF
    scale = math.log2(math.e) / math.sqrt(HEAD_DIM)

    for sb in range(x_ref.shape[0] // rows):
        base = sb * rows
        tile_rows = slice(base, base + rows)
        xn = _rmsnorm(x_ref[tile_rows, :], g_ref[...])
        n = xn.astype(BF16)
        for c in range(D_MODEL // HEAD_DIM):
            n_scr[c, tile_rows, :] = xn[:, c * HEAD_DIM:(c + 1) * HEAD_DIM]

        ang = invf_ref[...] * pos_ref[:, tile_rows].astype(F32)
        cos_t = jnp.cos(ang)
        sin_t = jnp.sin(ang)
        rest = (HEAD_DIM - ROPE_DIM, rows)
        cos_scr[tile_rows, :] = jnp.concatenate([cos_t, cos_t, jnp.ones(rest, F32)], axis=0).T
        sin_scr[tile_rows, :] = jnp.concatenate([-sin_t, sin_t, jnp.zeros(rest, F32)], axis=0).T

        for g, (d, qkv_ref) in enumerate(zip(ATTN_DILATIONS, (qkv0, qkv1, qkv2))):
            def residue_major(ref_2d):
                if d == 1:
                    return ref_2d[tile_rows, :]
                return jnp.concatenate(
                    [ref_2d[pl.ds(base + r, rows // d, stride=d), :] for r in range(d)], axis=0)

            cos = residue_major(cos_scr)
            sin_signed = residue_major(sin_scr)
            if d == 1:
                ng = n
            else:
                for c in range(D_MODEL // HEAD_DIM):
                    nperm_scr[tile_rows, c * HEAD_DIM:(c + 1) * HEAD_DIM] = (
                        residue_major(n_scr.at[c]).astype(BF16))
                ng = nperm_scr[tile_rows, :]

            def rotary(t):
                partner = jnp.where(first_half,
                                    pltpu.roll(t, HEAD_DIM - ROPE_HALF, 1),
                                    pltpu.roll(t, ROPE_HALF, 1))
                return t * cos + partner * sin_signed

            c0 = g * GROUP_WIDTH
            zq = _bdot(ng, w_ref[:, c0:c0 + GROUP_WIDTH])
            zk = _bdot(ng, w_ref[:, QK_WIDTH + c0:QK_WIDTH + c0 + GROUP_WIDTH])
            zv = _bdot(ng, w_ref[:, 2 * QK_WIDTH + c0:2 * QK_WIDTH + c0 + GROUP_WIDTH])
            sub_rows = slice(base // d, (base + rows) // d)
            for h in range(HEADS_PER_GROUP):
                hs = slice(h * HEAD_DIM, (h + 1) * HEAD_DIM)
                ks = slice(GROUP_WIDTH + h * HEAD_DIM, GROUP_WIDTH + (h + 1) * HEAD_DIM)
                qkv_ref[:, sub_rows, hs] = (
                    (rotary(zq[:, hs]) * scale).astype(BF16).reshape(d, rows // d, HEAD_DIM))
                qkv_ref[:, sub_rows, ks] = rotary(zk[:, hs]).astype(BF16).reshape(d, rows // d, HEAD_DIM)
            qkv_ref[:, sub_rows, 2 * GROUP_WIDTH:] = zv.astype(BF16).reshape(d, rows // d, GROUP_WIDTH)
        o1 = 3 * QK_WIDTH
        u_ref[tile_rows, :] = _bdot(n, w_ref[:, o1:o1 + SSM_WIDTH])
        o2 = o1 + SSM_WIDTH
        gates_ref[tile_rows, :] = jax.nn.sigmoid(_bdot(n, w_ref[:, o2:o2 + 2 * D_MODEL])).astype(BF16)


def _proj(x, pos_row, g_mix, inv_freq_col, w_in, later_weights):
    s = x.shape[0]
    tm = min(PROJ_ROWS, s)
    row = lambda w: pl.BlockSpec((tm, w), lambda i: (i, 0))
    cast_specs, cast_shapes = _cast_plan(later_weights, s // tm)
    grp_specs, grp_shapes = [], []
    for d in ATTN_DILATIONS:
        grp_specs.append(pl.BlockSpec((d, tm // d, 3 * GROUP_WIDTH), lambda i: (0, i, 0)))
        grp_shapes.append(jax.ShapeDtypeStruct((d, s // d, 3 * GROUP_WIDTH), BF16))
    return pl.pallas_call(
        functools.partial(_proj_kernel, n_cast=len(later_weights)),
        grid=(s // tm,),
        in_specs=[row(D_MODEL), pl.BlockSpec((1, tm), lambda i: (0, i)), _resident((1, D_MODEL)),
                  _resident((ROPE_HALF, 1)), _resident(w_in.shape)] + cast_specs,
        out_specs=grp_specs + [row(SSM_WIDTH), row(2 * D_MODEL)] + cast_specs,
        out_shape=grp_shapes + [jax.ShapeDtypeStruct((s, SSM_WIDTH), F32),
                                jax.ShapeDtypeStruct((s, 2 * D_MODEL), BF16)] + cast_shapes,
        scratch_shapes=[pltpu.VMEM((D_MODEL // HEAD_DIM, tm, HEAD_DIM), F32), pltpu.VMEM((tm, D_MODEL), BF16),
                        pltpu.VMEM((tm, HEAD_DIM), F32), pltpu.VMEM((tm, HEAD_DIM), F32)],
        compiler_params=_params("parallel"),
        name="proj",
    )(x, pos_row, g_mix, inv_freq_col, w_in, *later_weights)


def _attn_kernel(qkv_ref, o_ref, l_ref, prev_scr, *scatter_scr, d, nsub):
    not_first_block = pl.program_id(0) > 0
    two_pass = d > SCATTER_STRIDE
    if two_pass:
        o_scr, l_scr, part_o, part_l = scatter_scr
    elif d > 1:
        o_scr, l_scr = scatter_scr

    @pl.when(pl.program_id(0) == 0)
    def _():
        prev_scr[...] = jnp.zeros_like(prev_scr)

    qi = lax.broadcasted_iota(jnp.int32, (BLOCK, 2 * BLOCK), 0)
    kj = lax.broadcasted_iota(jnp.int32, (BLOCK, 2 * BLOCK), 1)
    rel = BLOCK + qi - kj
    band = (rel >= 0) & (rel <= BLOCK)
    band_first = band & ((kj >= BLOCK) | not_first_block)
    lane = lax.broadcasted_iota(jnp.int32, (BLOCK, HEAD_DIM), 1)

    def residue(r):
        for b in range(nsub):
            rows = slice(b * BLOCK, (b + 1) * BLOCK)
            mask = band_first if b == 0 else band
            token_rows = pl.ds(b * BLOCK * d + r, BLOCK, stride=d) if d > 1 else rows
            if two_pass:
                r_hi, r_lo = divmod(r, SCATTER_STRIDE)
                part_rows = pl.ds(b * BLOCK * (d // SCATTER_STRIDE) + r_hi, BLOCK, stride=d // SCATTER_STRIDE)
            for h in range(HEADS_PER_GROUP):
                hs = slice(h * HEAD_DIM, (h + 1) * HEAD_DIM)
                ks = slice(GROUP_WIDTH + h * HEAD_DIM, GROUP_WIDTH + (h + 1) * HEAD_DIM)
                vs = slice(2 * GROUP_WIDTH + h * HEAD_DIM, 2 * GROUP_WIDTH + (h + 1) * HEAD_DIM)
                q = qkv_ref[r, rows, hs]
                if b == 0:
                    prev_k = slice(h * HEAD_DIM, (h + 1) * HEAD_DIM)
                    prev_v = slice(GROUP_WIDTH + h * HEAD_DIM, GROUP_WIDTH + (h + 1) * HEAD_DIM)
                    kw = jnp.concatenate([prev_scr[r, :, prev_k], qkv_ref[r, rows, ks]], axis=0)
                    vw = jnp.concatenate([prev_scr[r, :, prev_v], qkv_ref[r, rows, vs]], axis=0)
                else:
                    win = slice((b - 1) * BLOCK, (b + 1) * BLOCK)
                    kw = qkv_ref[r, win, ks]
                    vw = qkv_ref[r, win, vs]
                s = lax.dot_general(q, kw, (((1,), (1,)), ((), ())), preferred_element_type=F32)
                s = jnp.where(mask, s, MASK_VALUE)
                m = jnp.max(s, axis=-1, keepdims=True)
                p = jnp.exp2(s - m)
                den = jnp.sum(p, axis=-1, keepdims=True)
                pv = _bdot(p.astype(BF16), vw)
                if two_pass:
                    part_o[r_lo, h, part_rows, :] = pv
                elif d > 1:
                    o_scr[h, token_rows, :] = pv
                else:
                    o_ref[rows, hs] = pv.astype(BF16)
                m_b = jnp.broadcast_to(m, (BLOCK, HEAD_DIM))
                den_b = jnp.broadcast_to(den, (BLOCK, HEAD_DIM))
                if h > 0:
                    m_b = jnp.where(lane >= h * STAT_LANES, m_b, stats)
                stats = jnp.where(lane >= h * STAT_LANES + STAT_LANES // 2, den_b, m_b)
            if two_pass:
                part_l[r_lo, part_rows, :] = stats
            elif d > 1:
                l_scr[token_rows, :] = stats
            else:
                l_ref[rows, :] = stats

    for r in range(d):
        residue(r)

    if two_pass:
        part_len = part_l.shape[1]
        for r_lo in range(SCATTER_STRIDE):
            token_rows = pl.ds(r_lo, part_len, stride=SCATTER_STRIDE)
            for h in range(HEADS_PER_GROUP):
                o_scr[h, token_rows, :] = part_o[r_lo, h]
            l_scr[token_rows, :] = part_l[r_lo]
    if d > 1:
        for h in range(HEADS_PER_GROUP):
            o_ref[:, h * HEAD_DIM:(h + 1) * HEAD_DIM] = o_scr[h].astype(BF16)
        l_ref[...] = l_scr[...]
    prev_scr[...] = qkv_ref[:, (nsub - 1) * BLOCK:nsub * BLOCK, GROUP_WIDTH:]


def _attn_group(qkv, dilation):
    sub_len = qkv.shape[1]
    s = sub_len * dilation
    step_tokens = min(ATTN_TOKENS, s)
    qb = step_tokens // dilation
    nsub = qb // BLOCK
    cur = pl.BlockSpec((dilation, qb, 3 * GROUP_WIDTH), lambda i: (0, i, 0))
    scatter_scratch = [] if dilation == 1 else [
        pltpu.VMEM((HEADS_PER_GROUP, step_tokens, HEAD_DIM), F32), pltpu.VMEM((step_tokens, HEAD_DIM), F32)]
    if dilation > SCATTER_STRIDE:
        part = step_tokens // SCATTER_STRIDE
        scatter_scratch += [pltpu.VMEM((SCATTER_STRIDE, HEADS_PER_GROUP, part, HEAD_DIM), F32),
                            pltpu.VMEM((SCATTER_STRIDE, part, HEAD_DIM), F32)]
    return pl.pallas_call(
        functools.partial(_attn_kernel, d=dilation, nsub=nsub),
        grid=(s // step_tokens,),
        in_specs=[cur],
        out_specs=[pl.BlockSpec((step_tokens, GROUP_WIDTH), lambda i: (i, 0)),
                   pl.BlockSpec((step_tokens, HEAD_DIM), lambda i: (i, 0))],
        out_shape=[jax.ShapeDtypeStruct((s, GROUP_WIDTH), BF16), jax.ShapeDtypeStruct((s, HEAD_DIM), F32)],
        scratch_shapes=[pltpu.VMEM((dilation, BLOCK, 2 * GROUP_WIDTH), BF16)] + scatter_scratch,
        compiler_params=_params("arbitrary"),
        name=f"attn_d{dilation}",
    )(qkv)


def _discretize(lr, li, log_dt):
    dt = jnp.exp(log_dt)
    mag = jnp.exp(lr * dt)
    bar_re = mag * jnp.cos(li * dt)
    bar_im = mag * jnp.sin(li * dt)
    nr = bar_re - 1.0
    ni = bar_im
    den = lr * lr + li * li
    return bar_re, bar_im, (nr * lr + ni * li) / den, (ni * lr - nr * li) / den


def _expand_block_diag(compact, n_blocks):
    rows, b = compact.shape
    a = rows // n_blocks
    wide = n_blocks * b
    src_lane = lax.broadcasted_iota(jnp.int32, (b, wide), 0)
    dst_lane = lax.broadcasted_iota(jnp.int32, (b, wide), 1)
    tiled = _bdot(compact, (dst_lane % b == src_lane).astype(BF16))
    row_block = lax.broadcasted_iota(jnp.int32, (rows, wide), 0) // a
    col_block = lax.broadcasted_iota(jnp.int32, (rows, wide), 1) // b
    return jnp.where(row_block == col_block, tiled, 0.0).astype(BF16)


def _ssm_prep_kernel(lr_ref, li_ref, logdt_ref, lr_rep_ref, li_rep_ref, logdt_rep_ref, b_re_ref, b_im_ref,
                     c_re_ref, c_im_ref, w_in_ref,
                     bmat_ref, cre_ref, cim_ref, e_re_ref, e_im_ref, einv_re_ref, einv_im_ref, lam_ref,
                     w_in_bf_ref):
    w_in_bf_ref[...] = w_in_ref[...].astype(BF16)
    lr = lr_ref[...]
    li = li_ref[...]
    dt = jnp.exp(logdt_ref[...])

    @pl.when(pl.program_id(0) == 0)
    def _():
        _, _, z_re, z_im = _discretize(lr_rep_ref[...], li_rep_ref[...], logdt_rep_ref[...])
        b_re = b_re_ref[...]
        b_im = b_im_ref[...]
        bb_re = (z_re * b_re - z_im * b_im).astype(BF16)
        bb_im = (z_re * b_im + z_im * b_re).astype(BF16)
        for hf in range(SSM_HALVES):
            rows = slice(hf * HALF_WIDTH, (hf + 1) * HALF_WIDTH)
            bmat_ref[rows, :HALF_STATES] = _expand_block_diag(bb_re[rows], HALF_GROUPS)
            bmat_ref[rows, HALF_STATES:] = _expand_block_diag(bb_im[rows], HALF_GROUPS)
            srows = slice(hf * HALF_STATES, (hf + 1) * HALF_STATES)
            cre_ref[srows, :] = _expand_block_diag(c_re_ref[srows, :].astype(BF16), HALF_GROUPS)
            cim_ref[srows, :] = _expand_block_diag(c_im_ref[srows, :].astype(BF16), HALF_GROUPS)
        whole = float(SSM_CHUNK)
        chunk_mag = jnp.exp(whole * (lr * dt))
        lam_ref[0:1, :] = chunk_mag * jnp.cos(whole * (li * dt))
        lam_ref[1:2, :] = chunk_mag * jnp.sin(whole * (li * dt))

    rows_per_step = e_re_ref.shape[0]
    t = (lax.broadcasted_iota(jnp.int32, (rows_per_step, N_STATES), 0)
         + pl.program_id(0) * rows_per_step).astype(F32)
    grow = jnp.exp(t * (lr * dt))
    theta = t * (li * dt)
    c = jnp.cos(theta)
    s = jnp.sin(theta)
    e_re_ref[...] = (grow * c).astype(BF16)
    e_im_ref[...] = (grow * s).astype(BF16)
    shrink = jnp.exp(-t * (lr * dt))
    einv_re_ref[...] = (shrink * c).astype(BF16)
    einv_im_ref[...] = (-(shrink * s)).astype(BF16)


def _ssm_prep(a_re, a_im, log_dt, b_re, b_im, c_re, c_im, w_in):
    row = lambda v: v.reshape(1, N_STATES)
    rep = lambda v: jnp.repeat(v, SSM_GROUP, axis=0)
    logdt_gp = jnp.broadcast_to(log_dt[:, None], (SSM_GROUPS, SSM_STATE))
    chan_state = lambda b: jnp.swapaxes(b, 1, 2).reshape(SSM_WIDTH, SSM_STATE)
    state_chan = lambda c: jnp.swapaxes(c, 1, 2).reshape(N_STATES, SSM_GROUP)
    params = (row(a_re), row(a_im), row(logdt_gp), rep(a_re), rep(a_im), rep(logdt_gp),
              chan_state(b_re), chan_state(b_im), state_chan(c_re), state_chan(c_im))
    whole = lambda shape: pl.BlockSpec(shape, lambda i: (0,) * len(shape))
    slab = lambda arr: pl.BlockSpec((arr.shape[0] // PREP_STEPS, arr.shape[1]), lambda i: (i, 0))
    tab = jax.ShapeDtypeStruct((SSM_CHUNK, N_STATES), BF16)
    cmat = jax.ShapeDtypeStruct((N_STATES, HALF_WIDTH), BF16)
    bmat = jax.ShapeDtypeStruct((SSM_WIDTH, 2 * HALF_STATES), BF16)
    lam = jax.ShapeDtypeStruct((2, N_STATES), F32)
    return pl.pallas_call(
        _ssm_prep_kernel,
        grid=(PREP_STEPS,),
        in_specs=[whole(p.shape) for p in params] + [slab(w_in)],
        out_specs=[whole(bmat.shape), whole(cmat.shape), whole(cmat.shape), slab(tab), slab(tab), slab(tab),
                   slab(tab), whole(lam.shape), slab(w_in)],
        out_shape=[bmat, cmat, cmat, tab, tab, tab, tab, lam, jax.ShapeDtypeStruct(w_in.shape, BF16)],
        compiler_params=_params("arbitrary"),
        name="ssm_prep",
    )(*params, w_in)


def _gelu_tanh(x):
    return 0.5 * x * (1.0 + jnp.tanh(math.sqrt(2.0 / math.pi) * (x + 0.044715 * (x * x * x))))


def _ssm_kernel(u_ref, bmat_ref, cre_ref, cim_ref, tri_ref, e_re_ref, e_im_ref, einv_re_ref, einv_im_ref,
                lam_ref, d_ref, y_ref, h_re_scr, h_im_scr, carry_scr):
    @pl.when(pl.program_id(0) == 0)
    def _():
        carry_scr[...] = jnp.zeros_like(carry_scr)

    u = u_ref[...]
    ub = u.astype(BF16)
    for hf in range(SSM_HALVES):
        chans = slice(hf * HALF_WIDTH, (hf + 1) * HALF_WIDTH)
        y = d_ref[:, chans] * u[:, chans]
        for blk in range(HALF_STATES // STATE_BLOCK):
            local = slice(blk * STATE_BLOCK, (blk + 1) * STATE_BLOCK)
            local_im = slice(HALF_STATES + blk * STATE_BLOCK, HALF_STATES + (blk + 1) * STATE_BLOCK)
            states = slice(hf * HALF_STATES + blk * STATE_BLOCK, hf * HALF_STATES + (blk + 1) * STATE_BLOCK)
            bu_re_all = _bdot(ub[:, chans], bmat_ref[chans, local])
            bu_im_all = _bdot(ub[:, chans], bmat_ref[chans, local_im])
            lam_re = lam_ref[0:1, states]
            lam_im = lam_ref[1:2, states]
            carry_re = carry_scr[0:1, states]
            carry_im = carry_scr[1:2, states]
            for c in range(u.shape[0] // SSM_CHUNK):
                rows = slice(c * SSM_CHUNK, (c + 1) * SSM_CHUNK)
                bu_re = bu_re_all[rows].astype(BF16)
                bu_im = bu_im_all[rows].astype(BF16)
                einv_re = einv_re_ref[:, states]
                einv_im = einv_im_ref[:, states]
                x = jnp.concatenate([bu_re * einv_re - bu_im * einv_im,
                                     bu_re * einv_im + bu_im * einv_re], axis=1)
                a = _bdot(tri_ref[...], x)
                a_re = a[:, :STATE_BLOCK] + carry_re
                a_im = a[:, STATE_BLOCK:] + carry_im
                e_re = e_re_ref[:, states]
                e_im = e_im_ref[:, states]
                a_re_b = a_re.astype(BF16)
                a_im_b = a_im.astype(BF16)
                h_re_scr[rows, :] = e_re * a_re_b - e_im * a_im_b
                h_im_scr[rows, :] = e_re * a_im_b + e_im * a_re_b
                last_re = a_re[SSM_CHUNK - 1:SSM_CHUNK, :]
                last_im = a_im[SSM_CHUNK - 1:SSM_CHUNK, :]
                carry_re = lam_re * last_re - lam_im * last_im
                carry_im = lam_re * last_im + lam_im * last_re
            carry_scr[0:1, states] = carry_re
            carry_scr[1:2, states] = carry_im
            y = y + _bdot(h_re_scr[...], cre_ref[states, :]) - _bdot(h_im_scr[...], cim_ref[states, :])
        y_ref[:, chans] = _gelu_tanh(y).astype(BF16)


def _ssm(u, bmat, cre, cim, tri, e_re, e_im, einv_re, einv_im, lam, d_row):
    s = u.shape[0]
    tm = min(SSM_ROWS, s)
    row = pl.BlockSpec((tm, SSM_WIDTH), lambda i: (i, 0))
    consts = (bmat, cre, cim, tri, e_re, e_im, einv_re, einv_im, lam, d_row)
    return pl.pallas_call(
        _ssm_kernel,
        grid=(s // tm,),
        in_specs=[row] + [_resident(c.shape) for c in consts],
        out_specs=row,
        out_shape=jax.ShapeDtypeStruct((s, SSM_WIDTH), BF16),
        scratch_shapes=[pltpu.VMEM((tm, STATE_BLOCK), BF16), pltpu.VMEM((tm, STATE_BLOCK), BF16),
                        pltpu.VMEM((2, N_STATES), F32)],
        compiler_params=_params("arbitrary"),
        name="ssm",
    )(u, *consts)


def _mix_kernel(*refs, n_cast):
    x_ref, o0_ref, o1_ref, o2_ref, l0_ref, l1_ref, l2_ref, y_ref, gates_ref = refs[:9]
    wap_ref, wa_ref, wb_ref, wout_ref = refs[9:13]
    cast_in, h_ref, cast_out = refs[13:13 + n_cast], refs[13 + n_cast], refs[14 + n_cast:]
    _cast_blocks(cast_in, cast_out)
    for rb in range(x_ref.shape[0] // MIX_SUB_ROWS):
        rows = slice(rb * MIX_SUB_ROWS, (rb + 1) * MIX_SUB_ROWS)
        ls = (l0_ref[rows, :], l1_ref[rows, :], l2_ref[rows, :])
        m_max = jnp.maximum(jnp.maximum(ls[0], ls[1]), ls[2])
        es = [jnp.exp2(l - m_max) for l in ls]
        dens = [pltpu.roll(l, HEAD_DIM - STAT_LANES // 2, 1) for l in ls]
        inv = 1.0 / (es[0] * dens[0] + es[1] * dens[1] + es[2] * dens[2])
        heads = []
        for h in range(HEADS_PER_GROUP):
            hs = slice(h * HEAD_DIM, (h + 1) * HEAD_DIM)
            col = slice(h * STAT_LANES, h * STAT_LANES + 1)
            heads.append(sum(jnp.broadcast_to((e * inv)[:, col], (MIX_SUB_ROWS, HEAD_DIM)).astype(BF16)
                             * o_ref[rows, hs] for e, o_ref in zip(es, (o0_ref, o1_ref, o2_ref))))
        attn = jnp.concatenate(heads, axis=1)
        attn_d = _bdot(attn, wap_ref[...])
        y = y_ref[rows, :]
        ssm_d = _bdot(y, wa_ref[...]) * jax.nn.sigmoid(_bdot(y, wb_ref[...]))
        mix = (gates_ref[rows, :D_MODEL] * attn_d.astype(BF16)
               + gates_ref[rows, D_MODEL:] * ssm_d.astype(BF16))
        h_ref[rows, :] = x_ref[rows, :] + _bdot(mix, wout_ref[...])


def _mix(x, os, ls, y, gates, wap, wa, wb, wout, later_weights):
    s = x.shape[0]
    tm = min(MIX_ROWS, s)
    row = lambda w: pl.BlockSpec((tm, w), lambda i: (i, 0))
    weights = (wap, wa, wb, wout)
    cast_specs, cast_shapes = _cast_plan(later_weights, s // tm)
    return pl.pallas_call(
        functools.partial(_mix_kernel, n_cast=len(later_weights)),
        grid=(s // tm,),
        in_specs=[row(D_MODEL)] + [row(GROUP_WIDTH)] * 3 + [row(HEAD_DIM)] * 3
        + [row(SSM_WIDTH), row(2 * D_MODEL)] + [_resident(w.shape) for w in weights] + cast_specs,
        out_specs=[row(D_MODEL)] + cast_specs,
        out_shape=[jax.ShapeDtypeStruct((s, D_MODEL), F32)] + cast_shapes,
        compiler_params=_params("parallel"),
        name="mix",
    )(x, *os, *ls, y, gates, *weights, *later_weights)


def _ffn_kernel(h_ref, p_ref, gffn_ref, wg_ref, wu_ref, wd_ref, wpg_ref, wpp_ref, gfin_ref,
                out_ref, acc_scr):
    for rb in range(h_ref.shape[0] // FFN_SUB_ROWS):
        rows = slice(rb * FFN_SUB_ROWS, (rb + 1) * FFN_SUB_ROWS)
        h = h_ref[rows, :]
        n2 = _rmsnorm(h, gffn_ref[...]).astype(BF16)
        for idx, (c0, width) in enumerate(FFN_CHUNKS):
            gate = _bdot(n2, wg_ref[:, c0:c0 + width])
            up = _bdot(n2, wu_ref[:, c0:c0 + width])
            act = (gate * jax.nn.sigmoid(gate) * up).astype(BF16)
            part = _bdot(act, wd_ref[c0:c0 + width, :])
            if idx == 0:
                acc_scr[rows, :] = h + part
            else:
                acc_scr[rows, :] += part
        h2 = acc_scr[rows, :]
        ple = (jax.nn.sigmoid(_bdot(h2.astype(BF16), wpg_ref[...]))
               * _bdot(p_ref[rows, :].astype(BF16), wpp_ref[...]))
        out_ref[rows, :] = _rmsnorm(h2 + ple, gfin_ref[...])


def _ffn(h, p, g_ffn, wg, wu, wd, wpg, wpp, g_final):
    s = h.shape[0]
    tm = min(FFN_ROWS, s)
    row = lambda w: pl.BlockSpec((tm, w), lambda i: (i, 0))
    consts = (g_ffn, wg, wu, wd, wpg, wpp, g_final)
    return pl.pallas_call(
        _ffn_kernel,
        grid=(s // tm,),
        in_specs=[row(D_MODEL), row(PLE_DIM)] + [_resident(c.shape) for c in consts],
        out_specs=row(D_MODEL),
        out_shape=jax.ShapeDtypeStruct((s, D_MODEL), F32),
        scratch_shapes=[pltpu.VMEM((tm, D_MODEL), F32)],
        compiler_params=_params("parallel"),
        name="ffn",
    )(h, p, *consts)


def _layer(x, p, positions, g_mix, w_in, a_re, a_im, log_dt, b_re, b_im, c_re, c_im, d_skip,
           w_attn_proj, w_glu_a, w_glu_b, w_out, g_ffn, w_ffn_gate, w_ffn_up, w_ffn_down,
           w_ple_gate, w_ple_proj, g_final):
    row = lambda v: v.reshape(1, -1).astype(F32)

    inv_freq = ROPE_THETA ** (-jnp.arange(ROPE_HALF, dtype=F32) * 2.0 / ROPE_DIM)
    bmat, cre, cim, e_re, e_im, einv_re, einv_im, lam, w_in_bf = _ssm_prep(
        a_re, a_im, log_dt, b_re, b_im, c_re, c_im, w_in)
    outs = _proj(x, positions.reshape(1, -1), row(g_mix), inv_freq.reshape(ROPE_HALF, 1), w_in_bf,
                 (w_attn_proj, w_glu_a, w_glu_b, w_out, w_ffn_gate, w_ffn_up, w_ple_gate))
    qkvs, (u, gates) = outs[:N_GROUPS], outs[N_GROUPS:N_GROUPS + 2]
    wap, wga, wgb, wout, wfg, wfu, wpg = outs[N_GROUPS + 2:]

    attn_os, attn_ls = zip(*[_attn_group(qkvs[g], d) for g, d in enumerate(ATTN_DILATIONS)])

    tri = jnp.tril(jnp.ones((SSM_CHUNK, SSM_CHUNK), F32)).astype(BF16)
    y = _ssm(u, bmat, cre, cim, tri, e_re, e_im, einv_re, einv_im, lam, row(d_skip))

    h1, wfd, wpp = _mix(x, attn_os, attn_ls, y, gates, wap, wga, wgb, wout, (w_ffn_down, w_ple_proj))
    return _ffn(h1, p, row(g_ffn), wfg, wfu, wfd, wpg, wpp, row(g_final))


def kernel(x, p, positions, g_mix, w_in, a_re, a_im, log_dt, b_re, b_im, c_re, c_im, d_skip,
           w_attn_proj, w_glu_a, w_glu_b, w_out, g_ffn, w_ffn_gate, w_ffn_up, w_ffn_down,
           w_ple_gate, w_ple_proj, g_final):
    batch, depth = x.shape[0], p.shape[0]
    assert batch == 1 and depth == 1, "kernel supports the stated BATCH=1, DEPTH=1 problem"
    out = _layer(x[0], p[0, 0], positions[0], g_mix[0], w_in[0], a_re[0], a_im[0], log_dt[0],
                 b_re[0], b_im[0], c_re[0], c_im[0], d_skip[0], w_attn_proj[0], w_glu_a[0],
                 w_glu_b[0], w_out[0], g_ffn[0], w_ffn_gate[0], w_ffn_up[0], w_ffn_down[0],
                 w_ple_gate[0], w_ple_proj[0], g_final)
    return out[None]
```

```python
import functools
import math

import jax
import jax.numpy as jnp
from jax import lax
from jax.experimental import pallas as pl
from jax.experimental.pallas import tpu as pltpu

F32 = jnp.float32
BF16 = jnp.bfloat16

D_MODEL = 1024
HEAD_DIM = 128
HEADS_PER_GROUP = 4
GROUP_WIDTH = HEADS_PER_GROUP * HEAD_DIM
ATTN_DILATIONS = (1, 4, 16)
N_GROUPS = len(ATTN_DILATIONS)
QK_WIDTH = N_GROUPS * GROUP_WIDTH
BLOCK = 128
ROPE_THETA = 500000.0
ROPE_DIM = HEAD_DIM // 4
ROPE_HALF = ROPE_DIM // 2
SSM_WIDTH = 512
SSM_GROUP = 16
SSM_GROUPS = SSM_WIDTH // SSM_GROUP
SSM_STATE = 64
N_STATES = SSM_GROUPS * SSM_STATE
SSM_HALVES = 2
HALF_GROUPS = SSM_GROUPS // SSM_HALVES
HALF_WIDTH = SSM_WIDTH // SSM_HALVES
HALF_STATES = N_STATES // SSM_HALVES
STATE_BLOCK = 256
D_FF = 2816
PLE_DIM = 256
EPS = 1e-6
MASK_VALUE = -1e30

V7X_VMEM_LIMIT_BYTES = 56 * 1024 * 1024

PROJ_ROWS = 512
PROJ_SUB_ROWS = 256
ATTN_TOKENS = {1: 4096, 4: 4096, 16: 2048}
STAT_LANES = HEAD_DIM // HEADS_PER_GROUP
SCATTER_STRIDE = 4
SSM_CHUNK = 64
SSM_ROWS = 512
PREP_STEPS = 2
MIX_ROWS = 1024
MIX_SUB_ROWS = 256
FFN_ROWS = 1024
FFN_SUB_ROWS = 256
FFN_CHUNKS = ((0, 1024), (1024, 1024), (2048, 768))


def _resident(shape):
    return pl.BlockSpec(shape, lambda *_: (0,) * len(shape), pipeline_mode=pl.Buffered(1))


def _params(*semantics):
    return pltpu.CompilerParams(dimension_semantics=semantics,
                                vmem_limit_bytes=V7X_VMEM_LIMIT_BYTES)


def _rmsnorm(x, g):
    return (x * lax.rsqrt(jnp.mean(x * x, axis=-1, keepdims=True) + EPS)) * g


def _bdot(a, b):
    return jnp.dot(a, b, preferred_element_type=F32)


def _cast_plan(weights, steps):
    specs = [pl.BlockSpec((w.shape[0] // steps, w.shape[1]), lambda i: (i, 0)) for w in weights]
    shapes = [jax.ShapeDtypeStruct(w.shape, BF16) for w in weights]
    return specs, shapes


def _cast_blocks(in_refs, out_refs):
    for src, dst in zip(in_refs, out_refs):
        dst[...] = src[...].astype(BF16)


def _proj_kernel(*refs, n_cast):
    x_ref, pos_ref, g_ref, invf_ref, w_ref = refs[:5]
    cast_in, refs = refs[5:5 + n_cast], refs[5 + n_cast:]
    qkv0, qkv1, qkv2, u_ref, gates_ref = refs[:5]
    cast_out, (n_scr, nperm_scr, cos_scr, sin_scr) = refs[5:5 + n_cast], refs[5 + n_cast:]
    _cast_blocks(cast_in, cast_out)
    rows = PROJ_SUB_ROWS
    lane = lax.broadcasted_iota(jnp.int32, (rows, HEAD_DIM), 1)
    first_half = lane < ROPE_HALF
    scale = math.log2(math.e) / math.sqrt(HEAD_DIM)

    for sb in range(x_ref.shape[0] // rows):
        base = sb * rows
        tile_rows = slice(base, base + rows)
        xn = _rmsnorm(x_ref[tile_rows, :], g_ref[...])
        n = xn.astype(BF16)
        for c in range(D_MODEL // HEAD_DIM):
            n_scr[c, tile_rows, :] = xn[:, c * HEAD_DIM:(c + 1) * HEAD_DIM]

        ang = invf_ref[...] * pos_ref[:, tile_rows].astype(F32)
        cos_t = jnp.cos(ang)
        sin_t = jnp.sin(ang)
        rest = (HEAD_DIM - ROPE_DIM, rows)
        cos_scr[tile_rows, :] = jnp.concatenate([cos_t, cos_t, jnp.ones(rest, F32)], axis=0).T
        sin_scr[tile_rows, :] = jnp.concatenate([-sin_t, sin_t, jnp.zeros(rest, F32)], axis=0).T

        for g, (d, qkv_ref) in enumerate(zip(ATTN_DILATIONS, (qkv0, qkv1, qkv2))):
            def residue_major(ref_2d):
                if d == 1:
                    return ref_2d[tile_rows, :]
                return jnp.concatenate(
                    [ref_2d[pl.ds(base + r, rows // d, stride=d), :] for r in range(d)], axis=0)

            cos = residue_major(cos_scr)
            sin_signed = residue_major(sin_scr)
            if d == 1:
                ng = n
            else:
                for c in range(D_MODEL // HEAD_DIM):
                    nperm_scr[tile_rows, c * HEAD_DIM:(c + 1) * HEAD_DIM] = (
                        residue_major(n_scr.at[c]).astype(BF16))
                ng = nperm_scr[tile_rows, :]

            def rotary(t):
                partner = jnp.where(first_half,
                                    pltpu.roll(t, HEAD_DIM - ROPE_HALF, 1),
                                    pltpu.roll(t, ROPE_HALF, 1))
                return t * cos + partner * sin_signed

            c0 = g * GROUP_WIDTH
            zq = _bdot(ng, w_ref[:, c0:c0 + GROUP_WIDTH])
            zk = _bdot(ng, w_ref[:, QK_WIDTH + c0:QK_WIDTH + c0 + GROUP_WIDTH])
            zv = _bdot(ng, w_ref[:, 2 * QK_WIDTH + c0:2 * QK_WIDTH + c0 + GROUP_WIDTH])
            sub_rows = slice(base // d, (base + rows) // d)
            for h in range(HEADS_PER_GROUP):
                hs = slice(h * HEAD_DIM, (h + 1) * HEAD_DIM)
                ks = slice(GROUP_WIDTH + h * HEAD_DIM, GROUP_WIDTH + (h + 1) * HEAD_DIM)
                qkv_ref[:, sub_rows, hs] = (
                    (rotary(zq[:, hs]) * scale).astype(BF16).reshape(d, rows // d, HEAD_DIM))
                qkv_ref[:, sub_rows, ks] = rotary(zk[:, hs]).astype(BF16).reshape(d, rows // d, HEAD_DIM)
            qkv_ref[:, sub_rows, 2 * GROUP_WIDTH:] = zv.astype(BF16).reshape(d, rows // d, GROUP_WIDTH)
        o1 = 3 * QK_WIDTH
        u_ref[tile_rows, :] = _bdot(n, w_ref[:, o1:o1 + SSM_WIDTH])
        o2 = o1 + SSM_WIDTH
        gates_ref[tile_rows, :] = jax.nn.sigmoid(_bdot(n, w_ref[:, o2:o2 + 2 * D_MODEL])).astype(BF16)


def _proj(x, pos_row, g_mix, inv_freq_col, w_in, later_weights):
    s = x.shape[0]
    tm = min(PROJ_ROWS, s)
    row = lambda w: pl.BlockSpec((tm, w), lambda i: (i, 0))
    cast_specs, cast_shapes = _cast_plan(later_weights, s // tm)
    grp_specs, grp_shapes = [], []
    for d in ATTN_DILATIONS:
        grp_specs.append(pl.BlockSpec((d, tm // d, 3 * GROUP_WIDTH), lambda i: (0, i, 0)))
        grp_shapes.append(jax.ShapeDtypeStruct((d, s // d, 3 * GROUP_WIDTH), BF16))
    return pl.pallas_call(
        functools.partial(_proj_kernel, n_cast=len(later_weights)),
        grid=(s // tm,),
        in_specs=[row(D_MODEL), pl.BlockSpec((1, tm), lambda i: (0, i)), _resident((1, D_MODEL)),
                  _resident((ROPE_HALF, 1)), _resident(w_in.shape)] + cast_specs,
        out_specs=grp_specs + [row(SSM_WIDTH), row(2 * D_MODEL)] + cast_specs,
        out_shape=grp_shapes + [jax.ShapeDtypeStruct((s, SSM_WIDTH), F32),
                                jax.ShapeDtypeStruct((s, 2 * D_MODEL), BF16)] + cast_shapes,
        scratch_shapes=[pltpu.VMEM((D_MODEL // HEAD_DIM, tm, HEAD_DIM), F32), pltpu.VMEM((tm, D_MODEL), BF16),
                        pltpu.VMEM((tm, HEAD_DIM), F32), pltpu.VMEM((tm, HEAD_DIM), F32)],
        compiler_params=_params("parallel"),
        name="proj",
    )(x, pos_row, g_mix, inv_freq_col, w_in, *later_weights)


def _attn_kernel(qkv_ref, o_ref, l_ref, prev_scr, *scatter_scr, d, nsub):
    not_first_block = pl.program_id(0) > 0
    two_pass = d > SCATTER_STRIDE
    if two_pass:
        o_scr, l_scr, part_o, part_l = scatter_scr
    elif d > 1:
        o_scr, l_scr = scatter_scr

    @pl.when(pl.program_id(0) == 0)
    def _():
        prev_scr[...] = jnp.zeros_like(prev_scr)

    qi = lax.broadcasted_iota(jnp.int32, (BLOCK, 2 * BLOCK), 0)
    kj = lax.broadcasted_iota(jnp.int32, (BLOCK, 2 * BLOCK), 1)
    rel = BLOCK + qi - kj
    band = (rel >= 0) & (rel <= BLOCK)
    band_first = band & ((kj >= BLOCK) | not_first_block)
    lane = lax.broadcasted_iota(jnp.int32, (BLOCK, HEAD_DIM), 1)

    def residue(r):
        for b in range(nsub):
            rows = slice(b * BLOCK, (b + 1) * BLOCK)
            mask = band_first if b == 0 else band
            token_rows = pl.ds(b * BLOCK * d + r, BLOCK, stride=d) if d > 1 else rows
            if two_pass:
                r_hi, r_lo = divmod(r, SCATTER_STRIDE)
                part_rows = pl.ds(b * BLOCK * (d // SCATTER_STRIDE) + r_hi, BLOCK, stride=d // SCATTER_STRIDE)
            for h in range(HEADS_PER_GROUP):
                hs = slice(h * HEAD_DIM, (h + 1) * HEAD_DIM)
                ks = slice(GROUP_WIDTH + h * HEAD_DIM, GROUP_WIDTH + (h + 1) * HEAD_DIM)
                vs = slice(2 * GROUP_WIDTH + h * HEAD_DIM, 2 * GROUP_WIDTH + (h + 1) * HEAD_DIM)
                q = qkv_ref[r, rows, hs]
                if b == 0:
                    prev_k = slice(h * HEAD_DIM, (h + 1) * HEAD_DIM)
                    prev_v = slice(GROUP_WIDTH + h * HEAD_DIM, GROUP_WIDTH + (h + 1) * HEAD_DIM)
                    kw = jnp.concatenate([prev_scr[r, :, prev_k], qkv_ref[r, rows, ks]], axis=0)
                    vw = jnp.concatenate([prev_scr[r, :, prev_v], qkv_ref[r, rows, vs]], axis=0)
                else:
                    win = slice((b - 1) * BLOCK, (b + 1) * BLOCK)
                    kw = qkv_ref[r, win, ks]
                    vw = qkv_ref[r, win, vs]
                s = lax.dot_general(q, kw, (((1,), (1,)), ((), ())), preferred_element_type=F32)
                s = jnp.where(mask, s, MASK_VALUE)
                m = jnp.max(s, axis=-1, keepdims=True)
                p = jnp.exp2(s - m)
                den = jnp.sum(p, axis=-1, keepdims=True)
                pv = _bdot(p.astype(BF16), vw)
                if two_pass:
                    part_o[r_lo, h, part_rows, :] = pv
                elif d > 1:
                    o_scr[h, token_rows, :] = pv
                else:
                    o_ref[rows, hs] = pv.astype(BF16)
                m_b = jnp.broadcast_to(m, (BLOCK, HEAD_DIM))
                den_b = jnp.broadcast_to(den, (BLOCK, HEAD_DIM))
                if h > 0:
                    m_b = jnp.where(lane >= h * STAT_LANES, m_b, stats)
                stats = jnp.where(lane >= h * STAT_LANES + STAT_LANES // 2, den_b, m_b)
            if two_pass:
                part_l[r_lo, part_rows, :] = stats
            elif d > 1:
                l_scr[token_rows, :] = stats
            else:
                l_ref[rows, :] = stats

    for r in range(d):
        residue(r)

    if two_pass:
        part_len = part_l.shape[1]
        for r_lo in range(SCATTER_STRIDE):
            token_rows = pl.ds(r_lo, part_len, stride=SCATTER_STRIDE)
            for h in range(HEADS_PER_GROUP):
                o_scr[h, token_rows, :] = part_o[r_lo, h]
            l_scr[token_rows, :] = part_l[r_lo]
    if d > 1:
        for h in range(HEADS_PER_GROUP):
            o_ref[:, h * HEAD_DIM:(h + 1) * HEAD_DIM] = o_scr[h].astype(BF16)
        l_ref[...] = l_scr[...]
    prev_scr[...] = qkv_ref[:, (nsub - 1) * BLOCK:nsub * BLOCK, GROUP_WIDTH:]


def _attn_group(qkv, dilation):
    sub_len = qkv.shape[1]
    s = sub_len * dilation
    step_tokens = min(ATTN_TOKENS[dilation], s)
    qb = step_tokens // dilation
    nsub = qb // BLOCK
    cur = pl.BlockSpec((dilation, qb, 3 * GROUP_WIDTH), lambda i: (0, i, 0))
    scatter_scratch = [] if dilation == 1 else [
        pltpu.VMEM((HEADS_PER_GROUP, step_tokens, HEAD_DIM), F32), pltpu.VMEM((step_tokens, HEAD_DIM), F32)]
    if dilation > SCATTER_STRIDE:
        part = step_tokens // SCATTER_STRIDE
        scatter_scratch += [pltpu.VMEM((SCATTER_STRIDE, HEADS_PER_GROUP, part, HEAD_DIM), F32),
                            pltpu.VMEM((SCATTER_STRIDE, part, HEAD_DIM), F32)]
    return pl.pallas_call(
        functools.partial(_attn_kernel, d=dilation, nsub=nsub),
        grid=(s // step_tokens,),
        in_specs=[cur],
        out_specs=[pl.BlockSpec((step_tokens, GROUP_WIDTH), lambda i: (i, 0)),
                   pl.BlockSpec((step_tokens, HEAD_DIM), lambda i: (i, 0))],
        out_shape=[jax.ShapeDtypeStruct((s, GROUP_WIDTH), BF16), jax.ShapeDtypeStruct((s, HEAD_DIM), F32)],
        scratch_shapes=[pltpu.VMEM((dilation, BLOCK, 2 * GROUP_WIDTH), BF16)] + scatter_scratch,
        compiler_params=_params("arbitrary"),
        name=f"attn_d{dilation}",
    )(qkv)


def _discretize(lr, li, log_dt):
    dt = jnp.exp(log_dt)
    mag = jnp.exp(lr * dt)
    bar_re = mag * jnp.cos(li * dt)
    bar_im = mag * jnp.sin(li * dt)
    nr = bar_re - 1.0
    ni = bar_im
    den = lr * lr + li * li
    return bar_re, bar_im, (nr * lr + ni * li) / den, (ni * lr - nr * li) / den


def _expand_block_diag(compact, n_blocks):
    rows, b = compact.shape
    a = rows // n_blocks
    wide = n_blocks * b
    src_lane = lax.broadcasted_iota(jnp.int32, (b, wide), 0)
    dst_lane = lax.broadcasted_iota(jnp.int32, (b, wide), 1)
    tiled = _bdot(compact, (dst_lane % b == src_lane).astype(BF16))
    row_block = lax.broadcasted_iota(jnp.int32, (rows, wide), 0) // a
    col_block = lax.broadcasted_iota(jnp.int32, (rows, wide), 1) // b
    return jnp.where(row_block == col_block, tiled, 0.0).astype(BF16)


def _ssm_prep_kernel(lr_ref, li_ref, logdt_ref, lr_rep_ref, li_rep_ref, logdt_rep_ref, b_re_ref, b_im_ref,
                     c_re_ref, c_im_ref, w_in_ref,
                     bmat_ref, cre_ref, cim_ref, e_re_ref, e_im_ref, einv_re_ref, einv_im_ref, lam_ref,
                     w_in_bf_ref):
    w_in_bf_ref[...] = w_in_ref[...].astype(BF16)
    lr = lr_ref[...]
    li = li_ref[...]
    dt = jnp.exp(logdt_ref[...])

    @pl.when(pl.program_id(0) == 0)
    def _():
        _, _, z_re, z_im = _discretize(lr_rep_ref[...], li_rep_ref[...], logdt_rep_ref[...])
        b_re = b_re_ref[...]
        b_im = b_im_ref[...]
        bb_re = (z_re * b_re - z_im * b_im).astype(BF16)
        bb_im = (z_re * b_im + z_im * b_re).astype(BF16)
        for hf in range(SSM_HALVES):
            rows = slice(hf * HALF_WIDTH, (hf + 1) * HALF_WIDTH)
            bmat_ref[rows, :HALF_STATES] = _expand_block_diag(bb_re[rows], HALF_GROUPS)
            bmat_ref[rows, HALF_STATES:] = _expand_block_diag(bb_im[rows], HALF_GROUPS)
            srows = slice(hf * HALF_STATES, (hf + 1) * HALF_STATES)
            cre_ref[srows, :] = _expand_block_diag(c_re_ref[srows, :].astype(BF16), HALF_GROUPS)
            cim_ref[srows, :] = _expand_block_diag(c_im_ref[srows, :].astype(BF16), HALF_GROUPS)
        whole = float(SSM_CHUNK)
        chunk_mag = jnp.exp(whole * (lr * dt))
        lam_ref[0:1, :] = chunk_mag * jnp.cos(whole * (li * dt))
        lam_ref[1:2, :] = chunk_mag * jnp.sin(whole * (li * dt))

    rows_per_step = e_re_ref.shape[0]
    t = (lax.broadcasted_iota(jnp.int32, (rows_per_step, N_STATES), 0)
         + pl.program_id(0) * rows_per_step).astype(F32)
    grow = jnp.exp(t * (lr * dt))
    theta = t * (li * dt)
    c = jnp.cos(theta)
    s = jnp.sin(theta)
    e_re_ref[...] = (grow * c).astype(BF16)
    e_im_ref[...] = (grow * s).astype(BF16)
    shrink = jnp.exp(-t * (lr * dt))
    einv_re_ref[...] = (shrink * c).astype(BF16)
    einv_im_ref[...] = (-(shrink * s)).astype(BF16)


def _ssm_prep(a_re, a_im, log_dt, b_re, b_im, c_re, c_im, w_in):
    row = lambda v: v.reshape(1, N_STATES)
    rep = lambda v: jnp.repeat(v, SSM_GROUP, axis=0)
    logdt_gp = jnp.broadcast_to(log_dt[:, None], (SSM_GROUPS, SSM_STATE))
    chan_state = lambda b: jnp.swapaxes(b, 1, 2).reshape(SSM_WIDTH, SSM_STATE)
    state_chan = lambda c: jnp.swapaxes(c, 1, 2).reshape(N_STATES, SSM_GROUP)
    params = (row(a_re), row(a_im), row(logdt_gp), rep(a_re), rep(a_im), rep(logdt_gp),
              chan_state(b_re), chan_state(b_im), state_chan(c_re), state_chan(c_im))
    whole = lambda shape: pl.BlockSpec(shape, lambda i: (0,) * len(shape))
    slab = lambda arr: pl.BlockSpec((arr.shape[0] // PREP_STEPS, arr.shape[1]), lambda i: (i, 0))
    tab = jax.ShapeDtypeStruct((SSM_CHUNK, N_STATES), BF16)
    cmat = jax.ShapeDtypeStruct((N_STATES, HALF_WIDTH), BF16)
    bmat = jax.ShapeDtypeStruct((SSM_WIDTH, 2 * HALF_STATES), BF16)
    lam = jax.ShapeDtypeStruct((2, N_STATES), F32)
    return pl.pallas_call(
        _ssm_prep_kernel,
        grid=(PREP_STEPS,),
        in_specs=[whole(p.shape) for p in params] + [slab(w_in)],
        out_specs=[whole(bmat.shape), whole(cmat.shape), whole(cmat.shape), slab(tab), slab(tab), slab(tab),
                   slab(tab), whole(lam.shape), slab(w_in)],
        out_shape=[bmat, cmat, cmat, tab, tab, tab, tab, lam, jax.ShapeDtypeStruct(w_in.shape, BF16)],
        compiler_params=_params("arbitrary"),
        name="ssm_prep",
    )(*params, w_in)


def _gelu_tanh(x):
    return 0.5 * x * (1.0 + jnp.tanh(math.sqrt(2.0 / math.pi) * (x + 0.044715 * (x * x * x))))


def _ssm_kernel(u_ref, bmat_ref, cre_ref, cim_ref, tri_ref, e_re_ref, e_im_ref, einv_re_ref, einv_im_ref,
                lam_ref, d_ref, y_ref, h_re_scr, h_im_scr, carry_scr):
    @pl.when(pl.program_id(0) == 0)
    def _():
        carry_scr[...] = jnp.zeros_like(carry_scr)

    u = u_ref[...]
    ub = u.astype(BF16)
    for hf in range(SSM_HALVES):
        chans = slice(hf * HALF_WIDTH, (hf + 1) * HALF_WIDTH)
        y = d_ref[:, chans] * u[:, chans]
        for blk in range(HALF_STATES // STATE_BLOCK):
            local = slice(blk * STATE_BLOCK, (blk + 1) * STATE_BLOCK)
            local_im = slice(HALF_STATES + blk * STATE_BLOCK, HALF_STATES + (blk + 1) * STATE_BLOCK)
            states = slice(hf * HALF_STATES + blk * STATE_BLOCK, hf * HALF_STATES + (blk + 1) * STATE_BLOCK)
            bu_re_all = _bdot(ub[:, chans], bmat_ref[chans, local])
            bu_im_all = _bdot(ub[:, chans], bmat_ref[chans, local_im])
            lam_re = lam_ref[0:1, states]
            lam_im = lam_ref[1:2, states]
            carry_re = carry_scr[0:1, states]
            carry_im = carry_scr[1:2, states]
            for c in range(u.shape[0] // SSM_CHUNK):
                rows = slice(c * SSM_CHUNK, (c + 1) * SSM_CHUNK)
                bu_re = bu_re_all[rows].astype(BF16)
                bu_im = bu_im_all[rows].astype(BF16)
                einv_re = einv_re_ref[:, states]
                einv_im = einv_im_ref[:, states]
                x = jnp.concatenate([bu_re * einv_re - bu_im * einv_im,
                                     bu_re * einv_im + bu_im * einv_re], axis=1)
                a = _bdot(tri_ref[...], x)
                a_re = a[:, :STATE_BLOCK] + carry_re
                a_im = a[:, STATE_BLOCK:] + carry_im
                e_re = e_re_ref[:, states]
                e_im = e_im_ref[:, states]
                a_re_b = a_re.astype(BF16)
                a_im_b = a_im.astype(BF16)
                h_re_scr[rows, :] = e_re * a_re_b - e_im * a_im_b
                h_im_scr[rows, :] = e_re * a_im_b + e_im * a_re_b
                last_re = a_re[SSM_CHUNK - 1:SSM_CHUNK, :]
                last_im = a_im[SSM_CHUNK - 1:SSM_CHUNK, :]
                carry_re = lam_re * last_re - lam_im * last_im
                carry_im = lam_re * last_im + lam_im * last_re
            carry_scr[0:1, states] = carry_re
            carry_scr[1:2, states] = carry_im
            y = y + _bdot(h_re_scr[...], cre_ref[states, :]) - _bdot(h_im_scr[...], cim_ref[states, :])
        y_ref[:, chans] = _gelu_tanh(y).astype(BF16)


def _ssm(u, bmat, cre, cim, tri, e_re, e_im, einv_re, einv_im, lam, d_row):
    s = u.shape[0]
    tm = min(SSM_ROWS, s)
    row = pl.BlockSpec((tm, SSM_WIDTH), lambda i: (i, 0))
    consts = (bmat, cre, cim, tri, e_re, e_im, einv_re, einv_im, lam, d_row)
    return pl.pallas_call(
        _ssm_kernel,
        grid=(s // tm,),
        in_specs=[row] + [_resident(c.shape) for c in consts],
        out_specs=row,
        out_shape=jax.ShapeDtypeStruct((s, SSM_WIDTH), BF16),
        scratch_shapes=[pltpu.VMEM((tm, STATE_BLOCK), BF16), pltpu.VMEM((tm, STATE_BLOCK), BF16),
                        pltpu.VMEM((2, N_STATES), F32)],
        compiler_params=_params("arbitrary"),
        name="ssm",
    )(u, *consts)


def _mix_kernel(*refs, n_cast):
    x_ref, o0_ref, o1_ref, o2_ref, l0_ref, l1_ref, l2_ref, y_ref, gates_ref = refs[:9]
    wap_ref, wa_ref, wb_ref, wout_ref = refs[9:13]
    cast_in, h_ref, cast_out = refs[13:13 + n_cast], refs[13 + n_cast], refs[14 + n_cast:]
    _cast_blocks(cast_in, cast_out)
    for rb in range(x_ref.shape[0] // MIX_SUB_ROWS):
        rows = slice(rb * MIX_SUB_ROWS, (rb + 1) * MIX_SUB_ROWS)
        ls = (l0_ref[rows, :], l1_ref[rows, :], l2_ref[rows, :])
        m_max = jnp.maximum(jnp.maximum(ls[0], ls[1]), ls[2])
        es = [jnp.exp2(l - m_max) for l in ls]
        dens = [pltpu.roll(l, HEAD_DIM - STAT_LANES // 2, 1) for l in ls]
        inv = 1.0 / (es[0] * dens[0] + es[1] * dens[1] + es[2] * dens[2])
        heads = []
        for h in range(HEADS_PER_GROUP):
            hs = slice(h * HEAD_DIM, (h + 1) * HEAD_DIM)
            col = slice(h * STAT_LANES, h * STAT_LANES + 1)
            heads.append(sum(jnp.broadcast_to((e * inv)[:, col], (MIX_SUB_ROWS, HEAD_DIM)).astype(BF16)
                             * o_ref[rows, hs] for e, o_ref in zip(es, (o0_ref, o1_ref, o2_ref))))
        attn = jnp.concatenate(heads, axis=1)
        attn_d = _bdot(attn, wap_ref[...])
        y = y_ref[rows, :]
        ssm_d = _bdot(y, wa_ref[...]) * jax.nn.sigmoid(_bdot(y, wb_ref[...]))
        mix = (gates_ref[rows, :D_MODEL] * attn_d.astype(BF16)
               + gates_ref[rows, D_MODEL:] * ssm_d.astype(BF16))
        h_ref[rows, :] = x_ref[rows, :] + _bdot(mix, wout_ref[...])


def _mix(x, os, ls, y, gates, wap, wa, wb, wout, later_weights):
    s = x.shape[0]
    tm = min(MIX_ROWS, s)
    row = lambda w: pl.BlockSpec((tm, w), lambda i: (i, 0))
    weights = (wap, wa, wb, wout)
    cast_specs, cast_shapes = _cast_plan(later_weights, s // tm)
    return pl.pallas_call(
        functools.partial(_mix_kernel, n_cast=len(later_weights)),
        grid=(s // tm,),
        in_specs=[row(D_MODEL)] + [row(GROUP_WIDTH)] * 3 + [row(HEAD_DIM)] * 3
        + [row(SSM_WIDTH), row(2 * D_MODEL)] + [_resident(w.shape) for w in weights] + cast_specs,
        out_specs=[row(D_MODEL)] + cast_specs,
        out_shape=[jax.ShapeDtypeStruct((s, D_MODEL), F32)] + cast_shapes,
        compiler_params=_params("parallel"),
        name="mix",
    )(x, *os, *ls, y, gates, *weights, *later_weights)


def _ffn_kernel(h_ref, p_ref, gffn_ref, wg_ref, wu_ref, wd_ref, wpg_ref, wpp_ref, gfin_ref,
                out_ref, acc_scr):
    for rb in range(h_ref.shape[0] // FFN_SUB_ROWS):
        rows = slice(rb * FFN_SUB_ROWS, (rb + 1) * FFN_SUB_ROWS)
        h = h_ref[rows, :]
        n2 = _rmsnorm(h, gffn_ref[...]).astype(BF16)
        for idx, (c0, width) in enumerate(FFN_CHUNKS):
            gate = _bdot(n2, wg_ref[:, c0:c0 + width])
            up = _bdot(n2, wu_ref[:, c0:c0 + width])
            act = (gate * jax.nn.sigmoid(gate) * up).astype(BF16)
            part = _bdot(act, wd_ref[c0:c0 + width, :])
            if idx == 0:
                acc_scr[rows, :] = h + part
            else:
                acc_scr[rows, :] += part
        h2 = acc_scr[rows, :]
        ple = (jax.nn.sigmoid(_bdot(h2.astype(BF16), wpg_ref[...]))
               * _bdot(p_ref[rows, :].astype(BF16), wpp_ref[...]))
        out_ref[rows, :] = _rmsnorm(h2 + ple, gfin_ref[...])


def _ffn(h, p, g_ffn, wg, wu, wd, wpg, wpp, g_final):
    s = h.shape[0]
    tm = min(FFN_ROWS, s)
    row = lambda w: pl.BlockSpec((tm, w), lambda i: (i, 0))
    consts = (g_ffn, wg, wu, wd, wpg, wpp, g_final)
    return pl.pallas_call(
        _ffn_kernel,
        grid=(s // tm,),
        in_specs=[row(D_MODEL), row(PLE_DIM)] + [_resident(c.shape) for c in consts],
        out_specs=row(D_MODEL),
        out_shape=jax.ShapeDtypeStruct((s, D_MODEL), F32),
        scratch_shapes=[pltpu.VMEM((tm, D_MODEL), F32)],
        compiler_params=_params("parallel"),
        name="ffn",
    )(h, p, *consts)


def _layer(x, p, positions, g_mix, w_in, a_re, a_im, log_dt, b_re, b_im, c_re, c_im, d_skip,
           w_attn_proj, w_glu_a, w_glu_b, w_out, g_ffn, w_ffn_gate, w_ffn_up, w_ffn_down,
           w_ple_gate, w_ple_proj, g_final):
    row = lambda v: v.reshape(1, -1).astype(F32)

    inv_freq = ROPE_THETA ** (-jnp.arange(ROPE_HALF, dtype=F32) * 2.0 / ROPE_DIM)
    bmat, cre, cim, e_re, e_im, einv_re, einv_im, lam, w_in_bf = _ssm_prep(
        a_re, a_im, log_dt, b_re, b_im, c_re, c_im, w_in)
    outs = _proj(x, positions.reshape(1, -1), row(g_mix), inv_freq.reshape(ROPE_HALF, 1), w_in_bf,
                 (w_attn_proj, w_glu_a, w_glu_b, w_out, w_ffn_gate, w_ffn_up, w_ple_gate))
    qkvs, (u, gates) = outs[:N_GROUPS], outs[N_GROUPS:N_GROUPS + 2]
    wap, wga, wgb, wout, wfg, wfu, wpg = outs[N_GROUPS + 2:]

    attn_os, attn_ls = zip(*[_attn_group(qkvs[g], d) for g, d in enumerate(ATTN_DILATIONS)])

    tri = jnp.tril(jnp.ones((SSM_CHUNK, SSM_CHUNK), F32)).astype(BF16)
    y = _ssm(u, bmat, cre, cim, tri, e_re, e_im, einv_re, einv_im, lam, row(d_skip))

    h1, wfd, wpp = _mix(x, attn_os, attn_ls, y, gates, wap, wga, wgb, wout, (w_ffn_down, w_ple_proj))
    return _ffn(h1, p, row(g_ffn), wfg, wfu, wfd, wpg, wpp, row(g_final))


def kernel(x, p, positions, g_mix, w_in, a_re, a_im, log_dt, b_re, b_im, c_re, c_im, d_skip,
           w_attn_proj, w_glu_a, w_glu_b, w_out, g_ffn, w_ffn_gate, w_ffn_up, w_ffn_down,
           w_ple_gate, w_ple_proj, g_final):
    batch, depth = x.shape[0], p.shape[0]
    assert batch == 1 and depth == 1, "kernel supports the stated BATCH=1, DEPTH=1 problem"
    out = _layer(x[0], p[0, 0], positions[0], g_mix[0], w_in[0], a_re[0], a_im[0], log_dt[0],
                 b_re[0], b_im[0], c_re[0], c_im[0], d_skip[0], w_attn_proj[0], w_glu_a[0],
                 w_glu_b[0], w_out[0], g_ffn[0], w_ffn_gate[0], w_ffn_up[0], w_ffn_down[0],
                 w_ple_gate[0], w_ple_proj[0], g_final)
    return out[None]
```

```python
import functools
import math

import jax
import jax.numpy as jnp
from jax import lax
from jax.experimental import pallas as pl
from jax.experimental.pallas import tpu as pltpu

F32 = jnp.float32
BF16 = jnp.bfloat16

D_MODEL = 1024
HEAD_DIM = 128
HEADS_PER_GROUP = 4
GROUP_WIDTH = HEADS_PER_GROUP * HEAD_DIM
ATTN_DILATIONS = (1, 4, 16)
N_GROUPS = len(ATTN_DILATIONS)
QK_WIDTH = N_GROUPS * GROUP_WIDTH
BLOCK = 128
ROPE_THETA = 500000.0
ROPE_DIM = HEAD_DIM // 4
ROPE_HALF = ROPE_DIM // 2
SSM_WIDTH = 512
SSM_GROUP = 16
SSM_GROUPS = SSM_WIDTH // SSM_GROUP
SSM_STATE = 64
N_STATES = SSM_GROUPS * SSM_STATE
SSM_HALVES = 2
HALF_GROUPS = SSM_GROUPS // SSM_HALVES
HALF_WIDTH = SSM_WIDTH // SSM_HALVES
HALF_STATES = N_STATES // SSM_HALVES
STATE_BLOCK = 256
D_FF = 2816
PLE_DIM = 256
EPS = 1e-6
MASK_VALUE = -1e30

V7X_VMEM_LIMIT_BYTES = 56 * 1024 * 1024

PROJ_ROWS = 512
PROJ_SUB_ROWS = 256
ATTN_TOKENS = 2048
STAT_LANES = HEAD_DIM // HEADS_PER_GROUP
SCATTER_STRIDE = 4
SSM_CHUNK = 64
SSM_ROWS = 1024
PREP_STEPS = 2
MIX_ROWS = 1024
MIX_SUB_ROWS = 256
FFN_ROWS = 1024
FFN_SUB_ROWS = 256
FFN_CHUNKS = ((0, 1024), (1024, 1024), (2048, 768))


def _resident(shape):
    return pl.BlockSpec(shape, lambda *_: (0,) * len(shape), pipeline_mode=pl.Buffered(1))


def _params(*semantics):
    return pltpu.CompilerParams(dimension_semantics=semantics,
                                vmem_limit_bytes=V7X_VMEM_LIMIT_BYTES)


def _rmsnorm(x, g):
    return (x * lax.rsqrt(jnp.mean(x * x, axis=-1, keepdims=True) + EPS)) * g


def _bdot(a, b):
    return jnp.dot(a, b, preferred_element_type=F32)


def _cast_plan(weights, steps):
    specs = [pl.BlockSpec((w.shape[0] // steps, w.shape[1]), lambda i: (i, 0)) for w in weights]
    shapes = [jax.ShapeDtypeStruct(w.shape, BF16) for w in weights]
    return specs, shapes


def _cast_blocks(in_refs, out_refs):
    for src, dst in zip(in_refs, out_refs):
        dst[...] = src[...].astype(BF16)


def _proj_kernel(*refs, n_cast):
    x_ref, pos_ref, g_ref, invf_ref, w_ref = refs[:5]
    cast_in, refs = refs[5:5 + n_cast], refs[5 + n_cast:]
    qkv0, qkv1, qkv2, u_ref, gates_ref = refs[:5]
    cast_out, (n_scr, nperm_scr, cos_scr, sin_scr) = refs[5:5 + n_cast], refs[5 + n_cast:]
    _cast_blocks(cast_in, cast_out)
    rows = PROJ_SUB_ROWS
    lane = lax.broadcasted_iota(jnp.int32, (rows, HEAD_DIM), 1)
    first_half = lane < ROPE_HALF
    scale = math.log2(math.e) / math.sqrt(HEAD_DIM)

    for sb in range(x_ref.shape[0] // rows):
        base = sb * rows
        tile_rows = slice(base, base + rows)
        xn = _rmsnorm(x_ref[tile_rows, :], g_ref[...])
        n = xn.astype(BF16)
        for c in range(D_MODEL // HEAD_DIM):
            n_scr[c, tile_rows, :] = xn[:, c * HEAD_DIM:(c + 1) * HEAD_DIM]

        ang = invf_ref[...] * pos_ref[:, tile_rows].astype(F32)
        cos_t = jnp.cos(ang)
        sin_t = jnp.sin(ang)
        rest = (HEAD_DIM - ROPE_DIM, rows)
        cos_scr[tile_rows, :] = jnp.concatenate([cos_t, cos_t, jnp.ones(rest, F32)], axis=0).T
        sin_scr[tile_rows, :] = jnp.concatenate([-sin_t, sin_t, jnp.zeros(rest, F32)], axis=0).T

        for g, (d, qkv_ref) in enumerate(zip(ATTN_DILATIONS, (qkv0, qkv1, qkv2))):
            def residue_major(ref_2d):
                if d == 1:
                    return ref_2d[tile_rows, :]
                return jnp.concatenate(
                    [ref_2d[pl.ds(base + r, rows // d, stride=d), :] for r in range(d)], axis=0)

            cos = residue_major(cos_scr)
            sin_signed = residue_major(sin_scr)
            if d == 1:
                ng = n
            else:
                for c in range(D_MODEL // HEAD_DIM):
                    nperm_scr[tile_rows, c * HEAD_DIM:(c + 1) * HEAD_DIM] = (
                        residue_major(n_scr.at[c]).astype(BF16))
                ng = nperm_scr[tile_rows, :]

            def rotary(t):
                partner = jnp.where(first_half,
                                    pltpu.roll(t, HEAD_DIM - ROPE_HALF, 1),
                                    pltpu.roll(t, ROPE_HALF, 1))
                return t * cos + partner * sin_signed

            c0 = g * GROUP_WIDTH
            zq = _bdot(ng, w_ref[:, c0:c0 + GROUP_WIDTH])
            zk = _bdot(ng, w_ref[:, QK_WIDTH + c0:QK_WIDTH + c0 + GROUP_WIDTH])
            zv = _bdot(ng, w_ref[:, 2 * QK_WIDTH + c0:2 * QK_WIDTH + c0 + GROUP_WIDTH])
            sub_rows = slice(base // d, (base + rows) // d)
            for h in range(HEADS_PER_GROUP):
                hs = slice(h * HEAD_DIM, (h + 1) * HEAD_DIM)
                ks = slice(GROUP_WIDTH + h * HEAD_DIM, GROUP_WIDTH + (h + 1) * HEAD_DIM)
                qkv_ref[:, sub_rows, hs] = (
                    (rotary(zq[:, hs]) * scale).astype(BF16).reshape(d, rows // d, HEAD_DIM))
                qkv_ref[:, sub_rows, ks] = rotary(zk[:, hs]).astype(BF16).reshape(d, rows // d, HEAD_DIM)
            qkv_ref[:, sub_rows, 2 * GROUP_WIDTH:] = zv.astype(BF16).reshape(d, rows // d, GROUP_WIDTH)
        o1 = 3 * QK_WIDTH
        u_ref[tile_rows, :] = _bdot(n, w_ref[:, o1:o1 + SSM_WIDTH])
        o2 = o1 + SSM_WIDTH
        gates_ref[tile_rows, :] = jax.nn.sigmoid(_bdot(n, w_ref[:, o2:o2 + 2 * D_MODEL])).astype(BF16)


def _proj(x, pos_row, g_mix, inv_freq_col, w_in, later_weights):
    s = x.shape[0]
    tm = min(PROJ_ROWS, s)
    row = lambda w: pl.BlockSpec((tm, w), lambda i: (i, 0))
    cast_specs, cast_shapes = _cast_plan(later_weights, s // tm)
    grp_specs, grp_shapes = [], []
    for d in ATTN_DILATIONS:
        grp_specs.append(pl.BlockSpec((d, tm // d, 3 * GROUP_WIDTH), lambda i: (0, i, 0)))
        grp_shapes.append(jax.ShapeDtypeStruct((d, s // d, 3 * GROUP_WIDTH), BF16))
    return pl.pallas_call(
        functools.partial(_proj_kernel, n_cast=len(later_weights)),
        grid=(s // tm,),
        in_specs=[row(D_MODEL), pl.BlockSpec((1, tm), lambda i: (0, i)), _resident((1, D_MODEL)),
                  _resident((ROPE_HALF, 1)), _resident(w_in.shape)] + cast_specs,
        out_specs=grp_specs + [row(SSM_WIDTH), row(2 * D_MODEL)] + cast_specs,
        out_shape=grp_shapes + [jax.ShapeDtypeStruct((s, SSM_WIDTH), F32),
                                jax.ShapeDtypeStruct((s, 2 * D_MODEL), BF16)] + cast_shapes,
        scratch_shapes=[pltpu.VMEM((D_MODEL // HEAD_DIM, tm, HEAD_DIM), F32), pltpu.VMEM((tm, D_MODEL), BF16),
                        pltpu.VMEM((tm, HEAD_DIM), F32), pltpu.VMEM((tm, HEAD_DIM), F32)],
        compiler_params=_params("parallel"),
        name="proj",
    )(x, pos_row, g_mix, inv_freq_col, w_in, *later_weights)


def _attn_kernel(qkv_ref, o_ref, l_ref, prev_scr, *scatter_scr, d, nsub):
    not_first_block = pl.program_id(0) > 0
    two_pass = d > SCATTER_STRIDE
    if two_pass:
        o_scr, l_scr, part_o, part_l = scatter_scr
    elif d > 1:
        o_scr, l_scr = scatter_scr

    @pl.when(pl.program_id(0) == 0)
    def _():
        prev_scr[...] = jnp.zeros_like(prev_scr)

    qi = lax.broadcasted_iota(jnp.int32, (BLOCK, 2 * BLOCK), 0)
    kj = lax.broadcasted_iota(jnp.int32, (BLOCK, 2 * BLOCK), 1)
    rel = BLOCK + qi - kj
    band = (rel >= 0) & (rel <= BLOCK)
    band_first = band & ((kj >= BLOCK) | not_first_block)
    lane = lax.broadcasted_iota(jnp.int32, (BLOCK, HEAD_DIM), 1)

    def residue(r):
        for b in range(nsub):
            rows = slice(b * BLOCK, (b + 1) * BLOCK)
            mask = band_first if b == 0 else band
            token_rows = pl.ds(b * BLOCK * d + r, BLOCK, stride=d) if d > 1 else rows
            if two_pass:
                r_hi, r_lo = divmod(r, SCATTER_STRIDE)
                part_rows = pl.ds(b * BLOCK * (d // SCATTER_STRIDE) + r_hi, BLOCK, stride=d // SCATTER_STRIDE)
            for h in range(HEADS_PER_GROUP):
                hs = slice(h * HEAD_DIM, (h + 1) * HEAD_DIM)
                ks = slice(GROUP_WIDTH + h * HEAD_DIM, GROUP_WIDTH + (h + 1) * HEAD_DIM)
                vs = slice(2 * GROUP_WIDTH + h * HEAD_DIM, 2 * GROUP_WIDTH + (h + 1) * HEAD_DIM)
                q = qkv_ref[r, rows, hs]
                if b == 0:
                    prev_k = slice(h * HEAD_DIM, (h + 1) * HEAD_DIM)
                    prev_v = slice(GROUP_WIDTH + h * HEAD_DIM, GROUP_WIDTH + (h + 1) * HEAD_DIM)
                    kw = jnp.concatenate([prev_scr[r, :, prev_k], qkv_ref[r, rows, ks]], axis=0)
                    vw = jnp.concatenate([prev_scr[r, :, prev_v], qkv_ref[r, rows, vs]], axis=0)
                else:
                    win = slice((b - 1) * BLOCK, (b + 1) * BLOCK)
                    kw = qkv_ref[r, win, ks]
                    vw = qkv_ref[r, win, vs]
                s = lax.dot_general(q, kw, (((1,), (1,)), ((), ())), preferred_element_type=F32)
                s = jnp.where(mask, s, MASK_VALUE)
                m = jnp.max(s, axis=-1, keepdims=True)
                p = jnp.exp2(s - m)
                den = jnp.sum(p, axis=-1, keepdims=True)
                pv = _bdot(p.astype(BF16), vw)
                if two_pass:
                    part_o[r_lo, h, part_rows, :] = pv
                elif d > 1:
                    o_scr[h, token_rows, :] = pv
                else:
                    o_ref[rows, hs] = pv.astype(BF16)
                m_b = jnp.broadcast_to(m, (BLOCK, HEAD_DIM))
                den_b = jnp.broadcast_to(den, (BLOCK, HEAD_DIM))
                if h > 0:
                    m_b = jnp.where(lane >= h * STAT_LANES, m_b, stats)
                stats = jnp.where(lane >= h * STAT_LANES + STAT_LANES // 2, den_b, m_b)
            if two_pass:
                part_l[r_lo, part_rows, :] = stats
            elif d > 1:
                l_scr[token_rows, :] = stats
            else:
                l_ref[rows, :] = stats

    for r in range(d):
        residue(r)

    if two_pass:
        part_len = part_l.shape[1]
        for r_lo in range(SCATTER_STRIDE):
            token_rows = pl.ds(r_lo, part_len, stride=SCATTER_STRIDE)
            for h in range(HEADS_PER_GROUP):
                o_scr[h, token_rows, :] = part_o[r_lo, h]
            l_scr[token_rows, :] = part_l[r_lo]
    if d > 1:
        for h in range(HEADS_PER_GROUP):
            o_ref[:, h * HEAD_DIM:(h + 1) * HEAD_DIM] = o_scr[h].astype(BF16)
        l_ref[...] = l_scr[...]
    prev_scr[...] = qkv_ref[:, (nsub - 1) * BLOCK:nsub * BLOCK, GROUP_WIDTH:]


def _attn_group(qkv, dilation):
    sub_len = qkv.shape[1]
    s = sub_len * dilation
    step_tokens = min(ATTN_TOKENS, s)
    qb = step_tokens // dilation
    nsub = qb // BLOCK
    cur = pl.BlockSpec((dilation, qb, 3 * GROUP_WIDTH), lambda i: (0, i, 0))
    scatter_scratch = [] if dilation == 1 else [
        pltpu.VMEM((HEADS_PER_GROUP, step_tokens, HEAD_DIM), F32), pltpu.VMEM((step_tokens, HEAD_DIM), F32)]
    if dilation > SCATTER_STRIDE:
        part = step_tokens // SCATTER_STRIDE
        scatter_scratch += [pltpu.VMEM((SCATTER_STRIDE, HEADS_PER_GROUP, part, HEAD_DIM), F32),
                            pltpu.VMEM((SCATTER_STRIDE, part, HEAD_DIM), F32)]
    return pl.pallas_call(
        functools.partial(_attn_kernel, d=dilation, nsub=nsub),
        grid=(s // step_tokens,),
        in_specs=[cur],
        out_specs=[pl.BlockSpec((step_tokens, GROUP_WIDTH), lambda i: (i, 0)),
                   pl.BlockSpec((step_tokens, HEAD_DIM), lambda i: (i, 0))],
        out_shape=[jax.ShapeDtypeStruct((s, GROUP_WIDTH), BF16), jax.ShapeDtypeStruct((s, HEAD_DIM), F32)],
        scratch_shapes=[pltpu.VMEM((dilation, BLOCK, 2 * GROUP_WIDTH), BF16)] + scatter_scratch,
        compiler_params=_params("arbitrary"),
        name=f"attn_d{dilation}",
    )(qkv)


def _discretize(lr, li, log_dt):
    dt = jnp.exp(log_dt)
    mag = jnp.exp(lr * dt)
    bar_re = mag * jnp.cos(li * dt)
    bar_im = mag * jnp.sin(li * dt)
    nr = bar_re - 1.0
    ni = bar_im
    den = lr * lr + li * li
    return bar_re, bar_im, (nr * lr + ni * li) / den, (ni * lr - nr * li) / den


def _expand_block_diag(compact, n_blocks):
    rows, b = compact.shape
    a = rows // n_blocks
    wide = n_blocks * b
    src_lane = lax.broadcasted_iota(jnp.int32, (b, wide), 0)
    dst_lane = lax.broadcasted_iota(jnp.int32, (b, wide), 1)
    tiled = _bdot(compact, (dst_lane % b == src_lane).astype(BF16))
    row_block = lax.broadcasted_iota(jnp.int32, (rows, wide), 0) // a
    col_block = lax.broadcasted_iota(jnp.int32, (rows, wide), 1) // b
    return jnp.where(row_block == col_block, tiled, 0.0).astype(BF16)


def _ssm_prep_kernel(rows_ref, rep_ref, b_ref, c_ref, w_in_ref,
                     bmat_ref, cre_ref, cim_ref, e_re_ref, e_im_ref, einv_re_ref, einv_im_ref, lam_ref,
                     w_in_bf_ref):
    w_in_bf_ref[...] = w_in_ref[...].astype(BF16)
    lr = rows_ref[0:1, :]
    li = rows_ref[1:2, :]
    dt = jnp.exp(rows_ref[2:3, :])

    @pl.when(pl.program_id(0) == 0)
    def _():
        _, _, z_re, z_im = _discretize(rep_ref[0], rep_ref[1], rep_ref[2])
        b_re = b_ref[0]
        b_im = b_ref[1]
        bb_re = (z_re * b_re - z_im * b_im).astype(BF16)
        bb_im = (z_re * b_im + z_im * b_re).astype(BF16)
        for hf in range(SSM_HALVES):
            rows = slice(hf * HALF_WIDTH, (hf + 1) * HALF_WIDTH)
            bmat_ref[rows, :HALF_STATES] = _expand_block_diag(bb_re[rows], HALF_GROUPS)
            bmat_ref[rows, HALF_STATES:] = _expand_block_diag(bb_im[rows], HALF_GROUPS)
            srows = slice(hf * HALF_STATES, (hf + 1) * HALF_STATES)
            cre_ref[srows, :] = _expand_block_diag(c_ref[0, srows, :].astype(BF16), HALF_GROUPS)
            cim_ref[srows, :] = _expand_block_diag(c_ref[1, srows, :].astype(BF16), HALF_GROUPS)
        whole = float(SSM_CHUNK)
        chunk_mag = jnp.exp(whole * (lr * dt))
        lam_ref[0:1, :] = chunk_mag * jnp.cos(whole * (li * dt))
        lam_ref[1:2, :] = chunk_mag * jnp.sin(whole * (li * dt))

    rows_per_step = e_re_ref.shape[0]
    t = (lax.broadcasted_iota(jnp.int32, (rows_per_step, N_STATES), 0)
         + pl.program_id(0) * rows_per_step).astype(F32)
    grow = jnp.exp(t * (lr * dt))
    theta = t * (li * dt)
    c = jnp.cos(theta)
    s = jnp.sin(theta)
    e_re_ref[...] = (grow * c).astype(BF16)
    e_im_ref[...] = (grow * s).astype(BF16)
    shrink = jnp.exp(-t * (lr * dt))
    einv_re_ref[...] = (shrink * c).astype(BF16)
    einv_im_ref[...] = (-(shrink * s)).astype(BF16)


def _ssm_prep(a_re, a_im, log_dt, b_re, b_im, c_re, c_im, w_in):
    per_state = jnp.stack([a_re, a_im, jnp.broadcast_to(log_dt[:, None], (SSM_GROUPS, SSM_STATE))])
    params = (per_state.reshape(3, N_STATES),
              jnp.repeat(per_state, SSM_GROUP, axis=1),
              jnp.swapaxes(jnp.stack([b_re, b_im]), 2, 3).reshape(2, SSM_WIDTH, SSM_STATE),
              jnp.swapaxes(jnp.stack([c_re, c_im]), 2, 3).reshape(2, N_STATES, SSM_GROUP))
    whole = lambda shape: pl.BlockSpec(shape, lambda i: (0,) * len(shape))
    slab = lambda arr: pl.BlockSpec((arr.shape[0] // PREP_STEPS, arr.shape[1]), lambda i: (i, 0))
    tab = jax.ShapeDtypeStruct((SSM_CHUNK, N_STATES), BF16)
    cmat = jax.ShapeDtypeStruct((N_STATES, HALF_WIDTH), BF16)
    bmat = jax.ShapeDtypeStruct((SSM_WIDTH, 2 * HALF_STATES), BF16)
    lam = jax.ShapeDtypeStruct((2, N_STATES), F32)
    return pl.pallas_call(
        _ssm_prep_kernel,
        grid=(PREP_STEPS,),
        in_specs=[whole(p.shape) for p in params] + [slab(w_in)],
        out_specs=[whole(bmat.shape), whole(cmat.shape), whole(cmat.shape), slab(tab), slab(tab), slab(tab),
                   slab(tab), whole(lam.shape), slab(w_in)],
        out_shape=[bmat, cmat, cmat, tab, tab, tab, tab, lam, jax.ShapeDtypeStruct(w_in.shape, BF16)],
        compiler_params=_params("arbitrary"),
        name="ssm_prep",
    )(*params, w_in)


def _gelu_tanh(x):
    return 0.5 * x * (1.0 + jnp.tanh(math.sqrt(2.0 / math.pi) * (x + 0.044715 * (x * x * x))))


def _ssm_kernel(u_ref, bmat_ref, cre_ref, cim_ref, tri_ref, e_re_ref, e_im_ref, einv_re_ref, einv_im_ref,
                lam_ref, d_ref, y_ref, h_re_scr, h_im_scr, carry_scr):
    @pl.when(pl.program_id(0) == 0)
    def _():
        carry_scr[...] = jnp.zeros_like(carry_scr)

    u = u_ref[...]
    ub = u.astype(BF16)
    for hf in range(SSM_HALVES):
        chans = slice(hf * HALF_WIDTH, (hf + 1) * HALF_WIDTH)
        y = d_ref[:, chans] * u[:, chans]
        for blk in range(HALF_STATES // STATE_BLOCK):
            local = slice(blk * STATE_BLOCK, (blk + 1) * STATE_BLOCK)
            local_im = slice(HALF_STATES + blk * STATE_BLOCK, HALF_STATES + (blk + 1) * STATE_BLOCK)
            states = slice(hf * HALF_STATES + blk * STATE_BLOCK, hf * HALF_STATES + (blk + 1) * STATE_BLOCK)
            bu_re_all = _bdot(ub[:, chans], bmat_ref[chans, local])
            bu_im_all = _bdot(ub[:, chans], bmat_ref[chans, local_im])
            lam_re = lam_ref[0:1, states]
            lam_im = lam_ref[1:2, states]
            carry_re = carry_scr[0:1, states]
            carry_im = carry_scr[1:2, states]
            for c in range(u.shape[0] // SSM_CHUNK):
                rows = slice(c * SSM_CHUNK, (c + 1) * SSM_CHUNK)
                bu_re = bu_re_all[rows].astype(BF16)
                bu_im = bu_im_all[rows].astype(BF16)
                einv_re = einv_re_ref[:, states]
                einv_im = einv_im_ref[:, states]
                x = jnp.concatenate([bu_re * einv_re - bu_im * einv_im,
                                     bu_re * einv_im + bu_im * einv_re], axis=1)
                a = _bdot(tri_ref[...], x)
                a_re = a[:, :STATE_BLOCK] + carry_re
                a_im = a[:, STATE_BLOCK:] + carry_im
                e_re = e_re_ref[:, states]
                e_im = e_im_ref[:, states]
                a_re_b = a_re.astype(BF16)
                a_im_b = a_im.astype(BF16)
                h_re_scr[rows, :] = e_re * a_re_b - e_im * a_im_b
                h_im_scr[rows, :] = e_re * a_im_b + e_im * a_re_b
                last_re = a_re[SSM_CHUNK - 1:SSM_CHUNK, :]
                last_im = a_im[SSM_CHUNK - 1:SSM_CHUNK, :]
                carry_re = lam_re * last_re - lam_im * last_im
                carry_im = lam_re * last_im + lam_im * last_re
            carry_scr[0:1, states] = carry_re
            carry_scr[1:2, states] = carry_im
            y = y + _bdot(h_re_scr[...], cre_ref[states, :]) - _bdot(h_im_scr[...], cim_ref[states, :])
        y_ref[:, chans] = _gelu_tanh(y).astype(BF16)


def _ssm(u, bmat, cre, cim, tri, e_re, e_im, einv_re, einv_im, lam, d_row):
    s = u.shape[0]
    tm = min(SSM_ROWS, s)
    row = pl.BlockSpec((tm, SSM_WIDTH), lambda i: (i, 0))
    consts = (bmat, cre, cim, tri, e_re, e_im, einv_re, einv_im, lam, d_row)
    return pl.pallas_call(
        _ssm_kernel,
        grid=(s // tm,),
        in_specs=[row] + [_resident(c.shape) for c in consts],
        out_specs=row,
        out_shape=jax.ShapeDtypeStruct((s, SSM_WIDTH), BF16),
        scratch_shapes=[pltpu.VMEM((tm, STATE_BLOCK), BF16), pltpu.VMEM((tm, STATE_BLOCK), BF16),
                        pltpu.VMEM((2, N_STATES), F32)],
        compiler_params=_params("arbitrary"),
        name="ssm",
    )(u, *consts)


def _mix_kernel(*refs, n_cast):
    x_ref, o0_ref, o1_ref, o2_ref, l0_ref, l1_ref, l2_ref, y_ref, gates_ref = refs[:9]
    wap_ref, wa_ref, wb_ref, wout_ref = refs[9:13]
    cast_in, h_ref, cast_out = refs[13:13 + n_cast], refs[13 + n_cast], refs[14 + n_cast:]
    _cast_blocks(cast_in, cast_out)
    for rb in range(x_ref.shape[0] // MIX_SUB_ROWS):
        rows = slice(rb * MIX_SUB_ROWS, (rb + 1) * MIX_SUB_ROWS)
        ls = (l0_ref[rows, :], l1_ref[rows, :], l2_ref[rows, :])
        m_max = jnp.maximum(jnp.maximum(ls[0], ls[1]), ls[2])
        es = [jnp.exp2(l - m_max) for l in ls]
        dens = [pltpu.roll(l, HEAD_DIM - STAT_LANES // 2, 1) for l in ls]
        inv = 1.0 / (es[0] * dens[0] + es[1] * dens[1] + es[2] * dens[2])
        heads = []
        for h in range(HEADS_PER_GROUP):
            hs = slice(h * HEAD_DIM, (h + 1) * HEAD_DIM)
            col = slice(h * STAT_LANES, h * STAT_LANES + 1)
            heads.append(sum(jnp.broadcast_to((e * inv)[:, col], (MIX_SUB_ROWS, HEAD_DIM)).astype(BF16)
                             * o_ref[rows, hs] for e, o_ref in zip(es, (o0_ref, o1_ref, o2_ref))))
        attn = jnp.concatenate(heads, axis=1)
        attn_d = _bdot(attn, wap_ref[...])
        y = y_ref[rows, :]
        ssm_d = _bdot(y, wa_ref[...]) * jax.nn.sigmoid(_bdot(y, wb_ref[...]))
        mix = (gates_ref[rows, :D_MODEL] * attn_d.astype(BF16)
               + gates_ref[rows, D_MODEL:] * ssm_d.astype(BF16))
        h_ref[rows, :] = x_ref[rows, :] + _bdot(mix, wout_ref[...])


def _mix(x, os, ls, y, gates, wap, wa, wb, wout, later_weights):
    s = x.shape[0]
    tm = min(MIX_ROWS, s)
    row = lambda w: pl.BlockSpec((tm, w), lambda i: (i, 0))
    weights = (wap, wa, wb, wout)
    cast_specs, cast_shapes = _cast_plan(later_weights, s // tm)
    return pl.pallas_call(
        functools.partial(_mix_kernel, n_cast=len(later_weights)),
        grid=(s // tm,),
        in_specs=[row(D_MODEL)] + [row(GROUP_WIDTH)] * 3 + [row(HEAD_DIM)] * 3
        + [row(SSM_WIDTH), row(2 * D_MODEL)] + [_resident(w.shape) for w in weights] + cast_specs,
        out_specs=[row(D_MODEL)] + cast_specs,
        out_shape=[jax.ShapeDtypeStruct((s, D_MODEL), F32)] + cast_shapes,
        compiler_params=_params("parallel"),
        name="mix",
    )(x, *os, *ls, y, gates, *weights, *later_weights)


def _ffn_kernel(h_ref, p_ref, gffn_ref, wg_ref, wu_ref, wd_ref, wpg_ref, wpp_ref, gfin_ref,
                out_ref, acc_scr):
    for rb in range(h_ref.shape[0] // FFN_SUB_ROWS):
        rows = slice(rb * FFN_SUB_ROWS, (rb + 1) * FFN_SUB_ROWS)
        h = h_ref[rows, :]
        n2 = _rmsnorm(h, gffn_ref[...]).astype(BF16)
        for idx, (c0, width) in enumerate(FFN_CHUNKS):
            gate = _bdot(n2, wg_ref[:, c0:c0 + width])
            up = _bdot(n2, wu_ref[:, c0:c0 + width])
            act = (gate * jax.nn.sigmoid(gate) * up).astype(BF16)
            part = _bdot(act, wd_ref[c0:c0 + width, :])
            if idx == 0:
                acc_scr[rows, :] = h + part
            else:
                acc_scr[rows, :] += part
        h2 = acc_scr[rows, :]
        ple = (jax.nn.sigmoid(_bdot(h2.astype(BF16), wpg_ref[...]))
               * _bdot(p_ref[rows, :].astype(BF16), wpp_ref[...]))
        out_ref[rows, :] = _rmsnorm(h2 + ple, gfin_ref[...])


def _ffn(h, p, g_ffn, wg, wu, wd, wpg, wpp, g_final):
    s = h.shape[0]
    tm = min(FFN_ROWS, s)
    row = lambda w: pl.BlockSpec((tm, w), lambda i: (i, 0))
    consts = (g_ffn, wg, wu, wd, wpg, wpp, g_final)
    return pl.pallas_call(
        _ffn_kernel,
        grid=(s // tm,),
        in_specs=[row(D_MODEL), row(PLE_DIM)] + [_resident(c.shape) for c in consts],
        out_specs=row(D_MODEL),
        out_shape=jax.ShapeDtypeStruct((s, D_MODEL), F32),
        scratch_shapes=[pltpu.VMEM((tm, D_MODEL), F32)],
        compiler_params=_params("parallel"),
        name="ffn",
    )(h, p, *consts)


def _layer(x, p, positions, g_mix, w_in, a_re, a_im, log_dt, b_re, b_im, c_re, c_im, d_skip,
           w_attn_proj, w_glu_a, w_glu_b, w_out, g_ffn, w_ffn_gate, w_ffn_up, w_ffn_down,
           w_ple_gate, w_ple_proj, g_final):
    row = lambda v: v.reshape(1, -1).astype(F32)

    inv_freq = ROPE_THETA ** (-jnp.arange(ROPE_HALF, dtype=F32) * 2.0 / ROPE_DIM)
    bmat, cre, cim, e_re, e_im, einv_re, einv_im, lam, w_in_bf = _ssm_prep(
        a_re, a_im, log_dt, b_re, b_im, c_re, c_im, w_in)
    outs = _proj(x, positions.reshape(1, -1), row(g_mix), inv_freq.reshape(ROPE_HALF, 1), w_in_bf,
                 (w_attn_proj, w_glu_a, w_glu_b, w_out, w_ffn_gate, w_ffn_up, w_ple_gate))
    qkvs, (u, gates) = outs[:N_GROUPS], outs[N_GROUPS:N_GROUPS + 2]
    wap, wga, wgb, wout, wfg, wfu, wpg = outs[N_GROUPS + 2:]

    tri = jnp.tril(jnp.ones((SSM_CHUNK, SSM_CHUNK), F32)).astype(BF16)
    y = _ssm(u, bmat, cre, cim, tri, e_re, e_im, einv_re, einv_im, lam, row(d_skip))

    attn_os, attn_ls = zip(*[_attn_group(qkvs[g], d) for g, d in enumerate(ATTN_DILATIONS)])

    h1, wfd, wpp = _mix(x, attn_os, attn_ls, y, gates, wap, wga, wgb, wout, (w_ffn_down, w_ple_proj))
    return _ffn(h1, p, row(g_ffn), wfg, wfu, wfd, wpg, wpp, row(g_final))


def kernel(x, p, positions, g_mix, w_in, a_re, a_im, log_dt, b_re, b_im, c_re, c_im, d_skip,
           w_attn_proj, w_glu_a, w_glu_b, w_out, g_ffn, w_ffn_gate, w_ffn_up, w_ffn_down,
           w_ple_gate, w_ple_proj, g_final):
    batch, depth = x.shape[0], p.shape[0]
    assert batch == 1 and depth == 1, "kernel supports the stated BATCH=1, DEPTH=1 problem"
    out = _layer(x[0], p[0, 0], positions[0], g_mix[0], w_in[0], a_re[0], a_im[0], log_dt[0],
                 b_re[0], b_im[0], c_re[0], c_im[0], d_skip[0], w_attn_proj[0], w_glu_a[0],
                 w_glu_b[0], w_out[0], g_ffn[0], w_ffn_gate[0], w_ffn_up[0], w_ffn_down[0],
                 w_ple_gate[0], w_ple_proj[0], g_final)
    return out[None]
```

```python
import functools
import math

import jax
import jax.numpy as jnp
from jax import lax
from jax.experimental import pallas as pl
from jax.experimental.pallas import tpu as pltpu

F32 = jnp.float32
BF16 = jnp.bfloat16

D_MODEL = 1024
HEAD_DIM = 128
HEADS_PER_GROUP = 4
GROUP_WIDTH = HEADS_PER_GROUP * HEAD_DIM
ATTN_DILATIONS = (1, 4, 16)
N_GROUPS = len(ATTN_DILATIONS)
QK_WIDTH = N_GROUPS * GROUP_WIDTH
BLOCK = 128
ROPE_THETA = 500000.0
ROPE_DIM = HEAD_DIM // 4
ROPE_HALF = ROPE_DIM // 2
SSM_WIDTH = 512
SSM_GROUP = 16
SSM_GROUPS = SSM_WIDTH // SSM_GROUP
SSM_STATE = 64
N_STATES = SSM_GROUPS * SSM_STATE
SSM_HALVES = 2
HALF_GROUPS = SSM_GROUPS // SSM_HALVES
HALF_WIDTH = SSM_WIDTH // SSM_HALVES
HALF_STATES = N_STATES // SSM_HALVES
STATE_BLOCK = 256
D_FF = 2816
PLE_DIM = 256
EPS = 1e-6
MASK_VALUE = -1e30

V7X_VMEM_LIMIT_BYTES = 56 * 1024 * 1024

PROJ_ROWS = 512
PROJ_SUB_ROWS = 256
ATTN_TOKENS = 2048
STAT_LANES = HEAD_DIM // HEADS_PER_GROUP
SCATTER_STRIDE = 4
SSM_CHUNK = 64
SSM_ROWS = 1024
PREP_STEPS = 2
MIX_ROWS = 1024
MIX_SUB_ROWS = 256
FFN_ROWS = 1024
FFN_SUB_ROWS = 256
FFN_CHUNKS = ((0, 1024), (1024, 1024), (2048, 768))


def _resident(shape):
    return pl.BlockSpec(shape, lambda *_: (0,) * len(shape), pipeline_mode=pl.Buffered(1))


def _params(*semantics):
    return pltpu.CompilerParams(dimension_semantics=semantics,
                                vmem_limit_bytes=V7X_VMEM_LIMIT_BYTES)


def _rmsnorm(x, g):
    return (x * lax.rsqrt(jnp.mean(x * x, axis=-1, keepdims=True) + EPS)) * g


def _bdot(a, b):
    return jnp.dot(a, b, preferred_element_type=F32)


def _cast_plan(weights, steps):
    specs = [pl.BlockSpec((w.shape[0] // steps, w.shape[1]), lambda i: (i, 0)) for w in weights]
    shapes = [jax.ShapeDtypeStruct(w.shape, BF16) for w in weights]
    return specs, shapes


def _cast_blocks(in_refs, out_refs):
    for src, dst in zip(in_refs, out_refs):
        dst[...] = src[...].astype(BF16)


def _proj_kernel(*refs, n_cast):
    x_ref, pos_ref, g_ref, invf_ref, w_ref = refs[:5]
    cast_in, refs = refs[5:5 + n_cast], refs[5 + n_cast:]
    qkv0, qkv1, qkv2, u_ref, gates_ref = refs[:5]
    cast_out, (n_scr, nperm_scr, cos_scr, sin_scr) = refs[5:5 + n_cast], refs[5 + n_cast:]
    _cast_blocks(cast_in, cast_out)
    rows = PROJ_SUB_ROWS
    lane = lax.broadcasted_iota(jnp.int32, (rows, HEAD_DIM), 1)
    first_half = lane < ROPE_HALF
    scale = math.log2(math.e) / math.sqrt(HEAD_DIM)

    for sb in range(x_ref.shape[0] // rows):
        base = sb * rows
        tile_rows = slice(base, base + rows)
        xn = _rmsnorm(x_ref[tile_rows, :], g_ref[...])
        n = xn.astype(BF16)
        for c in range(D_MODEL // HEAD_DIM):
            n_scr[c, tile_rows, :] = xn[:, c * HEAD_DIM:(c + 1) * HEAD_DIM]

        ang = invf_ref[...] * pos_ref[:, tile_rows].astype(F32)
        cos_t = jnp.cos(ang)
        sin_t = jnp.sin(ang)
        rest = (HEAD_DIM - ROPE_DIM, rows)
        cos_scr[tile_rows, :] = jnp.concatenate([cos_t, cos_t, jnp.ones(rest, F32)], axis=0).T
        sin_scr[tile_rows, :] = jnp.concatenate([-sin_t, sin_t, jnp.zeros(rest, F32)], axis=0).T

        for g, (d, qkv_ref) in enumerate(zip(ATTN_DILATIONS, (qkv0, qkv1, qkv2))):
            def residue_major(ref_2d):
                if d == 1:
                    return ref_2d[tile_rows, :]
                return jnp.concatenate(
                    [ref_2d[pl.ds(base + r, rows // d, stride=d), :] for r in range(d)], axis=0)

            cos = residue_major(cos_scr)
            sin_signed = residue_major(sin_scr)
            if d == 1:
                ng = n
            else:
                for c in range(D_MODEL // HEAD_DIM):
                    nperm_scr[tile_rows, c * HEAD_DIM:(c + 1) * HEAD_DIM] = (
                        residue_major(n_scr.at[c]).astype(BF16))
                ng = nperm_scr[tile_rows, :]

            def rotary(t):
                partner = jnp.where(first_half,
                                    pltpu.roll(t, HEAD_DIM - ROPE_HALF, 1),
                                    pltpu.roll(t, ROPE_HALF, 1))
                return t * cos + partner * sin_signed

            c0 = g * GROUP_WIDTH
            zq = _bdot(ng, w_ref[:, c0:c0 + GROUP_WIDTH])
            zk = _bdot(ng, w_ref[:, QK_WIDTH + c0:QK_WIDTH + c0 + GROUP_WIDTH])
            zv = _bdot(ng, w_ref[:, 2 * QK_WIDTH + c0:2 * QK_WIDTH + c0 + GROUP_WIDTH])
            sub_rows = slice(base // d, (base + rows) // d)
            for h in range(HEADS_PER_GROUP):
                hs = slice(h * HEAD_DIM, (h + 1) * HEAD_DIM)
                ks = slice(GROUP_WIDTH + h * HEAD_DIM, GROUP_WIDTH + (h + 1) * HEAD_DIM)
                qkv_ref[:, sub_rows, hs] = (
                    (rotary(zq[:, hs]) * scale).astype(BF16).reshape(d, rows // d, HEAD_DIM))
                qkv_ref[:, sub_rows, ks] = rotary(zk[:, hs]).astype(BF16).reshape(d, rows // d, HEAD_DIM)
            qkv_ref[:, sub_rows, 2 * GROUP_WIDTH:] = zv.astype(BF16).reshape(d, rows // d, GROUP_WIDTH)
        o1 = 3 * QK_WIDTH
        u_ref[tile_rows, :] = _bdot(n, w_ref[:, o1:o1 + SSM_WIDTH])
        o2 = o1 + SSM_WIDTH
        gates_ref[tile_rows, :] = jax.nn.sigmoid(_bdot(n, w_ref[:, o2:o2 + 2 * D_MODEL])).astype(BF16)


def _proj(x, pos_row, g_mix, inv_freq_col, w_in, later_weights):
    s = x.shape[0]
    tm = min(PROJ_ROWS, s)
    row = lambda w: pl.BlockSpec((tm, w), lambda i: (i, 0))
    cast_specs, cast_shapes = _cast_plan(later_weights, s // tm)
    grp_specs, grp_shapes = [], []
    for d in ATTN_DILATIONS:
        grp_specs.append(pl.BlockSpec((d, tm // d, 3 * GROUP_WIDTH), lambda i: (0, i, 0)))
        grp_shapes.append(jax.ShapeDtypeStruct((d, s // d, 3 * GROUP_WIDTH), BF16))
    return pl.pallas_call(
        functools.partial(_proj_kernel, n_cast=len(later_weights)),
        grid=(s // tm,),
        in_specs=[row(D_MODEL), pl.BlockSpec((1, tm), lambda i: (0, i)), _resident((1, D_MODEL)),
                  _resident((ROPE_HALF, 1)), _resident(w_in.shape)] + cast_specs,
        out_specs=grp_specs + [row(SSM_WIDTH), row(2 * D_MODEL)] + cast_specs,
        out_shape=grp_shapes + [jax.ShapeDtypeStruct((s, SSM_WIDTH), F32),
                                jax.ShapeDtypeStruct((s, 2 * D_MODEL), BF16)] + cast_shapes,
        scratch_shapes=[pltpu.VMEM((D_MODEL // HEAD_DIM, tm, HEAD_DIM), F32), pltpu.VMEM((tm, D_MODEL), BF16),
                        pltpu.VMEM((tm, HEAD_DIM), F32), pltpu.VMEM((tm, HEAD_DIM), F32)],
        compiler_params=_params("parallel"),
        name="proj",
    )(x, pos_row, g_mix, inv_freq_col, w_in, *later_weights)


def _attn_kernel(qkv_ref, o_ref, l_ref, prev_scr, *scatter_scr, d, nsub):
    not_first_block = pl.program_id(0) > 0
    two_pass = d > SCATTER_STRIDE
    if two_pass:
        o_scr, l_scr, part_o, part_l = scatter_scr
    elif d > 1:
        o_scr, l_scr = scatter_scr

    @pl.when(pl.program_id(0) == 0)
    def _():
        prev_scr[...] = jnp.zeros_like(prev_scr)

    qi = lax.broadcasted_iota(jnp.int32, (BLOCK, 2 * BLOCK), 0)
    kj = lax.broadcasted_iota(jnp.int32, (BLOCK, 2 * BLOCK), 1)
    rel = BLOCK + qi - kj
    band = (rel >= 0) & (rel <= BLOCK)
    band_first = band & ((kj >= BLOCK) | not_first_block)
    lane = lax.broadcasted_iota(jnp.int32, (BLOCK, HEAD_DIM), 1)

    def residue(r):
        for b in range(nsub):
            rows = slice(b * BLOCK, (b + 1) * BLOCK)
            mask = band_first if b == 0 else band
            token_rows = pl.ds(b * BLOCK * d + r, BLOCK, stride=d) if d > 1 else rows
            if two_pass:
                r_hi, r_lo = divmod(r, SCATTER_STRIDE)
                part_rows = pl.ds(b * BLOCK * (d // SCATTER_STRIDE) + r_hi, BLOCK, stride=d // SCATTER_STRIDE)
            for h in range(HEADS_PER_GROUP):
                hs = slice(h * HEAD_DIM, (h + 1) * HEAD_DIM)
                ks = slice(GROUP_WIDTH + h * HEAD_DIM, GROUP_WIDTH + (h + 1) * HEAD_DIM)
                vs = slice(2 * GROUP_WIDTH + h * HEAD_DIM, 2 * GROUP_WIDTH + (h + 1) * HEAD_DIM)
                q = qkv_ref[r, rows, hs]
                if b == 0:
                    prev_k = slice(h * HEAD_DIM, (h + 1) * HEAD_DIM)
                    prev_v = slice(GROUP_WIDTH + h * HEAD_DIM, GROUP_WIDTH + (h + 1) * HEAD_DIM)
                    kw = jnp.concatenate([prev_scr[r, :, prev_k], qkv_ref[r, rows, ks]], axis=0)
                    vw = jnp.concatenate([prev_scr[r, :, prev_v], qkv_ref[r, rows, vs]], axis=0)
                else:
                    win = slice((b - 1) * BLOCK, (b + 1) * BLOCK)
                    kw = qkv_ref[r, win, ks]
                    vw = qkv_ref[r, win, vs]
                s = lax.dot_general(q, kw, (((1,), (1,)), ((), ())), preferred_element_type=F32)
                s = jnp.where(mask, s, MASK_VALUE)
                m = jnp.max(s, axis=-1, keepdims=True)
                p = jnp.exp2(s - m)
                den = jnp.sum(p, axis=-1, keepdims=True)
                pv = _bdot(p.astype(BF16), vw)
                if two_pass:
                    part_o[r_lo, h, part_rows, :] = pv
                elif d > 1:
                    o_scr[h, token_rows, :] = pv
                else:
                    o_ref[rows, hs] = pv.astype(BF16)
                m_b = jnp.broadcast_to(m, (BLOCK, HEAD_DIM))
                den_b = jnp.broadcast_to(den, (BLOCK, HEAD_DIM))
                if h > 0:
                    m_b = jnp.where(lane >= h * STAT_LANES, m_b, stats)
                stats = jnp.where(lane >= h * STAT_LANES + STAT_LANES // 2, den_b, m_b)
            if two_pass:
                part_l[r_lo, part_rows, :] = stats
            elif d > 1:
                l_scr[token_rows, :] = stats
            else:
                l_ref[rows, :] = stats

    for r in range(d):
        residue(r)

    if two_pass:
        part_len = part_l.shape[1]
        for r_lo in range(SCATTER_STRIDE):
            token_rows = pl.ds(r_lo, part_len, stride=SCATTER_STRIDE)
            for h in range(HEADS_PER_GROUP):
                o_scr[h, token_rows, :] = part_o[r_lo, h]
            l_scr[token_rows, :] = part_l[r_lo]
    if d > 1:
        for h in range(HEADS_PER_GROUP):
            o_ref[:, h * HEAD_DIM:(h + 1) * HEAD_DIM] = o_scr[h].astype(BF16)
        l_ref[...] = l_scr[...]
    prev_scr[...] = qkv_ref[:, (nsub - 1) * BLOCK:nsub * BLOCK, GROUP_WIDTH:]


def _attn_group(qkv, dilation):
    sub_len = qkv.shape[1]
    s = sub_len * dilation
    step_tokens = min(ATTN_TOKENS, s)
    qb = step_tokens // dilation
    nsub = qb // BLOCK
    cur = pl.BlockSpec((dilation, qb, 3 * GROUP_WIDTH), lambda i: (0, i, 0))
    scatter_scratch = [] if dilation == 1 else [
        pltpu.VMEM((HEADS_PER_GROUP, step_tokens, HEAD_DIM), F32), pltpu.VMEM((step_tokens, HEAD_DIM), F32)]
    if dilation > SCATTER_STRIDE:
        part = step_tokens // SCATTER_STRIDE
        scatter_scratch += [pltpu.VMEM((SCATTER_STRIDE, HEADS_PER_GROUP, part, HEAD_DIM), F32),
                            pltpu.VMEM((SCATTER_STRIDE, part, HEAD_DIM), F32)]
    return pl.pallas_call(
        functools.partial(_attn_kernel, d=dilation, nsub=nsub),
        grid=(s // step_tokens,),
        in_specs=[cur],
        out_specs=[pl.BlockSpec((step_tokens, GROUP_WIDTH), lambda i: (i, 0)),
                   pl.BlockSpec((step_tokens, HEAD_DIM), lambda i: (i, 0))],
        out_shape=[jax.ShapeDtypeStruct((s, GROUP_WIDTH), BF16), jax.ShapeDtypeStruct((s, HEAD_DIM), F32)],
        scratch_shapes=[pltpu.VMEM((dilation, BLOCK, 2 * GROUP_WIDTH), BF16)] + scatter_scratch,
        compiler_params=_params("arbitrary"),
        name=f"attn_d{dilation}",
    )(qkv)


def _discretize(lr, li, log_dt):
    dt = jnp.exp(log_dt)
    mag = jnp.exp(lr * dt)
    bar_re = mag * jnp.cos(li * dt)
    bar_im = mag * jnp.sin(li * dt)
    nr = bar_re - 1.0
    ni = bar_im
    den = lr * lr + li * li
    return bar_re, bar_im, (nr * lr + ni * li) / den, (ni * lr - nr * li) / den


def _expand_block_diag(compact, n_blocks):
    rows, b = compact.shape
    a = rows // n_blocks
    wide = n_blocks * b
    src_lane = lax.broadcasted_iota(jnp.int32, (b, wide), 0)
    dst_lane = lax.broadcasted_iota(jnp.int32, (b, wide), 1)
    tiled = _bdot(compact, (dst_lane % b == src_lane).astype(BF16))
    row_block = lax.broadcasted_iota(jnp.int32, (rows, wide), 0) // a
    col_block = lax.broadcasted_iota(jnp.int32, (rows, wide), 1) // b
    return jnp.where(row_block == col_block, tiled, 0.0).astype(BF16)


def _ssm_prep_kernel(rows_ref, rep_ref, b_ref, c_ref, w_in_ref,
                     bmat_ref, cre_ref, cim_ref, e_re_ref, e_im_ref, einv_re_ref, einv_im_ref, lam_ref,
                     w_in_bf_ref):
    w_in_bf_ref[...] = w_in_ref[...].astype(BF16)
    lr = rows_ref[0:1, :]
    li = rows_ref[1:2, :]
    dt = jnp.exp(rows_ref[2:3, :])

    @pl.when(pl.program_id(0) == 0)
    def _():
        _, _, z_re, z_im = _discretize(rep_ref[0], rep_ref[1], rep_ref[2])
        b_re = b_ref[0]
        b_im = b_ref[1]
        bb_re = (z_re * b_re - z_im * b_im).astype(BF16)
        bb_im = (z_re * b_im + z_im * b_re).astype(BF16)
        for hf in range(SSM_HALVES):
            rows = slice(hf * HALF_WIDTH, (hf + 1) * HALF_WIDTH)
            bmat_ref[rows, :HALF_STATES] = _expand_block_diag(bb_re[rows], HALF_GROUPS)
            bmat_ref[rows, HALF_STATES:] = _expand_block_diag(bb_im[rows], HALF_GROUPS)
            srows = slice(hf * HALF_STATES, (hf + 1) * HALF_STATES)
            cre_ref[srows, :] = _expand_block_diag(c_ref[0, srows, :].astype(BF16), HALF_GROUPS)
            cim_ref[srows, :] = _expand_block_diag(c_ref[1, srows, :].astype(BF16), HALF_GROUPS)
        whole = float(SSM_CHUNK)
        chunk_mag = jnp.exp(whole * (lr * dt))
        lam_ref[0:1, :] = chunk_mag * jnp.cos(whole * (li * dt))
        lam_ref[1:2, :] = chunk_mag * jnp.sin(whole * (li * dt))

    rows_per_step = e_re_ref.shape[0]
    t = (lax.broadcasted_iota(jnp.int32, (rows_per_step, N_STATES), 0)
         + pl.program_id(0) * rows_per_step).astype(F32)
    grow = jnp.exp(t * (lr * dt))
    theta = t * (li * dt)
    c = jnp.cos(theta)
    s = jnp.sin(theta)
    e_re_ref[...] = (grow * c).astype(BF16)
    e_im_ref[...] = (grow * s).astype(BF16)
    shrink = jnp.exp(-t * (lr * dt))
    einv_re_ref[...] = (shrink * c).astype(BF16)
    einv_im_ref[...] = (-(shrink * s)).astype(BF16)


def _ssm_prep(a_re, a_im, log_dt, b_re, b_im, c_re, c_im, w_in):
    per_state = jnp.stack([a_re, a_im, jnp.broadcast_to(log_dt[:, None], (SSM_GROUPS, SSM_STATE))])
    params = (per_state.reshape(3, N_STATES),
              jnp.repeat(per_state, SSM_GROUP, axis=1),
              jnp.swapaxes(jnp.stack([b_re, b_im]), 2, 3).reshape(2, SSM_WIDTH, SSM_STATE),
              jnp.swapaxes(jnp.stack([c_re, c_im]), 2, 3).reshape(2, N_STATES, SSM_GROUP))
    whole = lambda shape: pl.BlockSpec(shape, lambda i: (0,) * len(shape))
    slab = lambda arr: pl.BlockSpec((arr.shape[0] // PREP_STEPS, arr.shape[1]), lambda i: (i, 0))
    tab = jax.ShapeDtypeStruct((SSM_CHUNK, N_STATES), BF16)
    cmat = jax.ShapeDtypeStruct((N_STATES, HALF_WIDTH), BF16)
    bmat = jax.ShapeDtypeStruct((SSM_WIDTH, 2 * HALF_STATES), BF16)
    lam = jax.ShapeDtypeStruct((2, N_STATES), F32)
    return pl.pallas_call(
        _ssm_prep_kernel,
        grid=(PREP_STEPS,),
        in_specs=[whole(p.shape) for p in params] + [slab(w_in)],
        out_specs=[whole(bmat.shape), whole(cmat.shape), whole(cmat.shape), slab(tab), slab(tab), slab(tab),
                   slab(tab), whole(lam.shape), slab(w_in)],
        out_shape=[bmat, cmat, cmat, tab, tab, tab, tab, lam, jax.ShapeDtypeStruct(w_in.shape, BF16)],
        compiler_params=_params("arbitrary"),
        name="ssm_prep",
    )(*params, w_in)


def _gelu_tanh(x):
    return 0.5 * x * (1.0 + jnp.tanh(math.sqrt(2.0 / math.pi) * (x + 0.044715 * (x * x * x))))


def _ssm_kernel(u_ref, bmat_ref, cre_ref, cim_ref, tri_ref, e_re_ref, e_im_ref, einv_re_ref, einv_im_ref,
                lam_ref, d_ref, y_ref, h_re_scr, h_im_scr, carry_scr):
    @pl.when(pl.program_id(0) == 0)
    def _():
        carry_scr[...] = jnp.zeros_like(carry_scr)

    u = u_ref[...]
    ub = u.astype(BF16)
    for hf in range(SSM_HALVES):
        chans = slice(hf * HALF_WIDTH, (hf + 1) * HALF_WIDTH)
        y = d_ref[:, chans] * u[:, chans]
        for blk in range(HALF_STATES // STATE_BLOCK):
            local = slice(blk * STATE_BLOCK, (blk + 1) * STATE_BLOCK)
            local_im = slice(HALF_STATES + blk * STATE_BLOCK, HALF_STATES + (blk + 1) * STATE_BLOCK)
            states = slice(hf * HALF_STATES + blk * STATE_BLOCK, hf * HALF_STATES + (blk + 1) * STATE_BLOCK)
            bu_re_all = _bdot(ub[:, chans], bmat_ref[chans, local])
            bu_im_all = _bdot(ub[:, chans], bmat_ref[chans, local_im])
            lam_re = lam_ref[0:1, states]
            lam_im = lam_ref[1:2, states]
            carry_re = carry_scr[0:1, states]
            carry_im = carry_scr[1:2, states]
            for c in range(u.shape[0] // SSM_CHUNK):
                rows = slice(c * SSM_CHUNK, (c + 1) * SSM_CHUNK)
                bu_re = bu_re_all[rows].astype(BF16)
                bu_im = bu_im_all[rows].astype(BF16)
                einv_re = einv_re_ref[:, states]
                einv_im = einv_im_ref[:, states]
                x = jnp.concatenate([bu_re * einv_re - bu_im * einv_im,
                                     bu_re * einv_im + bu_im * einv_re], axis=1)
                a = _bdot(tri_ref[...], x)
                a_re = a[:, :STATE_BLOCK] + carry_re
                a_im = a[:, STATE_BLOCK:] + carry_im
                e_re = e_re_ref[:, states]
                e_im = e_im_ref[:, states]
                a_re_b = a_re.astype(BF16)
                a_im_b = a_im.astype(BF16)
                h_re_scr[rows, :] = e_re * a_re_b - e_im * a_im_b
                h_im_scr[rows, :] = e_re * a_im_b + e_im * a_re_b
                last_re = a_re[SSM_CHUNK - 1:SSM_CHUNK, :]
                last_im = a_im[SSM_CHUNK - 1:SSM_CHUNK, :]
                carry_re = lam_re * last_re - lam_im * last_im
                carry_im = lam_re * last_im + lam_im * last_re
            carry_scr[0:1, states] = carry_re
            carry_scr[1:2, states] = carry_im
            y = y + _bdot(h_re_scr[...], cre_ref[states, :]) - _bdot(h_im_scr[...], cim_ref[states, :])
        y_ref[:, chans] = _gelu_tanh(y).astype(BF16)


def _ssm(u, bmat, cre, cim, tri, e_re, e_im, einv_re, einv_im, lam, d_row):
    s = u.shape[0]
    tm = min(SSM_ROWS, s)
    row = pl.BlockSpec((tm, SSM_WIDTH), lambda i: (i, 0))
    consts = (bmat, cre, cim, tri, e_re, e_im, einv_re, einv_im, lam, d_row)
    return pl.pallas_call(
        _ssm_kernel,
        grid=(s // tm,),
        in_specs=[row] + [_resident(c.shape) for c in consts],
        out_specs=row,
        out_shape=jax.ShapeDtypeStruct((s, SSM_WIDTH), BF16),
        scratch_shapes=[pltpu.VMEM((tm, STATE_BLOCK), BF16), pltpu.VMEM((tm, STATE_BLOCK), BF16),
                        pltpu.VMEM((2, N_STATES), F32)],
        compiler_params=_params("arbitrary"),
        name="ssm",
    )(u, *consts)


def _mix_kernel(*refs, n_cast):
    x_ref, o0_ref, o1_ref, o2_ref, l0_ref, l1_ref, l2_ref, y_ref, gates_ref = refs[:9]
    wap_ref, wa_ref, wb_ref, wout_ref = refs[9:13]
    cast_in, h_ref, cast_out = refs[13:13 + n_cast], refs[13 + n_cast], refs[14 + n_cast:]
    _cast_blocks(cast_in, cast_out)
    for rb in range(x_ref.shape[0] // MIX_SUB_ROWS):
        rows = slice(rb * MIX_SUB_ROWS, (rb + 1) * MIX_SUB_ROWS)
        ls = (l0_ref[rows, :], l1_ref[rows, :], l2_ref[rows, :])
        m_max = jnp.maximum(jnp.maximum(ls[0], ls[1]), ls[2])
        es = [jnp.exp2(l - m_max) for l in ls]
        dens = [pltpu.roll(l, HEAD_DIM - STAT_LANES // 2, 1) for l in ls]
        inv = 1.0 / (es[0] * dens[0] + es[1] * dens[1] + es[2] * dens[2])
        heads = []
        for h in range(HEADS_PER_GROUP):
            hs = slice(h * HEAD_DIM, (h + 1) * HEAD_DIM)
            col = slice(h * STAT_LANES, h * STAT_LANES + 1)
            heads.append(sum(jnp.broadcast_to((e * inv)[:, col], (MIX_SUB_ROWS, HEAD_DIM)).astype(BF16)
                             * o_ref[rows, hs] for e, o_ref in zip(es, (o0_ref, o1_ref, o2_ref))))
        attn = jnp.concatenate(heads, axis=1)
        attn_d = _bdot(attn, wap_ref[...])
        y = y_ref[rows, :]
        ssm_d = _bdot(y, wa_ref[...]) * jax.nn.sigmoid(_bdot(y, wb_ref[...]))
        mix = (gates_ref[rows, :D_MODEL] * attn_d.astype(BF16)
               + gates_ref[rows, D_MODEL:] * ssm_d.astype(BF16))
        h_ref[rows, :] = x_ref[rows, :] + _bdot(mix, wout_ref[...])


def _mix(x, os, ls, y, gates, wap, wa, wb, wout, later_weights):
    s = x.shape[0]
    tm = min(MIX_ROWS, s)
    row = lambda w: pl.BlockSpec((tm, w), lambda i: (i, 0))
    weights = (wap, wa, wb, wout)
    cast_specs, cast_shapes = _cast_plan(later_weights, s // tm)
    return pl.pallas_call(
        functools.partial(_mix_kernel, n_cast=len(later_weights)),
        grid=(s // tm,),
        in_specs=[row(D_MODEL)] + [row(GROUP_WIDTH)] * 3 + [row(HEAD_DIM)] * 3
        + [row(SSM_WIDTH), row(2 * D_MODEL)] + [_resident(w.shape) for w in weights] + cast_specs,
        out_specs=[row(D_MODEL)] + cast_specs,
        out_shape=[jax.ShapeDtypeStruct((s, D_MODEL), F32)] + cast_shapes,
        compiler_params=_params("parallel"),
        name="mix",
    )(x, *os, *ls, y, gates, *weights, *later_weights)


def _ffn_kernel(h_ref, p_ref, gffn_ref, wg_ref, wu_ref, wd_ref, wpg_ref, wpp_ref, gfin_ref,
                out_ref, acc_scr):
    for rb in range(h_ref.shape[0] // FFN_SUB_ROWS):
        rows = slice(rb * FFN_SUB_ROWS, (rb + 1) * FFN_SUB_ROWS)
        h = h_ref[rows, :]
        n2 = _rmsnorm(h, gffn_ref[...]).astype(BF16)
        for idx, (c0, width) in enumerate(FFN_CHUNKS):
            gate = _bdot(n2, wg_ref[:, c0:c0 + width])
            up = _bdot(n2, wu_ref[:, c0:c0 + width])
            act = (gate * jax.nn.sigmoid(gate) * up).astype(BF16)
            part = _bdot(act, wd_ref[c0:c0 + width, :])
            if idx == 0:
                acc_scr[rows, :] = h + part
            else:
                acc_scr[rows, :] += part
        h2 = acc_scr[rows, :]
        ple = (jax.nn.sigmoid(_bdot(h2.astype(BF16), wpg_ref[...]))
               * _bdot(p_ref[rows, :].astype(BF16), wpp_ref[...]))
        out_ref[rows, :] = _rmsnorm(h2 + ple, gfin_ref[...])


def _ffn(h, p, g_ffn, wg, wu, wd, wpg, wpp, g_final):
    s = h.shape[0]
    tm = min(FFN_ROWS, s)
    row = lambda w: pl.BlockSpec((tm, w), lambda i: (i, 0))
    consts = (g_ffn, wg, wu, wd, wpg, wpp, g_final)
    return pl.pallas_call(
        _ffn_kernel,
        grid=(s // tm,),
        in_specs=[row(D_MODEL), row(PLE_DIM)] + [_resident(c.shape) for c in consts],
        out_specs=row(D_MODEL),
        out_shape=jax.ShapeDtypeStruct((s, D_MODEL), F32),
        scratch_shapes=[pltpu.VMEM((tm, D_MODEL), F32)],
        compiler_params=_params("parallel"),
        name="ffn",
    )(h, p, *consts)


def _layer(x, p, positions, g_mix, w_in, a_re, a_im, log_dt, b_re, b_im, c_re, c_im, d_skip,
           w_attn_proj, w_glu_a, w_glu_b, w_out, g_ffn, w_ffn_gate, w_ffn_up, w_ffn_down,
           w_ple_gate, w_ple_proj, g_final):
    row = lambda v: v.reshape(1, -1).astype(F32)

    inv_freq = ROPE_THETA ** (-jnp.arange(ROPE_HALF, dtype=F32) * 2.0 / ROPE_DIM)
    bmat, cre, cim, e_re, e_im, einv_re, einv_im, lam, w_in_bf = _ssm_prep(
        a_re, a_im, log_dt, b_re, b_im, c_re, c_im, w_in)
    outs = _proj(x, positions.reshape(1, -1), row(g_mix), inv_freq.reshape(ROPE_HALF, 1), w_in_bf,
                 (w_attn_proj, w_glu_a, w_glu_b, w_out, w_ffn_gate, w_ffn_up, w_ple_gate))
    qkvs, (u, gates) = outs[:N_GROUPS], outs[N_GROUPS:N_GROUPS + 2]
    wap, wga, wgb, wout, wfg, wfu, wpg = outs[N_GROUPS + 2:]

    attn_os, attn_ls = zip(*[_attn_group(qkvs[g], d) for g, d in reversed(list(enumerate(ATTN_DILATIONS)))][::-1])

    tri = jnp.tril(jnp.ones((SSM_CHUNK, SSM_CHUNK), F32)).astype(BF16)
    y = _ssm(u, bmat, cre, cim, tri, e_re, e_im, einv_re, einv_im, lam, row(d_skip))

    h1, wfd, wpp = _mix(x, attn_os, attn_ls, y, gates, wap, wga, wgb, wout, (w_ffn_down, w_ple_proj))
    return _ffn(h1, p, row(g_ffn), wfg, wfu, wfd, wpg, wpp, row(g_final))


def kernel(x, p, positions, g_mix, w_in, a_re, a_im, log_dt, b_re, b_im, c_re, c_im, d_skip,
           w_attn_proj, w_glu_a, w_glu_b, w_out, g_ffn, w_ffn_gate, w_ffn_up, w_ffn_down,
           w_ple_gate, w_ple_proj, g_final):
    batch, depth = x.shape[0], p.shape[0]
    assert batch == 1 and depth == 1, "kernel supports the stated BATCH=1, DEPTH=1 problem"
    out = _layer(x[0], p[0, 0], positions[0], g_mix[0], w_in[0], a_re[0], a_im[0], log_dt[0],
                 b_re[0], b_im[0], c_re[0], c_im[0], d_skip[0], w_attn_proj[0], w_glu_a[0],
                 w_glu_b[0], w_out[0], g_ffn[0], w_ffn_gate[0], w_ffn_up[0], w_ffn_down[0],
                 w_ple_gate[0], w_ple_proj[0], g_final)
    return out[None]
```
